```python
import jax
import jax.numpy as jnp
from jax import lax
import numpy as np

D_MODEL = 1024
BATCH = 16
SEQ = 2048
DEPTH = 2

N_A_LAYERS = DEPTH // 2
N_B_LAYERS = DEPTH - N_A_LAYERS
RMS_EPS = 1e-6

GLA_HEADS = 4
GLA_DK = D_MODEL // 2 // GLA_HEADS
GLA_DV = D_MODEL // GLA_HEADS
GLA_GATE_RANK = 16
GLA_GATE_TAU = 16.0
GLA_CHUNK = 64
GLA_IN_WIDTH = 2 * GLA_HEADS * GLA_DK + 2 * GLA_HEADS * GLA_DV + GLA_GATE_RANK

FOX_HEADS = 16
FOX_HEAD_DIM = D_MODEL // FOX_HEADS
FOX_WIDTH = FOX_HEADS * FOX_HEAD_DIM
FOX_Q_BLOCK = 128

MOE_GROUPS = 4
MOE_EXPERTS_PER_GROUP = 8
MOE_EXPERTS = MOE_GROUPS * MOE_EXPERTS_PER_GROUP
MOE_TOP_K = 2
MOE_D_FF = D_MODEL // 2
MOE_BLOCK = 256

kernel_name = 'hybrid_gla_fox_hmoe'


def rmsnorm(x, g):
    xf = x.astype(jnp.float32)
    y = xf * lax.rsqrt(jnp.mean(xf * xf, axis=-1, keepdims=True) + RMS_EPS)
    return (y * g.astype(jnp.float32)).astype(x.dtype)


def gla_chunked(q, k, v, log_a):
    b_, s_, h_, dk = q.shape
    dv = v.shape[-1]
    nc = s_ // GLA_CHUNK

    def to_chunks(t):
        return t.reshape(b_, nc, GLA_CHUNK, h_, t.shape[-1]).transpose(1, 0, 3, 2, 4)

    causal = jnp.tril(jnp.ones((GLA_CHUNK, GLA_CHUNK), dtype=bool))[None, None, :, :, None]

    def step(state, inp):
        qc, kc, vc, gc = inp
        qf = qc.astype(jnp.float32)
        kf = kc.astype(jnp.float32)
        vf = vc.astype(jnp.float32)
        cum = jnp.cumsum(gc.astype(jnp.float32), axis=2)
        last = cum[:, :, -1:, :]
        o_inter = jnp.einsum('bhid,bhde->bhie', qf * jnp.exp(cum), state)
        rel = jnp.where(causal, cum[:, :, :, None, :] - cum[:, :, None, :, :], -jnp.inf)
        scores = jnp.einsum('bhijd,bhjd->bhij', qf[:, :, :, None, :] * jnp.exp(rel), kf)
        o_intra = jnp.einsum('bhij,bhje->bhie', scores, vf)
        state = state * jnp.exp(last)[:, :, 0, :, None] + jnp.einsum(
            'bhjd,bhje->bhde', kf * jnp.exp(last - cum), vf)
        return state, o_inter + o_intra

    state0 = jnp.zeros((b_, h_, dk, dv), jnp.float32)
    _, o = lax.scan(step, state0, (to_chunks(q), to_chunks(k), to_chunks(v), to_chunks(log_a)))
    return o.transpose(1, 0, 3, 2, 4).reshape(b_, s_, h_, dv).astype(v.dtype)


def gla_mixer(x, w_in, w_a2, b_a, out_norm, w_out):
    b_, s_, _ = x.shape
    qk = GLA_HEADS * GLA_DK
    vw = GLA_HEADS * GLA_DV
    proj = x @ w_in
    q, k, v, r, a = jnp.split(proj, [qk, 2 * qk, 2 * qk + vw, 2 * qk + 2 * vw], axis=-1)
    q = q.reshape(b_, s_, GLA_HEADS, GLA_DK) * (GLA_DK ** -0.5)
    k = k.reshape(b_, s_, GLA_HEADS, GLA_DK)
    v = v.reshape(b_, s_, GLA_HEADS, GLA_DV)
    log_a = jax.nn.log_sigmoid((a @ w_a2 + b_a).astype(jnp.float32)) / GLA_GATE_TAU
    o = gla_chunked(q, k, v, log_a.reshape(b_, s_, GLA_HEADS, GLA_DK))
    o = rmsnorm(o, out_norm)
    o = o.reshape(b_, s_, vw) * jax.nn.silu(r)
    return o @ w_out


def shared_kv(h, kv_norm, kv_w, kv_b_f):
    b_, s_, _ = h.shape
    proj = rmsnorm(h, kv_norm) @ kv_w
    k, v, f = jnp.split(proj, [FOX_WIDTH, 2 * FOX_WIDTH], axis=-1)
    k = k.reshape(b_, s_, FOX_HEADS, FOX_HEAD_DIM)
    v = v.reshape(b_, s_, FOX_HEADS, FOX_HEAD_DIM)
    log_f = jax.nn.log_sigmoid((f + kv_b_f).astype(jnp.float32))
    c = jnp.cumsum(log_f, axis=1).transpose(0, 2, 1)
    return k, v, c


def forgetting_attention(q, k, v, c):
    b_, s_, h_, dh = q.shape
    scale = dh ** -0.5
    outs = []
    for blk in range(s_ // FOX_Q_BLOCK):
        q0 = blk * FOX_Q_BLOCK
        q1 = q0 + FOX_Q_BLOCK
        logits = jnp.einsum('bqhd,bkhd->bhqk', q[:, q0:q1], k[:, :q1]).astype(jnp.float32) * scale
        logits = logits + c[:, :, q0:q1, None] - c[:, :, None, :q1]
        mask = (q0 + jnp.arange(FOX_Q_BLOCK))[:, None] >= jnp.arange(q1)[None, :]
        p = jax.nn.softmax(jnp.where(mask, logits, -jnp.inf), axis=-1)
        outs.append(jnp.einsum('bhqk,bkhd->bqhd', p.astype(v.dtype), v[:, :q1]))
    return jnp.concatenate(outs, axis=1)


def fox_mixer(x, k, v, c, w_in, w_out):
    b_, s_, _ = x.shape
    q, og = jnp.split(x @ w_in, [FOX_WIDTH], axis=-1)
    q = q.reshape(b_, s_, FOX_HEADS, FOX_HEAD_DIM)
    o = forgetting_attention(q, k, v, c).reshape(b_, s_, FOX_WIDTH)
    return (o * jax.nn.sigmoid(og)) @ w_out


def grouped_experts(xt, e_idx, gate, w_gate, w_up, w_down):
    n_tok, d = xt.shape
    n_slot = n_tok * MOE_TOP_K
    flat_e = e_idx.reshape(n_slot)
    order = jnp.argsort(flat_e)
    sorted_e = flat_e[order]
    sorted_tok = order // MOE_TOP_K
    counts = jnp.bincount(flat_e, length=MOE_EXPERTS)
    padded = (counts + MOE_BLOCK - 1) // MOE_BLOCK * MOE_BLOCK
    start = jnp.cumsum(counts) - counts
    pad_end = jnp.cumsum(padded)
    pad_start = pad_end - padded
    dest = pad_start[sorted_e] + jnp.arange(n_slot) - start[sorted_e]
    n_blocks = -(-n_slot // MOE_BLOCK) + MOE_EXPERTS
    xbuf = jnp.zeros((n_blocks * MOE_BLOCK, d), xt.dtype).at[dest].set(xt[sorted_tok])
    block_e = jnp.minimum(
        jnp.searchsorted(pad_end, jnp.arange(n_blocks) * MOE_BLOCK, side='right'), MOE_EXPERTS - 1)

    def expert_block(args):
        xb, e = args
        hb = jax.nn.silu(xb @ w_gate[e]) * (xb @ w_up[e])
        return hb @ w_down[e]

    ybuf = lax.map(expert_block, (xbuf.reshape(n_blocks, MOE_BLOCK, d), block_e))
    y_sorted = ybuf.reshape(n_blocks * MOE_BLOCK, d)[dest]
    w_sorted = gate.reshape(n_slot)[order].astype(y_sorted.dtype)
    return jax.ops.segment_sum(y_sorted * w_sorted[:, None], sorted_tok, num_segments=n_tok)


def hier_moe(x, w_coarse, b_coarse, w_fine, b_fine, w_gate, w_up, w_down):
    b_, s_, d = x.shape
    xt = x.reshape(b_ * s_, d)
    p_coarse = jax.nn.softmax((xt @ w_coarse).astype(jnp.float32) + b_coarse, axis=-1)
    pc_top, g_idx = lax.top_k(p_coarse, 1)
    lf = ((xt @ w_fine).astype(jnp.float32) + b_fine).reshape(-1, MOE_GROUPS, MOE_EXPERTS_PER_GROUP)
    lf_sel = jnp.take_along_axis(lf, g_idx[:, :, None], axis=1)[:, 0]
    pf_top, j_idx = lax.top_k(jax.nn.softmax(lf_sel, axis=-1), MOE_TOP_K)
    gate = pc_top * pf_top / jnp.sum(pf_top, axis=-1, keepdims=True)
    e_idx = g_idx * MOE_EXPERTS_PER_GROUP + j_idx
    y = grouped_experts(xt, e_idx, gate, w_gate, w_up, w_down)
    return y.reshape(b_, s_, d).astype(x.dtype)


def setup_inputs(seed: int = 0) -> dict:
    key = jax.random.key(seed)
    ks = jax.random.split(key, 24)
    d = D_MODEL

    def nrm(k, shape, fan_in):
        return jax.random.normal(k, shape, jnp.float32) * (fan_in ** -0.5)

    def gain(k, shape):
        return 1.0 + 0.02 * jax.random.normal(k, shape, jnp.float32)

    def bias(k, shape):
        return 0.01 * jax.random.normal(k, shape, jnp.float32)

    return {
        'x': jax.random.normal(ks[0], (BATCH, SEQ, d), jnp.float32),
        'gla_norm': gain(ks[1], (N_A_LAYERS, d)),
        'gla_w_in': nrm(ks[2], (N_A_LAYERS, d, GLA_IN_WIDTH), d),
        'gla_w_a2': nrm(ks[3], (N_A_LAYERS, GLA_GATE_RANK, GLA_HEADS * GLA_DK), GLA_GATE_RANK),
        'gla_b_a': bias(ks[4], (N_A_LAYERS, GLA_HEADS * GLA_DK)),
        'gla_out_norm': gain(ks[5], (N_A_LAYERS, GLA_HEADS, GLA_DV)),
        'gla_w_out': nrm(ks[6], (N_A_LAYERS, GLA_HEADS * GLA_DV, d), GLA_HEADS * GLA_DV),
        'kv_norm': gain(ks[7], (d,)),
        'kv_w': nrm(ks[8], (d, 2 * FOX_WIDTH + FOX_HEADS), d),
        'kv_b_f': bias(ks[9], (FOX_HEADS,)),
        'fox_norm': gain(ks[10], (N_B_LAYERS, d)),
        'fox_w_in': nrm(ks[11], (N_B_LAYERS, d, 2 * FOX_WIDTH), d),
        'fox_w_out': nrm(ks[12], (N_B_LAYERS, FOX_WIDTH, d), FOX_WIDTH),
        'ffn_norm': gain(ks[13], (DEPTH, d)),
        'router_coarse_w': nrm(ks[14], (DEPTH, d, MOE_GROUPS), d),
        'router_coarse_b': bias(ks[15], (DEPTH, MOE_GROUPS)),
        'router_fine_w': nrm(ks[16], (DEPTH, d, MOE_EXPERTS), d),
        'router_fine_b': bias(ks[17], (DEPTH, MOE_EXPERTS)),
        'expert_w_gate': nrm(ks[18], (DEPTH, MOE_EXPERTS, d, MOE_D_FF), d),
        'expert_w_up': nrm(ks[19], (DEPTH, MOE_EXPERTS, d, MOE_D_FF), d),
        'expert_w_down': nrm(ks[20], (DEPTH, MOE_EXPERTS, MOE_D_FF, d), MOE_D_FF),
        'final_norm': gain(ks[21], (d,)),
    }


def reference(x, gla_norm, gla_w_in, gla_w_a2, gla_b_a, gla_out_norm, gla_w_out,
              kv_norm, kv_w, kv_b_f, fox_norm, fox_w_in, fox_w_out,
              ffn_norm, router_coarse_w, router_coarse_b, router_fine_w, router_fine_b,
              expert_w_gate, expert_w_up, expert_w_down, final_norm):
    h = x
    k_sh = v_sh = c_sh = None
    for layer in range(DEPTH):
        if layer < N_A_LAYERS:
            i = layer
            h = h + gla_mixer(rmsnorm(h, gla_norm[i]), gla_w_in[i], gla_w_a2[i], gla_b_a[i],
                              gla_out_norm[i], gla_w_out[i])
        else:
            if layer == N_A_LAYERS:
                k_sh, v_sh, c_sh = shared_kv(h, kv_norm, kv_w, kv_b_f)
            i = layer - N_A_LAYERS
            h = h + fox_mixer(rmsnorm(h, fox_norm[i]), k_sh, v_sh, c_sh, fox_w_in[i], fox_w_out[i])
        h = h + hier_moe(rmsnorm(h, ffn_norm[layer]), router_coarse_w[layer], router_coarse_b[layer],
                         router_fine_w[layer], router_fine_b[layer], expert_w_gate[layer],
                         expert_w_up[layer], expert_w_down[layer])
    return rmsnorm(h, final_norm)
```

```python
import functools

import jax
import jax.numpy as jnp
from jax import lax
from jax.experimental import pallas as pl
from jax.experimental.pallas import tpu as pltpu

F32 = jnp.float32
BF16 = jnp.bfloat16

RMS_EPS = 1e-6
GLA_HEADS = 4
GLA_GATE_RANK = 16
GLA_GATE_TAU = 16.0
FOX_HEADS = 16
MOE_GROUPS = 4
MOE_EXPERTS_PER_GROUP = 8
MOE_EXPERTS = MOE_GROUPS * MOE_EXPERTS_PER_GROUP
MOE_TOP_K = 2
MOE_BLOCK = 256

LANES = 128
SUBLANES = 8
VMEM_LIMIT = 56 * 1024 * 1024

GLA_CHUNK = 128
ROUTER_ROWS = 40


def _cparams(sem):
    return pltpu.CompilerParams(dimension_semantics=sem, vmem_limit_bytes=VMEM_LIMIT)


def _rms_unit(x):
    return x * lax.rsqrt(jnp.mean(x * x, axis=-1, keepdims=True) + RMS_EPS)


def _log_sigmoid(z):
    return jnp.minimum(z, 0.0) - jnp.log1p(jnp.exp(-jnp.abs(z)))


def _sigmoid(z):
    return 1.0 / (1.0 + jnp.exp(-z))


def _dot(a, b):
    return jnp.dot(a, b, preferred_element_type=F32)


def _dot_nt(a, b):
    return lax.dot_general(a, b, (((1,), (1,)), ((), ())), preferred_element_type=F32)


def _dot_tn(a, b):
    return lax.dot_general(a, b, (((0,), (0,)), ((), ())), preferred_element_type=F32)


def _split_bf16(x):
    hi = x.astype(BF16)
    lo = (x - hi.astype(F32)).astype(BF16)
    return hi, lo


def _gla_in_kernel(h_ref, g_ref, w_ref, wa1_ref, wa2_ref, ba_ref, q_ref, k_ref, v_ref, r_ref, la_ref, *, qk, vw):
    xn = (_rms_unit(h_ref[...]) * g_ref[...]).astype(BF16)
    q_ref[...] = _dot(xn, w_ref[:, 0:qk]).astype(BF16)
    k_ref[...] = _dot(xn, w_ref[:, qk:2 * qk]).astype(BF16)
    v_ref[...] = _dot(xn, w_ref[:, 2 * qk:2 * qk + vw]).astype(BF16)
    r_ref[...] = _dot(xn, w_ref[:, 2 * qk + vw:2 * qk + 2 * vw]).astype(BF16)
    a = _dot(xn, wa1_ref[...])
    z = _dot(a.astype(BF16), wa2_ref[...]) + ba_ref[...]
    la_ref[...] = _log_sigmoid(z) * (1.0 / GLA_GATE_TAU)


def _gla_in(h, gain, w_in, w_a2, b_a, tm):
    t, d = h.shape
    qk = w_a2.shape[1]
    vw = (w_in.shape[1] - 2 * qk - GLA_GATE_RANK) // 2
    w_main = w_in[:, :2 * qk + 2 * vw].astype(BF16)
    w_a1 = w_in[:, 2 * qk + 2 * vw:].astype(BF16)
    const = lambda i: (0, 0)
    row = lambda i: (i, 0)
    return pl.pallas_call(
        functools.partial(_gla_in_kernel, qk=qk, vw=vw),
        grid=(t // tm,),
        in_specs=[
            pl.BlockSpec((tm, d), row),
            pl.BlockSpec((1, d), const),
            pl.BlockSpec(w_main.shape, const),
            pl.BlockSpec(w_a1.shape, const),
            pl.BlockSpec(w_a2.shape, const),
            pl.BlockSpec((1, qk), const),
        ],
        out_specs=[
            pl.BlockSpec((tm, qk), row),
            pl.BlockSpec((tm, qk), row),
            pl.BlockSpec((tm, vw), row),
            pl.BlockSpec((tm, vw), row),
            pl.BlockSpec((tm, qk), row),
        ],
        out_shape=[
            jax.ShapeDtypeStruct((t, qk), BF16),
            jax.ShapeDtypeStruct((t, qk), BF16),
            jax.ShapeDtypeStruct((t, vw), BF16),
            jax.ShapeDtypeStruct((t, vw), BF16),
            jax.ShapeDtypeStruct((t, qk), F32),
        ],
        compiler_params=_cparams(("arbitrary",)),
        name="gla_in",
    )(h, gain.reshape(1, d), w_main, w_a1, w_a2.astype(BF16), b_a.reshape(1, qk))


def _gla_core_kernel(q_ref, k_ref, v_ref, r_ref, la_ref, gn_ref, o_ref, st_ref, *, n_chunks, scale):
    c_len = GLA_CHUNK

    @pl.when(pl.program_id(2) == 0)
    def _():
        st_ref[...] = jnp.zeros_like(st_ref)

    ri = lax.broadcasted_iota(jnp.int32, (c_len, c_len), 0)
    ci = lax.broadcasted_iota(jnp.int32, (c_len, c_len), 1)
    causal = ri >= ci
    ltri = jnp.where(causal, 1.0, 0.0).astype(BF16)
    gain = gn_ref[0]

    for c in range(n_chunks):
        sl = pl.ds(c * c_len, c_len)
        g_hi, g_lo = _split_bf16(la_ref[sl, :])
        cum = _dot(ltri, g_hi) + _dot(ltri, g_lo)
        last = cum[c_len - 1:c_len, :]
        mid = cum[c_len // 2 - 1:c_len // 2, :]
        qf = q_ref[sl, :].astype(F32) * scale
        kf = k_ref[sl, :].astype(F32)
        vb = v_ref[sl, :]
        qs = (qf * jnp.exp(cum - mid)).astype(BF16)
        ks = (kf * jnp.exp(mid - cum)).astype(BF16)
        scores = jnp.where(causal, _dot_nt(qs, ks), 0.0)
        o = _dot(scores.astype(BF16), vb)
        st = st_ref[...]
        qd = (qf * jnp.exp(cum)).astype(BF16)
        o = o + _dot_nt(qd, st.astype(BF16))
        kd = (kf * jnp.exp(last - cum)).astype(BF16)
        st_ref[...] = st * jnp.exp(last) + _dot_tn(vb, kd)
        on = _rms_unit(o) * gain
        rr = r_ref[sl, :].astype(F32)
        o_ref[sl, :] = (on * (rr * _sigmoid(rr))).astype(BF16)


def _gla_core(q, k, v, r, la, out_norm, batch, seq, lc):
    t, qk = q.shape
    vw = v.shape[1]
    heads = GLA_HEADS
    dk, dv = qk // heads, vw // heads
    nj = seq // lc
    tok = lambda b, h, j: (b * nj + j, h)
    return pl.pallas_call(
        functools.partial(_gla_core_kernel, n_chunks=lc // GLA_CHUNK, scale=dk ** -0.5),
        grid=(batch, heads, nj),
        in_specs=[
            pl.BlockSpec((lc, dk), tok),
            pl.BlockSpec((lc, dk), tok),
            pl.BlockSpec((lc, dv), tok),
            pl.BlockSpec((lc, dv), tok),
            pl.BlockSpec((lc, dk), tok),
            pl.BlockSpec((1, 1, dv), lambda b, h, j: (h, 0, 0)),
        ],
        out_specs=pl.BlockSpec((lc, dv), tok),
        out_shape=jax.ShapeDtypeStruct((t, vw), BF16),
        scratch_shapes=[pltpu.VMEM((dv, dk), F32)],
        compiler_params=_cparams(("arbitrary", "arbitrary", "arbitrary")),
        name="gla_core",
    )(q, k, v, r, la, out_norm.reshape(heads, 1, dv))


def _post_kernel(a_ref, w_ref, h_ref, g_ref, wr_ref, br_ref,
                 hn_ref, xr_ref, eidx_ref, gate_ref, rank_ref, cnt_ref, base_ref, *, tm, d):
    i = pl.program_id(0)

    @pl.when(i == 0)
    def _():
        base_ref[...] = jnp.zeros_like(base_ref)

    hn = h_ref[...] + _dot(a_ref[...], w_ref[...])
    hn_ref[...] = hn
    xn = _rms_unit(hn) * g_ref[...]
    for c in range(d // LANES):
        xr_ref[pl.ds(c, tm, stride=SUBLANES), :] = xn[:, c * LANES:(c + 1) * LANES]

    lg = lax.dot_general(wr_ref[...], xn, (((1,), (1,)), ((), ())), preferred_element_type=F32,
                         precision=lax.Precision.HIGHEST) + br_ref[...]
    lc = lg[0:MOE_GROUPS]
    m_c = jnp.max(lc, axis=0, keepdims=True)
    pc_top = 1.0 / jnp.sum(jnp.exp(lc - m_c), axis=0, keepdims=True)
    row_g = lax.broadcasted_iota(jnp.int32, lc.shape, 0)
    g_idx = jnp.min(jnp.where(lc == m_c, row_g, MOE_GROUPS), axis=0, keepdims=True)
    eg = MOE_EXPERTS_PER_GROUP
    lf = jnp.zeros((eg, tm), F32)
    for g in range(MOE_GROUPS):
        lf = lf + jnp.where(g_idx == g, lg[SUBLANES + g * eg:SUBLANES + (g + 1) * eg], 0.0)
    ef = jnp.exp(lf - jnp.max(lf, axis=0, keepdims=True))
    pf = ef / jnp.sum(ef, axis=0, keepdims=True)
    row_e = lax.broadcasted_iota(jnp.int32, pf.shape, 0)
    v1 = jnp.max(pf, axis=0, keepdims=True)
    j1 = jnp.min(jnp.where(pf == v1, row_e, eg), axis=0, keepdims=True)
    pf2 = jnp.where(row_e == j1, -1.0, pf)
    v2 = jnp.max(pf2, axis=0, keepdims=True)
    j2 = jnp.min(jnp.where(pf2 == v2, row_e, eg), axis=0, keepdims=True)
    denom = v1 + v2
    e1 = g_idx * eg + j1
    e2 = g_idx * eg + j2
    eidx_ref[0:1, :] = e1
    eidx_ref[1:2, :] = e2
    gate_ref[0:1, :] = pc_top * v1 / denom
    gate_ref[1:2, :] = pc_top * v2 / denom

    ti = lax.broadcasted_iota(jnp.int32, (tm, tm), 0)
    tj = lax.broadcasted_iota(jnp.int32, (tm, tm), 1)
    upper = jnp.where(ti <= tj, 1.0, 0.0).astype(BF16)
    row_x = lax.broadcasted_iota(jnp.int32, (MOE_EXPERTS, tm), 0)
    oh1 = row_x == e1
    oh2 = row_x == e2
    pre1 = _dot(jnp.where(oh1, 1.0, 0.0).astype(BF16), upper)
    pre2 = _dot(jnp.where(oh2, 1.0, 0.0).astype(BF16), upper)
    cnt1 = pre1[:, tm - 1:tm]
    cnt2 = pre2[:, tm - 1:tm]
    base = base_ref[:, 0:1]
    rank1 = jnp.sum(jnp.where(oh1, base + pre1 - 1.0, 0.0), axis=0, keepdims=True)
    rank2 = jnp.sum(jnp.where(oh2, base + cnt1 + pre2 - 1.0, 0.0), axis=0, keepdims=True)
    rank_ref[0:1, :] = rank1.astype(jnp.int32)
    rank_ref[1:2, :] = rank2.astype(jnp.int32)
    new_base = base_ref[...] + (cnt1 + cnt2)
    base_ref[...] = new_base
    cnt_ref[...] = new_base


def _post(a, w_out, h, gain, w_coarse, b_coarse, w_fine, b_fine, tm):
    t, d = h.shape
    wr = jnp.zeros((ROUTER_ROWS, d), F32)
    wr = wr.at[0:MOE_GROUPS].set(w_coarse.T).at[SUBLANES:].set(w_fine.T)
    br = jnp.zeros((ROUTER_ROWS, 1), F32)
    br = br.at[0:MOE_GROUPS, 0].set(b_coarse).at[SUBLANES:, 0].set(b_fine)
    const = lambda i: (0, 0)
    row = lambda i: (i, 0)
    col = lambda i: (0, i)
    return pl.pallas_call(
        functools.partial(_post_kernel, tm=tm, d=d),
        grid=(t // tm,),
        in_specs=[
            pl.BlockSpec((tm, a.shape[1]), row),
            pl.BlockSpec(w_out.shape, const),
            pl.BlockSpec((tm, d), row),
            pl.BlockSpec((1, d), const),
            pl.BlockSpec((ROUTER_ROWS, d), const),
            pl.BlockSpec((ROUTER_ROWS, 1), const),
        ],
        out_specs=[
            pl.BlockSpec((tm, d), row),
            pl.BlockSpec((tm * SUBLANES, LANES), row),
            pl.BlockSpec((MOE_TOP_K, tm), col),
            pl.BlockSpec((MOE_TOP_K, tm), col),
            pl.BlockSpec((MOE_TOP_K, tm), col),
            pl.BlockSpec((MOE_EXPERTS, LANES), const),
        ],
        out_shape=[
            jax.ShapeDtypeStruct((t, d), F32),
            jax.ShapeDtypeStruct((t * SUBLANES, LANES), F32),
            jax.ShapeDtypeStruct((MOE_TOP_K, t), jnp.int32),
            jax.ShapeDtypeStruct((MOE_TOP_K, t), F32),
            jax.ShapeDtypeStruct((MOE_TOP_K, t), jnp.int32),
            jax.ShapeDtypeStruct((MOE_EXPERTS, LANES), F32),
        ],
        scratch_shapes=[pltpu.VMEM((MOE_EXPERTS, LANES), F32)],
        compiler_params=_cparams(("arbitrary",)),
        name="post",
    )(a, w_out.astype(BF16), h, gain.reshape(1, d), wr, br)


def _row_tile(ref, idx):
    return ref.at[pl.ds(pl.multiple_of(idx * SUBLANES, SUBLANES), SUBLANES), :]


def _dispatch_kernel(dest_ref, xr_ref, zero_ref, xbuf_ref, sem, *, tm):
    del zero_ref

    def issue(tk, carry):
        for s in range(MOE_TOP_K):
            pltpu.make_async_copy(_row_tile(xr_ref, tk), _row_tile(xbuf_ref, dest_ref[0, s, tk]), sem).start()
        return carry

    lax.fori_loop(0, tm, issue, 0, unroll=8)
    for s in range(MOE_TOP_K):
        pltpu.make_async_copy(xr_ref, xbuf_ref.at[pl.ds(0, tm * SUBLANES), :], sem).wait()


def _dispatch(xr, dest3, n_rows, tm):
    nt = dest3.shape[0]
    zeros = jnp.zeros((n_rows * SUBLANES, LANES), F32)
    return pl.pallas_call(
        functools.partial(_dispatch_kernel, tm=tm),
        grid=(nt,),
        in_specs=[
            pl.BlockSpec((1, MOE_TOP_K, tm), lambda i: (i, 0, 0), memory_space=pltpu.SMEM),
            pl.BlockSpec((tm * SUBLANES, LANES), lambda i: (i, 0)),
            pl.BlockSpec(memory_space=pl.ANY),
        ],
        out_specs=pl.BlockSpec(memory_space=pl.ANY),
        out_shape=jax.ShapeDtypeStruct(zeros.shape, F32),
        scratch_shapes=[pltpu.SemaphoreType.DMA],
        input_output_aliases={2: 0},
        compiler_params=_cparams(("arbitrary",)),
        name="dispatch",
    )(dest3, xr, zeros)


def _experts_kernel(be_ref, nu_ref, x_ref, wg_ref, wu_ref, wd_ref, y_ref, x2_ref, *, d):
    del be_ref
    rows = MOE_BLOCK

    @pl.when(pl.program_id(0) < nu_ref[0])
    def _():
        for c in range(d // LANES):
            x2_ref[:, c * LANES:(c + 1) * LANES] = x_ref[pl.ds(c, rows, stride=SUBLANES), :].astype(BF16)
        x2 = x2_ref[...]
        gp = _dot(x2, wg_ref[...])
        up = _dot(x2, wu_ref[...])
        hid = (gp * _sigmoid(gp) * up).astype(BF16)
        y = _dot(hid, wd_ref[...])
        for c in range(d // LANES):
            y_ref[pl.ds(c, rows, stride=SUBLANES), :] = y[:, c * LANES:(c + 1) * LANES]


def _experts(xbuf, block_e, n_used, w_gate, w_up, w_down):
    n_blocks = block_e.shape[0]
    d, dff = w_gate.shape[1], w_gate.shape[2]
    xmap = lambda b, be, nu: (jnp.minimum(b, nu[0] - 1), 0)
    wmap = lambda b, be, nu: (be[b], 0, 0)
    grid_spec = pltpu.PrefetchScalarGridSpec(
        num_scalar_prefetch=2,
        grid=(n_blocks,),
        in_specs=[
            pl.BlockSpec((MOE_BLOCK * SUBLANES, LANES), xmap),
            pl.BlockSpec((None, d, dff), wmap),
            pl.BlockSpec((None, d, dff), wmap),
            pl.BlockSpec((None, dff, d), wmap),
        ],
        out_specs=pl.BlockSpec((MOE_BLOCK * SUBLANES, LANES), xmap),
        scratch_shapes=[pltpu.VMEM((MOE_BLOCK, d), BF16)],
    )
    return pl.pallas_call(
        functools.partial(_experts_kernel, d=d),
        grid_spec=grid_spec,
        out_shape=jax.ShapeDtypeStruct(xbuf.shape, F32),
        input_output_aliases={2: 0},
        compiler_params=_cparams(("arbitrary",)),
        name="experts",
    )(block_e, n_used, xbuf, w_gate.astype(BF16), w_up.astype(BF16), w_down.astype(BF16))


def _combine_kernel(dest_ref, h_ref, gate_ref, fg_ref, ybuf_ref, o_ref, y_ref, sem, *, tm, d, final_norm):
    def issue(tk, carry):
        for s in range(MOE_TOP_K):
            pltpu.make_async_copy(_row_tile(ybuf_ref, dest_ref[0, s, tk]), _row_tile(y_ref.at[s], tk), sem).start()
        return carry

    lax.fori_loop(0, tm, issue, 0, unroll=8)
    for s in range(MOE_TOP_K):
        pltpu.make_async_copy(ybuf_ref.at[pl.ds(0, tm * SUBLANES), :], y_ref.at[s], sem).wait()

    g0 = gate_ref[:, 0:1]
    g1 = gate_ref[:, 1:2]
    cols = []
    for c in range(d // LANES):
        y0 = y_ref[0, pl.ds(c, tm, stride=SUBLANES), :]
        y1 = y_ref[1, pl.ds(c, tm, stride=SUBLANES), :]
        cols.append(h_ref[:, c * LANES:(c + 1) * LANES] + (y0 * g0 + y1 * g1))
    out = jnp.concatenate(cols, axis=1)
    if final_norm:
        out = _rms_unit(out) * fg_ref[...]
    o_ref[...] = out


def _combine(h, gate_t, dest3, ybuf, final_gain, final_norm, tm):
    t, d = h.shape
    nt = dest3.shape[0]
    return pl.pallas_call(
        functools.partial(_combine_kernel, tm=tm, d=d, final_norm=final_norm),
        grid=(nt,),
        in_specs=[
            pl.BlockSpec((1, MOE_TOP_K, tm), lambda i: (i, 0, 0), memory_space=pltpu.SMEM),
            pl.BlockSpec((tm, d), lambda i: (i, 0)),
            pl.BlockSpec((tm, MOE_TOP_K), lambda i: (i, 0)),
            pl.BlockSpec((1, d), lambda i: (0, 0)),
            pl.BlockSpec(memory_space=pl.ANY),
        ],
        out_specs=pl.BlockSpec((tm, d), lambda i: (i, 0)),
        out_shape=jax.ShapeDtypeStruct((t, d), F32),
        scratch_shapes=[pltpu.VMEM((MOE_TOP_K, tm * SUBLANES, LANES), F32), pltpu.SemaphoreType.DMA],
        compiler_params=_cparams(("arbitrary",)),
        name="combine",
    )(dest3, h, gate_t, final_gain.reshape(1, d), ybuf)


def _moe(xr, eidx, gate, rank, counts, h, w_gate, w_up, w_down, final_gain, final_norm, tm_row):
    t = h.shape[0]
    n_slot = t * MOE_TOP_K
    n_blocks = -(-n_slot // MOE_BLOCK) + MOE_EXPERTS
    cnt = counts[:, 0].astype(jnp.int32)
    padded = (cnt + MOE_BLOCK - 1) // MOE_BLOCK * MOE_BLOCK
    pad_end = jnp.cumsum(padded)
    pad_start = pad_end - padded
    dest = pad_start[eidx] + rank
    block_e = jnp.minimum(
        jnp.searchsorted(pad_end, jnp.arange(n_blocks, dtype=jnp.int32) * MOE_BLOCK, side='right'),
        MOE_EXPERTS - 1).astype(jnp.int32)
    n_used = (pad_end[-1:] // MOE_BLOCK).astype(jnp.int32)
    dest3 = dest.reshape(MOE_TOP_K, t // tm_row, tm_row).transpose(1, 0, 2)
    xbuf = _dispatch(xr, dest3, n_blocks * MOE_BLOCK, tm_row)
    ybuf = _experts(xbuf, block_e, n_used, w_gate, w_up, w_down)
    return _combine(h, gate.T, dest3, ybuf, final_gain, final_norm, tm_row)


def _fox_in_kernel(h_ref, gkv_ref, gq_ref, wkv_ref, wf_ref, bf_ref, wq_ref,
                   k_ref, v_ref, q_ref, og_ref, c_ref, carry_ref, *, tm, width, tiles_per_seq, qscale):
    i = pl.program_id(0)

    @pl.when(i % tiles_per_seq == 0)
    def _():
        carry_ref[...] = jnp.zeros_like(carry_ref)

    y = _rms_unit(h_ref[...])
    xkv = (y * gkv_ref[...]).astype(BF16)
    xq = (y * gq_ref[...]).astype(BF16)
    k_ref[...] = _dot(xkv, wkv_ref[:, 0:width]).astype(BF16)
    v_ref[...] = _dot(xkv, wkv_ref[:, width:2 * width]).astype(BF16)
    q_ref[...] = (_dot(xq, wq_ref[:, 0:width]) * qscale).astype(BF16)
    og_ref[...] = _dot(xq, wq_ref[:, width:2 * width]).astype(BF16)

    log_f = _log_sigmoid(_dot(xkv, wf_ref[...]) + bf_ref[...])
    ri = lax.broadcasted_iota(jnp.int32, (tm, tm), 0)
    ci = lax.broadcasted_iota(jnp.int32, (tm, tm), 1)
    ltri = jnp.where(ri >= ci, 1.0, 0.0).astype(BF16)
    f_hi, f_mid = _split_bf16(log_f)
    f_lo = (log_f - f_hi.astype(F32) - f_mid.astype(F32)).astype(BF16)
    c = carry_ref[0:1, :] + (_dot(ltri, f_hi) + (_dot(ltri, f_mid) + _dot(ltri, f_lo)))
    c_ref[...] = c
    carry_ref[0:1, :] = c[tm - 1:tm, :]


def _fox_in(h, kv_gain, q_gain, kv_w, kv_b_f, w_in, seq, tm):
    t, d = h.shape
    width = w_in.shape[1] // 2
    heads = kv_w.shape[1] - 2 * width
    const = lambda i: (0, 0)
    row = lambda i: (i, 0)
    big = jax.ShapeDtypeStruct((t, width), BF16)
    return pl.pallas_call(
        functools.partial(_fox_in_kernel, tm=tm, width=width, tiles_per_seq=seq // tm,
                          qscale=(width // heads) ** -0.5),
        grid=(t // tm,),
        in_specs=[
            pl.BlockSpec((tm, d), row),
            pl.BlockSpec((1, d), const),
            pl.BlockSpec((1, d), const),
            pl.BlockSpec((d, 2 * width), const),
            pl.BlockSpec((d, heads), const),
            pl.BlockSpec((1, heads), const),
            pl.BlockSpec((d, 2 * width), const),
        ],
        out_specs=[pl.BlockSpec((tm, width), row)] * 4 + [pl.BlockSpec((tm, heads), row)],
        out_shape=[big, big, big, big, jax.ShapeDtypeStruct((t, heads), F32)],
        scratch_shapes=[pltpu.VMEM((SUBLANES, heads), F32)],
        compiler_params=_cparams(("arbitrary",)),
        name="fox_in",
    )(h, kv_gain.reshape(1, d), q_gain.reshape(1, d), kv_w[:, :2 * width].astype(BF16),
      kv_w[:, 2 * width:].astype(BF16), kv_b_f.reshape(1, heads), w_in.astype(BF16))


def _attn_kernel(q_ref, k_ref, v_ref, og_ref, cc_ref, cr_ref, o_ref, *, tq, dh, hpb):
    qi = pl.program_id(2)
    hp = pl.program_id(1)
    heads = cc_ref.shape[1]
    lane = lax.broadcasted_iota(jnp.int32, (tq, heads), 1)
    ri = lax.broadcasted_iota(jnp.int32, (tq, tq), 0)
    ci = lax.broadcasted_iota(jnp.int32, (tq, tq), 1)
    causal = ri >= ci
    cc = cc_ref[...]

    for hh in range(hpb):
        hs = slice(hh * dh, (hh + 1) * dh)
        qh = q_ref[:, hs]
        ct = jnp.sum(jnp.where(lane == hp * hpb + hh, cc, 0.0), axis=1, keepdims=True)

        def logits(j):
            start = pl.multiple_of(j * tq, tq)
            s = _dot_nt(qh, k_ref[pl.ds(start, tq), hs])
            return s + (ct - cr_ref[0, 0, hh:hh + 1, pl.ds(start, tq)]), start

        def update(carry, s, start):
            m, l, acc = carry
            m_new = jnp.maximum(m, jnp.max(s, axis=1, keepdims=True))
            p = jnp.exp(s - m_new)
            alpha = jnp.exp(m - m_new)
            l = alpha * l + jnp.sum(p, axis=1, keepdims=True)
            acc = alpha * acc + _dot(p.astype(BF16), v_ref[pl.ds(start, tq), hs])
            return m_new, l, acc

        def body(j, carry):
            s, start = logits(j)
            return update(carry, s, start)

        init = (jnp.full((tq, 1), -jnp.inf, F32), jnp.zeros((tq, 1), F32), jnp.zeros((tq, dh), F32))
        carry = lax.fori_loop(0, qi, body, init)
        s, start = logits(qi)
        _, l, acc = update(carry, jnp.where(causal, s, -jnp.inf), start)
        og = og_ref[:, hs].astype(F32)
        o_ref[:, hs] = (acc / l * _sigmoid(og)).astype(BF16)


def _attn(q, k, v, og, c_col, c_row, batch, seq, tq):
    t, width = q.shape
    heads = c_col.shape[1]
    dh = width // heads
    hpb = LANES // dh
    nq = seq // tq
    qmap = lambda b, hp, i: (b * nq + i, hp)
    kmap = lambda b, hp, i: (b, hp)
    return pl.pallas_call(
        functools.partial(_attn_kernel, tq=tq, dh=dh, hpb=hpb),
        grid=(batch, heads // hpb, nq),
        in_specs=[
            pl.BlockSpec((tq, LANES), qmap),
            pl.BlockSpec((seq, LANES), kmap),
            pl.BlockSpec((seq, LANES), kmap),
            pl.BlockSpec((tq, LANES), qmap),
            pl.BlockSpec((tq, heads), lambda b, hp, i: (b * nq + i, 0)),
            pl.BlockSpec((1, 1, hpb, seq), lambda b, hp, i: (b, hp, 0, 0)),
        ],
        out_specs=pl.BlockSpec((tq, LANES), qmap),
        out_shape=jax.ShapeDtypeStruct((t, width), BF16),
        compiler_params=_cparams(("arbitrary", "arbitrary", "arbitrary")),
        name="attn",
    )(q, k, v, og, c_col, c_row)


def _tile(n, pref):
    while n % pref:
        pref //= 2
    return pref


def kernel(x, gla_norm, gla_w_in, gla_w_a2, gla_b_a, gla_out_norm, gla_w_out, kv_norm, kv_w, kv_b_f, fox_norm,
           fox_w_in, fox_w_out, ffn_norm, router_coarse_w, router_coarse_b, router_fine_w, router_fine_b,
           expert_w_gate, expert_w_up, expert_w_down, final_norm):
    batch, seq, d = x.shape
    t = batch * seq
    n_a = gla_norm.shape[0]
    depth = ffn_norm.shape[0]
    tm = _tile(seq, 512)
    tm_row = _tile(seq, 256)
    h = x.reshape(t, d)
    for layer in range(depth):
        if layer < n_a:
            i = layer
            q, k, v, r, la = _gla_in(h, gla_norm[i], gla_w_in[i], gla_w_a2[i], gla_b_a[i], tm)
            a = _gla_core(q, k, v, r, la, gla_out_norm[i], batch, seq, tm)
            w_out = gla_w_out[i]
        else:
            i = layer - n_a
            assert i == 0 and depth - n_a == 1, "one FoX layer per trunk"
            kk, vv, qq, og, c_col = _fox_in(h, kv_norm, fox_norm[i], kv_w, kv_b_f, fox_w_in[i], seq, tm)
            heads = c_col.shape[1]
            hpb = LANES // (kk.shape[1] // heads)
            c_row = c_col.reshape(batch, seq, heads // hpb, hpb).transpose(0, 2, 3, 1)
            a = _attn(qq, kk, vv, og, c_col, c_row, batch, seq, _tile(seq, 256))
            w_out = fox_w_out[i]
        h, xr, eidx, gate, rank, counts = _post(a, w_out, h, ffn_norm[layer], router_coarse_w[layer],
                                                router_coarse_b[layer], router_fine_w[layer],
                                                router_fine_b[layer], tm)
        last = layer == depth - 1
        h = _moe(xr, eidx, gate, rank, counts, h, expert_w_gate[layer], expert_w_up[layer], expert_w_down[layer],
                 final_norm, last, tm_row)
    return h.reshape(batch, seq, d)
```

```python
import functools

import jax
import jax.numpy as jnp
from jax import lax
from jax.experimental import pallas as pl
from jax.experimental.pallas import tpu as pltpu

F32 = jnp.float32
BF16 = jnp.bfloat16

RMS_EPS = 1e-6
GLA_HEADS = 4
GLA_GATE_RANK = 16
GLA_GATE_TAU = 16.0
FOX_HEADS = 16
MOE_GROUPS = 4
MOE_EXPERTS_PER_GROUP = 8
MOE_EXPERTS = MOE_GROUPS * MOE_EXPERTS_PER_GROUP
MOE_TOP_K = 2
MOE_BLOCK = 256

LANES = 128
SUBLANES = 8
VMEM_LIMIT = 56 * 1024 * 1024

GLA_CHUNK = 128
ROUTER_ROWS = 40


def _cparams(sem):
    return pltpu.CompilerParams(dimension_semantics=sem, vmem_limit_bytes=VMEM_LIMIT)


def _rms_unit(x):
    return x * lax.rsqrt(jnp.mean(x * x, axis=-1, keepdims=True) + RMS_EPS)


def _log_sigmoid(z):
    return jnp.minimum(z, 0.0) - jnp.log1p(jnp.exp(-jnp.abs(z)))


def _sigmoid(z):
    return 1.0 / (1.0 + jnp.exp(-z))


def _dot(a, b):
    return jnp.dot(a, b, preferred_element_type=F32)


def _dot_nt(a, b):
    return lax.dot_general(a, b, (((1,), (1,)), ((), ())), preferred_element_type=F32)


def _dot_tn(a, b):
    return lax.dot_general(a, b, (((0,), (0,)), ((), ())), preferred_element_type=F32)


def _split_bf16(x):
    hi = x.astype(BF16)
    lo = (x - hi.astype(F32)).astype(BF16)
    return hi, lo


def _gla_in_kernel(h_ref, g_ref, w_ref, wa1_ref, wa2_ref, ba_ref, q_ref, k_ref, v_ref, r_ref, la_ref, *, qk, vw):
    xn = (_rms_unit(h_ref[...]) * g_ref[...]).astype(BF16)
    q_ref[...] = _dot(xn, w_ref[:, 0:qk]).astype(BF16)
    k_ref[...] = _dot(xn, w_ref[:, qk:2 * qk]).astype(BF16)
    v_ref[...] = _dot(xn, w_ref[:, 2 * qk:2 * qk + vw]).astype(BF16)
    r_ref[...] = _dot(xn, w_ref[:, 2 * qk + vw:2 * qk + 2 * vw]).astype(BF16)
    a = _dot(xn, wa1_ref[...])
    z = _dot(a.astype(BF16), wa2_ref[...]) + ba_ref[...]
    la_ref[...] = _log_sigmoid(z) * (1.0 / GLA_GATE_TAU)


def _gla_in(h, gain, w_in, w_a2, b_a, tm):
    t, d = h.shape
    qk = w_a2.shape[1]
    vw = (w_in.shape[1] - 2 * qk - GLA_GATE_RANK) // 2
    w_main = w_in[:, :2 * qk + 2 * vw].astype(BF16)
    w_a1 = w_in[:, 2 * qk + 2 * vw:].astype(BF16)
    const = lambda i: (0, 0)
    row = lambda i: (i, 0)
    return pl.pallas_call(
        functools.partial(_gla_in_kernel, qk=qk, vw=vw),
        grid=(t // tm,),
        in_specs=[
            pl.BlockSpec((tm, d), row),
            pl.BlockSpec((1, d), const),
            pl.BlockSpec(w_main.shape, const),
            pl.BlockSpec(w_a1.shape, const),
            pl.BlockSpec(w_a2.shape, const),
            pl.BlockSpec((1, qk), const),
        ],
        out_specs=[
            pl.BlockSpec((tm, qk), row),
            pl.BlockSpec((tm, qk), row),
            pl.BlockSpec((tm, vw), row),
            pl.BlockSpec((tm, vw), row),
            pl.BlockSpec((tm, qk), row),
        ],
        out_shape=[
            jax.ShapeDtypeStruct((t, qk), BF16),
            jax.ShapeDtypeStruct((t, qk), BF16),
            jax.ShapeDtypeStruct((t, vw), BF16),
            jax.ShapeDtypeStruct((t, vw), BF16),
            jax.ShapeDtypeStruct((t, qk), F32),
        ],
        compiler_params=_cparams(("arbitrary",)),
        name="gla_in",
    )(h, gain.reshape(1, d), w_main, w_a1, w_a2.astype(BF16), b_a.reshape(1, qk))


def _gla_core_kernel(q_ref, k_ref, v_ref, r_ref, la_ref, gn_ref, o_ref, st_ref, *, n_chunks, scale):
    c_len = GLA_CHUNK

    @pl.when(pl.program_id(2) == 0)
    def _():
        st_ref[...] = jnp.zeros_like(st_ref)

    ri = lax.broadcasted_iota(jnp.int32, (c_len, c_len), 0)
    ci = lax.broadcasted_iota(jnp.int32, (c_len, c_len), 1)
    causal = ri >= ci
    ltri = jnp.where(causal, 1.0, 0.0).astype(BF16)
    gain = gn_ref[0]

    for c in range(n_chunks):
        sl = pl.ds(c * c_len, c_len)
        g_hi, g_lo = _split_bf16(la_ref[sl, :])
        cum = _dot(ltri, g_hi) + _dot(ltri, g_lo)
        last = cum[c_len - 1:c_len, :]
        mid = cum[c_len // 2 - 1:c_len // 2, :]
        qf = q_ref[sl, :].astype(F32) * scale
        kf = k_ref[sl, :].astype(F32)
        vb = v_ref[sl, :]
        qs = (qf * jnp.exp(cum - mid)).astype(BF16)
        ks = (kf * jnp.exp(mid - cum)).astype(BF16)
        scores = jnp.where(causal, _dot_nt(qs, ks), 0.0)
        o = _dot(scores.astype(BF16), vb)
        st = st_ref[...]
        qd = (qf * jnp.exp(cum)).astype(BF16)
        o = o + _dot_nt(qd, st.astype(BF16))
        kd = (kf * jnp.exp(last - cum)).astype(BF16)
        st_ref[...] = st * jnp.exp(last) + _dot_tn(vb, kd)
        on = _rms_unit(o) * gain
        rr = r_ref[sl, :].astype(F32)
        o_ref[sl, :] = (on * (rr * _sigmoid(rr))).astype(BF16)


def _gla_core(q, k, v, r, la, out_norm, batch, seq, lc):
    t, qk = q.shape
    vw = v.shape[1]
    heads = GLA_HEADS
    dk, dv = qk // heads, vw // heads
    nj = seq // lc
    tok = lambda b, h, j: (b * nj + j, h)
    return pl.pallas_call(
        functools.partial(_gla_core_kernel, n_chunks=lc // GLA_CHUNK, scale=dk ** -0.5),
        grid=(batch, heads, nj),
        in_specs=[
            pl.BlockSpec((lc, dk), tok),
            pl.BlockSpec((lc, dk), tok),
            pl.BlockSpec((lc, dv), tok),
            pl.BlockSpec((lc, dv), tok),
            pl.BlockSpec((lc, dk), tok),
            pl.BlockSpec((1, 1, dv), lambda b, h, j: (h, 0, 0)),
        ],
        out_specs=pl.BlockSpec((lc, dv), tok),
        out_shape=jax.ShapeDtypeStruct((t, vw), BF16),
        scratch_shapes=[pltpu.VMEM((dv, dk), F32)],
        compiler_params=_cparams(("arbitrary", "arbitrary", "arbitrary")),
        name="gla_core",
    )(q, k, v, r, la, out_norm.reshape(heads, 1, dv))


def _post_kernel(a_ref, w_ref, h_ref, g_ref, wr_ref, br_ref,
                 hn_ref, xr_ref, eidx_ref, gate_ref, rank_ref, cnt_ref, base_ref, *, tm, d):
    i = pl.program_id(0)

    @pl.when(i == 0)
    def _():
        base_ref[...] = jnp.zeros_like(base_ref)

    hn = h_ref[...] + _dot(a_ref[...], w_ref[...])
    hn_ref[...] = hn
    xn = _rms_unit(hn) * g_ref[...]
    for c in range(d // LANES):
        xr_ref[pl.ds(c, tm, stride=SUBLANES), :] = xn[:, c * LANES:(c + 1) * LANES]

    lg = lax.dot_general(wr_ref[...], xn, (((1,), (1,)), ((), ())), preferred_element_type=F32,
                         precision=lax.Precision.HIGHEST) + br_ref[...]
    lc = lg[0:MOE_GROUPS]
    m_c = jnp.max(lc, axis=0, keepdims=True)
    pc_top = 1.0 / jnp.sum(jnp.exp(lc - m_c), axis=0, keepdims=True)
    row_g = lax.broadcasted_iota(jnp.int32, lc.shape, 0)
    g_idx = jnp.min(jnp.where(lc == m_c, row_g, MOE_GROUPS), axis=0, keepdims=True)
    eg = MOE_EXPERTS_PER_GROUP
    lf = jnp.zeros((eg, tm), F32)
    for g in range(MOE_GROUPS):
        lf = lf + jnp.where(g_idx == g, lg[SUBLANES + g * eg:SUBLANES + (g + 1) * eg], 0.0)
    ef = jnp.exp(lf - jnp.max(lf, axis=0, keepdims=True))
    pf = ef / jnp.sum(ef, axis=0, keepdims=True)
    row_e = lax.broadcasted_iota(jnp.int32, pf.shape, 0)
    v1 = jnp.max(pf, axis=0, keepdims=True)
    j1 = jnp.min(jnp.where(pf == v1, row_e, eg), axis=0, keepdims=True)
    pf2 = jnp.where(row_e == j1, -1.0, pf)
    v2 = jnp.max(pf2, axis=0, keepdims=True)
    j2 = jnp.min(jnp.where(pf2 == v2, row_e, eg), axis=0, keepdims=True)
    denom = v1 + v2
    e1 = g_idx * eg + j1
    e2 = g_idx * eg + j2
    eidx_ref[0:1, :] = e1
    eidx_ref[1:2, :] = e2
    gate_ref[0:1, :] = pc_top * v1 / denom
    gate_ref[1:2, :] = pc_top * v2 / denom

    ti = lax.broadcasted_iota(jnp.int32, (tm, tm), 0)
    tj = lax.broadcasted_iota(jnp.int32, (tm, tm), 1)
    upper = jnp.where(ti <= tj, 1.0, 0.0).astype(BF16)
    row_x = lax.broadcasted_iota(jnp.int32, (MOE_EXPERTS, tm), 0)
    oh1 = row_x == e1
    oh2 = row_x == e2
    pre1 = _dot(jnp.where(oh1, 1.0, 0.0).astype(BF16), upper)
    pre2 = _dot(jnp.where(oh2, 1.0, 0.0).astype(BF16), upper)
    cnt1 = pre1[:, tm - 1:tm]
    cnt2 = pre2[:, tm - 1:tm]
    base = base_ref[:, 0:1]
    rank1 = jnp.sum(jnp.where(oh1, base + pre1 - 1.0, 0.0), axis=0, keepdims=True)
    rank2 = jnp.sum(jnp.where(oh2, base + cnt1 + pre2 - 1.0, 0.0), axis=0, keepdims=True)
    rank_ref[0:1, :] = rank1.astype(jnp.int32)
    rank_ref[1:2, :] = rank2.astype(jnp.int32)
    new_base = base_ref[...] + (cnt1 + cnt2)
    base_ref[...] = new_base
    cnt_ref[...] = new_base


def _post(a, w_out, h, gain, w_coarse, b_coarse, w_fine, b_fine, tm):
    t, d = h.shape
    wr = jnp.zeros((ROUTER_ROWS, d), F32)
    wr = wr.at[0:MOE_GROUPS].set(w_coarse.T).at[SUBLANES:].set(w_fine.T)
    br = jnp.zeros((ROUTER_ROWS, 1), F32)
    br = br.at[0:MOE_GROUPS, 0].set(b_coarse).at[SUBLANES:, 0].set(b_fine)
    const = lambda i: (0, 0)
    row = lambda i: (i, 0)
    col = lambda i: (0, i)
    return pl.pallas_call(
        functools.partial(_post_kernel, tm=tm, d=d),
        grid=(t // tm,),
        in_specs=[
            pl.BlockSpec((tm, a.shape[1]), row),
            pl.BlockSpec(w_out.shape, const),
            pl.BlockSpec((tm, d), row),
            pl.BlockSpec((1, d), const),
            pl.BlockSpec((ROUTER_ROWS, d), const),
            pl.BlockSpec((ROUTER_ROWS, 1), const),
        ],
        out_specs=[
            pl.BlockSpec((tm, d), row),
            pl.BlockSpec((tm * SUBLANES, LANES), row),
            pl.BlockSpec((MOE_TOP_K, tm), col),
            pl.BlockSpec((MOE_TOP_K, tm), col),
            pl.BlockSpec((MOE_TOP_K, tm), col),
            pl.BlockSpec((MOE_EXPERTS, LANES), const),
        ],
        out_shape=[
            jax.ShapeDtypeStruct((t, d), F32),
            jax.ShapeDtypeStruct((t * SUBLANES, LANES), F32),
            jax.ShapeDtypeStruct((MOE_TOP_K, t), jnp.int32),
            jax.ShapeDtypeStruct((MOE_TOP_K, t), F32),
            jax.ShapeDtypeStruct((MOE_TOP_K, t), jnp.int32),
            jax.ShapeDtypeStruct((MOE_EXPERTS, LANES), F32),
        ],
        scratch_shapes=[pltpu.VMEM((MOE_EXPERTS, LANES), F32)],
        compiler_params=_cparams(("arbitrary",)),
        name="post",
    )(a, w_out.astype(BF16), h, gain.reshape(1, d), wr, br)


def _row_tile(ref, idx):
    return ref.at[pl.ds(pl.multiple_of(idx * SUBLANES, SUBLANES), SUBLANES), :]


def _dispatch_kernel(dest_ref, xr_ref, zero_ref, xbuf_ref, sem, *, tm):
    del zero_ref

    def issue(tk, carry):
        for s in range(MOE_TOP_K):
            pltpu.make_async_copy(_row_tile(xr_ref, tk), _row_tile(xbuf_ref, dest_ref[0, s, tk]), sem).start()
        return carry

    lax.fori_loop(0, tm, issue, 0, unroll=8)
    for s in range(MOE_TOP_K):
        pltpu.make_async_copy(xr_ref, xbuf_ref.at[pl.ds(0, tm * SUBLANES), :], sem).wait()


def _dispatch(xr, dest3, n_rows, tm):
    nt = dest3.shape[0]
    zeros = jnp.zeros((n_rows * SUBLANES, LANES), F32)
    return pl.pallas_call(
        functools.partial(_dispatch_kernel, tm=tm),
        grid=(nt,),
        in_specs=[
            pl.BlockSpec((1, MOE_TOP_K, tm), lambda i: (i, 0, 0), memory_space=pltpu.SMEM),
            pl.BlockSpec((tm * SUBLANES, LANES), lambda i: (i, 0)),
            pl.BlockSpec(memory_space=pl.ANY),
        ],
        out_specs=pl.BlockSpec(memory_space=pl.ANY),
        out_shape=jax.ShapeDtypeStruct(zeros.shape, F32),
        scratch_shapes=[pltpu.SemaphoreType.DMA],
        input_output_aliases={2: 0},
        compiler_params=_cparams(("arbitrary",)),
        name="dispatch",
    )(dest3, xr, zeros)


def _experts_kernel(be_ref, nu_ref, x_ref, wg_ref, wu_ref, wd_ref, y_ref, x2_ref, wgb_ref, wub_ref, wdb_ref, *, d):
    rows = MOE_BLOCK
    b = pl.program_id(0)
    live = b < nu_ref[0]
    new_expert = jnp.logical_or(b == 0, be_ref[b] != be_ref[jnp.maximum(b - 1, 0)])

    @pl.when(jnp.logical_and(live, new_expert))
    def _():
        wgb_ref[...] = wg_ref[...].astype(BF16)
        wub_ref[...] = wu_ref[...].astype(BF16)
        wdb_ref[...] = wd_ref[...].astype(BF16)

    @pl.when(live)
    def _():
        for c in range(d // LANES):
            x2_ref[:, c * LANES:(c + 1) * LANES] = x_ref[pl.ds(c, rows, stride=SUBLANES), :].astype(BF16)
        x2 = x2_ref[...]
        gp = _dot(x2, wgb_ref[...])
        up = _dot(x2, wub_ref[...])
        hid = (gp * _sigmoid(gp) * up).astype(BF16)
        y = _dot(hid, wdb_ref[...])
        for c in range(d // LANES):
            y_ref[pl.ds(c, rows, stride=SUBLANES), :] = y[:, c * LANES:(c + 1) * LANES]


def _experts(xbuf, block_e, n_used, w_gate, w_up, w_down):
    n_blocks = block_e.shape[0]
    d, dff = w_gate.shape[1], w_gate.shape[2]
    xmap = lambda b, be, nu: (jnp.minimum(b, nu[0] - 1), 0)
    wmap = lambda b, be, nu: (be[b], 0, 0)
    grid_spec = pltpu.PrefetchScalarGridSpec(
        num_scalar_prefetch=2,
        grid=(n_blocks,),
        in_specs=[
            pl.BlockSpec((MOE_BLOCK * SUBLANES, LANES), xmap),
            pl.BlockSpec((None, d, dff), wmap),
            pl.BlockSpec((None, d, dff), wmap),
            pl.BlockSpec((None, dff, d), wmap),
        ],
        out_specs=pl.BlockSpec((MOE_BLOCK * SUBLANES, LANES), xmap),
        scratch_shapes=[pltpu.VMEM((MOE_BLOCK, d), BF16), pltpu.VMEM((d, dff), BF16), pltpu.VMEM((d, dff), BF16),
                        pltpu.VMEM((dff, d), BF16)],
    )
    return pl.pallas_call(
        functools.partial(_experts_kernel, d=d),
        grid_spec=grid_spec,
        out_shape=jax.ShapeDtypeStruct(xbuf.shape, F32),
        input_output_aliases={2: 0},
        compiler_params=_cparams(("arbitrary",)),
        name="experts",
    )(block_e, n_used, xbuf, w_gate, w_up, w_down)


def _combine_kernel(dest_ref, h_ref, gate_ref, fg_ref, ybuf_ref, o_ref, y_ref, sem, *, tm, d, final_norm):
    def issue(tk, carry):
        for s in range(MOE_TOP_K):
            pltpu.make_async_copy(_row_tile(ybuf_ref, dest_ref[0, s, tk]), _row_tile(y_ref.at[s], tk), sem).start()
        return carry

    lax.fori_loop(0, tm, issue, 0, unroll=8)
    for s in range(MOE_TOP_K):
        pltpu.make_async_copy(ybuf_ref.at[pl.ds(0, tm * SUBLANES), :], y_ref.at[s], sem).wait()

    g0 = gate_ref[:, 0:1]
    g1 = gate_ref[:, 1:2]
    cols = []
    for c in range(d // LANES):
        y0 = y_ref[0, pl.ds(c, tm, stride=SUBLANES), :]
        y1 = y_ref[1, pl.ds(c, tm, stride=SUBLANES), :]
        cols.append(h_ref[:, c * LANES:(c + 1) * LANES] + (y0 * g0 + y1 * g1))
    out = jnp.concatenate(cols, axis=1)
    if final_norm:
        out = _rms_unit(out) * fg_ref[...]
    o_ref[...] = out


def _combine(h, gate_t, dest3, ybuf, final_gain, final_norm, tm):
    t, d = h.shape
    nt = dest3.shape[0]
    return pl.pallas_call(
        functools.partial(_combine_kernel, tm=tm, d=d, final_norm=final_norm),
        grid=(nt,),
        in_specs=[
            pl.BlockSpec((1, MOE_TOP_K, tm), lambda i: (i, 0, 0), memory_space=pltpu.SMEM),
            pl.BlockSpec((tm, d), lambda i: (i, 0)),
            pl.BlockSpec((tm, MOE_TOP_K), lambda i: (i, 0)),
            pl.BlockSpec((1, d), lambda i: (0, 0)),
            pl.BlockSpec(memory_space=pl.ANY),
        ],
        out_specs=pl.BlockSpec((tm, d), lambda i: (i, 0)),
        out_shape=jax.ShapeDtypeStruct((t, d), F32),
        scratch_shapes=[pltpu.VMEM((MOE_TOP_K, tm * SUBLANES, LANES), F32), pltpu.SemaphoreType.DMA],
        compiler_params=_cparams(("arbitrary",)),
        name="combine",
    )(dest3, h, gate_t, final_gain.reshape(1, d), ybuf)


def _moe(xr, eidx, gate, rank, counts, h, w_gate, w_up, w_down, final_gain, final_norm, tm_row):
    t = h.shape[0]
    n_slot = t * MOE_TOP_K
    n_blocks = -(-n_slot // MOE_BLOCK) + MOE_EXPERTS
    cnt = counts[:, 0].astype(jnp.int32)
    padded = (cnt + MOE_BLOCK - 1) // MOE_BLOCK * MOE_BLOCK
    pad_end = jnp.cumsum(padded)
    pad_start = pad_end - padded
    dest = rank
    for e in range(MOE_EXPERTS):
        dest = dest + jnp.where(eidx == e, pad_start[e], 0)
    block_start = jnp.arange(n_blocks, dtype=jnp.int32) * MOE_BLOCK
    block_e = jnp.minimum(jnp.sum((pad_end[None, :] <= block_start[:, None]).astype(jnp.int32), axis=1),
                          MOE_EXPERTS - 1)
    n_used = (pad_end[-1:] // MOE_BLOCK).astype(jnp.int32)
    dest3 = dest.reshape(MOE_TOP_K, t // tm_row, tm_row).transpose(1, 0, 2)
    xbuf = _dispatch(xr, dest3, n_blocks * MOE_BLOCK, tm_row)
    ybuf = _experts(xbuf, block_e, n_used, w_gate, w_up, w_down)
    return _combine(h, gate.T, dest3, ybuf, final_gain, final_norm, tm_row)


def _fox_in_kernel(h_ref, gkv_ref, gq_ref, wkv_ref, wf_ref, bf_ref, wq_ref,
                   k_ref, v_ref, q_ref, og_ref, c_ref, carry_ref, *, tm, width, tiles_per_seq, qscale):
    i = pl.program_id(0)

    @pl.when(i % tiles_per_seq == 0)
    def _():
        carry_ref[...] = jnp.zeros_like(carry_ref)

    y = _rms_unit(h_ref[...])
    xkv = (y * gkv_ref[...]).astype(BF16)
    xq = (y * gq_ref[...]).astype(BF16)
    k_ref[...] = _dot(xkv, wkv_ref[:, 0:width]).astype(BF16)
    v_ref[...] = _dot(xkv, wkv_ref[:, width:2 * width]).astype(BF16)
    q_ref[...] = (_dot(xq, wq_ref[:, 0:width]) * qscale).astype(BF16)
    og_ref[...] = _dot(xq, wq_ref[:, width:2 * width]).astype(BF16)

    log_f = _log_sigmoid(_dot(xkv, wf_ref[...]) + bf_ref[...])
    ri = lax.broadcasted_iota(jnp.int32, (tm, tm), 0)
    ci = lax.broadcasted_iota(jnp.int32, (tm, tm), 1)
    ltri = jnp.where(ri >= ci, 1.0, 0.0).astype(BF16)
    f_hi, f_mid = _split_bf16(log_f)
    f_lo = (log_f - f_hi.astype(F32) - f_mid.astype(F32)).astype(BF16)
    c = carry_ref[0:1, :] + (_dot(ltri, f_hi) + (_dot(ltri, f_mid) + _dot(ltri, f_lo)))
    c_ref[...] = c
    carry_ref[0:1, :] = c[tm - 1:tm, :]


def _fox_in(h, kv_gain, q_gain, kv_w, kv_b_f, w_in, seq, tm):
    t, d = h.shape
    width = w_in.shape[1] // 2
    heads = kv_w.shape[1] - 2 * width
    const = lambda i: (0, 0)
    row = lambda i: (i, 0)
    big = jax.ShapeDtypeStruct((t, width), BF16)
    return pl.pallas_call(
        functools.partial(_fox_in_kernel, tm=tm, width=width, tiles_per_seq=seq // tm,
                          qscale=(width // heads) ** -0.5),
        grid=(t // tm,),
        in_specs=[
            pl.BlockSpec((tm, d), row),
            pl.BlockSpec((1, d), const),
            pl.BlockSpec((1, d), const),
            pl.BlockSpec((d, 2 * width), const),
            pl.BlockSpec((d, heads), const),
            pl.BlockSpec((1, heads), const),
            pl.BlockSpec((d, 2 * width), const),
        ],
        out_specs=[pl.BlockSpec((tm, width), row)] * 4 + [pl.BlockSpec((tm, heads), row)],
        out_shape=[big, big, big, big, jax.ShapeDtypeStruct((t, heads), F32)],
        scratch_shapes=[pltpu.VMEM((SUBLANES, heads), F32)],
        compiler_params=_cparams(("arbitrary",)),
        name="fox_in",
    )(h, kv_gain.reshape(1, d), q_gain.reshape(1, d), kv_w[:, :2 * width].astype(BF16),
      kv_w[:, 2 * width:].astype(BF16), kv_b_f.reshape(1, heads), w_in.astype(BF16))


ATTN_KV_CHUNK = 512


def _attn_kernel(q_ref, k_ref, v_ref, og_ref, cc_ref, o_ref, qa_ref, ka_ref, va_ref, s_ref, *, seq, tq, dh):
    hp = pl.program_id(1)
    heads = cc_ref.shape[1]
    lane = lax.broadcasted_iota(jnp.int32, (seq, LANES), 1)
    lane_c = lax.broadcasted_iota(jnp.int32, (seq, heads), 1)
    cc = cc_ref[...]
    for hh in range(2):
        c = jnp.sum(jnp.where(lane_c == hp * 2 + hh, cc, 0.0), axis=1, keepdims=True)
        c_hi = c.astype(BF16).astype(F32)
        c_mid = (c - c_hi).astype(BF16).astype(F32)
        c_lo = (c - c_hi - c_mid).astype(BF16).astype(F32)
        a0 = dh if hh == 0 else 0
        own = (lane < dh) if hh == 0 else (lane >= dh)
        q_aug = jnp.where(lane == a0, c_hi, jnp.where(lane == a0 + 1, c_mid, jnp.where(lane == a0 + 2, c_lo,
                jnp.where((lane >= a0 + 3) & (lane < a0 + 6), 1.0, 0.0))))
        k_aug = jnp.where(lane == a0 + 3, -c_hi, jnp.where(lane == a0 + 4, -c_mid, jnp.where(lane == a0 + 5, -c_lo,
                jnp.where((lane >= a0) & (lane < a0 + 3), 1.0, 0.0))))
        qa_ref[hh] = jnp.where(own, q_ref[...], q_aug.astype(BF16))
        ka_ref[hh] = jnp.where(own, k_ref[...], k_aug.astype(BF16))
        va_ref[hh] = jnp.where(own, v_ref[...], jnp.where(lane == a0, 1.0, 0.0).astype(BF16))

    ri = lax.broadcasted_iota(jnp.int32, (tq, tq), 0)
    ci = lax.broadcasted_iota(jnp.int32, (tq, tq), 1)
    causal = ri >= ci
    lane_q = lax.broadcasted_iota(jnp.int32, (tq, LANES), 1)

    def fold_max(mx, s):
        for g in range(s.shape[1] // LANES):
            mx = jnp.maximum(mx, s[:, g * LANES:(g + 1) * LANES])
        return mx

    for qi in range(seq // tq):
        rows = slice(qi * tq, (qi + 1) * tq)
        past = [(c0, min(c0 + ATTN_KV_CHUNK, qi * tq)) for c0 in range(0, qi * tq, ATTN_KV_CHUNK)]
        outs = []
        for hh in range(2):
            qa = qa_ref[hh, rows, :]
            mx = jnp.full((tq, LANES), -jnp.inf, F32)
            for c0, c1 in past:
                s = _dot_nt(qa, ka_ref[hh, c0:c1, :])
                s_ref[hh, :, c0:c1] = s
                mx = fold_max(mx, s)
            s = jnp.where(causal, _dot_nt(qa, ka_ref[hh, rows, :]), -jnp.inf)
            s_ref[hh, :, rows] = s
            mx = fold_max(mx, s)
            m = jnp.max(mx, axis=1, keepdims=True)
            acc = jnp.zeros((tq, LANES), F32)
            for c0, c1 in past + [(qi * tq, (qi + 1) * tq)]:
                p = jnp.exp(s_ref[hh, :, c0:c1] - m).astype(BF16)
                acc = acc + _dot(p, va_ref[hh, c0:c1, :])
            a0 = dh if hh == 0 else 0
            outs.append(acc / acc[:, a0:a0 + 1])
        o = jnp.where(lane_q < dh, outs[0], outs[1])
        o_ref[rows, :] = (o * _sigmoid(og_ref[rows, :].astype(F32))).astype(BF16)


def _attn(q, k, v, og, c_col, batch, seq, tq):
    t, width = q.shape
    heads = c_col.shape[1]
    dh = width // heads
    assert 2 * dh == LANES, "two heads per 128-lane block"
    blk = pl.BlockSpec((seq, LANES), lambda b, hp: (b, hp))
    return pl.pallas_call(
        functools.partial(_attn_kernel, seq=seq, tq=tq, dh=dh),
        grid=(batch, heads // 2),
        in_specs=[blk, blk, blk, blk, pl.BlockSpec((seq, heads), lambda b, hp: (b, 0))],
        out_specs=blk,
        out_shape=jax.ShapeDtypeStruct((t, width), BF16),
        scratch_shapes=[pltpu.VMEM((2, seq, LANES), BF16)] * 3 + [pltpu.VMEM((2, tq, seq), F32)],
        compiler_params=_cparams(("arbitrary", "arbitrary")),
        name="attn",
    )(q, k, v, og, c_col)


def _tile(n, pref):
    while n % pref:
        pref //= 2
    return pref


def kernel(x, gla_norm, gla_w_in, gla_w_a2, gla_b_a, gla_out_norm, gla_w_out, kv_norm, kv_w, kv_b_f, fox_norm,
           fox_w_in, fox_w_out, ffn_norm, router_coarse_w, router_coarse_b, router_fine_w, router_fine_b,
           expert_w_gate, expert_w_up, expert_w_down, final_norm):
    batch, seq, d = x.shape
    t = batch * seq
    n_a = gla_norm.shape[0]
    depth = ffn_norm.shape[0]
    tm = _tile(seq, 512)
    tm_row = _tile(seq, 256)
    h = x.reshape(t, d)
    for layer in range(depth):
        if layer < n_a:
            i = layer
            q, k, v, r, la = _gla_in(h, gla_norm[i], gla_w_in[i], gla_w_a2[i], gla_b_a[i], tm)
            a = _gla_core(q, k, v, r, la, gla_out_norm[i], batch, seq, tm)
            w_out = gla_w_out[i]
        else:
            i = layer - n_a
            assert i == 0 and depth - n_a == 1, "one FoX layer per trunk"
            kk, vv, qq, og, c_col = _fox_in(h, kv_norm, fox_norm[i], kv_w, kv_b_f, fox_w_in[i], seq, tm)
            a = _attn(qq, kk, vv, og, c_col, batch, seq, _tile(seq, 256))
            w_out = fox_w_out[i]
        h, xr, eidx, gate, rank, counts = _post(a, w_out, h, ffn_norm[layer], router_coarse_w[layer],
                                                router_coarse_b[layer], router_fine_w[layer],
                                                router_fine_b[layer], tm)
        last = layer == depth - 1
        h = _moe(xr, eidx, gate, rank, counts, h, expert_w_gate[layer], expert_w_up[layer], expert_w_down[layer],
                 final_norm, last, tm_row)
    return h.reshape(batch, seq, d)
```

```python
import functools

import jax
import jax.numpy as jnp
from jax import lax
from jax.experimental import pallas as pl
from jax.experimental.pallas import tpu as pltpu

F32 = jnp.float32
BF16 = jnp.bfloat16

RMS_EPS = 1e-6
GLA_HEADS = 4
GLA_GATE_RANK = 16
GLA_GATE_TAU = 16.0
FOX_HEADS = 16
MOE_GROUPS = 4
MOE_EXPERTS_PER_GROUP = 8
MOE_EXPERTS = MOE_GROUPS * MOE_EXPERTS_PER_GROUP
MOE_TOP_K = 2
MOE_BLOCK = 256

LANES = 128
SUBLANES = 8
VMEM_LIMIT = 56 * 1024 * 1024

GLA_CHUNK = 128
ROUTER_ROWS = 40


def _cparams(sem):
    return pltpu.CompilerParams(dimension_semantics=sem, vmem_limit_bytes=VMEM_LIMIT)


def _rms_unit(x):
    return x * lax.rsqrt(jnp.mean(x * x, axis=-1, keepdims=True) + RMS_EPS)


def _log_sigmoid(z):
    return jnp.minimum(z, 0.0) - jnp.log1p(jnp.exp(-jnp.abs(z)))


def _sigmoid(z):
    return 1.0 / (1.0 + jnp.exp(-z))


def _dot(a, b):
    return jnp.dot(a, b, preferred_element_type=F32)


def _dot_nt(a, b):
    return lax.dot_general(a, b, (((1,), (1,)), ((), ())), preferred_element_type=F32)


def _dot_tn(a, b):
    return lax.dot_general(a, b, (((0,), (0,)), ((), ())), preferred_element_type=F32)


def _split_bf16(x):
    hi = x.astype(BF16)
    lo = (x - hi.astype(F32)).astype(BF16)
    return hi, lo


def _gla_in_kernel(h_ref, g_ref, w_ref, wa1_ref, wa2_ref, ba_ref, q_ref, k_ref, v_ref, r_ref, la_ref, *, qk, vw):
    xn = (_rms_unit(h_ref[...]) * g_ref[...]).astype(BF16)
    q_ref[...] = _dot(xn, w_ref[:, 0:qk]).astype(BF16)
    k_ref[...] = _dot(xn, w_ref[:, qk:2 * qk]).astype(BF16)
    v_ref[...] = _dot(xn, w_ref[:, 2 * qk:2 * qk + vw]).astype(BF16)
    r_ref[...] = _dot(xn, w_ref[:, 2 * qk + vw:2 * qk + 2 * vw]).astype(BF16)
    a = _dot(xn, wa1_ref[...])
    z = _dot(a.astype(BF16), wa2_ref[...]) + ba_ref[...]
    la_ref[...] = _log_sigmoid(z) * (1.0 / GLA_GATE_TAU)


def _gla_in(h, gain, w_in, w_a2, b_a, tm):
    t, d = h.shape
    qk = w_a2.shape[1]
    vw = (w_in.shape[1] - 2 * qk - GLA_GATE_RANK) // 2
    w_main = w_in[:, :2 * qk + 2 * vw].astype(BF16)
    w_a1 = w_in[:, 2 * qk + 2 * vw:].astype(BF16)
    const = lambda i: (0, 0)
    row = lambda i: (i, 0)
    return pl.pallas_call(
        functools.partial(_gla_in_kernel, qk=qk, vw=vw),
        grid=(t // tm,),
        in_specs=[
            pl.BlockSpec((tm, d), row),
            pl.BlockSpec((1, d), const),
            pl.BlockSpec(w_main.shape, const),
            pl.BlockSpec(w_a1.shape, const),
            pl.BlockSpec(w_a2.shape, const),
            pl.BlockSpec((1, qk), const),
        ],
        out_specs=[
            pl.BlockSpec((tm, qk), row),
            pl.BlockSpec((tm, qk), row),
            pl.BlockSpec((tm, vw), row),
            pl.BlockSpec((tm, vw), row),
            pl.BlockSpec((tm, qk), row),
        ],
        out_shape=[
            jax.ShapeDtypeStruct((t, qk), BF16),
            jax.ShapeDtypeStruct((t, qk), BF16),
            jax.ShapeDtypeStruct((t, vw), BF16),
            jax.ShapeDtypeStruct((t, vw), BF16),
            jax.ShapeDtypeStruct((t, qk), F32),
        ],
        compiler_params=_cparams(("arbitrary",)),
        name="gla_in",
    )(h, gain.reshape(1, d), w_main, w_a1, w_a2.astype(BF16), b_a.reshape(1, qk))


def _gla_core_kernel(q_ref, k_ref, v_ref, r_ref, la_ref, gn_ref, o_ref, st_ref, *, n_chunks, scale):
    c_len = GLA_CHUNK

    @pl.when(pl.program_id(2) == 0)
    def _():
        st_ref[...] = jnp.zeros_like(st_ref)

    ri = lax.broadcasted_iota(jnp.int32, (c_len, c_len), 0)
    ci = lax.broadcasted_iota(jnp.int32, (c_len, c_len), 1)
    causal = ri >= ci
    ltri = jnp.where(causal, 1.0, 0.0).astype(BF16)
    gain = gn_ref[0]

    for c in range(n_chunks):
        sl = pl.ds(c * c_len, c_len)
        g_hi, g_lo = _split_bf16(la_ref[sl, :])
        cum = _dot(ltri, g_hi) + _dot(ltri, g_lo)
        last = cum[c_len - 1:c_len, :]
        mid = cum[c_len // 2 - 1:c_len // 2, :]
        qf = q_ref[sl, :].astype(F32) * scale
        kf = k_ref[sl, :].astype(F32)
        vb = v_ref[sl, :]
        qs = (qf * jnp.exp(cum - mid)).astype(BF16)
        ks = (kf * jnp.exp(mid - cum)).astype(BF16)
        scores = jnp.where(causal, _dot_nt(qs, ks), 0.0)
        o = _dot(scores.astype(BF16), vb)
        st = st_ref[...]
        qd = (qf * jnp.exp(cum)).astype(BF16)
        o = o + _dot_nt(qd, st.astype(BF16))
        kd = (kf * jnp.exp(last - cum)).astype(BF16)
        st_ref[...] = st * jnp.exp(last) + _dot_tn(vb, kd)
        on = _rms_unit(o) * gain
        rr = r_ref[sl, :].astype(F32)
        o_ref[sl, :] = (on * (rr * _sigmoid(rr))).astype(BF16)


def _gla_core(q, k, v, r, la, out_norm, batch, seq, lc):
    t, qk = q.shape
    vw = v.shape[1]
    heads = GLA_HEADS
    dk, dv = qk // heads, vw // heads
    nj = seq // lc
    tok = lambda b, h, j: (b * nj + j, h)
    return pl.pallas_call(
        functools.partial(_gla_core_kernel, n_chunks=lc // GLA_CHUNK, scale=dk ** -0.5),
        grid=(batch, heads, nj),
        in_specs=[
            pl.BlockSpec((lc, dk), tok),
            pl.BlockSpec((lc, dk), tok),
            pl.BlockSpec((lc, dv), tok),
            pl.BlockSpec((lc, dv), tok),
            pl.BlockSpec((lc, dk), tok),
            pl.BlockSpec((1, 1, dv), lambda b, h, j: (h, 0, 0)),
        ],
        out_specs=pl.BlockSpec((lc, dv), tok),
        out_shape=jax.ShapeDtypeStruct((t, vw), BF16),
        scratch_shapes=[pltpu.VMEM((dv, dk), F32)],
        compiler_params=_cparams(("arbitrary", "arbitrary", "arbitrary")),
        name="gla_core",
    )(q, k, v, r, la, out_norm.reshape(heads, 1, dv))


def _post_kernel(a_ref, w_ref, h_ref, g_ref, wr_ref, br_ref,
                 hn_ref, xr_ref, eidx_ref, gate_ref, rank_ref, cnt_ref, base_ref, *, tm, sub, d):
    @pl.when(pl.program_id(0) == 0)
    def _():
        base_ref[...] = jnp.zeros_like(base_ref)

    eg = MOE_EXPERTS_PER_GROUP
    ti = lax.broadcasted_iota(jnp.int32, (sub, sub), 0)
    tj = lax.broadcasted_iota(jnp.int32, (sub, sub), 1)
    upper = jnp.where(ti <= tj, 1.0, 0.0).astype(BF16)
    row_g = lax.broadcasted_iota(jnp.int32, (MOE_GROUPS, sub), 0)
    row_e = lax.broadcasted_iota(jnp.int32, (eg, sub), 0)
    row_x = lax.broadcasted_iota(jnp.int32, (MOE_EXPERTS, sub), 0)
    base = base_ref[:, 0:1]

    for u in range(tm // sub):
        rows = slice(u * sub, (u + 1) * sub)
        hn = h_ref[rows, :] + _dot(a_ref[rows, :], w_ref[...])
        hn_ref[rows, :] = hn
        xn = _rms_unit(hn) * g_ref[...]
        for c in range(d // LANES):
            xr_ref[pl.ds(u * sub * SUBLANES + c, sub, stride=SUBLANES), :] = xn[:, c * LANES:(c + 1) * LANES]

        x_hi, x_lo = _split_bf16(xn)
        r_hi = _dot(x_hi, wr_ref[...])
        r_lo = _dot(x_lo, wr_ref[...])
        lg_t = (r_hi[:, 0:LANES] + r_hi[:, LANES:]) + (r_lo[:, 0:LANES] + r_lo[:, LANES:])
        lg = lg_t.T[0:ROUTER_ROWS] + br_ref[...]
        lc = lg[0:MOE_GROUPS]
        m_c = jnp.max(lc, axis=0, keepdims=True)
        pc_top = 1.0 / jnp.sum(jnp.exp(lc - m_c), axis=0, keepdims=True)
        g_idx = jnp.min(jnp.where(lc == m_c, row_g, MOE_GROUPS), axis=0, keepdims=True)
        lf = jnp.zeros((eg, sub), F32)
        for g in range(MOE_GROUPS):
            lf = lf + jnp.where(g_idx == g, lg[SUBLANES + g * eg:SUBLANES + (g + 1) * eg], 0.0)
        ef = jnp.exp(lf - jnp.max(lf, axis=0, keepdims=True))
        pf = ef / jnp.sum(ef, axis=0, keepdims=True)
        v1 = jnp.max(pf, axis=0, keepdims=True)
        j1 = jnp.min(jnp.where(pf == v1, row_e, eg), axis=0, keepdims=True)
        pf2 = jnp.where(row_e == j1, -1.0, pf)
        v2 = jnp.max(pf2, axis=0, keepdims=True)
        j2 = jnp.min(jnp.where(pf2 == v2, row_e, eg), axis=0, keepdims=True)
        denom = v1 + v2
        e1 = g_idx * eg + j1
        e2 = g_idx * eg + j2
        eidx_ref[0:1, rows] = e1
        eidx_ref[1:2, rows] = e2
        gate_ref[0:1, rows] = pc_top * v1 / denom
        gate_ref[1:2, rows] = pc_top * v2 / denom

        oh1 = row_x == e1
        oh2 = row_x == e2
        pre1 = _dot(jnp.where(oh1, 1.0, 0.0).astype(BF16), upper)
        pre2 = _dot(jnp.where(oh2, 1.0, 0.0).astype(BF16), upper)
        cnt1 = pre1[:, sub - 1:sub]
        cnt2 = pre2[:, sub - 1:sub]
        rank1 = jnp.sum(jnp.where(oh1, base + pre1 - 1.0, 0.0), axis=0, keepdims=True)
        rank2 = jnp.sum(jnp.where(oh2, base + cnt1 + pre2 - 1.0, 0.0), axis=0, keepdims=True)
        rank_ref[0:1, rows] = rank1.astype(jnp.int32)
        rank_ref[1:2, rows] = rank2.astype(jnp.int32)
        base = base + (cnt1 + cnt2)

    new_base = jnp.broadcast_to(base, base_ref.shape)
    base_ref[...] = new_base
    cnt_ref[...] = new_base


def _post(a, w_out, h, gain, w_coarse, b_coarse, w_fine, b_fine, tm, sub):
    t, d = h.shape
    wr = jnp.concatenate([w_coarse, jnp.zeros((d, SUBLANES - MOE_GROUPS), F32), w_fine,
                          jnp.zeros((d, LANES - ROUTER_ROWS), F32)], axis=1)
    wr = jnp.concatenate(_split_bf16(wr), axis=1)
    br = jnp.zeros((ROUTER_ROWS, 1), F32)
    br = br.at[0:MOE_GROUPS, 0].set(b_coarse).at[SUBLANES:, 0].set(b_fine)
    const = lambda i: (0, 0)
    row = lambda i: (i, 0)
    col = lambda i: (0, i)
    return pl.pallas_call(
        functools.partial(_post_kernel, tm=tm, sub=sub, d=d),
        grid=(t // tm,),
        in_specs=[
            pl.BlockSpec((tm, a.shape[1]), row),
            pl.BlockSpec(w_out.shape, const),
            pl.BlockSpec((tm, d), row),
            pl.BlockSpec((1, d), const),
            pl.BlockSpec((d, 2 * LANES), const),
            pl.BlockSpec((ROUTER_ROWS, 1), const),
        ],
        out_specs=[
            pl.BlockSpec((tm, d), row),
            pl.BlockSpec((tm * SUBLANES, LANES), row),
            pl.BlockSpec((MOE_TOP_K, tm), col),
            pl.BlockSpec((MOE_TOP_K, tm), col),
            pl.BlockSpec((MOE_TOP_K, tm), col),
            pl.BlockSpec((MOE_EXPERTS, LANES), const),
        ],
        out_shape=[
            jax.ShapeDtypeStruct((t, d), F32),
            jax.ShapeDtypeStruct((t * SUBLANES, LANES), F32),
            jax.ShapeDtypeStruct((MOE_TOP_K, t), jnp.int32),
            jax.ShapeDtypeStruct((MOE_TOP_K, t), F32),
            jax.ShapeDtypeStruct((MOE_TOP_K, t), jnp.int32),
            jax.ShapeDtypeStruct((MOE_EXPERTS, LANES), F32),
        ],
        scratch_shapes=[pltpu.VMEM((MOE_EXPERTS, LANES), F32)],
        compiler_params=_cparams(("arbitrary",)),
        name="post",
    )(a, w_out.astype(BF16), h, gain.reshape(1, d), wr, br)


def _row_tile(ref, idx):
    return ref.at[pl.ds(pl.multiple_of(idx * SUBLANES, SUBLANES), SUBLANES), :]


def _block_rows(ref, blk):
    return ref.at[pl.ds(pl.multiple_of(blk * (MOE_BLOCK * SUBLANES), MOE_BLOCK * SUBLANES), MOE_BLOCK * SUBLANES), :]


def _dispatch_kernel(pe_ref, nu_ref, dest_ref, xr_ref, xbuf_ref, zero_ref, sem, zsem, *, tm, n_blocks):
    @pl.when(pl.program_id(0) == 0)
    def _():
        zero_ref[...] = jnp.zeros_like(zero_ref)

        def last_block(e):
            return pe_ref[e] // MOE_BLOCK - 1

        def has_rows(e):
            return pe_ref[e] > jnp.where(e == 0, 0, pe_ref[jnp.maximum(e - 1, 0)])

        def zero_block(blk):
            return pltpu.make_async_copy(zero_ref, _block_rows(xbuf_ref, blk), zsem)

        def start_e(e, carry):
            @pl.when(has_rows(e))
            def _():
                zero_block(last_block(e)).start()
            return carry

        def wait_e(e, carry):
            @pl.when(has_rows(e))
            def _():
                zero_block(last_block(e)).wait()
            return carry

        def start_b(blk, carry):
            zero_block(blk).start()
            return carry

        def wait_b(blk, carry):
            zero_block(blk).wait()
            return carry

        lax.fori_loop(0, MOE_EXPERTS, start_e, 0)
        lax.fori_loop(nu_ref[0], n_blocks, start_b, 0)
        lax.fori_loop(0, MOE_EXPERTS, wait_e, 0)
        lax.fori_loop(nu_ref[0], n_blocks, wait_b, 0)

    def issue(tk, carry):
        for s in range(MOE_TOP_K):
            pltpu.make_async_copy(_row_tile(xr_ref, tk), _row_tile(xbuf_ref, dest_ref[0, s, tk]), sem).start(priority=s)
        return carry

    lax.fori_loop(0, tm, issue, 0, unroll=8)
    for s in range(MOE_TOP_K):
        pltpu.make_async_copy(xr_ref, xbuf_ref.at[pl.ds(0, tm * SUBLANES), :], sem).wait()


def _dispatch(xr, dest3, pad_end, n_used, n_blocks, tm):
    nt = dest3.shape[0]
    grid_spec = pltpu.PrefetchScalarGridSpec(
        num_scalar_prefetch=2,
        grid=(nt,),
        in_specs=[
            pl.BlockSpec((1, MOE_TOP_K, tm), lambda i, pe, nu: (i, 0, 0), memory_space=pltpu.SMEM),
            pl.BlockSpec((tm * SUBLANES, LANES), lambda i, pe, nu: (i, 0)),
        ],
        out_specs=pl.BlockSpec(memory_space=pl.ANY),
        scratch_shapes=[pltpu.VMEM((MOE_BLOCK * SUBLANES, LANES), F32), pltpu.SemaphoreType.DMA,
                        pltpu.SemaphoreType.DMA],
    )
    return pl.pallas_call(
        functools.partial(_dispatch_kernel, tm=tm, n_blocks=n_blocks),
        grid_spec=grid_spec,
        out_shape=jax.ShapeDtypeStruct((n_blocks * MOE_BLOCK * SUBLANES, LANES), F32),
        compiler_params=_cparams(("arbitrary",)),
        name="dispatch",
    )(pad_end, n_used, dest3, xr)


def _experts_kernel(be_ref, nu_ref, x_ref, wg_ref, wu_ref, wd_ref, y_ref, x2_ref, wgb_ref, wub_ref, wdb_ref, *, d):
    rows = MOE_BLOCK
    b = pl.program_id(0)
    live = b < nu_ref[0]
    new_expert = jnp.logical_or(b == 0, be_ref[b] != be_ref[jnp.maximum(b - 1, 0)])

    @pl.when(jnp.logical_and(live, new_expert))
    def _():
        wgb_ref[...] = wg_ref[...].astype(BF16)
        wub_ref[...] = wu_ref[...].astype(BF16)
        wdb_ref[...] = wd_ref[...].astype(BF16)

    @pl.when(live)
    def _():
        for c in range(d // LANES):
            x2_ref[:, c * LANES:(c + 1) * LANES] = x_ref[pl.ds(c, rows, stride=SUBLANES), :].astype(BF16)
        x2 = x2_ref[...]
        gp = _dot(x2, wgb_ref[...])
        up = _dot(x2, wub_ref[...])
        hid = (gp * _sigmoid(gp) * up).astype(BF16)
        y = _dot(hid, wdb_ref[...])
        for c in range(d // LANES):
            y_ref[pl.ds(c, rows, stride=SUBLANES), :] = y[:, c * LANES:(c + 1) * LANES]


def _experts(xbuf, block_e, n_used, w_gate, w_up, w_down, layer):
    n_blocks = block_e.shape[0]
    d, dff = w_gate.shape[2], w_gate.shape[3]
    xmap = lambda b, be, nu: (jnp.minimum(b, nu[0] - 1), 0)
    wmap = lambda b, be, nu: (layer, be[b], 0, 0)
    grid_spec = pltpu.PrefetchScalarGridSpec(
        num_scalar_prefetch=2,
        grid=(n_blocks,),
        in_specs=[
            pl.BlockSpec((MOE_BLOCK * SUBLANES, LANES), xmap),
            pl.BlockSpec((None, None, d, dff), wmap),
            pl.BlockSpec((None, None, d, dff), wmap),
            pl.BlockSpec((None, None, dff, d), wmap),
        ],
        out_specs=pl.BlockSpec((MOE_BLOCK * SUBLANES, LANES), xmap),
        scratch_shapes=[pltpu.VMEM((MOE_BLOCK, d), BF16), pltpu.VMEM((d, dff), BF16), pltpu.VMEM((d, dff), BF16),
                        pltpu.VMEM((dff, d), BF16)],
    )
    return pl.pallas_call(
        functools.partial(_experts_kernel, d=d),
        grid_spec=grid_spec,
        out_shape=jax.ShapeDtypeStruct(xbuf.shape, F32),
        input_output_aliases={2: 0},
        compiler_params=_cparams(("arbitrary",)),
        name="experts",
    )(block_e, n_used, xbuf, w_gate, w_up, w_down)


def _combine_kernel(dest_ref, h_ref, gate_ref, fg_ref, ybuf_ref, o_ref, y_ref, sem, *, tm, d, final_norm):
    def issue(tk, carry):
        for s in range(MOE_TOP_K):
            pltpu.make_async_copy(_row_tile(ybuf_ref, dest_ref[0, s, tk]), _row_tile(y_ref.at[s], tk),
                                  sem).start(priority=s)
        return carry

    lax.fori_loop(0, tm, issue, 0, unroll=8)
    for s in range(MOE_TOP_K):
        pltpu.make_async_copy(ybuf_ref.at[pl.ds(0, tm * SUBLANES), :], y_ref.at[s], sem).wait()

    g0 = gate_ref[:, 0:1]
    g1 = gate_ref[:, 1:2]
    cols = []
    for c in range(d // LANES):
        y0 = y_ref[0, pl.ds(c, tm, stride=SUBLANES), :]
        y1 = y_ref[1, pl.ds(c, tm, stride=SUBLANES), :]
        cols.append(h_ref[:, c * LANES:(c + 1) * LANES] + (y0 * g0 + y1 * g1))
    out = jnp.concatenate(cols, axis=1)
    if final_norm:
        out = _rms_unit(out) * fg_ref[...]
    o_ref[...] = out


def _combine(h, gate_t, dest3, ybuf, final_gain, final_norm, tm):
    t, d = h.shape
    nt = dest3.shape[0]
    return pl.pallas_call(
        functools.partial(_combine_kernel, tm=tm, d=d, final_norm=final_norm),
        grid=(nt,),
        in_specs=[
            pl.BlockSpec((1, MOE_TOP_K, tm), lambda i: (i, 0, 0), memory_space=pltpu.SMEM),
            pl.BlockSpec((tm, d), lambda i: (i, 0)),
            pl.BlockSpec((tm, MOE_TOP_K), lambda i: (i, 0)),
            pl.BlockSpec((1, d), lambda i: (0, 0)),
            pl.BlockSpec(memory_space=pl.ANY),
        ],
        out_specs=pl.BlockSpec((tm, d), lambda i: (i, 0)),
        out_shape=jax.ShapeDtypeStruct((t, d), F32),
        scratch_shapes=[pltpu.VMEM((MOE_TOP_K, tm * SUBLANES, LANES), F32), pltpu.SemaphoreType.DMA],
        compiler_params=_cparams(("arbitrary",)),
        name="combine",
    )(dest3, h, gate_t, final_gain.reshape(1, d), ybuf)


def _moe(xr, eidx, gate, rank, counts, h, w_gate, w_up, w_down, layer, final_gain, final_norm, tm_row):
    t = h.shape[0]
    n_slot = t * MOE_TOP_K
    n_blocks = -(-n_slot // MOE_BLOCK) + MOE_EXPERTS
    cnt = counts[:, 0].astype(jnp.int32)
    padded = (cnt + MOE_BLOCK - 1) // MOE_BLOCK * MOE_BLOCK
    pad_end = jnp.cumsum(padded)
    pad_start = pad_end - padded
    dest = rank
    for e in range(MOE_EXPERTS):
        dest = dest + jnp.where(eidx == e, pad_start[e], 0)
    block_start = jnp.arange(n_blocks, dtype=jnp.int32) * MOE_BLOCK
    block_e = jnp.minimum(jnp.sum((pad_end[None, :] <= block_start[:, None]).astype(jnp.int32), axis=1),
                          MOE_EXPERTS - 1)
    n_used = (pad_end[-1:] // MOE_BLOCK).astype(jnp.int32)
    dest3 = dest.reshape(MOE_TOP_K, t // tm_row, tm_row).transpose(1, 0, 2)
    xbuf = _dispatch(xr, dest3, pad_end.astype(jnp.int32), n_used, n_blocks, tm_row)
    ybuf = _experts(xbuf, block_e, n_used, w_gate, w_up, w_down, layer)
    return _combine(h, gate.T, dest3, ybuf, final_gain, final_norm, tm_row)


def _fox_in_kernel(h_ref, gkv_ref, gq_ref, wkv_ref, wf_ref, bf_ref, wq_ref,
                   k_ref, v_ref, q_ref, og_ref, c_ref, carry_ref, *, tm, width, tiles_per_seq, qscale):
    i = pl.program_id(0)

    @pl.when(i % tiles_per_seq == 0)
    def _():
        carry_ref[...] = jnp.zeros_like(carry_ref)

    y = _rms_unit(h_ref[...])
    xkv = (y * gkv_ref[...]).astype(BF16)
    xq = (y * gq_ref[...]).astype(BF16)
    k_ref[...] = _dot(xkv, wkv_ref[:, 0:width]).astype(BF16)
    v_ref[...] = _dot(xkv, wkv_ref[:, width:2 * width]).astype(BF16)
    q_ref[...] = (_dot(xq, wq_ref[:, 0:width]) * qscale).astype(BF16)
    og_ref[...] = _dot(xq, wq_ref[:, width:2 * width]).astype(BF16)

    log_f = _log_sigmoid(_dot(xkv, wf_ref[...]) + bf_ref[...])
    ri = lax.broadcasted_iota(jnp.int32, (tm, tm), 0)
    ci = lax.broadcasted_iota(jnp.int32, (tm, tm), 1)
    ltri = jnp.where(ri >= ci, 1.0, 0.0).astype(BF16)
    f_hi, f_mid = _split_bf16(log_f)
    f_lo = (log_f - f_hi.astype(F32) - f_mid.astype(F32)).astype(BF16)
    c = carry_ref[0:1, :] + (_dot(ltri, f_hi) + (_dot(ltri, f_mid) + _dot(ltri, f_lo)))
    c_ref[...] = c
    carry_ref[0:1, :] = c[tm - 1:tm, :]


def _fox_in(h, kv_gain, q_gain, kv_w, kv_b_f, w_in, seq, tm):
    t, d = h.shape
    width = w_in.shape[1] // 2
    heads = kv_w.shape[1] - 2 * width
    const = lambda i: (0, 0)
    row = lambda i: (i, 0)
    big = jax.ShapeDtypeStruct((t, width), BF16)
    return pl.pallas_call(
        functools.partial(_fox_in_kernel, tm=tm, width=width, tiles_per_seq=seq // tm,
                          qscale=(width // heads) ** -0.5),
        grid=(t // tm,),
        in_specs=[
            pl.BlockSpec((tm, d), row),
            pl.BlockSpec((1, d), const),
            pl.BlockSpec((1, d), const),
            pl.BlockSpec((d, 2 * width), const),
            pl.BlockSpec((d, heads), const),
            pl.BlockSpec((1, heads), const),
            pl.BlockSpec((d, 2 * width), const),
        ],
        out_specs=[pl.BlockSpec((tm, width), row)] * 4 + [pl.BlockSpec((tm, heads), row)],
        out_shape=[big, big, big, big, jax.ShapeDtypeStruct((t, heads), F32)],
        scratch_shapes=[pltpu.VMEM((SUBLANES, heads), F32)],
        compiler_params=_cparams(("arbitrary",)),
        name="fox_in",
    )(h, kv_gain.reshape(1, d), q_gain.reshape(1, d), kv_w[:, :2 * width].astype(BF16),
      kv_w[:, 2 * width:].astype(BF16), kv_b_f.reshape(1, heads), w_in.astype(BF16))


ATTN_KV_CHUNK = 512


def _attn_kernel(q_ref, k_ref, v_ref, og_ref, cc_ref, o_ref, qa_ref, ka_ref, va_ref, s_ref, *, seq, tq, dh):
    hp = pl.program_id(1)
    heads = cc_ref.shape[1]
    lane = lax.broadcasted_iota(jnp.int32, (seq, LANES), 1)
    lane_c = lax.broadcasted_iota(jnp.int32, (seq, heads), 1)
    cc = cc_ref[...]
    for hh in range(2):
        c = jnp.sum(jnp.where(lane_c == hp * 2 + hh, cc, 0.0), axis=1, keepdims=True)
        c_hi = c.astype(BF16).astype(F32)
        c_mid = (c - c_hi).astype(BF16).astype(F32)
        c_lo = (c - c_hi - c_mid).astype(BF16).astype(F32)
        a0 = dh if hh == 0 else 0
        own = (lane < dh) if hh == 0 else (lane >= dh)
        q_aug = jnp.where(lane == a0, c_hi, jnp.where(lane == a0 + 1, c_mid, jnp.where(lane == a0 + 2, c_lo,
                jnp.where((lane >= a0 + 3) & (lane < a0 + 6), 1.0, 0.0))))
        k_aug = jnp.where(lane == a0 + 3, -c_hi, jnp.where(lane == a0 + 4, -c_mid, jnp.where(lane == a0 + 5, -c_lo,
                jnp.where((lane >= a0) & (lane < a0 + 3), 1.0, 0.0))))
        qa_ref[hh] = jnp.where(own, q_ref[...], q_aug.astype(BF16))
        ka_ref[hh] = jnp.where(own, k_ref[...], k_aug.astype(BF16))
        va_ref[hh] = jnp.where(own, v_ref[...], jnp.where(lane == a0, 1.0, 0.0).astype(BF16))

    ri = lax.broadcasted_iota(jnp.int32, (tq, tq), 0)
    ci = lax.broadcasted_iota(jnp.int32, (tq, tq), 1)
    causal = ri >= ci
    lane_q = lax.broadcasted_iota(jnp.int32, (tq, LANES), 1)

    def fold_max(mx, s):
        for g in range(s.shape[1] // LANES):
            mx = jnp.maximum(mx, s[:, g * LANES:(g + 1) * LANES])
        return mx

    for qi in range(seq // tq):
        rows = slice(qi * tq, (qi + 1) * tq)
        past = [(c0, min(c0 + ATTN_KV_CHUNK, qi * tq)) for c0 in range(0, qi * tq, ATTN_KV_CHUNK)]
        outs = []
        for hh in range(2):
            qa = qa_ref[hh, rows, :]
            mx = jnp.full((tq, LANES), -jnp.inf, F32)
            for c0, c1 in past:
                s = _dot_nt(qa, ka_ref[hh, c0:c1, :])
                s_ref[hh, :, c0:c1] = s
                mx = fold_max(mx, s)
            s = jnp.where(causal, _dot_nt(qa, ka_ref[hh, rows, :]), -jnp.inf)
            s_ref[hh, :, rows] = s
            mx = fold_max(mx, s)
            m = jnp.max(mx, axis=1, keepdims=True)
            acc = jnp.zeros((tq, LANES), F32)
            for c0, c1 in past + [(qi * tq, (qi + 1) * tq)]:
                p = jnp.exp(s_ref[hh, :, c0:c1] - m).astype(BF16)
                acc = acc + _dot(p, va_ref[hh, c0:c1, :])
            a0 = dh if hh == 0 else 0
            outs.append(acc / acc[:, a0:a0 + 1])
        o = jnp.where(lane_q < dh, outs[0], outs[1])
        o_ref[rows, :] = (o * _sigmoid(og_ref[rows, :].astype(F32))).astype(BF16)


def _attn(q, k, v, og, c_col, batch, seq, tq):
    t, width = q.shape
    heads = c_col.shape[1]
    dh = width // heads
    assert 2 * dh == LANES, "two heads per 128-lane block"
    blk = pl.BlockSpec((seq, LANES), lambda b, hp: (b, hp))
    return pl.pallas_call(
        functools.partial(_attn_kernel, seq=seq, tq=tq, dh=dh),
        grid=(batch, heads // 2),
        in_specs=[blk, blk, blk, blk, pl.BlockSpec((seq, heads), lambda b, hp: (b, 0))],
        out_specs=blk,
        out_shape=jax.ShapeDtypeStruct((t, width), BF16),
        scratch_shapes=[pltpu.VMEM((2, seq, LANES), BF16)] * 3 + [pltpu.VMEM((2, tq, seq), F32)],
        compiler_params=_cparams(("arbitrary", "arbitrary")),
        name="attn",
    )(q, k, v, og, c_col)


def _tile(n, pref):
    while n % pref:
        pref //= 2
    return pref


def kernel(x, gla_norm, gla_w_in, gla_w_a2, gla_b_a, gla_out_norm, gla_w_out, kv_norm, kv_w, kv_b_f, fox_norm,
           fox_w_in, fox_w_out, ffn_norm, router_coarse_w, router_coarse_b, router_fine_w, router_fine_b,
           expert_w_gate, expert_w_up, expert_w_down, final_norm):
    batch, seq, d = x.shape
    t = batch * seq
    n_a = gla_norm.shape[0]
    depth = ffn_norm.shape[0]
    tm = _tile(seq, 512)
    tm_row = _tile(seq, 256)
    h = x.reshape(t, d)
    for layer in range(depth):
        if layer < n_a:
            i = layer
            q, k, v, r, la = _gla_in(h, gla_norm[i], gla_w_in[i], gla_w_a2[i], gla_b_a[i], tm)
            a = _gla_core(q, k, v, r, la, gla_out_norm[i], batch, seq, tm)
            w_out = gla_w_out[i]
        else:
            i = layer - n_a
            assert i == 0 and depth - n_a == 1, "one FoX layer per trunk"
            kk, vv, qq, og, c_col = _fox_in(h, kv_norm, fox_norm[i], kv_w, kv_b_f, fox_w_in[i], seq, tm)
            a = _attn(qq, kk, vv, og, c_col, batch, seq, _tile(seq, 256))
            w_out = fox_w_out[i]
        h, xr, eidx, gate, rank, counts = _post(a, w_out, h, ffn_norm[layer], router_coarse_w[layer],
                                                router_coarse_b[layer], router_fine_w[layer],
                                                router_fine_b[layer], _tile(seq, 1024), tm)
        last = layer == depth - 1
        h = _moe(xr, eidx, gate, rank, counts, h, expert_w_gate, expert_w_up, expert_w_down, layer,
                 final_norm, last, tm_row)
    return h.reshape(batch, seq, d)
```

```python
import functools

import jax
import jax.numpy as jnp
import numpy as np
from jax import lax
from jax.experimental import pallas as pl
from jax.experimental.pallas import tpu as pltpu

F32 = jnp.float32
BF16 = jnp.bfloat16

RMS_EPS = 1e-6
GLA_HEADS = 4
GLA_GATE_RANK = 16
GLA_GATE_TAU = 16.0
FOX_HEADS = 16
MOE_GROUPS = 4
MOE_EXPERTS_PER_GROUP = 8
MOE_EXPERTS = MOE_GROUPS * MOE_EXPERTS_PER_GROUP
MOE_TOP_K = 2
MOE_BLOCK = 256

LANES = 128
SUBLANES = 8
VMEM_LIMIT = 56 * 1024 * 1024

GLA_CHUNK = 128
ROUTER_ROWS = 40


def _cparams(sem):
    return pltpu.CompilerParams(dimension_semantics=sem, vmem_limit_bytes=VMEM_LIMIT)


def _rms_unit(x):
    return x * lax.rsqrt(jnp.mean(x * x, axis=-1, keepdims=True) + RMS_EPS)


def _log_sigmoid(z):
    return jnp.minimum(z, 0.0) - jnp.log1p(jnp.exp(-jnp.abs(z)))


def _sigmoid(z):
    return 1.0 / (1.0 + jnp.exp(-z))


def _dot(a, b):
    return jnp.dot(a, b, preferred_element_type=F32)


def _dot_nt(a, b):
    return lax.dot_general(a, b, (((1,), (1,)), ((), ())), preferred_element_type=F32)


def _dot_tn(a, b):
    return lax.dot_general(a, b, (((0,), (0,)), ((), ())), preferred_element_type=F32)


def _split_bf16(x):
    hi = x.astype(BF16)
    lo = (x - hi.astype(F32)).astype(BF16)
    return hi, lo


def _gla_in_kernel(h_ref, g_ref, w_ref, wa1_ref, wa2_ref, ba_ref, q_ref, k_ref, v_ref, r_ref, la_ref, *, qk, vw):
    xn = (_rms_unit(h_ref[...]) * g_ref[...]).astype(BF16)
    q_ref[...] = _dot(xn, w_ref[:, 0:qk]).astype(BF16)
    k_ref[...] = _dot(xn, w_ref[:, qk:2 * qk]).astype(BF16)
    v_ref[...] = _dot(xn, w_ref[:, 2 * qk:2 * qk + vw]).astype(BF16)
    r_ref[...] = _dot(xn, w_ref[:, 2 * qk + vw:2 * qk + 2 * vw]).astype(BF16)
    a = _dot(xn, wa1_ref[...])
    z = _dot(a.astype(BF16), wa2_ref[...]) + ba_ref[...]
    la_ref[...] = _log_sigmoid(z) * (1.0 / GLA_GATE_TAU)


def _gla_in(h, gain, w_in, w_a2, b_a, tm):
    t, d = h.shape
    qk = w_a2.shape[1]
    vw = (w_in.shape[1] - 2 * qk - GLA_GATE_RANK) // 2
    w_main = w_in[:, :2 * qk + 2 * vw].astype(BF16)
    w_a1 = w_in[:, 2 * qk + 2 * vw:].astype(BF16)
    const = lambda i: (0, 0)
    row = lambda i: (i, 0)
    return pl.pallas_call(
        functools.partial(_gla_in_kernel, qk=qk, vw=vw),
        grid=(t // tm,),
        in_specs=[
            pl.BlockSpec((tm, d), row),
            pl.BlockSpec((1, d), const),
            pl.BlockSpec(w_main.shape, const),
            pl.BlockSpec(w_a1.shape, const),
            pl.BlockSpec(w_a2.shape, const),
            pl.BlockSpec((1, qk), const),
        ],
        out_specs=[
            pl.BlockSpec((tm, qk), row),
            pl.BlockSpec((tm, qk), row),
            pl.BlockSpec((tm, vw), row),
            pl.BlockSpec((tm, vw), row),
            pl.BlockSpec((tm, qk), row),
        ],
        out_shape=[
            jax.ShapeDtypeStruct((t, qk), BF16),
            jax.ShapeDtypeStruct((t, qk), BF16),
            jax.ShapeDtypeStruct((t, vw), BF16),
            jax.ShapeDtypeStruct((t, vw), BF16),
            jax.ShapeDtypeStruct((t, qk), F32),
        ],
        compiler_params=_cparams(("arbitrary",)),
        name="gla_in",
    )(h, gain.reshape(1, d), w_main, w_a1, w_a2.astype(BF16), b_a.reshape(1, qk))


def _gla_core_kernel(q_ref, k_ref, v_ref, r_ref, la_ref, gn_ref, o_ref, st_ref, *, n_chunks, scale):
    c_len = GLA_CHUNK

    @pl.when(pl.program_id(2) == 0)
    def _():
        st_ref[...] = jnp.zeros_like(st_ref)

    ri = lax.broadcasted_iota(jnp.int32, (c_len, c_len), 0)
    ci = lax.broadcasted_iota(jnp.int32, (c_len, c_len), 1)
    causal = ri >= ci
    ltri = jnp.where(causal, 1.0, 0.0).astype(BF16)
    gain = gn_ref[0]

    for c in range(n_chunks):
        sl = pl.ds(c * c_len, c_len)
        g_hi, g_lo = _split_bf16(la_ref[sl, :])
        cum = _dot(ltri, g_hi) + _dot(ltri, g_lo)
        last = cum[c_len - 1:c_len, :]
        mid = cum[c_len // 2 - 1:c_len // 2, :]
        qf = q_ref[sl, :].astype(F32) * scale
        kf = k_ref[sl, :].astype(F32)
        vb = v_ref[sl, :]
        qs = (qf * jnp.exp(cum - mid)).astype(BF16)
        ks = (kf * jnp.exp(mid - cum)).astype(BF16)
        scores = jnp.where(causal, _dot_nt(qs, ks), 0.0)
        o = _dot(scores.astype(BF16), vb)
        st = st_ref[...]
        qd = (qf * jnp.exp(cum)).astype(BF16)
        o = o + _dot_nt(qd, st.astype(BF16))
        kd = (kf * jnp.exp(last - cum)).astype(BF16)
        st_ref[...] = st * jnp.exp(last) + _dot_tn(vb, kd)
        on = _rms_unit(o) * gain
        rr = r_ref[sl, :].astype(F32)
        o_ref[sl, :] = (on * (rr * _sigmoid(rr))).astype(BF16)


def _gla_core(q, k, v, r, la, out_norm, batch, seq, lc):
    t, qk = q.shape
    vw = v.shape[1]
    heads = GLA_HEADS
    dk, dv = qk // heads, vw // heads
    nj = seq // lc
    tok = lambda b, h, j: (b * nj + j, h)
    return pl.pallas_call(
        functools.partial(_gla_core_kernel, n_chunks=lc // GLA_CHUNK, scale=dk ** -0.5),
        grid=(batch, heads, nj),
        in_specs=[
            pl.BlockSpec((lc, dk), tok),
            pl.BlockSpec((lc, dk), tok),
            pl.BlockSpec((lc, dv), tok),
            pl.BlockSpec((lc, dv), tok),
            pl.BlockSpec((lc, dk), tok),
            pl.BlockSpec((1, 1, dv), lambda b, h, j: (h, 0, 0)),
        ],
        out_specs=pl.BlockSpec((lc, dv), tok),
        out_shape=jax.ShapeDtypeStruct((t, vw), BF16),
        scratch_shapes=[pltpu.VMEM((dv, dk), F32)],
        compiler_params=_cparams(("arbitrary", "arbitrary", "arbitrary")),
        name="gla_core",
    )(q, k, v, r, la, out_norm.reshape(heads, 1, dv))


def _post_kernel(a_ref, w_ref, h_ref, g_ref, wr_ref, br_ref,
                 hn_ref, xr_ref, eidx_ref, gate_ref, rank_ref, cnt_ref, base_ref, *, tm, sub, d):
    @pl.when(pl.program_id(0) == 0)
    def _():
        base_ref[...] = jnp.zeros_like(base_ref)

    eg = MOE_EXPERTS_PER_GROUP
    ti = lax.broadcasted_iota(jnp.int32, (sub, sub), 0)
    tj = lax.broadcasted_iota(jnp.int32, (sub, sub), 1)
    upper = jnp.where(ti <= tj, 1.0, 0.0).astype(BF16)
    row_g = lax.broadcasted_iota(jnp.int32, (MOE_GROUPS, sub), 0)
    row_e = lax.broadcasted_iota(jnp.int32, (eg, sub), 0)
    row_x = lax.broadcasted_iota(jnp.int32, (MOE_EXPERTS, sub), 0)
    base = base_ref[:, 0:1]

    for u in range(tm // sub):
        rows = slice(u * sub, (u + 1) * sub)
        hn = h_ref[rows, :] + _dot(a_ref[rows, :], w_ref[...])
        hn_ref[rows, :] = hn
        xn = _rms_unit(hn) * g_ref[...]
        for c in range(d // LANES):
            xr_ref[pl.ds(u * sub * SUBLANES + c, sub, stride=SUBLANES), :] = xn[:, c * LANES:(c + 1) * LANES]

        x_hi, x_lo = _split_bf16(xn)
        r_hi = _dot(x_hi, wr_ref[...])
        r_lo = _dot(x_lo, wr_ref[...])
        lg_t = (r_hi[:, 0:LANES] + r_hi[:, LANES:]) + (r_lo[:, 0:LANES] + r_lo[:, LANES:])
        lg = lg_t.T[0:ROUTER_ROWS] + br_ref[...]
        lc = lg[0:MOE_GROUPS]
        m_c = jnp.max(lc, axis=0, keepdims=True)
        pc_top = 1.0 / jnp.sum(jnp.exp(lc - m_c), axis=0, keepdims=True)
        g_idx = jnp.min(jnp.where(lc == m_c, row_g, MOE_GROUPS), axis=0, keepdims=True)
        lf = jnp.zeros((eg, sub), F32)
        for g in range(MOE_GROUPS):
            lf = lf + jnp.where(g_idx == g, lg[SUBLANES + g * eg:SUBLANES + (g + 1) * eg], 0.0)
        ef = jnp.exp(lf - jnp.max(lf, axis=0, keepdims=True))
        pf = ef / jnp.sum(ef, axis=0, keepdims=True)
        v1 = jnp.max(pf, axis=0, keepdims=True)
        j1 = jnp.min(jnp.where(pf == v1, row_e, eg), axis=0, keepdims=True)
        pf2 = jnp.where(row_e == j1, -1.0, pf)
        v2 = jnp.max(pf2, axis=0, keepdims=True)
        j2 = jnp.min(jnp.where(pf2 == v2, row_e, eg), axis=0, keepdims=True)
        denom = v1 + v2
        e1 = g_idx * eg + j1
        e2 = g_idx * eg + j2
        eidx_ref[0:1, rows] = e1
        eidx_ref[1:2, rows] = e2
        gate_ref[0:1, rows] = pc_top * v1 / denom
        gate_ref[1:2, rows] = pc_top * v2 / denom

        oh1 = row_x == e1
        oh2 = row_x == e2
        pre1 = _dot(jnp.where(oh1, 1.0, 0.0).astype(BF16), upper)
        pre2 = _dot(jnp.where(oh2, 1.0, 0.0).astype(BF16), upper)
        cnt1 = pre1[:, sub - 1:sub]
        cnt2 = pre2[:, sub - 1:sub]
        rank1 = jnp.sum(jnp.where(oh1, base + pre1 - 1.0, 0.0), axis=0, keepdims=True)
        rank2 = jnp.sum(jnp.where(oh2, base + cnt1 + pre2 - 1.0, 0.0), axis=0, keepdims=True)
        rank_ref[0:1, rows] = rank1.astype(jnp.int32)
        rank_ref[1:2, rows] = rank2.astype(jnp.int32)
        base = base + (cnt1 + cnt2)

    new_base = jnp.broadcast_to(base, base_ref.shape)
    base_ref[...] = new_base
    cnt_ref[...] = new_base


def _post(a, w_out, h, gain, w_coarse, b_coarse, w_fine, b_fine, tm, sub):
    t, d = h.shape
    wr = jnp.concatenate([w_coarse, jnp.zeros((d, SUBLANES - MOE_GROUPS), F32), w_fine,
                          jnp.zeros((d, LANES - ROUTER_ROWS), F32)], axis=1)
    wr = jnp.concatenate(_split_bf16(wr), axis=1)
    br = jnp.zeros((ROUTER_ROWS, 1), F32)
    br = br.at[0:MOE_GROUPS, 0].set(b_coarse).at[SUBLANES:, 0].set(b_fine)
    const = lambda i: (0, 0)
    row = lambda i: (i, 0)
    col = lambda i: (0, i)
    return pl.pallas_call(
        functools.partial(_post_kernel, tm=tm, sub=sub, d=d),
        grid=(t // tm,),
        in_specs=[
            pl.BlockSpec((tm, a.shape[1]), row),
            pl.BlockSpec(w_out.shape, const),
            pl.BlockSpec((tm, d), row),
            pl.BlockSpec((1, d), const),
            pl.BlockSpec((d, 2 * LANES), const),
            pl.BlockSpec((ROUTER_ROWS, 1), const),
        ],
        out_specs=[
            pl.BlockSpec((tm, d), row),
            pl.BlockSpec((tm * SUBLANES, LANES), row),
            pl.BlockSpec((MOE_TOP_K, tm), col),
            pl.BlockSpec((MOE_TOP_K, tm), col),
            pl.BlockSpec((MOE_TOP_K, tm), col),
            pl.BlockSpec((MOE_EXPERTS, LANES), const),
        ],
        out_shape=[
            jax.ShapeDtypeStruct((t, d), F32),
            jax.ShapeDtypeStruct((t * SUBLANES, LANES), F32),
            jax.ShapeDtypeStruct((MOE_TOP_K, t), jnp.int32),
            jax.ShapeDtypeStruct((MOE_TOP_K, t), F32),
            jax.ShapeDtypeStruct((MOE_TOP_K, t), jnp.int32),
            jax.ShapeDtypeStruct((MOE_EXPERTS, LANES), F32),
        ],
        scratch_shapes=[pltpu.VMEM((MOE_EXPERTS, LANES), F32)],
        compiler_params=_cparams(("arbitrary",)),
        name="post",
    )(a, w_out.astype(BF16), h, gain.reshape(1, d), wr, br)


def _row_tile(ref, idx):
    return ref.at[pl.ds(pl.multiple_of(idx * SUBLANES, SUBLANES), SUBLANES), :]


def _block_rows(ref, blk):
    return ref.at[pl.ds(pl.multiple_of(blk * (MOE_BLOCK * SUBLANES), MOE_BLOCK * SUBLANES), MOE_BLOCK * SUBLANES), :]


def _dispatch_kernel(pe_ref, nu_ref, dest_ref, xr_ref, xbuf_ref, zero_ref, sem, zsem, *, tm, n_blocks):
    @pl.when(pl.program_id(0) == 0)
    def _():
        zero_ref[...] = jnp.zeros_like(zero_ref)

        def last_block(e):
            return pe_ref[e] // MOE_BLOCK - 1

        def has_rows(e):
            return pe_ref[e] > jnp.where(e == 0, 0, pe_ref[jnp.maximum(e - 1, 0)])

        def zero_block(blk):
            return pltpu.make_async_copy(zero_ref, _block_rows(xbuf_ref, blk), zsem)

        def start_e(e, carry):
            @pl.when(has_rows(e))
            def _():
                zero_block(last_block(e)).start()
            return carry

        def wait_e(e, carry):
            @pl.when(has_rows(e))
            def _():
                zero_block(last_block(e)).wait()
            return carry

        def start_b(blk, carry):
            zero_block(blk).start()
            return carry

        def wait_b(blk, carry):
            zero_block(blk).wait()
            return carry

        lax.fori_loop(0, MOE_EXPERTS, start_e, 0)
        lax.fori_loop(nu_ref[0], n_blocks, start_b, 0)
        lax.fori_loop(0, MOE_EXPERTS, wait_e, 0)
        lax.fori_loop(nu_ref[0], n_blocks, wait_b, 0)

    def issue(tk, carry):
        for s in range(MOE_TOP_K):
            pltpu.make_async_copy(_row_tile(xr_ref, tk), _row_tile(xbuf_ref, dest_ref[0, s, tk]), sem).start(priority=s)
        return carry

    lax.fori_loop(0, tm, issue, 0, unroll=8)
    for s in range(MOE_TOP_K):
        pltpu.make_async_copy(xr_ref, xbuf_ref.at[pl.ds(0, tm * SUBLANES), :], sem).wait()


def _dispatch(xr, dest3, pad_end, n_used, n_blocks, tm):
    nt = dest3.shape[0]
    grid_spec = pltpu.PrefetchScalarGridSpec(
        num_scalar_prefetch=2,
        grid=(nt,),
        in_specs=[
            pl.BlockSpec((1, MOE_TOP_K, tm), lambda i, pe, nu: (i, 0, 0), memory_space=pltpu.SMEM),
            pl.BlockSpec((tm * SUBLANES, LANES), lambda i, pe, nu: (i, 0)),
        ],
        out_specs=pl.BlockSpec(memory_space=pl.ANY),
        scratch_shapes=[pltpu.VMEM((MOE_BLOCK * SUBLANES, LANES), F32), pltpu.SemaphoreType.DMA,
                        pltpu.SemaphoreType.DMA],
    )
    return pl.pallas_call(
        functools.partial(_dispatch_kernel, tm=tm, n_blocks=n_blocks),
        grid_spec=grid_spec,
        out_shape=jax.ShapeDtypeStruct((n_blocks * MOE_BLOCK * SUBLANES, LANES), F32),
        compiler_params=_cparams(("arbitrary",)),
        name="dispatch",
    )(pad_end, n_used, dest3, xr)


def _experts_kernel(be_ref, nu_ref, x_ref, wg_ref, wu_ref, wd_ref, y_ref, x2_ref, wgb_ref, wub_ref, wdb_ref, *, d):
    rows = MOE_BLOCK
    b = pl.program_id(0)
    live = b < nu_ref[0]
    new_expert = jnp.logical_or(b == 0, be_ref[b] != be_ref[jnp.maximum(b - 1, 0)])

    @pl.when(jnp.logical_and(live, new_expert))
    def _():
        wgb_ref[...] = wg_ref[...].astype(BF16)
        wub_ref[...] = wu_ref[...].astype(BF16)
        wdb_ref[...] = wd_ref[...].astype(BF16)

    @pl.when(live)
    def _():
        for c in range(d // LANES):
            x2_ref[:, c * LANES:(c + 1) * LANES] = x_ref[pl.ds(c, rows, stride=SUBLANES), :].astype(BF16)
        x2 = x2_ref[...]
        gp = _dot(x2, wgb_ref[...])
        up = _dot(x2, wub_ref[...])
        hid = (gp * _sigmoid(gp) * up).astype(BF16)
        y = _dot(hid, wdb_ref[...])
        for c in range(d // LANES):
            y_ref[pl.ds(c, rows, stride=SUBLANES), :] = y[:, c * LANES:(c + 1) * LANES]


def _experts(xbuf, block_e, n_used, w_gate, w_up, w_down, layer):
    n_blocks = block_e.shape[0]
    d, dff = w_gate.shape[2], w_gate.shape[3]
    xmap = lambda b, be, nu: (jnp.minimum(b, nu[0] - 1), 0)
    wmap = lambda b, be, nu: (layer, be[b], 0, 0)
    grid_spec = pltpu.PrefetchScalarGridSpec(
        num_scalar_prefetch=2,
        grid=(n_blocks,),
        in_specs=[
            pl.BlockSpec((MOE_BLOCK * SUBLANES, LANES), xmap),
            pl.BlockSpec((None, None, d, dff), wmap),
            pl.BlockSpec((None, None, d, dff), wmap),
            pl.BlockSpec((None, None, dff, d), wmap),
        ],
        out_specs=pl.BlockSpec((MOE_BLOCK * SUBLANES, LANES), xmap),
        scratch_shapes=[pltpu.VMEM((MOE_BLOCK, d), BF16), pltpu.VMEM((d, dff), BF16), pltpu.VMEM((d, dff), BF16),
                        pltpu.VMEM((dff, d), BF16)],
    )
    return pl.pallas_call(
        functools.partial(_experts_kernel, d=d),
        grid_spec=grid_spec,
        out_shape=jax.ShapeDtypeStruct(xbuf.shape, F32),
        input_output_aliases={2: 0},
        compiler_params=_cparams(("arbitrary",)),
        name="experts",
    )(block_e, n_used, xbuf, w_gate, w_up, w_down)


def _gather_combine(dcur_ref, dnxt_ref, ybuf_ref, h_ref, gate_ref, y_ref, sems, *, tm, d):
    i = pl.program_id(0)

    def start(dref, slot):
        def issue(tk, carry):
            for s in range(MOE_TOP_K):
                pltpu.make_async_copy(_row_tile(ybuf_ref, dref[0, s, tk]), _row_tile(y_ref.at[slot, s], tk),
                                      sems.at[slot]).start(priority=s)
            return carry

        lax.fori_loop(0, tm, issue, 0, unroll=8)

    @pl.when(i == 0)
    def _():
        start(dcur_ref, 0)

    @pl.when(i + 1 < pl.num_programs(0))
    def _():
        start(dnxt_ref, (i + 1) % 2)

    slot = i % 2
    for s in range(MOE_TOP_K):
        pltpu.make_async_copy(ybuf_ref.at[pl.ds(0, tm * SUBLANES), :], y_ref.at[slot, s], sems.at[slot]).wait()
    g0 = gate_ref[:, 0:1]
    g1 = gate_ref[:, 1:2]
    cols = []
    for c in range(d // LANES):
        y0 = y_ref[slot, 0, pl.ds(c, tm, stride=SUBLANES), :]
        y1 = y_ref[slot, 1, pl.ds(c, tm, stride=SUBLANES), :]
        cols.append(h_ref[:, c * LANES:(c + 1) * LANES] + (y0 * g0 + y1 * g1))
    return jnp.concatenate(cols, axis=1)


def _combine_specs(tm, d, nt):
    in_specs = [
        pl.BlockSpec((1, MOE_TOP_K, tm), lambda i: (i, 0, 0), memory_space=pltpu.SMEM),
        pl.BlockSpec((1, MOE_TOP_K, tm), lambda i: (jnp.minimum(i + 1, nt - 1), 0, 0), memory_space=pltpu.SMEM),
        pl.BlockSpec(memory_space=pl.ANY),
        pl.BlockSpec((tm, d), lambda i: (i, 0)),
        pl.BlockSpec((tm, MOE_TOP_K), lambda i: (i, 0)),
    ]
    scratch = [pltpu.VMEM((2, MOE_TOP_K, tm * SUBLANES, LANES), F32), pltpu.SemaphoreType.DMA((2,))]
    return in_specs, scratch


def _final_kernel(dcur_ref, dnxt_ref, ybuf_ref, h_ref, gate_ref, fg_ref, o_ref, y_ref, sems, *, tm, d):
    out = _gather_combine(dcur_ref, dnxt_ref, ybuf_ref, h_ref, gate_ref, y_ref, sems, tm=tm, d=d)
    o_ref[...] = _rms_unit(out) * fg_ref[...]


def _final(moe, final_gain, tm):
    ybuf, dest3, gate_t, h = moe
    t, d = h.shape
    nt = dest3.shape[0]
    in_specs, scratch = _combine_specs(tm, d, nt)
    return pl.pallas_call(
        functools.partial(_final_kernel, tm=tm, d=d),
        grid=(nt,),
        in_specs=in_specs + [pl.BlockSpec((1, d), lambda i: (0, 0))],
        out_specs=pl.BlockSpec((tm, d), lambda i: (i, 0)),
        out_shape=jax.ShapeDtypeStruct((t, d), F32),
        scratch_shapes=scratch,
        compiler_params=_cparams(("arbitrary",)),
        name="final",
    )(dest3, dest3, ybuf, h, gate_t, final_gain.reshape(1, d))


def _moe_experts(xr, eidx, gate, rank, counts, h, w_gate, w_up, w_down, layer, tm_row):
    t = h.shape[0]
    n_slot = t * MOE_TOP_K
    n_blocks = -(-n_slot // MOE_BLOCK) + MOE_EXPERTS
    cnt = counts[:, 0].astype(jnp.int32)
    padded = (cnt + MOE_BLOCK - 1) // MOE_BLOCK * MOE_BLOCK
    pad_end = jnp.cumsum(padded)
    pad_start = pad_end - padded
    dest = rank
    for e in range(MOE_EXPERTS):
        dest = dest + jnp.where(eidx == e, pad_start[e], 0)
    block_start = jnp.arange(n_blocks, dtype=jnp.int32) * MOE_BLOCK
    block_e = jnp.minimum(jnp.sum((pad_end[None, :] <= block_start[:, None]).astype(jnp.int32), axis=1),
                          MOE_EXPERTS - 1)
    n_used = (pad_end[-1:] // MOE_BLOCK).astype(jnp.int32)
    dest3 = dest.reshape(MOE_TOP_K, t // tm_row, tm_row).transpose(1, 0, 2)
    xbuf = _dispatch(xr, dest3, pad_end.astype(jnp.int32), n_used, n_blocks, tm_row)
    ybuf = _experts(xbuf, block_e, n_used, w_gate, w_up, w_down, layer)
    return ybuf, dest3, gate.T, h


def _fox_in_kernel(dcur_ref, dnxt_ref, ybuf_ref, h_ref, gate_ref, gkv_ref, gq_ref, wkv_ref, wf_ref, bf_ref, wq_ref,
                   place_ref, hn_ref, k_ref, v_ref, q_ref, og_ref, ka_ref, y_ref, sems, carry_ref,
                   *, tm, d, width, tiles_per_seq, qscale):
    i = pl.program_id(0)

    @pl.when(i % tiles_per_seq == 0)
    def _():
        carry_ref[...] = jnp.zeros_like(carry_ref)

    hn_ref[...] = _gather_combine(dcur_ref, dnxt_ref, ybuf_ref, h_ref, gate_ref, y_ref, sems, tm=tm, d=d)
    y = _rms_unit(hn_ref[...])
    xkv = (y * gkv_ref[...]).astype(BF16)
    xq = (y * gq_ref[...]).astype(BF16)
    k_ref[...] = _dot(xkv, wkv_ref[:, 0:width]).astype(BF16)
    v_ref[...] = _dot(xkv, wkv_ref[:, width:2 * width]).astype(BF16)
    q_ref[...] = (_dot(xq, wq_ref[:, 0:width]) * qscale).astype(BF16)
    og_ref[...] = _dot(xq, wq_ref[:, width:2 * width]).astype(BF16)

    log_f = _log_sigmoid(_dot(xkv, wf_ref[...]) + bf_ref[...])
    ri = lax.broadcasted_iota(jnp.int32, (tm, tm), 0)
    ci = lax.broadcasted_iota(jnp.int32, (tm, tm), 1)
    ltri = jnp.where(ri >= ci, 1.0, 0.0).astype(BF16)
    f_hi, f_mid = _split_bf16(log_f)
    f_lo = (log_f - f_hi.astype(F32) - f_mid.astype(F32)).astype(BF16)
    c = carry_ref[0:1, :] + (_dot(ltri, f_hi) + (_dot(ltri, f_mid) + _dot(ltri, f_lo)))
    carry_ref[0:1, :] = c[tm - 1:tm, :]
    n_hi, n_mid = _split_bf16(-c)
    n_lo = (-c - n_hi.astype(F32) - n_mid.astype(F32)).astype(BF16)
    ka_ref[...] = _dot(jnp.concatenate([n_hi, n_mid, n_lo], axis=1), place_ref[...]).astype(BF16)


ATTN_BIAS_LANES = 3


def _bias_lane0(hh, dh):
    return dh if hh == 0 else 0


def _fox_in(moe, kv_gain, q_gain, kv_w, kv_b_f, w_in, seq, tm):
    ybuf, dest3, gate_t, h = moe
    t, d = h.shape
    nt = dest3.shape[0]
    width = w_in.shape[1] // 2
    heads = kv_w.shape[1] - 2 * width
    dh = width // heads
    place = np.zeros((ATTN_BIAS_LANES * heads, width), np.float32)
    for part in range(ATTN_BIAS_LANES):
        for hd in range(heads):
            place[part * heads + hd, (hd // 2) * LANES + _bias_lane0(hd % 2, dh) + part] = 1.0
    const = lambda i: (0, 0)
    row = lambda i: (i, 0)
    big = jax.ShapeDtypeStruct((t, width), BF16)
    comb_specs, comb_scratch = _combine_specs(tm, d, nt)
    return pl.pallas_call(
        functools.partial(_fox_in_kernel, tm=tm, d=d, width=width, tiles_per_seq=seq // tm, qscale=dh ** -0.5),
        grid=(nt,),
        in_specs=comb_specs + [
            pl.BlockSpec((1, d), const),
            pl.BlockSpec((1, d), const),
            pl.BlockSpec((d, 2 * width), const),
            pl.BlockSpec((d, heads), const),
            pl.BlockSpec((1, heads), const),
            pl.BlockSpec((d, 2 * width), const),
            pl.BlockSpec(place.shape, const),
        ],
        out_specs=[pl.BlockSpec((tm, d), row)] + [pl.BlockSpec((tm, width), row)] * 5,
        out_shape=[jax.ShapeDtypeStruct((t, d), F32)] + [big] * 5,
        scratch_shapes=comb_scratch + [pltpu.VMEM((SUBLANES, heads), F32)],
        compiler_params=_cparams(("arbitrary",)),
        name="fox_in",
    )(dest3, dest3, ybuf, h, gate_t, kv_gain.reshape(1, d), q_gain.reshape(1, d), kv_w[:, :2 * width].astype(BF16),
      kv_w[:, 2 * width:].astype(BF16), kv_b_f.reshape(1, heads), w_in.astype(BF16), jnp.asarray(place, BF16))


ATTN_KV_CHUNK = 512


def _attn_kernel(q_ref, k_ref, v_ref, og_ref, kb_ref, o_ref, qa_ref, ka_ref, va_ref, s_ref, *, seq, tq, dh):
    lane = lax.broadcasted_iota(jnp.int32, (seq, LANES), 1)
    for hh in range(2):
        a0 = _bias_lane0(hh, dh)
        own = (lane < dh) if hh == 0 else (lane >= dh)
        ones_q = jnp.where((lane >= a0) & (lane < a0 + ATTN_BIAS_LANES), 1.0, 0.0).astype(BF16)
        qa_ref[hh] = jnp.where(own, q_ref[...], ones_q)
        ka_ref[hh] = jnp.where(own, k_ref[...], kb_ref[...])
        va_ref[hh] = jnp.where(own, v_ref[...], jnp.where(lane == a0, 1.0, 0.0).astype(BF16))

    ri = lax.broadcasted_iota(jnp.int32, (tq, tq), 0)
    ci = lax.broadcasted_iota(jnp.int32, (tq, tq), 1)
    causal = ri >= ci
    lane_q = lax.broadcasted_iota(jnp.int32, (tq, LANES), 1)

    def fold_max(mx, s):
        for g in range(s.shape[1] // LANES):
            mx = jnp.maximum(mx, s[:, g * LANES:(g + 1) * LANES])
        return mx

    for qi in range(seq // tq):
        rows = slice(qi * tq, (qi + 1) * tq)
        past = [(c0, min(c0 + ATTN_KV_CHUNK, qi * tq)) for c0 in range(0, qi * tq, ATTN_KV_CHUNK)]
        outs = []
        for hh in range(2):
            qa = qa_ref[hh, rows, :]
            mx = jnp.full((tq, LANES), -jnp.inf, F32)
            for c0, c1 in past:
                s = _dot_nt(qa, ka_ref[hh, c0:c1, :])
                s_ref[hh, :, c0:c1] = s
                mx = fold_max(mx, s)
            s = jnp.where(causal, _dot_nt(qa, ka_ref[hh, rows, :]), -jnp.inf)
            s_ref[hh, :, rows] = s
            mx = fold_max(mx, s)
            m = jnp.max(mx, axis=1, keepdims=True)
            acc = jnp.zeros((tq, LANES), F32)
            for c0, c1 in past + [(qi * tq, (qi + 1) * tq)]:
                p = jnp.exp(s_ref[hh, :, c0:c1] - m).astype(BF16)
                acc = acc + _dot(p, va_ref[hh, c0:c1, :])
            a0 = _bias_lane0(hh, dh)
            outs.append(acc / acc[:, a0:a0 + 1])
        o = jnp.where(lane_q < dh, outs[0], outs[1])
        o_ref[rows, :] = (o * _sigmoid(og_ref[rows, :].astype(F32))).astype(BF16)


def _attn(q, k, v, og, kb, heads, batch, seq, tq):
    t, width = q.shape
    dh = width // heads
    assert 2 * dh == LANES, "two heads per 128-lane block"
    blk = pl.BlockSpec((seq, LANES), lambda b, hp: (b, hp))
    return pl.pallas_call(
        functools.partial(_attn_kernel, seq=seq, tq=tq, dh=dh),
        grid=(batch, heads // 2),
        in_specs=[blk] * 5,
        out_specs=blk,
        out_shape=jax.ShapeDtypeStruct((t, width), BF16),
        scratch_shapes=[pltpu.VMEM((2, seq, LANES), BF16)] * 3 + [pltpu.VMEM((2, tq, seq), F32)],
        compiler_params=_cparams(("arbitrary", "arbitrary")),
        name="attn",
    )(q, k, v, og, kb)


def _tile(n, pref):
    while n % pref:
        pref //= 2
    return pref


def kernel(x, gla_norm, gla_w_in, gla_w_a2, gla_b_a, gla_out_norm, gla_w_out, kv_norm, kv_w, kv_b_f, fox_norm,
           fox_w_in, fox_w_out, ffn_norm, router_coarse_w, router_coarse_b, router_fine_w, router_fine_b,
           expert_w_gate, expert_w_up, expert_w_down, final_norm):
    batch, seq, d = x.shape
    t = batch * seq
    assert gla_norm.shape[0] == 1 and ffn_norm.shape[0] == 2, "trunk is GLA layer + FoX layer"
    tm = _tile(seq, 512)

    def ffn(a, w_out, h, layer):
        h, xr, eidx, gate, rank, counts = _post(a, w_out, h, ffn_norm[layer], router_coarse_w[layer],
                                                router_coarse_b[layer], router_fine_w[layer],
                                                router_fine_b[layer], _tile(seq, 1024), tm)
        return _moe_experts(xr, eidx, gate, rank, counts, h, expert_w_gate, expert_w_up, expert_w_down, layer, tm)

    h = x.reshape(t, d)
    q, k, v, r, la = _gla_in(h, gla_norm[0], gla_w_in[0], gla_w_a2[0], gla_b_a[0], tm)
    a = _gla_core(q, k, v, r, la, gla_out_norm[0], batch, seq, tm)
    moe = ffn(a, gla_w_out[0], h, 0)
    h, kk, vv, qq, og, kb = _fox_in(moe, kv_norm, fox_norm[0], kv_w, kv_b_f, fox_w_in[0], seq, tm)
    a = _attn(qq, kk, vv, og, kb, kv_b_f.shape[0], batch, seq, _tile(seq, 256))
    moe = ffn(a, fox_w_out[0], h, 1)
    return _final(moe, final_norm, tm).reshape(batch, seq, d)
```

```python
import functools

import jax
import jax.numpy as jnp
import numpy as np
from jax import lax
from jax.experimental import pallas as pl
from jax.experimental.pallas import tpu as pltpu

F32 = jnp.float32
BF16 = jnp.bfloat16

RMS_EPS = 1e-6
GLA_HEADS = 4
GLA_GATE_RANK = 16
GLA_GATE_TAU = 16.0
FOX_HEADS = 16
MOE_GROUPS = 4
MOE_EXPERTS_PER_GROUP = 8
MOE_EXPERTS = MOE_GROUPS * MOE_EXPERTS_PER_GROUP
MOE_TOP_K = 2
MOE_BLOCK = 256

LANES = 128
SUBLANES = 8
VMEM_LIMIT = 56 * 1024 * 1024

GLA_CHUNK = 128
ROUTER_ROWS = 40


def _cparams(sem):
    return pltpu.CompilerParams(dimension_semantics=sem, vmem_limit_bytes=VMEM_LIMIT)


def _rms_unit(x):
    return x * lax.rsqrt(jnp.mean(x * x, axis=-1, keepdims=True) + RMS_EPS)


def _log_sigmoid(z):
    return jnp.minimum(z, 0.0) - jnp.log1p(jnp.exp(-jnp.abs(z)))


def _sigmoid(z):
    return 1.0 / (1.0 + jnp.exp(-z))


def _dot(a, b):
    return jnp.dot(a, b, preferred_element_type=F32)


def _dot_nt(a, b):
    return lax.dot_general(a, b, (((1,), (1,)), ((), ())), preferred_element_type=F32)


def _dot_tn(a, b):
    return lax.dot_general(a, b, (((0,), (0,)), ((), ())), preferred_element_type=F32)


def _split_bf16(x):
    hi = x.astype(BF16)
    lo = (x - hi.astype(F32)).astype(BF16)
    return hi, lo


def _gla_in_kernel(h_ref, g_ref, w_ref, wa1_ref, wa2_ref, ba_ref, q_ref, k_ref, v_ref, r_ref, la_ref, *, qk, vw):
    xn = (_rms_unit(h_ref[...]) * g_ref[...]).astype(BF16)
    q_ref[...] = _dot(xn, w_ref[:, 0:qk]).astype(BF16)
    k_ref[...] = _dot(xn, w_ref[:, qk:2 * qk]).astype(BF16)
    v_ref[...] = _dot(xn, w_ref[:, 2 * qk:2 * qk + vw]).astype(BF16)
    r_ref[...] = _dot(xn, w_ref[:, 2 * qk + vw:2 * qk + 2 * vw]).astype(BF16)
    a = _dot(xn, wa1_ref[...])
    z = _dot(a.astype(BF16), wa2_ref[...]) + ba_ref[...]
    la_ref[...] = _log_sigmoid(z) * (1.0 / GLA_GATE_TAU)


def _gla_in(h, gain, w_in, w_a2, b_a, tm):
    t, d = h.shape
    qk = w_a2.shape[1]
    vw = (w_in.shape[1] - 2 * qk - GLA_GATE_RANK) // 2
    w_main = w_in[:, :2 * qk + 2 * vw].astype(BF16)
    w_a1 = w_in[:, 2 * qk + 2 * vw:].astype(BF16)
    const = lambda i: (0, 0)
    row = lambda i: (i, 0)
    return pl.pallas_call(
        functools.partial(_gla_in_kernel, qk=qk, vw=vw),
        grid=(t // tm,),
        in_specs=[
            pl.BlockSpec((tm, d), row),
            pl.BlockSpec((1, d), const),
            pl.BlockSpec(w_main.shape, const),
            pl.BlockSpec(w_a1.shape, const),
            pl.BlockSpec(w_a2.shape, const),
            pl.BlockSpec((1, qk), const),
        ],
        out_specs=[
            pl.BlockSpec((tm, qk), row),
            pl.BlockSpec((tm, qk), row),
            pl.BlockSpec((tm, vw), row),
            pl.BlockSpec((tm, vw), row),
            pl.BlockSpec((tm, qk), row),
        ],
        out_shape=[
            jax.ShapeDtypeStruct((t, qk), BF16),
            jax.ShapeDtypeStruct((t, qk), BF16),
            jax.ShapeDtypeStruct((t, vw), BF16),
            jax.ShapeDtypeStruct((t, vw), BF16),
            jax.ShapeDtypeStruct((t, qk), F32),
        ],
        compiler_params=_cparams(("arbitrary",)),
        name="gla_in",
    )(h, gain.reshape(1, d), w_main, w_a1, w_a2.astype(BF16), b_a.reshape(1, qk))


def _gla_core_kernel(q_ref, k_ref, v_ref, r_ref, la_ref, gn_ref, o_ref, st_ref, *, n_chunks, heads, dk, dv, scale):
    c_len = GLA_CHUNK

    @pl.when(pl.program_id(1) == 0)
    def _():
        st_ref[...] = jnp.zeros_like(st_ref)

    ri = lax.broadcasted_iota(jnp.int32, (c_len, c_len), 0)
    ci = lax.broadcasted_iota(jnp.int32, (c_len, c_len), 1)
    causal = ri >= ci
    ltri = jnp.where(causal, 1.0, 0.0).astype(BF16)

    for c in range(n_chunks):
        sl = pl.ds(c * c_len, c_len)
        for hd in range(heads):
            ks_ = slice(hd * dk, (hd + 1) * dk)
            vs_ = slice(hd * dv, (hd + 1) * dv)
            g_hi, g_lo = _split_bf16(la_ref[sl, ks_])
            cum = _dot(ltri, g_hi) + _dot(ltri, g_lo)
            last = cum[c_len - 1:c_len, :]
            mid = cum[c_len // 2 - 1:c_len // 2, :]
            qf = q_ref[sl, ks_].astype(F32) * scale
            kf = k_ref[sl, ks_].astype(F32)
            vb = v_ref[sl, vs_]
            qs = (qf * jnp.exp(cum - mid)).astype(BF16)
            ks = (kf * jnp.exp(mid - cum)).astype(BF16)
            scores = jnp.where(causal, _dot_nt(qs, ks), 0.0)
            o = _dot(scores.astype(BF16), vb)
            st = st_ref[hd]
            qd = (qf * jnp.exp(cum)).astype(BF16)
            o = o + _dot_nt(qd, st.astype(BF16))
            kd = (kf * jnp.exp(last - cum)).astype(BF16)
            st_ref[hd] = st * jnp.exp(last) + _dot_tn(vb, kd)
            on = _rms_unit(o) * gn_ref[hd:hd + 1, :]
            rr = r_ref[sl, vs_].astype(F32)
            o_ref[sl, vs_] = (on * (rr * _sigmoid(rr))).astype(BF16)


def _gla_core(q, k, v, r, la, out_norm, batch, seq, lc):
    t, qk = q.shape
    vw = v.shape[1]
    heads = GLA_HEADS
    dk, dv = qk // heads, vw // heads
    nj = seq // lc
    tok = lambda b, j: (b * nj + j, 0)
    return pl.pallas_call(
        functools.partial(_gla_core_kernel, n_chunks=lc // GLA_CHUNK, heads=heads, dk=dk, dv=dv, scale=dk ** -0.5),
        grid=(batch, nj),
        in_specs=[
            pl.BlockSpec((lc, qk), tok),
            pl.BlockSpec((lc, qk), tok),
            pl.BlockSpec((lc, vw), tok),
            pl.BlockSpec((lc, vw), tok),
            pl.BlockSpec((lc, qk), tok),
            pl.BlockSpec((heads, dv), lambda b, j: (0, 0)),
        ],
        out_specs=pl.BlockSpec((lc, vw), tok),
        out_shape=jax.ShapeDtypeStruct((t, vw), BF16),
        scratch_shapes=[pltpu.VMEM((heads, dv, dk), F32)],
        compiler_params=_cparams(("arbitrary", "arbitrary")),
        name="gla_core",
    )(q, k, v, r, la, out_norm)


def _post_kernel(a_ref, w_ref, h_ref, g_ref, wr_ref, br_ref,
                 hn_ref, xr_ref, eidx_ref, gate_ref, rank_ref, cnt_ref, base_ref, *, tm, sub, d):
    @pl.when(pl.program_id(0) == 0)
    def _():
        base_ref[...] = jnp.zeros_like(base_ref)

    eg = MOE_EXPERTS_PER_GROUP
    ti = lax.broadcasted_iota(jnp.int32, (sub, sub), 0)
    tj = lax.broadcasted_iota(jnp.int32, (sub, sub), 1)
    upper = jnp.where(ti <= tj, 1.0, 0.0).astype(BF16)
    row_g = lax.broadcasted_iota(jnp.int32, (MOE_GROUPS, sub), 0)
    row_e = lax.broadcasted_iota(jnp.int32, (eg, sub), 0)
    row_x = lax.broadcasted_iota(jnp.int32, (MOE_EXPERTS, sub), 0)
    base = base_ref[:, 0:1]

    for u in range(tm // sub):
        rows = slice(u * sub, (u + 1) * sub)
        hn = h_ref[rows, :] + _dot(a_ref[rows, :], w_ref[...])
        hn_ref[rows, :] = hn
        xn = _rms_unit(hn) * g_ref[...]
        xr_ref[rows, :] = xn

        x_hi, x_lo = _split_bf16(xn)
        r_hi = _dot(x_hi, wr_ref[...])
        r_lo = _dot(x_lo, wr_ref[...])
        lg_t = (r_hi[:, 0:LANES] + r_hi[:, LANES:]) + (r_lo[:, 0:LANES] + r_lo[:, LANES:])
        lg = lg_t.T[0:ROUTER_ROWS] + br_ref[...]
        lc = lg[0:MOE_GROUPS]
        m_c = jnp.max(lc, axis=0, keepdims=True)
        pc_top = 1.0 / jnp.sum(jnp.exp(lc - m_c), axis=0, keepdims=True)
        g_idx = jnp.min(jnp.where(lc == m_c, row_g, MOE_GROUPS), axis=0, keepdims=True)
        lf = jnp.zeros((eg, sub), F32)
        for g in range(MOE_GROUPS):
            lf = lf + jnp.where(g_idx == g, lg[SUBLANES + g * eg:SUBLANES + (g + 1) * eg], 0.0)
        ef = jnp.exp(lf - jnp.max(lf, axis=0, keepdims=True))
        pf = ef / jnp.sum(ef, axis=0, keepdims=True)
        v1 = jnp.max(pf, axis=0, keepdims=True)
        j1 = jnp.min(jnp.where(pf == v1, row_e, eg), axis=0, keepdims=True)
        pf2 = jnp.where(row_e == j1, -1.0, pf)
        v2 = jnp.max(pf2, axis=0, keepdims=True)
        j2 = jnp.min(jnp.where(pf2 == v2, row_e, eg), axis=0, keepdims=True)
        denom = v1 + v2
        e1 = g_idx * eg + j1
        e2 = g_idx * eg + j2
        eidx_ref[0:1, rows] = e1
        eidx_ref[1:2, rows] = e2
        gate_ref[0:1, rows] = pc_top * v1 / denom
        gate_ref[1:2, rows] = pc_top * v2 / denom

        oh1 = row_x == e1
        oh2 = row_x == e2
        pre1 = _dot(jnp.where(oh1, 1.0, 0.0).astype(BF16), upper)
        pre2 = _dot(jnp.where(oh2, 1.0, 0.0).astype(BF16), upper)
        cnt1 = pre1[:, sub - 1:sub]
        cnt2 = pre2[:, sub - 1:sub]
        rank1 = jnp.sum(jnp.where(oh1, base + pre1 - 1.0, 0.0), axis=0, keepdims=True)
        rank2 = jnp.sum(jnp.where(oh2, base + cnt1 + pre2 - 1.0, 0.0), axis=0, keepdims=True)
        rank_ref[0:1, rows] = rank1.astype(jnp.int32)
        rank_ref[1:2, rows] = rank2.astype(jnp.int32)
        base = base + (cnt1 + cnt2)

    new_base = jnp.broadcast_to(base, base_ref.shape)
    base_ref[...] = new_base
    cnt_ref[...] = new_base


def _post(a, w_out, h, gain, w_coarse, b_coarse, w_fine, b_fine, tm, sub):
    t, d = h.shape
    wr = jnp.concatenate([w_coarse, jnp.zeros((d, SUBLANES - MOE_GROUPS), F32), w_fine,
                          jnp.zeros((d, LANES - ROUTER_ROWS), F32)], axis=1)
    wr = jnp.concatenate(_split_bf16(wr), axis=1)
    br = jnp.zeros((ROUTER_ROWS, 1), F32)
    br = br.at[0:MOE_GROUPS, 0].set(b_coarse).at[SUBLANES:, 0].set(b_fine)
    const = lambda i: (0, 0)
    row = lambda i: (i, 0)
    col = lambda i: (0, i)
    return pl.pallas_call(
        functools.partial(_post_kernel, tm=tm, sub=sub, d=d),
        grid=(t // tm,),
        in_specs=[
            pl.BlockSpec((tm, a.shape[1]), row),
            pl.BlockSpec(w_out.shape, const),
            pl.BlockSpec((tm, d), row),
            pl.BlockSpec((1, d), const),
            pl.BlockSpec((d, 2 * LANES), const),
            pl.BlockSpec((ROUTER_ROWS, 1), const),
        ],
        out_specs=[
            pl.BlockSpec((tm, d), row),
            pl.BlockSpec((tm, d), row),
            pl.BlockSpec((MOE_TOP_K, tm), col),
            pl.BlockSpec((MOE_TOP_K, tm), col),
            pl.BlockSpec((MOE_TOP_K, tm), col),
            pl.BlockSpec((MOE_EXPERTS, LANES), const),
        ],
        out_shape=[
            jax.ShapeDtypeStruct((t, d), F32),
            jax.ShapeDtypeStruct((t, d), F32),
            jax.ShapeDtypeStruct((MOE_TOP_K, t), jnp.int32),
            jax.ShapeDtypeStruct((MOE_TOP_K, t), F32),
            jax.ShapeDtypeStruct((MOE_TOP_K, t), jnp.int32),
            jax.ShapeDtypeStruct((MOE_EXPERTS, LANES), F32),
        ],
        scratch_shapes=[pltpu.VMEM((MOE_EXPERTS, LANES), F32)],
        compiler_params=_cparams(("arbitrary",)),
        name="post",
    )(a, w_out.astype(BF16), h, gain.reshape(1, d), wr, br)


def _row(ref, idx):
    return ref.at[pl.ds(idx, 1), :]


def _block_rows(ref, blk):
    return ref.at[pl.ds(pl.multiple_of(blk * MOE_BLOCK, MOE_BLOCK), MOE_BLOCK), :]


def _dispatch_kernel(pe_ref, nu_ref, dest_ref, xr_ref, xbuf_ref, zero_ref, sem, zsem, *, tm, n_blocks):
    @pl.when(pl.program_id(0) == 0)
    def _():
        zero_ref[...] = jnp.zeros_like(zero_ref)

        def last_block(e):
            return pe_ref[e] // MOE_BLOCK - 1

        def has_rows(e):
            return pe_ref[e] > jnp.where(e == 0, 0, pe_ref[jnp.maximum(e - 1, 0)])

        def zero_block(blk):
            return pltpu.make_async_copy(zero_ref, _block_rows(xbuf_ref, blk), zsem)

        def start_e(e, carry):
            @pl.when(has_rows(e))
            def _():
                zero_block(last_block(e)).start()
            return carry

        def wait_e(e, carry):
            @pl.when(has_rows(e))
            def _():
                zero_block(last_block(e)).wait()
            return carry

        def start_b(blk, carry):
            zero_block(blk).start()
            return carry

        def wait_b(blk, carry):
            zero_block(blk).wait()
            return carry

        lax.fori_loop(0, MOE_EXPERTS, start_e, 0)
        lax.fori_loop(nu_ref[0], n_blocks, start_b, 0)
        lax.fori_loop(0, MOE_EXPERTS, wait_e, 0)
        lax.fori_loop(nu_ref[0], n_blocks, wait_b, 0)

    def issue(tk, carry):
        for s in range(MOE_TOP_K):
            pltpu.make_async_copy(_row(xr_ref, tk), _row(xbuf_ref, dest_ref[0, s, tk]), sem).start(priority=s)
        return carry

    lax.fori_loop(0, tm, issue, 0, unroll=8)
    for s in range(MOE_TOP_K):
        pltpu.make_async_copy(xr_ref, xbuf_ref.at[pl.ds(0, tm), :], sem).wait()


def _dispatch(xr, dest3, pad_end, n_used, n_blocks, tm):
    nt = dest3.shape[0]
    d = xr.shape[1]
    grid_spec = pltpu.PrefetchScalarGridSpec(
        num_scalar_prefetch=2,
        grid=(nt,),
        in_specs=[
            pl.BlockSpec((1, MOE_TOP_K, tm), lambda i, pe, nu: (i, 0, 0), memory_space=pltpu.SMEM),
            pl.BlockSpec((tm, d), lambda i, pe, nu: (i, 0)),
        ],
        out_specs=pl.BlockSpec(memory_space=pl.ANY),
        scratch_shapes=[pltpu.VMEM((MOE_BLOCK, d), F32), pltpu.SemaphoreType.DMA, pltpu.SemaphoreType.DMA],
    )
    return pl.pallas_call(
        functools.partial(_dispatch_kernel, tm=tm, n_blocks=n_blocks),
        grid_spec=grid_spec,
        out_shape=jax.ShapeDtypeStruct((n_blocks * MOE_BLOCK, d), F32),
        compiler_params=_cparams(("arbitrary",)),
        name="dispatch",
    )(pad_end, n_used, dest3, xr)


def _experts_kernel(be_ref, nu_ref, x_ref, wg_ref, wu_ref, wd_ref, y_ref, wgb_ref, wub_ref, wdb_ref):
    b = pl.program_id(0)
    live = b < nu_ref[0]
    new_expert = jnp.logical_or(b == 0, be_ref[b] != be_ref[jnp.maximum(b - 1, 0)])

    @pl.when(jnp.logical_and(live, new_expert))
    def _():
        wgb_ref[...] = wg_ref[...].astype(BF16)
        wub_ref[...] = wu_ref[...].astype(BF16)
        wdb_ref[...] = wd_ref[...].astype(BF16)

    @pl.when(live)
    def _():
        x2 = x_ref[...].astype(BF16)
        gp = _dot(x2, wgb_ref[...])
        up = _dot(x2, wub_ref[...])
        hid = (gp * _sigmoid(gp) * up).astype(BF16)
        y_ref[...] = _dot(hid, wdb_ref[...])


def _experts(xbuf, block_e, n_used, w_gate, w_up, w_down, layer):
    n_blocks = block_e.shape[0]
    d, dff = w_gate.shape[2], w_gate.shape[3]
    xmap = lambda b, be, nu: (jnp.minimum(b, nu[0] - 1), 0)
    wmap = lambda b, be, nu: (layer, be[b], 0, 0)
    grid_spec = pltpu.PrefetchScalarGridSpec(
        num_scalar_prefetch=2,
        grid=(n_blocks,),
        in_specs=[
            pl.BlockSpec((MOE_BLOCK, d), xmap),
            pl.BlockSpec((None, None, d, dff), wmap),
            pl.BlockSpec((None, None, d, dff), wmap),
            pl.BlockSpec((None, None, dff, d), wmap),
        ],
        out_specs=pl.BlockSpec((MOE_BLOCK, d), xmap),
        scratch_shapes=[pltpu.VMEM((d, dff), BF16), pltpu.VMEM((d, dff), BF16), pltpu.VMEM((dff, d), BF16)],
    )
    return pl.pallas_call(
        _experts_kernel,
        grid_spec=grid_spec,
        out_shape=jax.ShapeDtypeStruct(xbuf.shape, F32),
        input_output_aliases={2: 0},
        compiler_params=_cparams(("arbitrary",)),
        name="experts",
    )(block_e, n_used, xbuf, w_gate, w_up, w_down)


def _gather_combine(dcur_ref, dnxt_ref, ybuf_ref, h_ref, gate_ref, y_ref, sems, *, tm, d):
    i = pl.program_id(0)

    def start(dref, slot):
        def issue(tk, carry):
            for s in range(MOE_TOP_K):
                pltpu.make_async_copy(_row(ybuf_ref, dref[0, s, tk]), _row(y_ref.at[slot, s], tk),
                                      sems.at[slot]).start(priority=s)
            return carry

        lax.fori_loop(0, tm, issue, 0, unroll=8)

    @pl.when(i == 0)
    def _():
        start(dcur_ref, 0)

    @pl.when(i + 1 < pl.num_programs(0))
    def _():
        start(dnxt_ref, (i + 1) % 2)

    slot = i % 2
    for s in range(MOE_TOP_K):
        pltpu.make_async_copy(ybuf_ref.at[pl.ds(0, tm), :], y_ref.at[slot, s], sems.at[slot]).wait()
    return h_ref[...] + (y_ref[slot, 0] * gate_ref[:, 0:1] + y_ref[slot, 1] * gate_ref[:, 1:2])


def _combine_specs(tm, d, nt):
    in_specs = [
        pl.BlockSpec((1, MOE_TOP_K, tm), lambda i: (i, 0, 0), memory_space=pltpu.SMEM),
        pl.BlockSpec((1, MOE_TOP_K, tm), lambda i: (jnp.minimum(i + 1, nt - 1), 0, 0), memory_space=pltpu.SMEM),
        pl.BlockSpec(memory_space=pl.ANY),
        pl.BlockSpec((tm, d), lambda i: (i, 0)),
        pl.BlockSpec((tm, MOE_TOP_K), lambda i: (i, 0)),
    ]
    scratch = [pltpu.VMEM((2, MOE_TOP_K, tm, d), F32), pltpu.SemaphoreType.DMA((2,))]
    return in_specs, scratch


def _final_kernel(dcur_ref, dnxt_ref, ybuf_ref, h_ref, gate_ref, fg_ref, o_ref, y_ref, sems, *, tm, d):
    out = _gather_combine(dcur_ref, dnxt_ref, ybuf_ref, h_ref, gate_ref, y_ref, sems, tm=tm, d=d)
    o_ref[...] = _rms_unit(out) * fg_ref[...]


def _final(moe, final_gain, tm):
    ybuf, dest3, gate_t, h = moe
    t, d = h.shape
    nt = dest3.shape[0]
    in_specs, scratch = _combine_specs(tm, d, nt)
    return pl.pallas_call(
        functools.partial(_final_kernel, tm=tm, d=d),
        grid=(nt,),
        in_specs=in_specs + [pl.BlockSpec((1, d), lambda i: (0, 0))],
        out_specs=pl.BlockSpec((tm, d), lambda i: (i, 0)),
        out_shape=jax.ShapeDtypeStruct((t, d), F32),
        scratch_shapes=scratch,
        compiler_params=_cparams(("arbitrary",)),
        name="final",
    )(dest3, dest3, ybuf, h, gate_t, final_gain.reshape(1, d))


def _moe_experts(xr, eidx, gate, rank, counts, h, w_gate, w_up, w_down, layer, tm_row):
    t = h.shape[0]
    n_slot = t * MOE_TOP_K
    n_blocks = -(-n_slot // MOE_BLOCK) + MOE_EXPERTS
    cnt = counts[:, 0].astype(jnp.int32)
    padded = (cnt + MOE_BLOCK - 1) // MOE_BLOCK * MOE_BLOCK
    pad_end = jnp.cumsum(padded)
    pad_start = pad_end - padded
    dest = rank
    for e in range(MOE_EXPERTS):
        dest = dest + jnp.where(eidx == e, pad_start[e], 0)
    block_start = jnp.arange(n_blocks, dtype=jnp.int32) * MOE_BLOCK
    block_e = jnp.minimum(jnp.sum((pad_end[None, :] <= block_start[:, None]).astype(jnp.int32), axis=1),
                          MOE_EXPERTS - 1)
    n_used = (pad_end[-1:] // MOE_BLOCK).astype(jnp.int32)
    dest3 = dest.reshape(MOE_TOP_K, t // tm_row, tm_row).transpose(1, 0, 2)
    xbuf = _dispatch(xr, dest3, pad_end.astype(jnp.int32), n_used, n_blocks, tm_row)
    ybuf = _experts(xbuf, block_e, n_used, w_gate, w_up, w_down, layer)
    return ybuf, dest3, gate.T, h


def _fox_in_kernel(dcur_ref, dnxt_ref, ybuf_ref, h_ref, gate_ref, gkv_ref, gq_ref, wkv_ref, wf_ref, bf_ref, wq_ref,
                   place_ref, hn_ref, k_ref, v_ref, q_ref, og_ref, ka_ref, y_ref, sems, carry_ref,
                   *, tm, d, width, tiles_per_seq, qscale):
    i = pl.program_id(0)

    @pl.when(i % tiles_per_seq == 0)
    def _():
        carry_ref[...] = jnp.zeros_like(carry_ref)

    hn_ref[...] = _gather_combine(dcur_ref, dnxt_ref, ybuf_ref, h_ref, gate_ref, y_ref, sems, tm=tm, d=d)
    y = _rms_unit(hn_ref[...])
    xkv = (y * gkv_ref[...]).astype(BF16)
    xq = (y * gq_ref[...]).astype(BF16)
    k_ref[...] = _dot(xkv, wkv_ref[:, 0:width]).astype(BF16)
    v_ref[...] = _dot(xkv, wkv_ref[:, width:2 * width]).astype(BF16)
    q_ref[...] = (_dot(xq, wq_ref[:, 0:width]) * qscale).astype(BF16)
    og_ref[...] = _dot(xq, wq_ref[:, width:2 * width]).astype(BF16)

    log_f = _log_sigmoid(_dot(xkv, wf_ref[...]) + bf_ref[...])
    ri = lax.broadcasted_iota(jnp.int32, (tm, tm), 0)
    ci = lax.broadcasted_iota(jnp.int32, (tm, tm), 1)
    ltri = jnp.where(ri >= ci, 1.0, 0.0).astype(BF16)
    f_hi, f_mid = _split_bf16(log_f)
    f_lo = (log_f - f_hi.astype(F32) - f_mid.astype(F32)).astype(BF16)
    c = carry_ref[0:1, :] + (_dot(ltri, f_hi) + (_dot(ltri, f_mid) + _dot(ltri, f_lo)))
    carry_ref[0:1, :] = c[tm - 1:tm, :]
    n_hi, n_mid = _split_bf16(-c)
    n_lo = (-c - n_hi.astype(F32) - n_mid.astype(F32)).astype(BF16)
    ka_ref[...] = _dot(jnp.concatenate([n_hi, n_mid, n_lo], axis=1), place_ref[...]).astype(BF16)


ATTN_BIAS_LANES = 3


def _bias_lane0(hh, dh):
    return dh if hh == 0 else 0


def _fox_in(moe, kv_gain, q_gain, kv_w, kv_b_f, w_in, seq, tm):
    ybuf, dest3, gate_t, h = moe
    t, d = h.shape
    nt = dest3.shape[0]
    width = w_in.shape[1] // 2
    heads = kv_w.shape[1] - 2 * width
    dh = width // heads
    place = np.zeros((ATTN_BIAS_LANES * heads, width), np.float32)
    for part in range(ATTN_BIAS_LANES):
        for hd in range(heads):
            place[part * heads + hd, (hd // 2) * LANES + _bias_lane0(hd % 2, dh) + part] = 1.0
    const = lambda i: (0, 0)
    row = lambda i: (i, 0)
    big = jax.ShapeDtypeStruct((t, width), BF16)
    comb_specs, comb_scratch = _combine_specs(tm, d, nt)
    return pl.pallas_call(
        functools.partial(_fox_in_kernel, tm=tm, d=d, width=width, tiles_per_seq=seq // tm, qscale=dh ** -0.5),
        grid=(nt,),
        in_specs=comb_specs + [
            pl.BlockSpec((1, d), const),
            pl.BlockSpec((1, d), const),
            pl.BlockSpec((d, 2 * width), const),
            pl.BlockSpec((d, heads), const),
            pl.BlockSpec((1, heads), const),
            pl.BlockSpec((d, 2 * width), const),
            pl.BlockSpec(place.shape, const),
        ],
        out_specs=[pl.BlockSpec((tm, d), row)] + [pl.BlockSpec((tm, width), row)] * 5,
        out_shape=[jax.ShapeDtypeStruct((t, d), F32)] + [big] * 5,
        scratch_shapes=comb_scratch + [pltpu.VMEM((SUBLANES, heads), F32)],
        compiler_params=_cparams(("arbitrary",)),
        name="fox_in",
    )(dest3, dest3, ybuf, h, gate_t, kv_gain.reshape(1, d), q_gain.reshape(1, d), kv_w[:, :2 * width].astype(BF16),
      kv_w[:, 2 * width:].astype(BF16), kv_b_f.reshape(1, heads), w_in.astype(BF16), jnp.asarray(place, BF16))


ATTN_KV_CHUNK = 512


def _attn_kernel(q_ref, k_ref, v_ref, og_ref, kb_ref, o_ref, qa_ref, ka_ref, va_ref, s_ref, *, seq, tq, dh):
    lane = lax.broadcasted_iota(jnp.int32, (seq, LANES), 1)
    for hh in range(2):
        a0 = _bias_lane0(hh, dh)
        own = (lane < dh) if hh == 0 else (lane >= dh)
        ones_q = jnp.where((lane >= a0) & (lane < a0 + ATTN_BIAS_LANES), 1.0, 0.0).astype(BF16)
        qa_ref[hh] = jnp.where(own, q_ref[...], ones_q)
        ka_ref[hh] = jnp.where(own, k_ref[...], kb_ref[...])
        va_ref[hh] = jnp.where(own, v_ref[...], jnp.where(lane == a0, 1.0, 0.0).astype(BF16))

    ri = lax.broadcasted_iota(jnp.int32, (tq, tq), 0)
    ci = lax.broadcasted_iota(jnp.int32, (tq, tq), 1)
    causal = ri >= ci
    lane_q = lax.broadcasted_iota(jnp.int32, (tq, LANES), 1)

    def fold_max(mx, s):
        for g in range(s.shape[1] // LANES):
            mx = jnp.maximum(mx, s[:, g * LANES:(g + 1) * LANES])
        return mx

    for qi in range(seq // tq):
        rows = slice(qi * tq, (qi + 1) * tq)
        past = [(c0, min(c0 + ATTN_KV_CHUNK, qi * tq)) for c0 in range(0, qi * tq, ATTN_KV_CHUNK)]
        outs = []
        for hh in range(2):
            qa = qa_ref[hh, rows, :]
            mx = jnp.full((tq, LANES), -jnp.inf, F32)
            for c0, c1 in past:
                s = _dot_nt(qa, ka_ref[hh, c0:c1, :])
                s_ref[hh, :, c0:c1] = s
                mx = fold_max(mx, s)
            s = jnp.where(causal, _dot_nt(qa, ka_ref[hh, rows, :]), -jnp.inf)
            s_ref[hh, :, rows] = s
            mx = fold_max(mx, s)
            m = jnp.max(mx, axis=1, keepdims=True)
            acc = jnp.zeros((tq, LANES), F32)
            for c0, c1 in past + [(qi * tq, (qi + 1) * tq)]:
                p = jnp.exp(s_ref[hh, :, c0:c1] - m).astype(BF16)
                acc = acc + _dot(p, va_ref[hh, c0:c1, :])
            a0 = _bias_lane0(hh, dh)
            outs.append(acc / acc[:, a0:a0 + 1])
        o = jnp.where(lane_q < dh, outs[0], outs[1])
        o_ref[rows, :] = (o * _sigmoid(og_ref[rows, :].astype(F32))).astype(BF16)


def _attn(q, k, v, og, kb, heads, batch, seq, tq):
    t, width = q.shape
    dh = width // heads
    assert 2 * dh == LANES, "two heads per 128-lane block"
    blk = pl.BlockSpec((seq, LANES), lambda b, hp: (b, hp))
    return pl.pallas_call(
        functools.partial(_attn_kernel, seq=seq, tq=tq, dh=dh),
        grid=(batch, heads // 2),
        in_specs=[blk] * 5,
        out_specs=blk,
        out_shape=jax.ShapeDtypeStruct((t, width), BF16),
        scratch_shapes=[pltpu.VMEM((2, seq, LANES), BF16)] * 3 + [pltpu.VMEM((2, tq, seq), F32)],
        compiler_params=_cparams(("arbitrary", "arbitrary")),
        name="attn",
    )(q, k, v, og, kb)


def _tile(n, pref):
    while n % pref:
        pref //= 2
    return pref


def kernel(x, gla_norm, gla_w_in, gla_w_a2, gla_b_a, gla_out_norm, gla_w_out, kv_norm, kv_w, kv_b_f, fox_norm,
           fox_w_in, fox_w_out, ffn_norm, router_coarse_w, router_coarse_b, router_fine_w, router_fine_b,
           expert_w_gate, expert_w_up, expert_w_down, final_norm):
    batch, seq, d = x.shape
    t = batch * seq
    assert gla_norm.shape[0] == 1 and ffn_norm.shape[0] == 2, "trunk is GLA layer + FoX layer"
    tm = _tile(seq, 512)

    def ffn(a, w_out, h, layer):
        h, xr, eidx, gate, rank, counts = _post(a, w_out, h, ffn_norm[layer], router_coarse_w[layer],
                                                router_coarse_b[layer], router_fine_w[layer],
                                                router_fine_b[layer], _tile(seq, 1024), tm)
        return _moe_experts(xr, eidx, gate, rank, counts, h, expert_w_gate, expert_w_up, expert_w_down, layer, tm)

    h = x.reshape(t, d)
    q, k, v, r, la = _gla_in(h, gla_norm[0], gla_w_in[0], gla_w_a2[0], gla_b_a[0], tm)
    a = _gla_core(q, k, v, r, la, gla_out_norm[0], batch, seq, tm)
    moe = ffn(a, gla_w_out[0], h, 0)
    h, kk, vv, qq, og, kb = _fox_in(moe, kv_norm, fox_norm[0], kv_w, kv_b_f, fox_w_in[0], seq, tm)
    a = _attn(qq, kk, vv, og, kb, kv_b_f.shape[0], batch, seq, _tile(seq, 256))
    moe = ffn(a, fox_w_out[0], h, 1)
    return _final(moe, final_norm, tm).reshape(batch, seq, d)
```

```python
import functools

import jax
import jax.numpy as jnp
import numpy as np
from jax import lax
from jax.experimental import pallas as pl
from jax.experimental.pallas import tpu as pltpu

F32 = jnp.float32
BF16 = jnp.bfloat16

RMS_EPS = 1e-6
GLA_HEADS = 4
GLA_GATE_RANK = 16
GLA_GATE_TAU = 16.0
FOX_HEADS = 16
MOE_GROUPS = 4
MOE_EXPERTS_PER_GROUP = 8
MOE_EXPERTS = MOE_GROUPS * MOE_EXPERTS_PER_GROUP
MOE_TOP_K = 2
MOE_BLOCK = 256

LANES = 128
SUBLANES = 8
VMEM_LIMIT = 56 * 1024 * 1024

GLA_CHUNK = 128
ROUTER_ROWS = 40


def _cparams(sem):
    return pltpu.CompilerParams(dimension_semantics=sem, vmem_limit_bytes=VMEM_LIMIT)


def _rms_unit(x):
    return x * lax.rsqrt(jnp.mean(x * x, axis=-1, keepdims=True) + RMS_EPS)


def _log_sigmoid(z):
    return jnp.minimum(z, 0.0) - jnp.log1p(jnp.exp(-jnp.abs(z)))


def _sigmoid(z):
    return 1.0 / (1.0 + jnp.exp(-z))


def _dot(a, b):
    return jnp.dot(a, b, preferred_element_type=F32)


def _dot_nt(a, b):
    return lax.dot_general(a, b, (((1,), (1,)), ((), ())), preferred_element_type=F32)


def _dot_tn(a, b):
    return lax.dot_general(a, b, (((0,), (0,)), ((), ())), preferred_element_type=F32)


def _split_bf16(x):
    hi = x.astype(BF16)
    lo = (x - hi.astype(F32)).astype(BF16)
    return hi, lo


def _gla_in_kernel(h_ref, g_ref, w_ref, wa1_ref, wa2_ref, ba_ref, q_ref, k_ref, v_ref, r_ref, la_ref, *, qk, vw):
    xn = (_rms_unit(h_ref[...]) * g_ref[...]).astype(BF16)
    q_ref[...] = _dot(xn, w_ref[:, 0:qk]).astype(BF16)
    k_ref[...] = _dot(xn, w_ref[:, qk:2 * qk]).astype(BF16)
    v_ref[...] = _dot(xn, w_ref[:, 2 * qk:2 * qk + vw]).astype(BF16)
    r_ref[...] = _dot(xn, w_ref[:, 2 * qk + vw:2 * qk + 2 * vw]).astype(BF16)
    a = _dot(xn, wa1_ref[...])
    z = _dot(a.astype(BF16), wa2_ref[...]) + ba_ref[...]
    la_ref[...] = _log_sigmoid(z) * (1.0 / GLA_GATE_TAU)


def _gla_in(h, gain, w_in, w_a2, b_a, tm):
    t, d = h.shape
    qk = w_a2.shape[1]
    vw = (w_in.shape[1] - 2 * qk - GLA_GATE_RANK) // 2
    w_main = w_in[:, :2 * qk + 2 * vw].astype(BF16)
    w_a1 = w_in[:, 2 * qk + 2 * vw:].astype(BF16)
    const = lambda i: (0, 0)
    row = lambda i: (i, 0)
    return pl.pallas_call(
        functools.partial(_gla_in_kernel, qk=qk, vw=vw),
        grid=(t // tm,),
        in_specs=[
            pl.BlockSpec((tm, d), row),
            pl.BlockSpec((1, d), const),
            pl.BlockSpec(w_main.shape, const),
            pl.BlockSpec(w_a1.shape, const),
            pl.BlockSpec(w_a2.shape, const),
            pl.BlockSpec((1, qk), const),
        ],
        out_specs=[
            pl.BlockSpec((tm, qk), row),
            pl.BlockSpec((tm, qk), row),
            pl.BlockSpec((tm, vw), row),
            pl.BlockSpec((tm, vw), row),
            pl.BlockSpec((tm, qk), row),
        ],
        out_shape=[
            jax.ShapeDtypeStruct((t, qk), BF16),
            jax.ShapeDtypeStruct((t, qk), BF16),
            jax.ShapeDtypeStruct((t, vw), BF16),
            jax.ShapeDtypeStruct((t, vw), BF16),
            jax.ShapeDtypeStruct((t, qk), F32),
        ],
        compiler_params=_cparams(("arbitrary",)),
        name="gla_in",
    )(h, gain.reshape(1, d), w_main, w_a1, w_a2.astype(BF16), b_a.reshape(1, qk))


def _gla_core_kernel(q_ref, k_ref, v_ref, r_ref, la_ref, gn_ref, o_ref, st_ref, *, n_chunks, heads, dk, dv, scale):
    c_len = GLA_CHUNK

    @pl.when(pl.program_id(1) == 0)
    def _():
        st_ref[...] = jnp.zeros_like(st_ref)

    ri = lax.broadcasted_iota(jnp.int32, (c_len, c_len), 0)
    ci = lax.broadcasted_iota(jnp.int32, (c_len, c_len), 1)
    causal = ri >= ci
    ltri = jnp.where(causal, 1.0, 0.0).astype(BF16)

    for c in range(n_chunks):
        sl = pl.ds(c * c_len, c_len)
        for hd in range(heads):
            ks_ = slice(hd * dk, (hd + 1) * dk)
            vs_ = slice(hd * dv, (hd + 1) * dv)
            g_hi, g_lo = _split_bf16(la_ref[sl, ks_])
            cum = _dot(ltri, g_hi) + _dot(ltri, g_lo)
            last = cum[c_len - 1:c_len, :]
            mid = cum[c_len // 2 - 1:c_len // 2, :]
            qf = q_ref[sl, ks_].astype(F32) * scale
            kf = k_ref[sl, ks_].astype(F32)
            vb = v_ref[sl, vs_]
            qs = (qf * jnp.exp(cum - mid)).astype(BF16)
            ks = (kf * jnp.exp(mid - cum)).astype(BF16)
            scores = jnp.where(causal, _dot_nt(qs, ks), 0.0)
            o = _dot(scores.astype(BF16), vb)
            st = st_ref[hd]
            qd = (qf * jnp.exp(cum)).astype(BF16)
            o = o + _dot_nt(qd, st.astype(BF16))
            kd = (kf * jnp.exp(last - cum)).astype(BF16)
            st_ref[hd] = st * jnp.exp(last) + _dot_tn(vb, kd)
            on = _rms_unit(o) * gn_ref[hd:hd + 1, :]
            rr = r_ref[sl, vs_].astype(F32)
            o_ref[sl, vs_] = (on * (rr * _sigmoid(rr))).astype(BF16)


def _gla_core(q, k, v, r, la, out_norm, batch, seq, lc):
    t, qk = q.shape
    vw = v.shape[1]
    heads = GLA_HEADS
    dk, dv = qk // heads, vw // heads
    nj = seq // lc
    tok = lambda b, j: (b * nj + j, 0)
    return pl.pallas_call(
        functools.partial(_gla_core_kernel, n_chunks=lc // GLA_CHUNK, heads=heads, dk=dk, dv=dv, scale=dk ** -0.5),
        grid=(batch, nj),
        in_specs=[
            pl.BlockSpec((lc, qk), tok),
            pl.BlockSpec((lc, qk), tok),
            pl.BlockSpec((lc, vw), tok),
            pl.BlockSpec((lc, vw), tok),
            pl.BlockSpec((lc, qk), tok),
            pl.BlockSpec((heads, dv), lambda b, j: (0, 0)),
        ],
        out_specs=pl.BlockSpec((lc, vw), tok),
        out_shape=jax.ShapeDtypeStruct((t, vw), BF16),
        scratch_shapes=[pltpu.VMEM((heads, dv, dk), F32)],
        compiler_params=_cparams(("arbitrary", "arbitrary")),
        name="gla_core",
    )(q, k, v, r, la, out_norm)


def _post_kernel(a_ref, w_ref, h_ref, g_ref, wr_ref, br_ref,
                 hn_ref, xr_ref, eidx_ref, gate_ref, rank_ref, cnt_ref, base_ref, *, tm, sub, d):
    @pl.when(pl.program_id(0) == 0)
    def _():
        base_ref[...] = jnp.zeros_like(base_ref)

    eg = MOE_EXPERTS_PER_GROUP
    ti = lax.broadcasted_iota(jnp.int32, (sub, sub), 0)
    tj = lax.broadcasted_iota(jnp.int32, (sub, sub), 1)
    upper = jnp.where(ti <= tj, 1.0, 0.0).astype(BF16)
    row_g = lax.broadcasted_iota(jnp.int32, (MOE_GROUPS, sub), 0)
    row_e = lax.broadcasted_iota(jnp.int32, (eg, sub), 0)
    row_x = lax.broadcasted_iota(jnp.int32, (MOE_EXPERTS, sub), 0)
    base = base_ref[:, 0:1]

    for u in range(tm // sub):
        rows = slice(u * sub, (u + 1) * sub)
        hn = h_ref[rows, :] + _dot(a_ref[rows, :], w_ref[...])
        hn_ref[rows, :] = hn
        xn = _rms_unit(hn) * g_ref[...]
        for c in range(d // LANES):
            xr_ref[pl.ds(u * sub * SUBLANES + c, sub, stride=SUBLANES), :] = xn[:, c * LANES:(c + 1) * LANES]

        x_hi, x_lo = _split_bf16(xn)
        r_hi = _dot(x_hi, wr_ref[...])
        r_lo = _dot(x_lo, wr_ref[...])
        lg_t = (r_hi[:, 0:LANES] + r_hi[:, LANES:]) + (r_lo[:, 0:LANES] + r_lo[:, LANES:])
        lg = lg_t.T[0:ROUTER_ROWS] + br_ref[...]
        lc = lg[0:MOE_GROUPS]
        m_c = jnp.max(lc, axis=0, keepdims=True)
        pc_top = 1.0 / jnp.sum(jnp.exp(lc - m_c), axis=0, keepdims=True)
        g_idx = jnp.min(jnp.where(lc == m_c, row_g, MOE_GROUPS), axis=0, keepdims=True)
        lf = jnp.zeros((eg, sub), F32)
        for g in range(MOE_GROUPS):
            lf = lf + jnp.where(g_idx == g, lg[SUBLANES + g * eg:SUBLANES + (g + 1) * eg], 0.0)
        ef = jnp.exp(lf - jnp.max(lf, axis=0, keepdims=True))
        pf = ef / jnp.sum(ef, axis=0, keepdims=True)
        v1 = jnp.max(pf, axis=0, keepdims=True)
        j1 = jnp.min(jnp.where(pf == v1, row_e, eg), axis=0, keepdims=True)
        pf2 = jnp.where(row_e == j1, -1.0, pf)
        v2 = jnp.max(pf2, axis=0, keepdims=True)
        j2 = jnp.min(jnp.where(pf2 == v2, row_e, eg), axis=0, keepdims=True)
        denom = v1 + v2
        e1 = g_idx * eg + j1
        e2 = g_idx * eg + j2
        eidx_ref[0:1, rows] = e1
        eidx_ref[1:2, rows] = e2
        gate_ref[0:1, rows] = pc_top * v1 / denom
        gate_ref[1:2, rows] = pc_top * v2 / denom

        oh1 = row_x == e1
        oh2 = row_x == e2
        pre1 = _dot(jnp.where(oh1, 1.0, 0.0).astype(BF16), upper)
        pre2 = _dot(jnp.where(oh2, 1.0, 0.0).astype(BF16), upper)
        cnt1 = pre1[:, sub - 1:sub]
        cnt2 = pre2[:, sub - 1:sub]
        rank1 = jnp.sum(jnp.where(oh1, base + pre1 - 1.0, 0.0), axis=0, keepdims=True)
        rank2 = jnp.sum(jnp.where(oh2, base + cnt1 + pre2 - 1.0, 0.0), axis=0, keepdims=True)
        rank_ref[0:1, rows] = rank1.astype(jnp.int32)
        rank_ref[1:2, rows] = rank2.astype(jnp.int32)
        base = base + (cnt1 + cnt2)

    new_base = jnp.broadcast_to(base, base_ref.shape)
    base_ref[...] = new_base
    cnt_ref[...] = new_base


def _post(a, w_out, h, gain, w_coarse, b_coarse, w_fine, b_fine, tm, sub):
    t, d = h.shape
    wr = jnp.concatenate([w_coarse, jnp.zeros((d, SUBLANES - MOE_GROUPS), F32), w_fine,
                          jnp.zeros((d, LANES - ROUTER_ROWS), F32)], axis=1)
    wr = jnp.concatenate(_split_bf16(wr), axis=1)
    br = jnp.zeros((ROUTER_ROWS, 1), F32)
    br = br.at[0:MOE_GROUPS, 0].set(b_coarse).at[SUBLANES:, 0].set(b_fine)
    const = lambda i: (0, 0)
    row = lambda i: (i, 0)
    col = lambda i: (0, i)
    return pl.pallas_call(
        functools.partial(_post_kernel, tm=tm, sub=sub, d=d),
        grid=(t // tm,),
        in_specs=[
            pl.BlockSpec((tm, a.shape[1]), row),
            pl.BlockSpec(w_out.shape, const),
            pl.BlockSpec((tm, d), row),
            pl.BlockSpec((1, d), const),
            pl.BlockSpec((d, 2 * LANES), const),
            pl.BlockSpec((ROUTER_ROWS, 1), const),
        ],
        out_specs=[
            pl.BlockSpec((tm, d), row),
            pl.BlockSpec((tm * SUBLANES, LANES), row),
            pl.BlockSpec((MOE_TOP_K, tm), col),
            pl.BlockSpec((MOE_TOP_K, tm), col),
            pl.BlockSpec((MOE_TOP_K, tm), col),
            pl.BlockSpec((MOE_EXPERTS, LANES), const),
        ],
        out_shape=[
            jax.ShapeDtypeStruct((t, d), F32),
            jax.ShapeDtypeStruct((t * SUBLANES, LANES), F32),
            jax.ShapeDtypeStruct((MOE_TOP_K, t), jnp.int32),
            jax.ShapeDtypeStruct((MOE_TOP_K, t), F32),
            jax.ShapeDtypeStruct((MOE_TOP_K, t), jnp.int32),
            jax.ShapeDtypeStruct((MOE_EXPERTS, LANES), F32),
        ],
        scratch_shapes=[pltpu.VMEM((MOE_EXPERTS, LANES), F32)],
        compiler_params=_cparams(("arbitrary",)),
        name="post",
    )(a, w_out.astype(BF16), h, gain.reshape(1, d), wr, br)


def _row_tile(ref, idx):
    return ref.at[pl.ds(pl.multiple_of(idx * SUBLANES, SUBLANES), SUBLANES), :]


def _block_rows(ref, blk):
    return ref.at[pl.ds(pl.multiple_of(blk * (MOE_BLOCK * SUBLANES), MOE_BLOCK * SUBLANES), MOE_BLOCK * SUBLANES), :]


def _dispatch_kernel(pe_ref, nu_ref, dest_ref, xr_ref, xbuf_ref, zero_ref, sem, zsem, *, tm, n_blocks):
    @pl.when(pl.program_id(0) == 0)
    def _():
        zero_ref[...] = jnp.zeros_like(zero_ref)

        def last_block(e):
            return pe_ref[e] // MOE_BLOCK - 1

        def has_rows(e):
            return pe_ref[e] > jnp.where(e == 0, 0, pe_ref[jnp.maximum(e - 1, 0)])

        def zero_block(blk):
            return pltpu.make_async_copy(zero_ref, _block_rows(xbuf_ref, blk), zsem)

        def start_e(e, carry):
            @pl.when(has_rows(e))
            def _():
                zero_block(last_block(e)).start()
            return carry

        def wait_e(e, carry):
            @pl.when(has_rows(e))
            def _():
                zero_block(last_block(e)).wait()
            return carry

        def start_b(blk, carry):
            zero_block(blk).start()
            return carry

        def wait_b(blk, carry):
            zero_block(blk).wait()
            return carry

        lax.fori_loop(0, MOE_EXPERTS, start_e, 0)
        lax.fori_loop(nu_ref[0], n_blocks, start_b, 0)
        lax.fori_loop(0, MOE_EXPERTS, wait_e, 0)
        lax.fori_loop(nu_ref[0], n_blocks, wait_b, 0)

    def issue(tk, carry):
        for s in range(MOE_TOP_K):
            pltpu.make_async_copy(_row_tile(xr_ref, tk), _row_tile(xbuf_ref, dest_ref[0, s, tk]), sem).start(priority=s)
        return carry

    lax.fori_loop(0, tm, issue, 0, unroll=8)
    for s in range(MOE_TOP_K):
        pltpu.make_async_copy(xr_ref, xbuf_ref.at[pl.ds(0, tm * SUBLANES), :], sem).wait()


def _dispatch(xr, dest3, pad_end, n_used, n_blocks, tm):
    nt = dest3.shape[0]
    grid_spec = pltpu.PrefetchScalarGridSpec(
        num_scalar_prefetch=2,
        grid=(nt,),
        in_specs=[
            pl.BlockSpec((1, MOE_TOP_K, tm), lambda i, pe, nu: (i, 0, 0), memory_space=pltpu.SMEM),
            pl.BlockSpec((tm * SUBLANES, LANES), lambda i, pe, nu: (i, 0)),
        ],
        out_specs=pl.BlockSpec(memory_space=pl.ANY),
        scratch_shapes=[pltpu.VMEM((MOE_BLOCK * SUBLANES, LANES), F32), pltpu.SemaphoreType.DMA,
                        pltpu.SemaphoreType.DMA],
    )
    return pl.pallas_call(
        functools.partial(_dispatch_kernel, tm=tm, n_blocks=n_blocks),
        grid_spec=grid_spec,
        out_shape=jax.ShapeDtypeStruct((n_blocks * MOE_BLOCK * SUBLANES, LANES), F32),
        compiler_params=_cparams(("arbitrary",)),
        name="dispatch",
    )(pad_end, n_used, dest3, xr)


def _experts_kernel(be_ref, nu_ref, x_ref, wg_ref, wu_ref, wd_ref, y_ref, x2_ref, wgb_ref, wub_ref, wdb_ref, *, d):
    rows = MOE_BLOCK
    b = pl.program_id(0)
    live = b < nu_ref[0]
    new_expert = jnp.logical_or(b == 0, be_ref[b] != be_ref[jnp.maximum(b - 1, 0)])

    @pl.when(jnp.logical_and(live, new_expert))
    def _():
        wgb_ref[...] = wg_ref[...].astype(BF16)
        wub_ref[...] = wu_ref[...].astype(BF16)
        wdb_ref[...] = wd_ref[...].astype(BF16)

    @pl.when(live)
    def _():
        for c in range(d // LANES):
            x2_ref[:, c * LANES:(c + 1) * LANES] = x_ref[pl.ds(c, rows, stride=SUBLANES), :].astype(BF16)
        x2 = x2_ref[...]
        gp = _dot(x2, wgb_ref[...])
        up = _dot(x2, wub_ref[...])
        hid = (gp * _sigmoid(gp) * up).astype(BF16)
        y = _dot(hid, wdb_ref[...])
        for c in range(d // LANES):
            y_ref[pl.ds(c, rows, stride=SUBLANES), :] = y[:, c * LANES:(c + 1) * LANES]


def _experts(xbuf, block_e, n_used, w_gate, w_up, w_down, layer):
    n_blocks = block_e.shape[0]
    d, dff = w_gate.shape[2], w_gate.shape[3]
    xmap = lambda b, be, nu: (jnp.minimum(b, nu[0] - 1), 0)
    wmap = lambda b, be, nu: (layer, be[b], 0, 0)
    grid_spec = pltpu.PrefetchScalarGridSpec(
        num_scalar_prefetch=2,
        grid=(n_blocks,),
        in_specs=[
            pl.BlockSpec((MOE_BLOCK * SUBLANES, LANES), xmap),
            pl.BlockSpec((None, None, d, dff), wmap),
            pl.BlockSpec((None, None, d, dff), wmap),
            pl.BlockSpec((None, None, dff, d), wmap),
        ],
        out_specs=pl.BlockSpec((MOE_BLOCK * SUBLANES, LANES), xmap),
        scratch_shapes=[pltpu.VMEM((MOE_BLOCK, d), BF16), pltpu.VMEM((d, dff), BF16), pltpu.VMEM((d, dff), BF16),
                        pltpu.VMEM((dff, d), BF16)],
    )
    return pl.pallas_call(
        functools.partial(_experts_kernel, d=d),
        grid_spec=grid_spec,
        out_shape=jax.ShapeDtypeStruct(xbuf.shape, F32),
        input_output_aliases={2: 0},
        compiler_params=_cparams(("arbitrary",)),
        name="experts",
    )(block_e, n_used, xbuf, w_gate, w_up, w_down)


def _gather_combine(dcur_ref, dnxt_ref, ybuf_ref, h_ref, gate_ref, y_ref, sems, *, tm, d):
    i = pl.program_id(0)

    def start(dref, slot):
        def issue(tk, carry):
            for s in range(MOE_TOP_K):
                pltpu.make_async_copy(_row_tile(ybuf_ref, dref[0, s, tk]), _row_tile(y_ref.at[slot, s], tk),
                                      sems.at[slot]).start(priority=s)
            return carry

        lax.fori_loop(0, tm, issue, 0, unroll=8)

    @pl.when(i == 0)
    def _():
        start(dcur_ref, 0)

    @pl.when(i + 1 < pl.num_programs(0))
    def _():
        start(dnxt_ref, (i + 1) % 2)

    slot = i % 2
    for s in range(MOE_TOP_K):
        pltpu.make_async_copy(ybuf_ref.at[pl.ds(0, tm * SUBLANES), :], y_ref.at[slot, s], sems.at[slot]).wait()
    g0 = gate_ref[:, 0:1]
    g1 = gate_ref[:, 1:2]
    cols = []
    for c in range(d // LANES):
        y0 = y_ref[slot, 0, pl.ds(c, tm, stride=SUBLANES), :]
        y1 = y_ref[slot, 1, pl.ds(c, tm, stride=SUBLANES), :]
        cols.append(h_ref[:, c * LANES:(c + 1) * LANES] + (y0 * g0 + y1 * g1))
    return jnp.concatenate(cols, axis=1)


def _combine_specs(tm, d, nt):
    in_specs = [
        pl.BlockSpec((1, MOE_TOP_K, tm), lambda i: (i, 0, 0), memory_space=pltpu.SMEM),
        pl.BlockSpec((1, MOE_TOP_K, tm), lambda i: (jnp.minimum(i + 1, nt - 1), 0, 0), memory_space=pltpu.SMEM),
        pl.BlockSpec(memory_space=pl.ANY),
        pl.BlockSpec((tm, d), lambda i: (i, 0)),
        pl.BlockSpec((tm, MOE_TOP_K), lambda i: (i, 0)),
    ]
    scratch = [pltpu.VMEM((2, MOE_TOP_K, tm * SUBLANES, LANES), F32), pltpu.SemaphoreType.DMA((2,))]
    return in_specs, scratch


def _final_kernel(dcur_ref, dnxt_ref, ybuf_ref, h_ref, gate_ref, fg_ref, o_ref, y_ref, sems, *, tm, d):
    out = _gather_combine(dcur_ref, dnxt_ref, ybuf_ref, h_ref, gate_ref, y_ref, sems, tm=tm, d=d)
    o_ref[...] = _rms_unit(out) * fg_ref[...]


def _final(moe, final_gain, tm):
    ybuf, dest3, gate_t, h = moe
    t, d = h.shape
    nt = dest3.shape[0]
    in_specs, scratch = _combine_specs(tm, d, nt)
    return pl.pallas_call(
        functools.partial(_final_kernel, tm=tm, d=d),
        grid=(nt,),
        in_specs=in_specs + [pl.BlockSpec((1, d), lambda i: (0, 0))],
        out_specs=pl.BlockSpec((tm, d), lambda i: (i, 0)),
        out_shape=jax.ShapeDtypeStruct((t, d), F32),
        scratch_shapes=scratch,
        compiler_params=_cparams(("arbitrary",)),
        name="final",
    )(dest3, dest3, ybuf, h, gate_t, final_gain.reshape(1, d))


def _moe_experts(xr, eidx, gate, rank, counts, h, w_gate, w_up, w_down, layer, tm_row):
    t = h.shape[0]
    n_slot = t * MOE_TOP_K
    n_blocks = -(-n_slot // MOE_BLOCK) + MOE_EXPERTS
    cnt = counts[:, 0].astype(jnp.int32)
    padded = (cnt + MOE_BLOCK - 1) // MOE_BLOCK * MOE_BLOCK
    pad_end = jnp.cumsum(padded)
    pad_start = pad_end - padded
    dest = rank
    for e in range(MOE_EXPERTS):
        dest = dest + jnp.where(eidx == e, pad_start[e], 0)
    block_start = jnp.arange(n_blocks, dtype=jnp.int32) * MOE_BLOCK
    block_e = jnp.minimum(jnp.sum((pad_end[None, :] <= block_start[:, None]).astype(jnp.int32), axis=1),
                          MOE_EXPERTS - 1)
    n_used = (pad_end[-1:] // MOE_BLOCK).astype(jnp.int32)
    dest3 = dest.reshape(MOE_TOP_K, t // tm_row, tm_row).transpose(1, 0, 2)
    xbuf = _dispatch(xr, dest3, pad_end.astype(jnp.int32), n_used, n_blocks, tm_row)
    ybuf = _experts(xbuf, block_e, n_used, w_gate, w_up, w_down, layer)
    return ybuf, dest3, gate.T, h


def _fox_in_kernel(dcur_ref, dnxt_ref, ybuf_ref, h_ref, gate_ref, gkv_ref, gq_ref, wkv_ref, wf_ref, bf_ref, wq_ref,
                   place_ref, hn_ref, k_ref, v_ref, q_ref, og_ref, ka_ref, y_ref, sems, carry_ref,
                   *, tm, d, width, tiles_per_seq, qscale):
    i = pl.program_id(0)

    @pl.when(i % tiles_per_seq == 0)
    def _():
        carry_ref[...] = jnp.zeros_like(carry_ref)

    hn_ref[...] = _gather_combine(dcur_ref, dnxt_ref, ybuf_ref, h_ref, gate_ref, y_ref, sems, tm=tm, d=d)
    y = _rms_unit(hn_ref[...])
    xkv = (y * gkv_ref[...]).astype(BF16)
    xq = (y * gq_ref[...]).astype(BF16)
    k_ref[...] = _dot(xkv, wkv_ref[:, 0:width]).astype(BF16)
    v_ref[...] = _dot(xkv, wkv_ref[:, width:2 * width]).astype(BF16)
    q_ref[...] = (_dot(xq, wq_ref[:, 0:width]) * qscale).astype(BF16)
    og_ref[...] = _dot(xq, wq_ref[:, width:2 * width]).astype(BF16)

    log_f = _log_sigmoid(_dot(xkv, wf_ref[...]) + bf_ref[...])
    ri = lax.broadcasted_iota(jnp.int32, (tm, tm), 0)
    ci = lax.broadcasted_iota(jnp.int32, (tm, tm), 1)
    ltri = jnp.where(ri >= ci, 1.0, 0.0).astype(BF16)
    f_hi, f_mid = _split_bf16(log_f)
    f_lo = (log_f - f_hi.astype(F32) - f_mid.astype(F32)).astype(BF16)
    c = carry_ref[0:1, :] + (_dot(ltri, f_hi) + (_dot(ltri, f_mid) + _dot(ltri, f_lo)))
    carry_ref[0:1, :] = c[tm - 1:tm, :]
    n_hi, n_mid = _split_bf16(-c)
    n_lo = (-c - n_hi.astype(F32) - n_mid.astype(F32)).astype(BF16)
    ka_ref[...] = _dot(jnp.concatenate([n_hi, n_mid, n_lo], axis=1), place_ref[...]).astype(BF16)


ATTN_BIAS_LANES = 3


def _bias_lane0(hh, dh):
    return dh if hh == 0 else 0


def _fox_in(moe, kv_gain, q_gain, kv_w, kv_b_f, w_in, seq, tm):
    ybuf, dest3, gate_t, h = moe
    t, d = h.shape
    nt = dest3.shape[0]
    width = w_in.shape[1] // 2
    heads = kv_w.shape[1] - 2 * width
    dh = width // heads
    place = np.zeros((ATTN_BIAS_LANES * heads, width), np.float32)
    for part in range(ATTN_BIAS_LANES):
        for hd in range(heads):
            place[part * heads + hd, (hd // 2) * LANES + _bias_lane0(hd % 2, dh) + part] = 1.0
    const = lambda i: (0, 0)
    row = lambda i: (i, 0)
    big = jax.ShapeDtypeStruct((t, width), BF16)
    comb_specs, comb_scratch = _combine_specs(tm, d, nt)
    return pl.pallas_call(
        functools.partial(_fox_in_kernel, tm=tm, d=d, width=width, tiles_per_seq=seq // tm, qscale=dh ** -0.5),
        grid=(nt,),
        in_specs=comb_specs + [
            pl.BlockSpec((1, d), const),
            pl.BlockSpec((1, d), const),
            pl.BlockSpec((d, 2 * width), const),
            pl.BlockSpec((d, heads), const),
            pl.BlockSpec((1, heads), const),
            pl.BlockSpec((d, 2 * width), const),
            pl.BlockSpec(place.shape, const),
        ],
        out_specs=[pl.BlockSpec((tm, d), row)] + [pl.BlockSpec((tm, width), row)] * 5,
        out_shape=[jax.ShapeDtypeStruct((t, d), F32)] + [big] * 5,
        scratch_shapes=comb_scratch + [pltpu.VMEM((SUBLANES, heads), F32)],
        compiler_params=_cparams(("arbitrary",)),
        name="fox_in",
    )(dest3, dest3, ybuf, h, gate_t, kv_gain.reshape(1, d), q_gain.reshape(1, d), kv_w[:, :2 * width].astype(BF16),
      kv_w[:, 2 * width:].astype(BF16), kv_b_f.reshape(1, heads), w_in.astype(BF16), jnp.asarray(place, BF16))


ATTN_KV_CHUNK = 512


def _attn_kernel(q_ref, k_ref, v_ref, og_ref, kb_ref, o_ref, qa_ref, ka_ref, va_ref, s_ref, *, seq, tq, dh):
    lane = lax.broadcasted_iota(jnp.int32, (seq, LANES), 1)
    for hh in range(2):
        a0 = _bias_lane0(hh, dh)
        own = (lane < dh) if hh == 0 else (lane >= dh)
        ones_q = jnp.where((lane >= a0) & (lane < a0 + ATTN_BIAS_LANES), 1.0, 0.0).astype(BF16)
        qa_ref[hh] = jnp.where(own, q_ref[...], ones_q)
        ka_ref[hh] = jnp.where(own, k_ref[...], kb_ref[...])
        va_ref[hh] = jnp.where(own, v_ref[...], jnp.where(lane == a0, 1.0, 0.0).astype(BF16))

    ri = lax.broadcasted_iota(jnp.int32, (tq, tq), 0)
    ci = lax.broadcasted_iota(jnp.int32, (tq, tq), 1)
    causal = ri >= ci
    lane_q = lax.broadcasted_iota(jnp.int32, (tq, LANES), 1)

    def fold_max(mx, s):
        for g in range(s.shape[1] // LANES):
            mx = jnp.maximum(mx, s[:, g * LANES:(g + 1) * LANES])
        return mx

    for qi in range(seq // tq):
        rows = slice(qi * tq, (qi + 1) * tq)
        past = [(c0, min(c0 + ATTN_KV_CHUNK, qi * tq)) for c0 in range(0, qi * tq, ATTN_KV_CHUNK)]
        row_max = []
        for hh in range(2):
            qa = qa_ref[hh, rows, :]
            mx = jnp.full((tq, LANES), -jnp.inf, F32)
            for c0, c1 in past:
                s = _dot_nt(qa, ka_ref[hh, c0:c1, :])
                s_ref[hh, :, c0:c1] = s
                mx = fold_max(mx, s)
            s = jnp.where(causal, _dot_nt(qa, ka_ref[hh, rows, :]), -jnp.inf)
            s_ref[hh, :, rows] = s
            row_max.append(jnp.max(fold_max(mx, s), axis=1, keepdims=True))
        outs = []
        for hh in range(2):
            acc = jnp.zeros((tq, LANES), F32)
            for c0, c1 in past + [(qi * tq, (qi + 1) * tq)]:
                p = jnp.exp(s_ref[hh, :, c0:c1] - row_max[hh]).astype(BF16)
                acc = acc + _dot(p, va_ref[hh, c0:c1, :])
            a0 = _bias_lane0(hh, dh)
            outs.append(acc / acc[:, a0:a0 + 1])
        o = jnp.where(lane_q < dh, outs[0], outs[1])
        o_ref[rows, :] = (o * _sigmoid(og_ref[rows, :].astype(F32))).astype(BF16)


def _attn(q, k, v, og, kb, heads, batch, seq, tq):
    t, width = q.shape
    dh = width // heads
    assert 2 * dh == LANES, "two heads per 128-lane block"
    blk = pl.BlockSpec((seq, LANES), lambda b, hp: (b, hp))
    return pl.pallas_call(
        functools.partial(_attn_kernel, seq=seq, tq=tq, dh=dh),
        grid=(batch, heads // 2),
        in_specs=[blk] * 5,
        out_specs=blk,
        out_shape=jax.ShapeDtypeStruct((t, width), BF16),
        scratch_shapes=[pltpu.VMEM((2, seq, LANES), BF16)] * 3 + [pltpu.VMEM((2, tq, seq), F32)],
        compiler_params=_cparams(("arbitrary", "arbitrary")),
        name="attn",
    )(q, k, v, og, kb)


def _tile(n, pref):
    while n % pref:
        pref //= 2
    return pref


def kernel(x, gla_norm, gla_w_in, gla_w_a2, gla_b_a, gla_out_norm, gla_w_out, kv_norm, kv_w, kv_b_f, fox_norm,
           fox_w_in, fox_w_out, ffn_norm, router_coarse_w, router_coarse_b, router_fine_w, router_fine_b,
           expert_w_gate, expert_w_up, expert_w_down, final_norm):
    batch, seq, d = x.shape
    t = batch * seq
    assert gla_norm.shape[0] == 1 and ffn_norm.shape[0] == 2, "trunk is GLA layer + FoX layer"
    tm = _tile(seq, 512)

    def ffn(a, w_out, h, layer):
        h, xr, eidx, gate, rank, counts = _post(a, w_out, h, ffn_norm[layer], router_coarse_w[layer],
                                                router_coarse_b[layer], router_fine_w[layer],
                                                router_fine_b[layer], _tile(seq, 1024), tm)
        return _moe_experts(xr, eidx, gate, rank, counts, h, expert_w_gate, expert_w_up, expert_w_down, layer, tm)

    h = x.reshape(t, d)
    q, k, v, r, la = _gla_in(h, gla_norm[0], gla_w_in[0], gla_w_a2[0], gla_b_a[0], tm)
    a = _gla_core(q, k, v, r, la, gla_out_norm[0], batch, seq, tm)
    moe = ffn(a, gla_w_out[0], h, 0)
    h, kk, vv, qq, og, kb = _fox_in(moe, kv_norm, fox_norm[0], kv_w, kv_b_f, fox_w_in[0], seq, tm)
    a = _attn(qq, kk, vv, og, kb, kv_b_f.shape[0], batch, seq, _tile(seq, 256))
    moe = ffn(a, fox_w_out[0], h, 1)
    return _final(moe, final_norm, tm).reshape(batch, seq, d)
```

```python
import functools

import jax
import jax.numpy as jnp
import numpy as np
from jax import lax
from jax.experimental import pallas as pl
from jax.experimental.pallas import tpu as pltpu

F32 = jnp.float32
BF16 = jnp.bfloat16

RMS_EPS = 1e-6
GLA_HEADS = 4
GLA_GATE_RANK = 16
GLA_GATE_TAU = 16.0
FOX_HEADS = 16
MOE_GROUPS = 4
MOE_EXPERTS_PER_GROUP = 8
MOE_EXPERTS = MOE_GROUPS * MOE_EXPERTS_PER_GROUP
MOE_TOP_K = 2
MOE_BLOCK = 256

LANES = 128
SUBLANES = 8
VMEM_LIMIT = 56 * 1024 * 1024

GLA_CHUNK = 128
ROUTER_ROWS = 40


def _cparams(sem):
    return pltpu.CompilerParams(dimension_semantics=sem, vmem_limit_bytes=VMEM_LIMIT)


def _rms_unit(x):
    return x * lax.rsqrt(jnp.mean(x * x, axis=-1, keepdims=True) + RMS_EPS)


def _log_sigmoid(z):
    return jnp.minimum(z, 0.0) - jnp.log1p(jnp.exp(-jnp.abs(z)))


def _sigmoid(z):
    return 1.0 / (1.0 + jnp.exp(-z))


def _dot(a, b):
    return jnp.dot(a, b, preferred_element_type=F32)


def _dot_nt(a, b):
    return lax.dot_general(a, b, (((1,), (1,)), ((), ())), preferred_element_type=F32)


def _dot_tn(a, b):
    return lax.dot_general(a, b, (((0,), (0,)), ((), ())), preferred_element_type=F32)


def _split_bf16(x):
    hi = x.astype(BF16)
    lo = (x - hi.astype(F32)).astype(BF16)
    return hi, lo


def _gla_in_kernel(h_ref, g_ref, w_ref, wa1_ref, wa2_ref, ba_ref, q_ref, k_ref, v_ref, r_ref, la_ref, *, qk, vw):
    xn = (_rms_unit(h_ref[...]) * g_ref[...]).astype(BF16)
    q_ref[...] = _dot(xn, w_ref[:, 0:qk]).astype(BF16)
    k_ref[...] = _dot(xn, w_ref[:, qk:2 * qk]).astype(BF16)
    v_ref[...] = _dot(xn, w_ref[:, 2 * qk:2 * qk + vw]).astype(BF16)
    r_ref[...] = _dot(xn, w_ref[:, 2 * qk + vw:2 * qk + 2 * vw]).astype(BF16)
    a = _dot(xn, wa1_ref[...])
    z = _dot(a.astype(BF16), wa2_ref[...]) + ba_ref[...]
    la_ref[...] = _log_sigmoid(z) * (1.0 / GLA_GATE_TAU)


def _gla_in(h, gain, w_in, w_a2, b_a, tm):
    t, d = h.shape
    qk = w_a2.shape[1]
    vw = (w_in.shape[1] - 2 * qk - GLA_GATE_RANK) // 2
    w_main = w_in[:, :2 * qk + 2 * vw].astype(BF16)
    w_a1 = w_in[:, 2 * qk + 2 * vw:].astype(BF16)
    const = lambda i: (0, 0)
    row = lambda i: (i, 0)
    return pl.pallas_call(
        functools.partial(_gla_in_kernel, qk=qk, vw=vw),
        grid=(t // tm,),
        in_specs=[
            pl.BlockSpec((tm, d), row),
            pl.BlockSpec((1, d), const),
            pl.BlockSpec(w_main.shape, const),
            pl.BlockSpec(w_a1.shape, const),
            pl.BlockSpec(w_a2.shape, const),
            pl.BlockSpec((1, qk), const),
        ],
        out_specs=[
            pl.BlockSpec((tm, qk), row),
            pl.BlockSpec((tm, qk), row),
            pl.BlockSpec((tm, vw), row),
            pl.BlockSpec((tm, vw), row),
            pl.BlockSpec((tm, qk), row),
        ],
        out_shape=[
            jax.ShapeDtypeStruct((t, qk), BF16),
            jax.ShapeDtypeStruct((t, qk), BF16),
            jax.ShapeDtypeStruct((t, vw), BF16),
            jax.ShapeDtypeStruct((t, vw), BF16),
            jax.ShapeDtypeStruct((t, qk), F32),
        ],
        compiler_params=_cparams(("arbitrary",)),
        name="gla_in",
    )(h, gain.reshape(1, d), w_main, w_a1, w_a2.astype(BF16), b_a.reshape(1, qk))


def _gla_core_kernel(q_ref, k_ref, v_ref, r_ref, la_ref, gn_ref, o_ref, st_ref, *, n_chunks, heads, dk, dv, scale):
    c_len = GLA_CHUNK

    @pl.when(pl.program_id(1) == 0)
    def _():
        st_ref[...] = jnp.zeros_like(st_ref)

    ri = lax.broadcasted_iota(jnp.int32, (c_len, c_len), 0)
    ci = lax.broadcasted_iota(jnp.int32, (c_len, c_len), 1)
    causal = ri >= ci
    ltri = jnp.where(causal, 1.0, 0.0).astype(BF16)

    for c in range(n_chunks):
        sl = pl.ds(c * c_len, c_len)
        for hd in range(heads):
            ks_ = slice(hd * dk, (hd + 1) * dk)
            vs_ = slice(hd * dv, (hd + 1) * dv)
            g_hi, g_lo = _split_bf16(la_ref[sl, ks_])
            cum = _dot(ltri, g_hi) + _dot(ltri, g_lo)
            last = cum[c_len - 1:c_len, :]
            mid = cum[c_len // 2 - 1:c_len // 2, :]
            qf = q_ref[sl, ks_].astype(F32) * scale
            kf = k_ref[sl, ks_].astype(F32)
            vb = v_ref[sl, vs_]
            qs = (qf * jnp.exp(cum - mid)).astype(BF16)
            ks = (kf * jnp.exp(mid - cum)).astype(BF16)
            scores = jnp.where(causal, _dot_nt(qs, ks), 0.0)
            o = _dot(scores.astype(BF16), vb)
            st = st_ref[hd]
            qd = (qf * jnp.exp(cum)).astype(BF16)
            o = o + _dot_nt(qd, st.astype(BF16))
            kd = (kf * jnp.exp(last - cum)).astype(BF16)
            st_ref[hd] = st * jnp.exp(last) + _dot_tn(vb, kd)
            on = _rms_unit(o) * gn_ref[hd:hd + 1, :]
            rr = r_ref[sl, vs_].astype(F32)
            o_ref[sl, vs_] = (on * (rr * _sigmoid(rr))).astype(BF16)


def _gla_core(q, k, v, r, la, out_norm, batch, seq, lc):
    t, qk = q.shape
    vw = v.shape[1]
    heads = GLA_HEADS
    dk, dv = qk // heads, vw // heads
    nj = seq // lc
    tok = lambda b, j: (b * nj + j, 0)
    return pl.pallas_call(
        functools.partial(_gla_core_kernel, n_chunks=lc // GLA_CHUNK, heads=heads, dk=dk, dv=dv, scale=dk ** -0.5),
        grid=(batch, nj),
        in_specs=[
            pl.BlockSpec((lc, qk), tok),
            pl.BlockSpec((lc, qk), tok),
            pl.BlockSpec((lc, vw), tok),
            pl.BlockSpec((lc, vw), tok),
            pl.BlockSpec((lc, qk), tok),
            pl.BlockSpec((heads, dv), lambda b, j: (0, 0)),
        ],
        out_specs=pl.BlockSpec((lc, vw), tok),
        out_shape=jax.ShapeDtypeStruct((t, vw), BF16),
        scratch_shapes=[pltpu.VMEM((heads, dv, dk), F32)],
        compiler_params=_cparams(("arbitrary", "arbitrary")),
        name="gla_core",
    )(q, k, v, r, la, out_norm)


def _post_kernel(a_ref, w_ref, h_ref, g_ref, wr_ref, br_ref,
                 hn_ref, xr_ref, eidx_ref, gate_ref, rank_ref, cnt_ref, base_ref, *, tm, sub, d):
    @pl.when(pl.program_id(0) == 0)
    def _():
        base_ref[...] = jnp.zeros_like(base_ref)

    eg = MOE_EXPERTS_PER_GROUP
    ti = lax.broadcasted_iota(jnp.int32, (sub, sub), 0)
    tj = lax.broadcasted_iota(jnp.int32, (sub, sub), 1)
    upper = jnp.where(ti <= tj, 1.0, 0.0).astype(BF16)
    row_g = lax.broadcasted_iota(jnp.int32, (MOE_GROUPS, sub), 0)
    row_e = lax.broadcasted_iota(jnp.int32, (eg, sub), 0)
    row_x = lax.broadcasted_iota(jnp.int32, (MOE_EXPERTS, sub), 0)
    base = base_ref[:, 0:1]

    for u in range(tm // sub):
        rows = slice(u * sub, (u + 1) * sub)
        hn = h_ref[rows, :] + _dot(a_ref[rows, :], w_ref[...])
        hn_ref[rows, :] = hn
        xn = _rms_unit(hn) * g_ref[...]
        for c in range(d // LANES):
            xr_ref[pl.ds(u * sub * SUBLANES + c, sub, stride=SUBLANES), :] = xn[:, c * LANES:(c + 1) * LANES]

        x_hi, x_lo = _split_bf16(xn)
        r_hi = _dot(x_hi, wr_ref[...])
        r_lo = _dot(x_lo, wr_ref[...])
        lg_t = (r_hi[:, 0:LANES] + r_hi[:, LANES:]) + (r_lo[:, 0:LANES] + r_lo[:, LANES:])
        lg = lg_t.T[0:ROUTER_ROWS] + br_ref[...]
        lc = lg[0:MOE_GROUPS]
        m_c = jnp.max(lc, axis=0, keepdims=True)
        pc_top = 1.0 / jnp.sum(jnp.exp(lc - m_c), axis=0, keepdims=True)
        g_idx = jnp.min(jnp.where(lc == m_c, row_g, MOE_GROUPS), axis=0, keepdims=True)
        lf = jnp.zeros((eg, sub), F32)
        for g in range(MOE_GROUPS):
            lf = lf + jnp.where(g_idx == g, lg[SUBLANES + g * eg:SUBLANES + (g + 1) * eg], 0.0)
        ef = jnp.exp(lf - jnp.max(lf, axis=0, keepdims=True))
        pf = ef / jnp.sum(ef, axis=0, keepdims=True)
        v1 = jnp.max(pf, axis=0, keepdims=True)
        j1 = jnp.min(jnp.where(pf == v1, row_e, eg), axis=0, keepdims=True)
        pf2 = jnp.where(row_e == j1, -1.0, pf)
        v2 = jnp.max(pf2, axis=0, keepdims=True)
        j2 = jnp.min(jnp.where(pf2 == v2, row_e, eg), axis=0, keepdims=True)
        denom = v1 + v2
        e1 = g_idx * eg + j1
        e2 = g_idx * eg + j2
        eidx_ref[0:1, rows] = e1
        eidx_ref[1:2, rows] = e2
        gate_ref[0:1, rows] = pc_top * v1 / denom
        gate_ref[1:2, rows] = pc_top * v2 / denom

        oh1 = row_x == e1
        oh2 = row_x == e2
        pre1 = _dot(jnp.where(oh1, 1.0, 0.0).astype(BF16), upper)
        pre2 = _dot(jnp.where(oh2, 1.0, 0.0).astype(BF16), upper)
        cnt1 = pre1[:, sub - 1:sub]
        cnt2 = pre2[:, sub - 1:sub]
        rank1 = jnp.sum(jnp.where(oh1, base + pre1 - 1.0, 0.0), axis=0, keepdims=True)
        rank2 = jnp.sum(jnp.where(oh2, base + cnt1 + pre2 - 1.0, 0.0), axis=0, keepdims=True)
        rank_ref[0:1, rows] = rank1.astype(jnp.int32)
        rank_ref[1:2, rows] = rank2.astype(jnp.int32)
        base = base + (cnt1 + cnt2)

    new_base = jnp.broadcast_to(base, base_ref.shape)
    base_ref[...] = new_base
    cnt_ref[...] = new_base


def _post(a, w_out, h, gain, w_coarse, b_coarse, w_fine, b_fine, tm, sub):
    t, d = h.shape
    wr = jnp.concatenate([w_coarse, jnp.zeros((d, SUBLANES - MOE_GROUPS), F32), w_fine,
                          jnp.zeros((d, LANES - ROUTER_ROWS), F32)], axis=1)
    wr = jnp.concatenate(_split_bf16(wr), axis=1)
    br = jnp.zeros((ROUTER_ROWS, 1), F32)
    br = br.at[0:MOE_GROUPS, 0].set(b_coarse).at[SUBLANES:, 0].set(b_fine)
    const = lambda i: (0, 0)
    row = lambda i: (i, 0)
    col = lambda i: (0, i)
    return pl.pallas_call(
        functools.partial(_post_kernel, tm=tm, sub=sub, d=d),
        grid=(t // tm,),
        in_specs=[
            pl.BlockSpec((tm, a.shape[1]), row),
            pl.BlockSpec(w_out.shape, const),
            pl.BlockSpec((tm, d), row),
            pl.BlockSpec((1, d), const),
            pl.BlockSpec((d, 2 * LANES), const),
            pl.BlockSpec((ROUTER_ROWS, 1), const),
        ],
        out_specs=[
            pl.BlockSpec((tm, d), row),
            pl.BlockSpec((tm * SUBLANES, LANES), row),
            pl.BlockSpec((MOE_TOP_K, tm), col),
            pl.BlockSpec((MOE_TOP_K, tm), col),
            pl.BlockSpec((MOE_TOP_K, tm), col),
            pl.BlockSpec((MOE_EXPERTS, LANES), const),
        ],
        out_shape=[
            jax.ShapeDtypeStruct((t, d), F32),
            jax.ShapeDtypeStruct((t * SUBLANES, LANES), F32),
            jax.ShapeDtypeStruct((MOE_TOP_K, t), jnp.int32),
            jax.ShapeDtypeStruct((MOE_TOP_K, t), F32),
            jax.ShapeDtypeStruct((MOE_TOP_K, t), jnp.int32),
            jax.ShapeDtypeStruct((MOE_EXPERTS, LANES), F32),
        ],
        scratch_shapes=[pltpu.VMEM((MOE_EXPERTS, LANES), F32)],
        compiler_params=_cparams(("arbitrary",)),
        name="post",
    )(a, w_out.astype(BF16), h, gain.reshape(1, d), wr, br)


def _row_tile(ref, idx):
    return ref.at[pl.ds(pl.multiple_of(idx * SUBLANES, SUBLANES), SUBLANES), :]


def _block_rows(ref, blk):
    return ref.at[pl.ds(pl.multiple_of(blk * (MOE_BLOCK * SUBLANES), MOE_BLOCK * SUBLANES), MOE_BLOCK * SUBLANES), :]


def _dispatch_kernel(pe_ref, nu_ref, dest_ref, xr_ref, xbuf_ref, zero_ref, sem, zsem, *, tm, n_blocks):
    @pl.when(pl.program_id(0) == 0)
    def _():
        zero_ref[...] = jnp.zeros_like(zero_ref)

        def last_block(e):
            return pe_ref[e] // MOE_BLOCK - 1

        def has_rows(e):
            return pe_ref[e] > jnp.where(e == 0, 0, pe_ref[jnp.maximum(e - 1, 0)])

        def zero_block(blk):
            return pltpu.make_async_copy(zero_ref, _block_rows(xbuf_ref, blk), zsem)

        def start_e(e, carry):
            @pl.when(has_rows(e))
            def _():
                zero_block(last_block(e)).start()
            return carry

        def wait_e(e, carry):
            @pl.when(has_rows(e))
            def _():
                zero_block(last_block(e)).wait()
            return carry

        def start_b(blk, carry):
            zero_block(blk).start()
            return carry

        def wait_b(blk, carry):
            zero_block(blk).wait()
            return carry

        lax.fori_loop(0, MOE_EXPERTS, start_e, 0)
        lax.fori_loop(nu_ref[0], n_blocks, start_b, 0)
        lax.fori_loop(0, MOE_EXPERTS, wait_e, 0)
        lax.fori_loop(nu_ref[0], n_blocks, wait_b, 0)

    def issue(tk, carry):
        for s in range(MOE_TOP_K):
            pltpu.make_async_copy(_row_tile(xr_ref, tk), _row_tile(xbuf_ref, dest_ref[0, s, tk]), sem).start(priority=s)
        return carry

    lax.fori_loop(0, tm, issue, 0, unroll=8)
    for s in range(MOE_TOP_K):
        pltpu.make_async_copy(xr_ref, xbuf_ref.at[pl.ds(0, tm * SUBLANES), :], sem).wait()


def _dispatch(xr, dest3, pad_end, n_used, n_blocks, tm):
    nt = dest3.shape[0]
    grid_spec = pltpu.PrefetchScalarGridSpec(
        num_scalar_prefetch=2,
        grid=(nt,),
        in_specs=[
            pl.BlockSpec((1, MOE_TOP_K, tm), lambda i, pe, nu: (i, 0, 0), memory_space=pltpu.SMEM),
            pl.BlockSpec((tm * SUBLANES, LANES), lambda i, pe, nu: (i, 0)),
        ],
        out_specs=pl.BlockSpec(memory_space=pl.ANY),
        scratch_shapes=[pltpu.VMEM((MOE_BLOCK * SUBLANES, LANES), F32), pltpu.SemaphoreType.DMA,
                        pltpu.SemaphoreType.DMA],
    )
    return pl.pallas_call(
        functools.partial(_dispatch_kernel, tm=tm, n_blocks=n_blocks),
        grid_spec=grid_spec,
        out_shape=jax.ShapeDtypeStruct((n_blocks * MOE_BLOCK * SUBLANES, LANES), F32),
        compiler_params=_cparams(("arbitrary",)),
        name="dispatch",
    )(pad_end, n_used, dest3, xr)


def _experts_kernel(be_ref, nu_ref, nxt_ref, par_ref, x_ref, wg_hbm, wu_hbm, wd_hbm, y_ref,
                    x2_ref, wgb_ref, wub_ref, wdb_ref, wgf_ref, wuf_ref, wdf_ref, sems, *, d, layer):
    rows = MOE_BLOCK
    b = pl.program_id(0)
    live = b < nu_ref[0]
    e = be_ref[b]
    new_expert = jnp.logical_or(b == 0, e != be_ref[jnp.maximum(b - 1, 0)])

    def weight_copies(ex, slot):
        return [pltpu.make_async_copy(src.at[layer, ex], dst.at[slot], sems.at[slot, j])
                for j, (src, dst) in enumerate(((wg_hbm, wgf_ref), (wu_hbm, wuf_ref), (wd_hbm, wdf_ref)))]

    @pl.when(jnp.logical_and(live, new_expert))
    def _():
        slot = par_ref[e]

        @pl.when(b == 0)
        def _():
            for cp in weight_copies(e, slot):
                cp.start()

        @pl.when(nxt_ref[e] >= 0)
        def _():
            for cp in weight_copies(nxt_ref[e], 1 - slot):
                cp.start()

        for cp in weight_copies(e, slot):
            cp.wait()
        wgb_ref[...] = wgf_ref[slot].astype(BF16)
        wub_ref[...] = wuf_ref[slot].astype(BF16)
        wdb_ref[...] = wdf_ref[slot].astype(BF16)

    @pl.when(live)
    def _():
        for c in range(d // LANES):
            x2_ref[:, c * LANES:(c + 1) * LANES] = x_ref[pl.ds(c, rows, stride=SUBLANES), :].astype(BF16)
        x2 = x2_ref[...]
        gp = _dot(x2, wgb_ref[...])
        up = _dot(x2, wub_ref[...])
        hid = (gp * _sigmoid(gp) * up).astype(BF16)
        y = _dot(hid, wdb_ref[...])
        for c in range(d // LANES):
            y_ref[pl.ds(c, rows, stride=SUBLANES), :] = y[:, c * LANES:(c + 1) * LANES]


def _experts(xbuf, block_e, n_used, padded, w_gate, w_up, w_down, layer):
    n_blocks = block_e.shape[0]
    d, dff = w_gate.shape[2], w_gate.shape[3]
    owns = padded > 0
    ids = jnp.arange(MOE_EXPERTS, dtype=jnp.int32)
    later = jnp.where(owns, ids, MOE_EXPERTS)
    nxt = lax.cummin(jnp.concatenate([later[1:], jnp.full((1,), MOE_EXPERTS, jnp.int32)]), reverse=True)
    nxt = jnp.where(nxt < MOE_EXPERTS, nxt, -1).astype(jnp.int32)
    par = ((jnp.cumsum(owns.astype(jnp.int32)) - owns.astype(jnp.int32)) % 2).astype(jnp.int32)
    xmap = lambda b, be, nu, nx, pa: (jnp.minimum(b, nu[0] - 1), 0)
    grid_spec = pltpu.PrefetchScalarGridSpec(
        num_scalar_prefetch=4,
        grid=(n_blocks,),
        in_specs=[
            pl.BlockSpec((MOE_BLOCK * SUBLANES, LANES), xmap),
            pl.BlockSpec(memory_space=pl.ANY),
            pl.BlockSpec(memory_space=pl.ANY),
            pl.BlockSpec(memory_space=pl.ANY),
        ],
        out_specs=pl.BlockSpec((MOE_BLOCK * SUBLANES, LANES), xmap),
        scratch_shapes=[pltpu.VMEM((MOE_BLOCK, d), BF16), pltpu.VMEM((d, dff), BF16), pltpu.VMEM((d, dff), BF16),
                        pltpu.VMEM((dff, d), BF16), pltpu.VMEM((2, d, dff), F32), pltpu.VMEM((2, d, dff), F32),
                        pltpu.VMEM((2, dff, d), F32), pltpu.SemaphoreType.DMA((2, 3))],
    )
    return pl.pallas_call(
        functools.partial(_experts_kernel, d=d, layer=layer),
        grid_spec=grid_spec,
        out_shape=jax.ShapeDtypeStruct(xbuf.shape, F32),
        input_output_aliases={4: 0},
        compiler_params=_cparams(("arbitrary",)),
        name="experts",
    )(block_e, n_used, nxt, par, xbuf, w_gate, w_up, w_down)


def _gather_combine(dcur_ref, dnxt_ref, ybuf_ref, h_ref, gate_ref, y_ref, sems, *, tm, d):
    i = pl.program_id(0)

    def start(dref, slot):
        def issue(tk, carry):
            for s in range(MOE_TOP_K):
                pltpu.make_async_copy(_row_tile(ybuf_ref, dref[0, s, tk]), _row_tile(y_ref.at[slot, s], tk),
                                      sems.at[slot]).start(priority=s)
            return carry

        lax.fori_loop(0, tm, issue, 0, unroll=8)

    @pl.when(i == 0)
    def _():
        start(dcur_ref, 0)

    @pl.when(i + 1 < pl.num_programs(0))
    def _():
        start(dnxt_ref, (i + 1) % 2)

    slot = i % 2
    for s in range(MOE_TOP_K):
        pltpu.make_async_copy(ybuf_ref.at[pl.ds(0, tm * SUBLANES), :], y_ref.at[slot, s], sems.at[slot]).wait()
    g0 = gate_ref[:, 0:1]
    g1 = gate_ref[:, 1:2]
    cols = []
    for c in range(d // LANES):
        y0 = y_ref[slot, 0, pl.ds(c, tm, stride=SUBLANES), :]
        y1 = y_ref[slot, 1, pl.ds(c, tm, stride=SUBLANES), :]
        cols.append(h_ref[:, c * LANES:(c + 1) * LANES] + (y0 * g0 + y1 * g1))
    return jnp.concatenate(cols, axis=1)


def _combine_specs(tm, d, nt):
    in_specs = [
        pl.BlockSpec((1, MOE_TOP_K, tm), lambda i: (i, 0, 0), memory_space=pltpu.SMEM),
        pl.BlockSpec((1, MOE_TOP_K, tm), lambda i: (jnp.minimum(i + 1, nt - 1), 0, 0), memory_space=pltpu.SMEM),
        pl.BlockSpec(memory_space=pl.ANY),
        pl.BlockSpec((tm, d), lambda i: (i, 0)),
        pl.BlockSpec((tm, MOE_TOP_K), lambda i: (i, 0)),
    ]
    scratch = [pltpu.VMEM((2, MOE_TOP_K, tm * SUBLANES, LANES), F32), pltpu.SemaphoreType.DMA((2,))]
    return in_specs, scratch


def _final_kernel(dcur_ref, dnxt_ref, ybuf_ref, h_ref, gate_ref, fg_ref, o_ref, y_ref, sems, *, tm, d):
    out = _gather_combine(dcur_ref, dnxt_ref, ybuf_ref, h_ref, gate_ref, y_ref, sems, tm=tm, d=d)
    o_ref[...] = _rms_unit(out) * fg_ref[...]


def _final(moe, final_gain, tm):
    ybuf, dest3, gate_t, h = moe
    t, d = h.shape
    nt = dest3.shape[0]
    in_specs, scratch = _combine_specs(tm, d, nt)
    return pl.pallas_call(
        functools.partial(_final_kernel, tm=tm, d=d),
        grid=(nt,),
        in_specs=in_specs + [pl.BlockSpec((1, d), lambda i: (0, 0))],
        out_specs=pl.BlockSpec((tm, d), lambda i: (i, 0)),
        out_shape=jax.ShapeDtypeStruct((t, d), F32),
        scratch_shapes=scratch,
        compiler_params=_cparams(("arbitrary",)),
        name="final",
    )(dest3, dest3, ybuf, h, gate_t, final_gain.reshape(1, d))


def _dest_kernel(ps_ref, eidx_ref, rank_ref, dest_ref):
    eidx = eidx_ref[...]
    dest = rank_ref[...]
    for e in range(MOE_EXPERTS):
        dest = dest + jnp.where(eidx == e, ps_ref[e], 0)
    dest_ref[0] = dest


def _dest(pad_start, eidx, rank, tm):
    t = eidx.shape[1]
    col = lambda i, ps: (0, i)
    grid_spec = pltpu.PrefetchScalarGridSpec(
        num_scalar_prefetch=1,
        grid=(t // tm,),
        in_specs=[pl.BlockSpec((MOE_TOP_K, tm), col), pl.BlockSpec((MOE_TOP_K, tm), col)],
        out_specs=pl.BlockSpec((1, MOE_TOP_K, tm), lambda i, ps: (i, 0, 0)),
    )
    return pl.pallas_call(
        _dest_kernel,
        grid_spec=grid_spec,
        out_shape=jax.ShapeDtypeStruct((t // tm, MOE_TOP_K, tm), jnp.int32),
        compiler_params=_cparams(("arbitrary",)),
        name="dest",
    )(pad_start, eidx, rank)


def _moe_experts(xr, eidx, gate, rank, counts, h, w_gate, w_up, w_down, layer, tm_row):
    t = h.shape[0]
    n_slot = t * MOE_TOP_K
    n_blocks = -(-n_slot // MOE_BLOCK) + MOE_EXPERTS
    cnt = counts[:, 0].astype(jnp.int32)
    padded = (cnt + MOE_BLOCK - 1) // MOE_BLOCK * MOE_BLOCK
    pad_end = jnp.cumsum(padded)
    pad_start = pad_end - padded
    block_start = jnp.arange(n_blocks, dtype=jnp.int32) * MOE_BLOCK
    block_e = jnp.minimum(jnp.sum((pad_end[None, :] <= block_start[:, None]).astype(jnp.int32), axis=1),
                          MOE_EXPERTS - 1)
    n_used = (pad_end[-1:] // MOE_BLOCK).astype(jnp.int32)
    dest3 = _dest(pad_start.astype(jnp.int32), eidx, rank, tm_row)
    xbuf = _dispatch(xr, dest3, pad_end.astype(jnp.int32), n_used, n_blocks, tm_row)
    ybuf = _experts(xbuf, block_e, n_used, padded, w_gate, w_up, w_down, layer)
    return ybuf, dest3, gate.T, h


def _fox_in_kernel(dcur_ref, dnxt_ref, ybuf_ref, h_ref, gate_ref, gkv_ref, gq_ref, wkv_ref, wf_ref, bf_ref, wq_ref,
                   place_ref, hn_ref, k_ref, v_ref, q_ref, og_ref, ka_ref, y_ref, sems, carry_ref,
                   *, tm, d, width, tiles_per_seq, qscale):
    i = pl.program_id(0)

    @pl.when(i % tiles_per_seq == 0)
    def _():
        carry_ref[...] = jnp.zeros_like(carry_ref)

    hn_ref[...] = _gather_combine(dcur_ref, dnxt_ref, ybuf_ref, h_ref, gate_ref, y_ref, sems, tm=tm, d=d)
    y = _rms_unit(hn_ref[...])
    xkv = (y * gkv_ref[...]).astype(BF16)
    xq = (y * gq_ref[...]).astype(BF16)
    k_ref[...] = _dot(xkv, wkv_ref[:, 0:width]).astype(BF16)
    v_ref[...] = _dot(xkv, wkv_ref[:, width:2 * width]).astype(BF16)
    q_ref[...] = (_dot(xq, wq_ref[:, 0:width]) * qscale).astype(BF16)
    og_ref[...] = _dot(xq, wq_ref[:, width:2 * width]).astype(BF16)

    log_f = _log_sigmoid(_dot(xkv, wf_ref[...]) + bf_ref[...])
    ri = lax.broadcasted_iota(jnp.int32, (tm, tm), 0)
    ci = lax.broadcasted_iota(jnp.int32, (tm, tm), 1)
    ltri = jnp.where(ri >= ci, 1.0, 0.0).astype(BF16)
    f_hi, f_mid = _split_bf16(log_f)
    f_lo = (log_f - f_hi.astype(F32) - f_mid.astype(F32)).astype(BF16)
    c = carry_ref[0:1, :] + (_dot(ltri, f_hi) + (_dot(ltri, f_mid) + _dot(ltri, f_lo)))
    carry_ref[0:1, :] = c[tm - 1:tm, :]
    nc = c * (-LOG2_E)
    n_hi, n_mid = _split_bf16(nc)
    n_lo = (nc - n_hi.astype(F32) - n_mid.astype(F32)).astype(BF16)
    ka_ref[...] = _dot(jnp.concatenate([n_hi, n_mid, n_lo], axis=1), place_ref[...]).astype(BF16)


ATTN_BIAS_LANES = 3
LOG2_E = 1.4426950408889634


def _bias_lane0(hh, dh):
    return dh if hh == 0 else 0


def _fox_in(moe, kv_gain, q_gain, kv_w, kv_b_f, w_in, seq, tm):
    ybuf, dest3, gate_t, h = moe
    t, d = h.shape
    nt = dest3.shape[0]
    width = w_in.shape[1] // 2
    heads = kv_w.shape[1] - 2 * width
    dh = width // heads
    place = np.zeros((ATTN_BIAS_LANES * heads, width), np.float32)
    for part in range(ATTN_BIAS_LANES):
        for hd in range(heads):
            place[part * heads + hd, (hd // 2) * LANES + _bias_lane0(hd % 2, dh) + part] = 1.0
    const = lambda i: (0, 0)
    row = lambda i: (i, 0)
    big = jax.ShapeDtypeStruct((t, width), BF16)
    comb_specs, comb_scratch = _combine_specs(tm, d, nt)
    return pl.pallas_call(
        functools.partial(_fox_in_kernel, tm=tm, d=d, width=width, tiles_per_seq=seq // tm, qscale=dh ** -0.5 * LOG2_E),
        grid=(nt,),
        in_specs=comb_specs + [
            pl.BlockSpec((1, d), const),
            pl.BlockSpec((1, d), const),
            pl.BlockSpec((d, 2 * width), const),
            pl.BlockSpec((d, heads), const),
            pl.BlockSpec((1, heads), const),
            pl.BlockSpec((d, 2 * width), const),
            pl.BlockSpec(place.shape, const),
        ],
        out_specs=[pl.BlockSpec((tm, d), row)] + [pl.BlockSpec((tm, width), row)] * 5,
        out_shape=[jax.ShapeDtypeStruct((t, d), F32)] + [big] * 5,
        scratch_shapes=comb_scratch + [pltpu.VMEM((SUBLANES, heads), F32)],
        compiler_params=_cparams(("arbitrary",)),
        name="fox_in",
    )(dest3, dest3, ybuf, h, gate_t, kv_gain.reshape(1, d), q_gain.reshape(1, d), kv_w[:, :2 * width].astype(BF16),
      kv_w[:, 2 * width:].astype(BF16), kv_b_f.reshape(1, heads), w_in.astype(BF16), jnp.asarray(place, BF16))


ATTN_KV_CHUNK = 512


def _attn_kernel(q_ref, k_ref, v_ref, og_ref, kb_ref, o_ref, qa_ref, ka_ref, va_ref, s_ref, *, seq, tq, dh):
    lane = lax.broadcasted_iota(jnp.int32, (seq, LANES), 1)
    for hh in range(2):
        a0 = _bias_lane0(hh, dh)
        own = (lane < dh) if hh == 0 else (lane >= dh)
        ones_q = jnp.where((lane >= a0) & (lane < a0 + ATTN_BIAS_LANES), 1.0, 0.0).astype(BF16)
        qa_ref[hh] = jnp.where(own, q_ref[...], ones_q)
        ka_ref[hh] = jnp.where(own, k_ref[...], kb_ref[...])
        va_ref[hh] = jnp.where(own, v_ref[...], jnp.where(lane == a0, 1.0, 0.0).astype(BF16))

    ri = lax.broadcasted_iota(jnp.int32, (tq, tq), 0)
    ci = lax.broadcasted_iota(jnp.int32, (tq, tq), 1)
    causal = ri >= ci
    lane_q = lax.broadcasted_iota(jnp.int32, (tq, LANES), 1)

    def fold_max(mx, s):
        for g in range(s.shape[1] // LANES):
            mx = jnp.maximum(mx, s[:, g * LANES:(g + 1) * LANES])
        return mx

    for qi in range(seq // tq):
        rows = slice(qi * tq, (qi + 1) * tq)
        past = [(c0, min(c0 + ATTN_KV_CHUNK, qi * tq)) for c0 in range(0, qi * tq, ATTN_KV_CHUNK)]
        row_max = []
        for hh in range(2):
            qa = qa_ref[hh, rows, :]
            mx = jnp.full((tq, LANES), -jnp.inf, F32)
            for c0, c1 in past:
                s = _dot_nt(qa, ka_ref[hh, c0:c1, :])
                s_ref[hh, :, c0:c1] = s
                mx = fold_max(mx, s)
            s = jnp.where(causal, _dot_nt(qa, ka_ref[hh, rows, :]), -jnp.inf)
            s_ref[hh, :, rows] = s
            row_max.append(jnp.max(fold_max(mx, s), axis=1, keepdims=True))
        outs = []
        for hh in range(2):
            acc = jnp.zeros((tq, LANES), F32)
            for c0, c1 in past + [(qi * tq, (qi + 1) * tq)]:
                p = jnp.exp2(s_ref[hh, :, c0:c1] - row_max[hh]).astype(BF16)
                acc = acc + _dot(p, va_ref[hh, c0:c1, :])
            a0 = _bias_lane0(hh, dh)
            outs.append(acc / acc[:, a0:a0 + 1])
        o = jnp.where(lane_q < dh, outs[0], outs[1])
        o_ref[rows, :] = (o * _sigmoid(og_ref[rows, :].astype(F32))).astype(BF16)


def _attn(q, k, v, og, kb, heads, batch, seq, tq):
    t, width = q.shape
    dh = width // heads
    assert 2 * dh == LANES, "two heads per 128-lane block"
    blk = pl.BlockSpec((seq, LANES), lambda b, hp: (b, hp))
    return pl.pallas_call(
        functools.partial(_attn_kernel, seq=seq, tq=tq, dh=dh),
        grid=(batch, heads // 2),
        in_specs=[blk] * 5,
        out_specs=blk,
        out_shape=jax.ShapeDtypeStruct((t, width), BF16),
        scratch_shapes=[pltpu.VMEM((2, seq, LANES), BF16)] * 3 + [pltpu.VMEM((2, tq, seq), F32)],
        compiler_params=_cparams(("arbitrary", "arbitrary")),
        name="attn",
    )(q, k, v, og, kb)


def _tile(n, pref):
    while n % pref:
        pref //= 2
    return pref


def kernel(x, gla_norm, gla_w_in, gla_w_a2, gla_b_a, gla_out_norm, gla_w_out, kv_norm, kv_w, kv_b_f, fox_norm,
           fox_w_in, fox_w_out, ffn_norm, router_coarse_w, router_coarse_b, router_fine_w, router_fine_b,
           expert_w_gate, expert_w_up, expert_w_down, final_norm):
    batch, seq, d = x.shape
    t = batch * seq
    assert gla_norm.shape[0] == 1 and ffn_norm.shape[0] == 2, "trunk is GLA layer + FoX layer"
    tm = _tile(seq, 512)

    def ffn(a, w_out, h, layer):
        h, xr, eidx, gate, rank, counts = _post(a, w_out, h, ffn_norm[layer], router_coarse_w[layer],
                                                router_coarse_b[layer], router_fine_w[layer],
                                                router_fine_b[layer], _tile(seq, 1024), tm)
        return _moe_experts(xr, eidx, gate, rank, counts, h, expert_w_gate, expert_w_up, expert_w_down, layer, tm)

    h = x.reshape(t, d)
    q, k, v, r, la = _gla_in(h, gla_norm[0], gla_w_in[0], gla_w_a2[0], gla_b_a[0], tm)
    a = _gla_core(q, k, v, r, la, gla_out_norm[0], batch, seq, tm)
    moe = ffn(a, gla_w_out[0], h, 0)
    h, kk, vv, qq, og, kb = _fox_in(moe, kv_norm, fox_norm[0], kv_w, kv_b_f, fox_w_in[0], seq, tm)
    a = _attn(qq, kk, vv, og, kb, kv_b_f.shape[0], batch, seq, _tile(seq, 256))
    moe = ffn(a, fox_w_out[0], h, 1)
    return _final(moe, final_norm, tm).reshape(batch, seq, d)
```

```python
import functools

import jax
import jax.numpy as jnp
import numpy as np
from jax import lax
from jax.experimental import pallas as pl
from jax.experimental.pallas import tpu as pltpu

F32 = jnp.float32
BF16 = jnp.bfloat16

RMS_EPS = 1e-6
GLA_HEADS = 4
GLA_GATE_RANK = 16
GLA_GATE_TAU = 16.0
FOX_HEADS = 16
MOE_GROUPS = 4
MOE_EXPERTS_PER_GROUP = 8
MOE_EXPERTS = MOE_GROUPS * MOE_EXPERTS_PER_GROUP
MOE_TOP_K = 2
MOE_BLOCK = 256

LANES = 128
SUBLANES = 8
VMEM_LIMIT = 56 * 1024 * 1024

GLA_CHUNK = 128
ROUTER_ROWS = 40


def _cparams(sem):
    return pltpu.CompilerParams(dimension_semantics=sem, vmem_limit_bytes=VMEM_LIMIT)


def _rms_unit(x):
    return x * lax.rsqrt(jnp.mean(x * x, axis=-1, keepdims=True) + RMS_EPS)


def _log_sigmoid(z):
    return jnp.minimum(z, 0.0) - jnp.log1p(jnp.exp(-jnp.abs(z)))


def _sigmoid(z):
    return 1.0 / (1.0 + jnp.exp(-z))


def _dot(a, b):
    return jnp.dot(a, b, preferred_element_type=F32)


def _dot_nt(a, b):
    return lax.dot_general(a, b, (((1,), (1,)), ((), ())), preferred_element_type=F32)


def _dot_tn(a, b):
    return lax.dot_general(a, b, (((0,), (0,)), ((), ())), preferred_element_type=F32)


def _split_bf16(x):
    hi = x.astype(BF16)
    lo = (x - hi.astype(F32)).astype(BF16)
    return hi, lo


def _gla_in_kernel(h_ref, g_ref, w_ref, wa1_ref, wa2_ref, ba_ref, q_ref, k_ref, v_ref, r_ref, la_ref, *, qk, vw):
    xn = (_rms_unit(h_ref[...]) * g_ref[...]).astype(BF16)
    q_ref[...] = _dot(xn, w_ref[:, 0:qk]).astype(BF16)
    k_ref[...] = _dot(xn, w_ref[:, qk:2 * qk]).astype(BF16)
    v_ref[...] = _dot(xn, w_ref[:, 2 * qk:2 * qk + vw]).astype(BF16)
    r_ref[...] = _dot(xn, w_ref[:, 2 * qk + vw:2 * qk + 2 * vw]).astype(BF16)
    a = _dot(xn, wa1_ref[...])
    z = _dot(a.astype(BF16), wa2_ref[...]) + ba_ref[...]
    la_ref[...] = _log_sigmoid(z) * (1.0 / GLA_GATE_TAU)


def _gla_in(h, gain, w_in, w_a2, b_a, tm):
    t, d = h.shape
    qk = w_a2.shape[1]
    vw = (w_in.shape[1] - 2 * qk - GLA_GATE_RANK) // 2
    w_main = w_in[:, :2 * qk + 2 * vw].astype(BF16)
    w_a1 = w_in[:, 2 * qk + 2 * vw:].astype(BF16)
    const = lambda i: (0, 0)
    row = lambda i: (i, 0)
    return pl.pallas_call(
        functools.partial(_gla_in_kernel, qk=qk, vw=vw),
        grid=(t // tm,),
        in_specs=[
            pl.BlockSpec((tm, d), row),
            pl.BlockSpec((1, d), const),
            pl.BlockSpec(w_main.shape, const),
            pl.BlockSpec(w_a1.shape, const),
            pl.BlockSpec(w_a2.shape, const),
            pl.BlockSpec((1, qk), const),
        ],
        out_specs=[
            pl.BlockSpec((tm, qk), row),
            pl.BlockSpec((tm, qk), row),
            pl.BlockSpec((tm, vw), row),
            pl.BlockSpec((tm, vw), row),
            pl.BlockSpec((tm, qk), row),
        ],
        out_shape=[
            jax.ShapeDtypeStruct((t, qk), BF16),
            jax.ShapeDtypeStruct((t, qk), BF16),
            jax.ShapeDtypeStruct((t, vw), BF16),
            jax.ShapeDtypeStruct((t, vw), BF16),
            jax.ShapeDtypeStruct((t, qk), F32),
        ],
        compiler_params=_cparams(("arbitrary",)),
        name="gla_in",
    )(h, gain.reshape(1, d), w_main, w_a1, w_a2.astype(BF16), b_a.reshape(1, qk))


def _gla_core_kernel(q_ref, k_ref, v_ref, r_ref, la_ref, gn_ref, o_ref, st_ref, *, n_chunks, heads, dk, dv, scale):
    c_len = GLA_CHUNK

    @pl.when(pl.program_id(1) == 0)
    def _():
        st_ref[...] = jnp.zeros_like(st_ref)

    ri = lax.broadcasted_iota(jnp.int32, (c_len, c_len), 0)
    ci = lax.broadcasted_iota(jnp.int32, (c_len, c_len), 1)
    causal = ri >= ci
    ltri = jnp.where(causal, 1.0, 0.0).astype(BF16)

    for c in range(n_chunks):
        sl = pl.ds(c * c_len, c_len)
        for hd in range(heads):
            ks_ = slice(hd * dk, (hd + 1) * dk)
            vs_ = slice(hd * dv, (hd + 1) * dv)
            g_hi, g_lo = _split_bf16(la_ref[sl, ks_])
            cum = _dot(ltri, g_hi) + _dot(ltri, g_lo)
            last = cum[c_len - 1:c_len, :]
            mid = cum[c_len // 2 - 1:c_len // 2, :]
            qf = q_ref[sl, ks_].astype(F32) * scale
            kf = k_ref[sl, ks_].astype(F32)
            vb = v_ref[sl, vs_]
            qs = (qf * jnp.exp(cum - mid)).astype(BF16)
            ks = (kf * jnp.exp(mid - cum)).astype(BF16)
            scores = jnp.where(causal, _dot_nt(qs, ks), 0.0)
            o = _dot(scores.astype(BF16), vb)
            st = st_ref[hd]
            qd = (qf * jnp.exp(cum)).astype(BF16)
            o = o + _dot_nt(qd, st.astype(BF16))
            kd = (kf * jnp.exp(last - cum)).astype(BF16)
            st_ref[hd] = st * jnp.exp(last) + _dot_tn(vb, kd)
            on = _rms_unit(o) * gn_ref[hd:hd + 1, :]
            rr = r_ref[sl, vs_].astype(F32)
            o_ref[sl, vs_] = (on * (rr * _sigmoid(rr))).astype(BF16)


def _gla_core(q, k, v, r, la, out_norm, batch, seq, lc):
    t, qk = q.shape
    vw = v.shape[1]
    heads = GLA_HEADS
    dk, dv = qk // heads, vw // heads
    nj = seq // lc
    tok = lambda b, j: (b * nj + j, 0)
    return pl.pallas_call(
        functools.partial(_gla_core_kernel, n_chunks=lc // GLA_CHUNK, heads=heads, dk=dk, dv=dv, scale=dk ** -0.5),
        grid=(batch, nj),
        in_specs=[
            pl.BlockSpec((lc, qk), tok),
            pl.BlockSpec((lc, qk), tok),
            pl.BlockSpec((lc, vw), tok),
            pl.BlockSpec((lc, vw), tok),
            pl.BlockSpec((lc, qk), tok),
            pl.BlockSpec((heads, dv), lambda b, j: (0, 0)),
        ],
        out_specs=pl.BlockSpec((lc, vw), tok),
        out_shape=jax.ShapeDtypeStruct((t, vw), BF16),
        scratch_shapes=[pltpu.VMEM((heads, dv, dk), F32)],
        compiler_params=_cparams(("arbitrary", "arbitrary")),
        name="gla_core",
    )(q, k, v, r, la, out_norm)


def _post_kernel(a_ref, w_ref, h_ref, g_ref, wr_ref, br_ref,
                 hn_ref, xr_ref, eidx_ref, gate_ref, rank_ref, cnt_ref, base_ref, *, tm, sub, d):
    @pl.when(pl.program_id(0) == 0)
    def _():
        base_ref[...] = jnp.zeros_like(base_ref)

    eg = MOE_EXPERTS_PER_GROUP
    ti = lax.broadcasted_iota(jnp.int32, (sub, sub), 0)
    tj = lax.broadcasted_iota(jnp.int32, (sub, sub), 1)
    upper = jnp.where(ti <= tj, 1.0, 0.0).astype(BF16)
    row_g = lax.broadcasted_iota(jnp.int32, (MOE_GROUPS, sub), 0)
    row_e = lax.broadcasted_iota(jnp.int32, (eg, sub), 0)
    row_x = lax.broadcasted_iota(jnp.int32, (MOE_EXPERTS, sub), 0)
    base = base_ref[:, 0:1]

    for u in range(tm // sub):
        rows = slice(u * sub, (u + 1) * sub)
        hn = h_ref[rows, :] + _dot(a_ref[rows, :], w_ref[...])
        hn_ref[rows, :] = hn
        xn = _rms_unit(hn) * g_ref[...]
        for c in range(d // LANES):
            xr_ref[pl.ds(u * sub * SUBLANES + c, sub, stride=SUBLANES), :] = xn[:, c * LANES:(c + 1) * LANES]

        x_hi, x_lo = _split_bf16(xn)
        r_hi = _dot(x_hi, wr_ref[...])
        r_lo = _dot(x_lo, wr_ref[...])
        lg_t = (r_hi[:, 0:LANES] + r_hi[:, LANES:]) + (r_lo[:, 0:LANES] + r_lo[:, LANES:])
        lg = lg_t.T[0:ROUTER_ROWS] + br_ref[...]
        lc = lg[0:MOE_GROUPS]
        m_c = jnp.max(lc, axis=0, keepdims=True)
        pc_top = 1.0 / jnp.sum(jnp.exp(lc - m_c), axis=0, keepdims=True)
        g_idx = jnp.min(jnp.where(lc == m_c, row_g, MOE_GROUPS), axis=0, keepdims=True)
        lf = jnp.zeros((eg, sub), F32)
        for g in range(MOE_GROUPS):
            lf = lf + jnp.where(g_idx == g, lg[SUBLANES + g * eg:SUBLANES + (g + 1) * eg], 0.0)
        ef = jnp.exp(lf - jnp.max(lf, axis=0, keepdims=True))
        pf = ef / jnp.sum(ef, axis=0, keepdims=True)
        v1 = jnp.max(pf, axis=0, keepdims=True)
        j1 = jnp.min(jnp.where(pf == v1, row_e, eg), axis=0, keepdims=True)
        pf2 = jnp.where(row_e == j1, -1.0, pf)
        v2 = jnp.max(pf2, axis=0, keepdims=True)
        j2 = jnp.min(jnp.where(pf2 == v2, row_e, eg), axis=0, keepdims=True)
        denom = v1 + v2
        e1 = g_idx * eg + j1
        e2 = g_idx * eg + j2
        eidx_ref[0:1, rows] = e1
        eidx_ref[1:2, rows] = e2
        gate_ref[0:1, rows] = pc_top * v1 / denom
        gate_ref[1:2, rows] = pc_top * v2 / denom

        oh1 = row_x == e1
        oh2 = row_x == e2
        pre1 = _dot(jnp.where(oh1, 1.0, 0.0).astype(BF16), upper)
        pre2 = _dot(jnp.where(oh2, 1.0, 0.0).astype(BF16), upper)
        cnt1 = pre1[:, sub - 1:sub]
        cnt2 = pre2[:, sub - 1:sub]
        rank1 = jnp.sum(jnp.where(oh1, base + pre1 - 1.0, 0.0), axis=0, keepdims=True)
        rank2 = jnp.sum(jnp.where(oh2, base + cnt1 + pre2 - 1.0, 0.0), axis=0, keepdims=True)
        rank_ref[0:1, rows] = rank1.astype(jnp.int32)
        rank_ref[1:2, rows] = rank2.astype(jnp.int32)
        base = base + (cnt1 + cnt2)

    new_base = jnp.broadcast_to(base, base_ref.shape)
    base_ref[...] = new_base
    cnt_ref[...] = new_base


def _post(a, w_out, h, gain, w_coarse, b_coarse, w_fine, b_fine, tm, sub):
    t, d = h.shape
    wr = jnp.concatenate([w_coarse, jnp.zeros((d, SUBLANES - MOE_GROUPS), F32), w_fine,
                          jnp.zeros((d, LANES - ROUTER_ROWS), F32)], axis=1)
    wr = jnp.concatenate(_split_bf16(wr), axis=1)
    br = jnp.zeros((ROUTER_ROWS, 1), F32)
    br = br.at[0:MOE_GROUPS, 0].set(b_coarse).at[SUBLANES:, 0].set(b_fine)
    const = lambda i: (0, 0)
    row = lambda i: (i, 0)
    col = lambda i: (0, i)
    return pl.pallas_call(
        functools.partial(_post_kernel, tm=tm, sub=sub, d=d),
        grid=(t // tm,),
        in_specs=[
            pl.BlockSpec((tm, a.shape[1]), row),
            pl.BlockSpec(w_out.shape, const),
            pl.BlockSpec((tm, d), row),
            pl.BlockSpec((1, d), const),
            pl.BlockSpec((d, 2 * LANES), const),
            pl.BlockSpec((ROUTER_ROWS, 1), const),
        ],
        out_specs=[
            pl.BlockSpec((tm, d), row),
            pl.BlockSpec((tm * SUBLANES, LANES), row),
            pl.BlockSpec((MOE_TOP_K, tm), col),
            pl.BlockSpec((MOE_TOP_K, tm), col),
            pl.BlockSpec((MOE_TOP_K, tm), col),
            pl.BlockSpec((MOE_EXPERTS, LANES), const),
        ],
        out_shape=[
            jax.ShapeDtypeStruct((t, d), F32),
            jax.ShapeDtypeStruct((t * SUBLANES, LANES), F32),
            jax.ShapeDtypeStruct((MOE_TOP_K, t), jnp.int32),
            jax.ShapeDtypeStruct((MOE_TOP_K, t), F32),
            jax.ShapeDtypeStruct((MOE_TOP_K, t), jnp.int32),
            jax.ShapeDtypeStruct((MOE_EXPERTS, LANES), F32),
        ],
        scratch_shapes=[pltpu.VMEM((MOE_EXPERTS, LANES), F32)],
        compiler_params=_cparams(("arbitrary",)),
        name="post",
    )(a, w_out.astype(BF16), h, gain.reshape(1, d), wr, br)


def _row_tile(ref, idx):
    return ref.at[pl.ds(pl.multiple_of(idx * SUBLANES, SUBLANES), SUBLANES), :]


def _block_rows(ref, blk):
    return ref.at[pl.ds(pl.multiple_of(blk * (MOE_BLOCK * SUBLANES), MOE_BLOCK * SUBLANES), MOE_BLOCK * SUBLANES), :]


def _dispatch_kernel(pe_ref, nu_ref, dest_ref, xr_ref, xbuf_ref, zero_ref, sem, zsem, *, tm, n_blocks):
    @pl.when(pl.program_id(0) == 0)
    def _():
        zero_ref[...] = jnp.zeros_like(zero_ref)

        def last_block(e):
            return pe_ref[e] // MOE_BLOCK - 1

        def has_rows(e):
            return pe_ref[e] > jnp.where(e == 0, 0, pe_ref[jnp.maximum(e - 1, 0)])

        def zero_block(blk):
            return pltpu.make_async_copy(zero_ref, _block_rows(xbuf_ref, blk), zsem)

        def start_e(e, carry):
            @pl.when(has_rows(e))
            def _():
                zero_block(last_block(e)).start()
            return carry

        def wait_e(e, carry):
            @pl.when(has_rows(e))
            def _():
                zero_block(last_block(e)).wait()
            return carry

        def start_b(blk, carry):
            zero_block(blk).start()
            return carry

        def wait_b(blk, carry):
            zero_block(blk).wait()
            return carry

        lax.fori_loop(0, MOE_EXPERTS, start_e, 0)
        lax.fori_loop(nu_ref[0], n_blocks, start_b, 0)
        lax.fori_loop(0, MOE_EXPERTS, wait_e, 0)
        lax.fori_loop(nu_ref[0], n_blocks, wait_b, 0)

    def issue(tk, carry):
        for s in range(MOE_TOP_K):
            pltpu.make_async_copy(_row_tile(xr_ref, tk), _row_tile(xbuf_ref, dest_ref[0, s, tk]), sem).start(priority=s)
        return carry

    lax.fori_loop(0, tm, issue, 0, unroll=8)
    for s in range(MOE_TOP_K):
        pltpu.make_async_copy(xr_ref, xbuf_ref.at[pl.ds(0, tm * SUBLANES), :], sem).wait()


def _dispatch(xr, dest3, pad_end, n_used, n_blocks, tm):
    nt = dest3.shape[0]
    grid_spec = pltpu.PrefetchScalarGridSpec(
        num_scalar_prefetch=2,
        grid=(nt,),
        in_specs=[
            pl.BlockSpec((1, MOE_TOP_K, tm), lambda i, pe, nu: (i, 0, 0), memory_space=pltpu.SMEM),
            pl.BlockSpec((tm * SUBLANES, LANES), lambda i, pe, nu: (i, 0)),
        ],
        out_specs=pl.BlockSpec(memory_space=pl.ANY),
        scratch_shapes=[pltpu.VMEM((MOE_BLOCK * SUBLANES, LANES), F32), pltpu.SemaphoreType.DMA,
                        pltpu.SemaphoreType.DMA],
    )
    return pl.pallas_call(
        functools.partial(_dispatch_kernel, tm=tm, n_blocks=n_blocks),
        grid_spec=grid_spec,
        out_shape=jax.ShapeDtypeStruct((n_blocks * MOE_BLOCK * SUBLANES, LANES), F32),
        compiler_params=_cparams(("arbitrary",)),
        name="dispatch",
    )(pad_end, n_used, dest3, xr)


def _experts_kernel(be_ref, nu_ref, nxt_ref, par_ref, x_ref, wg_hbm, wu_hbm, wd_hbm, y_ref,
                    x2_ref, wgb_ref, wub_ref, wdb_ref, wgf_ref, wuf_ref, wdf_ref, sems, *, d, layer):
    rows = MOE_BLOCK
    b = pl.program_id(0)
    live = b < nu_ref[0]
    e = be_ref[b]
    new_expert = jnp.logical_or(b == 0, e != be_ref[jnp.maximum(b - 1, 0)])

    def weight_copies(ex, slot):
        return [pltpu.make_async_copy(src.at[layer, ex], dst.at[slot], sems.at[slot, j])
                for j, (src, dst) in enumerate(((wg_hbm, wgf_ref), (wu_hbm, wuf_ref), (wd_hbm, wdf_ref)))]

    @pl.when(jnp.logical_and(live, new_expert))
    def _():
        slot = par_ref[e]

        @pl.when(b == 0)
        def _():
            for cp in weight_copies(e, slot):
                cp.start()

        @pl.when(nxt_ref[e] >= 0)
        def _():
            for cp in weight_copies(nxt_ref[e], 1 - slot):
                cp.start()

        for cp in weight_copies(e, slot):
            cp.wait()
        wgb_ref[...] = wgf_ref[slot].astype(BF16)
        wub_ref[...] = wuf_ref[slot].astype(BF16)
        wdb_ref[...] = wdf_ref[slot].astype(BF16)

    @pl.when(live)
    def _():
        for c in range(d // LANES):
            x2_ref[:, c * LANES:(c + 1) * LANES] = x_ref[pl.ds(c, rows, stride=SUBLANES), :].astype(BF16)
        x2 = x2_ref[...]
        gp = _dot(x2, wgb_ref[...])
        up = _dot(x2, wub_ref[...])
        hid = (gp * _sigmoid(gp) * up).astype(BF16)
        y = _dot(hid, wdb_ref[...])
        for c in range(d // LANES):
            y_ref[pl.ds(c, rows, stride=SUBLANES), :] = y[:, c * LANES:(c + 1) * LANES]


def _experts(xbuf, block_e, n_used, padded, w_gate, w_up, w_down, layer):
    n_blocks = block_e.shape[0]
    d, dff = w_gate.shape[2], w_gate.shape[3]
    owns = padded > 0
    ids = jnp.arange(MOE_EXPERTS, dtype=jnp.int32)
    later = jnp.where(owns, ids, MOE_EXPERTS)
    nxt = lax.cummin(jnp.concatenate([later[1:], jnp.full((1,), MOE_EXPERTS, jnp.int32)]), reverse=True)
    nxt = jnp.where(nxt < MOE_EXPERTS, nxt, -1).astype(jnp.int32)
    par = ((jnp.cumsum(owns.astype(jnp.int32)) - owns.astype(jnp.int32)) % 2).astype(jnp.int32)
    xmap = lambda b, be, nu, nx, pa: (jnp.minimum(b, nu[0] - 1), 0)
    grid_spec = pltpu.PrefetchScalarGridSpec(
        num_scalar_prefetch=4,
        grid=(n_blocks,),
        in_specs=[
            pl.BlockSpec((MOE_BLOCK * SUBLANES, LANES), xmap),
            pl.BlockSpec(memory_space=pl.ANY),
            pl.BlockSpec(memory_space=pl.ANY),
            pl.BlockSpec(memory_space=pl.ANY),
        ],
        out_specs=pl.BlockSpec((MOE_BLOCK * SUBLANES, LANES), xmap),
        scratch_shapes=[pltpu.VMEM((MOE_BLOCK, d), BF16), pltpu.VMEM((d, dff), BF16), pltpu.VMEM((d, dff), BF16),
                        pltpu.VMEM((dff, d), BF16), pltpu.VMEM((2, d, dff), F32), pltpu.VMEM((2, d, dff), F32),
                        pltpu.VMEM((2, dff, d), F32), pltpu.SemaphoreType.DMA((2, 3))],
    )
    return pl.pallas_call(
        functools.partial(_experts_kernel, d=d, layer=layer),
        grid_spec=grid_spec,
        out_shape=jax.ShapeDtypeStruct(xbuf.shape, F32),
        input_output_aliases={4: 0},
        compiler_params=_cparams(("arbitrary",)),
        name="experts",
    )(block_e, n_used, nxt, par, xbuf, w_gate, w_up, w_down)


def _gather_combine(dcur_ref, dnxt_ref, ybuf_ref, h_ref, gate_ref, y_ref, sems, *, tm, d):
    i = pl.program_id(0)

    def start(dref, slot):
        def issue(tk, carry):
            for s in range(MOE_TOP_K):
                pltpu.make_async_copy(_row_tile(ybuf_ref, dref[0, s, tk]), _row_tile(y_ref.at[slot, s], tk),
                                      sems.at[slot]).start(priority=s)
            return carry

        lax.fori_loop(0, tm, issue, 0, unroll=8)

    @pl.when(i == 0)
    def _():
        start(dcur_ref, 0)

    @pl.when(i + 1 < pl.num_programs(0))
    def _():
        start(dnxt_ref, (i + 1) % 2)

    slot = i % 2
    for s in range(MOE_TOP_K):
        pltpu.make_async_copy(ybuf_ref.at[pl.ds(0, tm * SUBLANES), :], y_ref.at[slot, s], sems.at[slot]).wait()
    g0 = gate_ref[:, 0:1]
    g1 = gate_ref[:, 1:2]
    cols = []
    for c in range(d // LANES):
        y0 = y_ref[slot, 0, pl.ds(c, tm, stride=SUBLANES), :]
        y1 = y_ref[slot, 1, pl.ds(c, tm, stride=SUBLANES), :]
        cols.append(h_ref[:, c * LANES:(c + 1) * LANES] + (y0 * g0 + y1 * g1))
    return jnp.concatenate(cols, axis=1)


def _combine_specs(tm, d, nt):
    in_specs = [
        pl.BlockSpec((1, MOE_TOP_K, tm), lambda i: (i, 0, 0), memory_space=pltpu.SMEM),
        pl.BlockSpec((1, MOE_TOP_K, tm), lambda i: (jnp.minimum(i + 1, nt - 1), 0, 0), memory_space=pltpu.SMEM),
        pl.BlockSpec(memory_space=pl.ANY),
        pl.BlockSpec((tm, d), lambda i: (i, 0)),
        pl.BlockSpec((tm, MOE_TOP_K), lambda i: (i, 0)),
    ]
    scratch = [pltpu.VMEM((2, MOE_TOP_K, tm * SUBLANES, LANES), F32), pltpu.SemaphoreType.DMA((2,))]
    return in_specs, scratch


def _final_kernel(dcur_ref, dnxt_ref, ybuf_ref, h_ref, gate_ref, fg_ref, o_ref, y_ref, sems, *, tm, d):
    out = _gather_combine(dcur_ref, dnxt_ref, ybuf_ref, h_ref, gate_ref, y_ref, sems, tm=tm, d=d)
    o_ref[...] = _rms_unit(out) * fg_ref[...]


def _final(moe, final_gain, tm):
    ybuf, dest3, gate_t, h = moe
    t, d = h.shape
    nt = dest3.shape[0]
    in_specs, scratch = _combine_specs(tm, d, nt)
    return pl.pallas_call(
        functools.partial(_final_kernel, tm=tm, d=d),
        grid=(nt,),
        in_specs=in_specs + [pl.BlockSpec((1, d), lambda i: (0, 0))],
        out_specs=pl.BlockSpec((tm, d), lambda i: (i, 0)),
        out_shape=jax.ShapeDtypeStruct((t, d), F32),
        scratch_shapes=scratch,
        compiler_params=_cparams(("arbitrary",)),
        name="final",
    )(dest3, dest3, ybuf, h, gate_t, final_gain.reshape(1, d))


def _dest_kernel(ps_ref, eidx_ref, rank_ref, dest_ref, *, tm):
    eidx = eidx_ref[...]
    dest = rank_ref[...]
    for e in range(MOE_EXPERTS):
        dest = dest + jnp.where(eidx == e, ps_ref[e], 0)
    for i in range(dest_ref.shape[0]):
        dest_ref[i] = dest[:, i * tm:(i + 1) * tm]


def _dest(pad_start, eidx, rank, tm):
    t = eidx.shape[1]
    whole = lambda i, ps: (0, 0)
    grid_spec = pltpu.PrefetchScalarGridSpec(
        num_scalar_prefetch=1,
        grid=(1,),
        in_specs=[pl.BlockSpec((MOE_TOP_K, t), whole), pl.BlockSpec((MOE_TOP_K, t), whole)],
        out_specs=pl.BlockSpec((t // tm, MOE_TOP_K, tm), lambda i, ps: (0, 0, 0)),
    )
    return pl.pallas_call(
        functools.partial(_dest_kernel, tm=tm),
        grid_spec=grid_spec,
        out_shape=jax.ShapeDtypeStruct((t // tm, MOE_TOP_K, tm), jnp.int32),
        compiler_params=_cparams(("arbitrary",)),
        name="dest",
    )(pad_start, eidx, rank)


def _moe_experts(xr, eidx, gate, rank, counts, h, w_gate, w_up, w_down, layer, tm_row):
    t = h.shape[0]
    n_slot = t * MOE_TOP_K
    n_blocks = -(-n_slot // MOE_BLOCK) + MOE_EXPERTS
    cnt = counts[:, 0].astype(jnp.int32)
    padded = (cnt + MOE_BLOCK - 1) // MOE_BLOCK * MOE_BLOCK
    pad_end = jnp.cumsum(padded)
    pad_start = pad_end - padded
    block_start = jnp.arange(n_blocks, dtype=jnp.int32) * MOE_BLOCK
    block_e = jnp.minimum(jnp.sum((pad_end[None, :] <= block_start[:, None]).astype(jnp.int32), axis=1),
                          MOE_EXPERTS - 1)
    n_used = (pad_end[-1:] // MOE_BLOCK).astype(jnp.int32)
    dest3 = _dest(pad_start.astype(jnp.int32), eidx, rank, tm_row)
    xbuf = _dispatch(xr, dest3, pad_end.astype(jnp.int32), n_used, n_blocks, tm_row)
    ybuf = _experts(xbuf, block_e, n_used, padded, w_gate, w_up, w_down, layer)
    return ybuf, dest3, gate.T, h


def _fox_in_kernel(dcur_ref, dnxt_ref, ybuf_ref, h_ref, gate_ref, gkv_ref, gq_ref, wkv_ref, wf_ref, bf_ref, wq_ref,
                   place_ref, hn_ref, k_ref, v_ref, q_ref, og_ref, ka_ref, y_ref, sems, carry_ref,
                   *, tm, sub, d, width, tiles_per_seq, qscale):
    i = pl.program_id(0)

    @pl.when(i % tiles_per_seq == 0)
    def _():
        carry_ref[...] = jnp.zeros_like(carry_ref)

    hn_ref[...] = _gather_combine(dcur_ref, dnxt_ref, ybuf_ref, h_ref, gate_ref, y_ref, sems, tm=tm, d=d)
    ri = lax.broadcasted_iota(jnp.int32, (sub, sub), 0)
    ci = lax.broadcasted_iota(jnp.int32, (sub, sub), 1)
    ltri = jnp.where(ri >= ci, 1.0, 0.0).astype(BF16)
    carry = carry_ref[0:1, :]
    for u in range(tm // sub):
        rows = slice(u * sub, (u + 1) * sub)
        y = _rms_unit(hn_ref[rows, :])
        xkv = (y * gkv_ref[...]).astype(BF16)
        xq = (y * gq_ref[...]).astype(BF16)
        k_ref[rows, :] = _dot(xkv, wkv_ref[:, 0:width]).astype(BF16)
        v_ref[rows, :] = _dot(xkv, wkv_ref[:, width:2 * width]).astype(BF16)
        q_ref[rows, :] = (_dot(xq, wq_ref[:, 0:width]) * qscale).astype(BF16)
        og_ref[rows, :] = _dot(xq, wq_ref[:, width:2 * width]).astype(BF16)

        log_f = _log_sigmoid(_dot(xkv, wf_ref[...]) + bf_ref[...])
        f_hi, f_mid = _split_bf16(log_f)
        f_lo = (log_f - f_hi.astype(F32) - f_mid.astype(F32)).astype(BF16)
        c = carry + (_dot(ltri, f_hi) + (_dot(ltri, f_mid) + _dot(ltri, f_lo)))
        carry = c[sub - 1:sub, :]
        nc = c * (-LOG2_E)
        n_hi, n_mid = _split_bf16(nc)
        n_lo = (nc - n_hi.astype(F32) - n_mid.astype(F32)).astype(BF16)
        ka_ref[rows, :] = _dot(jnp.concatenate([n_hi, n_mid, n_lo], axis=1), place_ref[...]).astype(BF16)
    carry_ref[0:1, :] = carry


ATTN_BIAS_LANES = 3
LOG2_E = 1.4426950408889634


def _bias_lane0(hh, dh):
    return dh if hh == 0 else 0


def _fox_in(moe, kv_gain, q_gain, kv_w, kv_b_f, w_in, seq, tm):
    ybuf, dest3, gate_t, h = moe
    t, d = h.shape
    nt = dest3.shape[0]
    width = w_in.shape[1] // 2
    heads = kv_w.shape[1] - 2 * width
    dh = width // heads
    place = np.zeros((ATTN_BIAS_LANES * heads, width), np.float32)
    for part in range(ATTN_BIAS_LANES):
        for hd in range(heads):
            place[part * heads + hd, (hd // 2) * LANES + _bias_lane0(hd % 2, dh) + part] = 1.0
    const = lambda i: (0, 0)
    row = lambda i: (i, 0)
    big = jax.ShapeDtypeStruct((t, width), BF16)
    comb_specs, comb_scratch = _combine_specs(tm, d, nt)
    return pl.pallas_call(
        functools.partial(_fox_in_kernel, tm=tm, sub=_tile(tm, 256), d=d, width=width, tiles_per_seq=seq // tm, qscale=dh ** -0.5 * LOG2_E),
        grid=(nt,),
        in_specs=comb_specs + [
            pl.BlockSpec((1, d), const),
            pl.BlockSpec((1, d), const),
            pl.BlockSpec((d, 2 * width), const),
            pl.BlockSpec((d, heads), const),
            pl.BlockSpec((1, heads), const),
            pl.BlockSpec((d, 2 * width), const),
            pl.BlockSpec(place.shape, const),
        ],
        out_specs=[pl.BlockSpec((tm, d), row)] + [pl.BlockSpec((tm, width), row)] * 5,
        out_shape=[jax.ShapeDtypeStruct((t, d), F32)] + [big] * 5,
        scratch_shapes=comb_scratch + [pltpu.VMEM((SUBLANES, heads), F32)],
        compiler_params=_cparams(("arbitrary",)),
        name="fox_in",
    )(dest3, dest3, ybuf, h, gate_t, kv_gain.reshape(1, d), q_gain.reshape(1, d), kv_w[:, :2 * width].astype(BF16),
      kv_w[:, 2 * width:].astype(BF16), kv_b_f.reshape(1, heads), w_in.astype(BF16), jnp.asarray(place, BF16))


ATTN_KV_CHUNK = 512


def _attn_kernel(q_ref, k_ref, v_ref, og_ref, kb_ref, o_ref, qa_ref, ka_ref, va_ref, s_ref, *, seq, tq, dh):
    lane = lax.broadcasted_iota(jnp.int32, (seq, LANES), 1)
    for hh in range(2):
        a0 = _bias_lane0(hh, dh)
        own = (lane < dh) if hh == 0 else (lane >= dh)
        ones_q = jnp.where((lane >= a0) & (lane < a0 + ATTN_BIAS_LANES), 1.0, 0.0).astype(BF16)
        qa_ref[hh] = jnp.where(own, q_ref[...], ones_q)
        ka_ref[hh] = jnp.where(own, k_ref[...], kb_ref[...])
        va_ref[hh] = jnp.where(own, v_ref[...], jnp.where(lane == a0, 1.0, 0.0).astype(BF16))

    ri = lax.broadcasted_iota(jnp.int32, (tq, tq), 0)
    ci = lax.broadcasted_iota(jnp.int32, (tq, tq), 1)
    causal = ri >= ci
    lane_q = lax.broadcasted_iota(jnp.int32, (tq, LANES), 1)

    def fold_max(mx, s):
        for g in range(s.shape[1] // LANES):
            mx = jnp.maximum(mx, s[:, g * LANES:(g + 1) * LANES])
        return mx

    for qi in range(seq // tq):
        rows = slice(qi * tq, (qi + 1) * tq)
        past = [(c0, min(c0 + ATTN_KV_CHUNK, qi * tq)) for c0 in range(0, qi * tq, ATTN_KV_CHUNK)]
        row_max = []
        for hh in range(2):
            qa = qa_ref[hh, rows, :]
            mx = jnp.full((tq, LANES), -jnp.inf, F32)
            for c0, c1 in past:
                s = _dot_nt(qa, ka_ref[hh, c0:c1, :])
                s_ref[hh, :, c0:c1] = s
                mx = fold_max(mx, s)
            s = jnp.where(causal, _dot_nt(qa, ka_ref[hh, rows, :]), -jnp.inf)
            s_ref[hh, :, rows] = s
            row_max.append(jnp.max(fold_max(mx, s), axis=1, keepdims=True))
        outs = []
        for hh in range(2):
            acc = jnp.zeros((tq, LANES), F32)
            for c0, c1 in past + [(qi * tq, (qi + 1) * tq)]:
                p = jnp.exp2(s_ref[hh, :, c0:c1] - row_max[hh]).astype(BF16)
                acc = acc + _dot(p, va_ref[hh, c0:c1, :])
            a0 = _bias_lane0(hh, dh)
            outs.append(acc / acc[:, a0:a0 + 1])
        o = jnp.where(lane_q < dh, outs[0], outs[1])
        o_ref[rows, :] = (o * _sigmoid(og_ref[rows, :].astype(F32))).astype(BF16)


def _attn(q, k, v, og, kb, heads, batch, seq, tq):
    t, width = q.shape
    dh = width // heads
    assert 2 * dh == LANES, "two heads per 128-lane block"
    blk = pl.BlockSpec((seq, LANES), lambda b, hp: (b, hp))
    return pl.pallas_call(
        functools.partial(_attn_kernel, seq=seq, tq=tq, dh=dh),
        grid=(batch, heads // 2),
        in_specs=[blk] * 5,
        out_specs=blk,
        out_shape=jax.ShapeDtypeStruct((t, width), BF16),
        scratch_shapes=[pltpu.VMEM((2, seq, LANES), BF16)] * 3 + [pltpu.VMEM((2, tq, seq), F32)],
        compiler_params=_cparams(("arbitrary", "arbitrary")),
        name="attn",
    )(q, k, v, og, kb)


def _tile(n, pref):
    while n % pref:
        pref //= 2
    return pref


def kernel(x, gla_norm, gla_w_in, gla_w_a2, gla_b_a, gla_out_norm, gla_w_out, kv_norm, kv_w, kv_b_f, fox_norm,
           fox_w_in, fox_w_out, ffn_norm, router_coarse_w, router_coarse_b, router_fine_w, router_fine_b,
           expert_w_gate, expert_w_up, expert_w_down, final_norm):
    batch, seq, d = x.shape
    t = batch * seq
    assert gla_norm.shape[0] == 1 and ffn_norm.shape[0] == 2, "trunk is GLA layer + FoX layer"
    tm = _tile(seq, 512)

    def ffn(a, w_out, h, layer):
        h, xr, eidx, gate, rank, counts = _post(a, w_out, h, ffn_norm[layer], router_coarse_w[layer],
                                                router_coarse_b[layer], router_fine_w[layer],
                                                router_fine_b[layer], _tile(seq, 1024), tm)
        return _moe_experts(xr, eidx, gate, rank, counts, h, expert_w_gate, expert_w_up, expert_w_down, layer, tm)

    h = x.reshape(t, d)
    q, k, v, r, la = _gla_in(h, gla_norm[0], gla_w_in[0], gla_w_a2[0], gla_b_a[0], tm)
    a = _gla_core(q, k, v, r, la, gla_out_norm[0], batch, seq, tm)
    moe = ffn(a, gla_w_out[0], h, 0)
    h, kk, vv, qq, og, kb = _fox_in(moe, kv_norm, fox_norm[0], kv_w, kv_b_f, fox_w_in[0], seq, tm)
    a = _attn(qq, kk, vv, og, kb, kv_b_f.shape[0], batch, seq, _tile(seq, 256))
    moe = ffn(a, fox_w_out[0], h, 1)
    return _final(moe, final_norm, tm).reshape(batch, seq, d)
```

```python
import functools

import jax
import jax.numpy as jnp
import numpy as np
from jax import lax
from jax.experimental import pallas as pl
from jax.experimental.pallas import tpu as pltpu

F32 = jnp.float32
BF16 = jnp.bfloat16

RMS_EPS = 1e-6
GLA_HEADS = 4
GLA_GATE_RANK = 16
GLA_GATE_TAU = 16.0
FOX_HEADS = 16
MOE_GROUPS = 4
MOE_EXPERTS_PER_GROUP = 8
MOE_EXPERTS = MOE_GROUPS * MOE_EXPERTS_PER_GROUP
MOE_TOP_K = 2
MOE_BLOCK = 512

LANES = 128
SUBLANES = 8
VMEM_LIMIT = 56 * 1024 * 1024

GLA_CHUNK = 128
ROUTER_ROWS = 40


def _cparams(sem):
    return pltpu.CompilerParams(dimension_semantics=sem, vmem_limit_bytes=VMEM_LIMIT)


def _rms_unit(x):
    return x * lax.rsqrt(jnp.mean(x * x, axis=-1, keepdims=True) + RMS_EPS)


def _log_sigmoid(z):
    return jnp.minimum(z, 0.0) - jnp.log1p(jnp.exp(-jnp.abs(z)))


def _sigmoid(z):
    return 1.0 / (1.0 + jnp.exp(-z))


def _dot(a, b):
    return jnp.dot(a, b, preferred_element_type=F32)


def _dot_nt(a, b):
    return lax.dot_general(a, b, (((1,), (1,)), ((), ())), preferred_element_type=F32)


def _dot_tn(a, b):
    return lax.dot_general(a, b, (((0,), (0,)), ((), ())), preferred_element_type=F32)


def _split_bf16(x):
    hi = x.astype(BF16)
    lo = (x - hi.astype(F32)).astype(BF16)
    return hi, lo


def _gla_in_kernel(h_ref, g_ref, w_ref, wa1_ref, wa2_ref, ba_ref, q_ref, k_ref, v_ref, r_ref, la_ref, *, qk, vw):
    xn = (_rms_unit(h_ref[...]) * g_ref[...]).astype(BF16)
    q_ref[...] = _dot(xn, w_ref[:, 0:qk]).astype(BF16)
    k_ref[...] = _dot(xn, w_ref[:, qk:2 * qk]).astype(BF16)
    v_ref[...] = _dot(xn, w_ref[:, 2 * qk:2 * qk + vw]).astype(BF16)
    r_ref[...] = _dot(xn, w_ref[:, 2 * qk + vw:2 * qk + 2 * vw]).astype(BF16)
    a = _dot(xn, wa1_ref[...])
    z = _dot(a.astype(BF16), wa2_ref[...]) + ba_ref[...]
    la_ref[...] = _log_sigmoid(z) * (1.0 / GLA_GATE_TAU)


def _gla_in(h, gain, w_in, w_a2, b_a, tm):
    t, d = h.shape
    qk = w_a2.shape[1]
    vw = (w_in.shape[1] - 2 * qk - GLA_GATE_RANK) // 2
    w_main = w_in[:, :2 * qk + 2 * vw].astype(BF16)
    w_a1 = w_in[:, 2 * qk + 2 * vw:].astype(BF16)
    const = lambda i: (0, 0)
    row = lambda i: (i, 0)
    return pl.pallas_call(
        functools.partial(_gla_in_kernel, qk=qk, vw=vw),
        grid=(t // tm,),
        in_specs=[
            pl.BlockSpec((tm, d), row),
            pl.BlockSpec((1, d), const),
            pl.BlockSpec(w_main.shape, const),
            pl.BlockSpec(w_a1.shape, const),
            pl.BlockSpec(w_a2.shape, const),
            pl.BlockSpec((1, qk), const),
        ],
        out_specs=[
            pl.BlockSpec((tm, qk), row),
            pl.BlockSpec((tm, qk), row),
            pl.BlockSpec((tm, vw), row),
            pl.BlockSpec((tm, vw), row),
            pl.BlockSpec((tm, qk), row),
        ],
        out_shape=[
            jax.ShapeDtypeStruct((t, qk), BF16),
            jax.ShapeDtypeStruct((t, qk), BF16),
            jax.ShapeDtypeStruct((t, vw), BF16),
            jax.ShapeDtypeStruct((t, vw), BF16),
            jax.ShapeDtypeStruct((t, qk), F32),
        ],
        compiler_params=_cparams(("arbitrary",)),
        name="gla_in",
    )(h, gain.reshape(1, d), w_main, w_a1, w_a2.astype(BF16), b_a.reshape(1, qk))


def _gla_core_kernel(q_ref, k_ref, v_ref, r_ref, la_ref, gn_ref, o_ref, st_ref, *, n_chunks, heads, dk, dv, scale):
    c_len = GLA_CHUNK

    @pl.when(pl.program_id(1) == 0)
    def _():
        st_ref[...] = jnp.zeros_like(st_ref)

    ri = lax.broadcasted_iota(jnp.int32, (c_len, c_len), 0)
    ci = lax.broadcasted_iota(jnp.int32, (c_len, c_len), 1)
    causal = ri >= ci
    ltri = jnp.where(causal, 1.0, 0.0).astype(BF16)

    for c in range(n_chunks):
        sl = pl.ds(c * c_len, c_len)
        for hd in range(heads):
            ks_ = slice(hd * dk, (hd + 1) * dk)
            vs_ = slice(hd * dv, (hd + 1) * dv)
            g_hi, g_lo = _split_bf16(la_ref[sl, ks_])
            cum = _dot(ltri, g_hi) + _dot(ltri, g_lo)
            last = cum[c_len - 1:c_len, :]
            mid = cum[c_len // 2 - 1:c_len // 2, :]
            qf = q_ref[sl, ks_].astype(F32) * scale
            kf = k_ref[sl, ks_].astype(F32)
            vb = v_ref[sl, vs_]
            qs = (qf * jnp.exp(cum - mid)).astype(BF16)
            ks = (kf * jnp.exp(mid - cum)).astype(BF16)
            scores = jnp.where(causal, _dot_nt(qs, ks), 0.0)
            o = _dot(scores.astype(BF16), vb)
            st = st_ref[hd]
            qd = (qf * jnp.exp(cum)).astype(BF16)
            o = o + _dot_nt(qd, st.astype(BF16))
            kd = (kf * jnp.exp(last - cum)).astype(BF16)
            st_ref[hd] = st * jnp.exp(last) + _dot_tn(vb, kd)
            on = _rms_unit(o) * gn_ref[hd:hd + 1, :]
            rr = r_ref[sl, vs_].astype(F32)
            o_ref[sl, vs_] = (on * (rr * _sigmoid(rr))).astype(BF16)


def _gla_core(q, k, v, r, la, out_norm, batch, seq, lc):
    t, qk = q.shape
    vw = v.shape[1]
    heads = GLA_HEADS
    dk, dv = qk // heads, vw // heads
    nj = seq // lc
    tok = lambda b, j: (b * nj + j, 0)
    return pl.pallas_call(
        functools.partial(_gla_core_kernel, n_chunks=lc // GLA_CHUNK, heads=heads, dk=dk, dv=dv, scale=dk ** -0.5),
        grid=(batch, nj),
        in_specs=[
            pl.BlockSpec((lc, qk), tok),
            pl.BlockSpec((lc, qk), tok),
            pl.BlockSpec((lc, vw), tok),
            pl.BlockSpec((lc, vw), tok),
            pl.BlockSpec((lc, qk), tok),
            pl.BlockSpec((heads, dv), lambda b, j: (0, 0)),
        ],
        out_specs=pl.BlockSpec((lc, vw), tok),
        out_shape=jax.ShapeDtypeStruct((t, vw), BF16),
        scratch_shapes=[pltpu.VMEM((heads, dv, dk), F32)],
        compiler_params=_cparams(("arbitrary", "arbitrary")),
        name="gla_core",
    )(q, k, v, r, la, out_norm)


def _post_kernel(a_ref, w_ref, h_ref, g_ref, wr_ref, br_ref,
                 hn_ref, xr_ref, eidx_ref, gate_ref, rank_ref, cnt_ref, base_ref, *, tm, sub, d):
    @pl.when(pl.program_id(0) == 0)
    def _():
        base_ref[...] = jnp.zeros_like(base_ref)

    eg = MOE_EXPERTS_PER_GROUP
    ti = lax.broadcasted_iota(jnp.int32, (sub, sub), 0)
    tj = lax.broadcasted_iota(jnp.int32, (sub, sub), 1)
    upper = jnp.where(ti <= tj, 1.0, 0.0).astype(BF16)
    row_g = lax.broadcasted_iota(jnp.int32, (MOE_GROUPS, sub), 0)
    row_e = lax.broadcasted_iota(jnp.int32, (eg, sub), 0)
    row_x = lax.broadcasted_iota(jnp.int32, (MOE_EXPERTS, sub), 0)
    base = base_ref[:, 0:1]

    for u in range(tm // sub):
        rows = slice(u * sub, (u + 1) * sub)
        hn = h_ref[rows, :] + _dot(a_ref[rows, :], w_ref[...])
        hn_ref[rows, :] = hn
        xn = _rms_unit(hn) * g_ref[...]
        for c in range(d // LANES):
            xr_ref[pl.ds(u * sub * SUBLANES + c, sub, stride=SUBLANES), :] = xn[:, c * LANES:(c + 1) * LANES]

        x_hi, x_lo = _split_bf16(xn)
        r_hi = _dot(x_hi, wr_ref[...])
        r_lo = _dot(x_lo, wr_ref[...])
        lg_t = (r_hi[:, 0:LANES] + r_hi[:, LANES:]) + (r_lo[:, 0:LANES] + r_lo[:, LANES:])
        lg = lg_t.T[0:ROUTER_ROWS] + br_ref[...]
        lc = lg[0:MOE_GROUPS]
        m_c = jnp.max(lc, axis=0, keepdims=True)
        pc_top = 1.0 / jnp.sum(jnp.exp(lc - m_c), axis=0, keepdims=True)
        g_idx = jnp.min(jnp.where(lc == m_c, row_g, MOE_GROUPS), axis=0, keepdims=True)
        lf = jnp.zeros((eg, sub), F32)
        for g in range(MOE_GROUPS):
            lf = lf + jnp.where(g_idx == g, lg[SUBLANES + g * eg:SUBLANES + (g + 1) * eg], 0.0)
        ef = jnp.exp(lf - jnp.max(lf, axis=0, keepdims=True))
        pf = ef / jnp.sum(ef, axis=0, keepdims=True)
        v1 = jnp.max(pf, axis=0, keepdims=True)
        j1 = jnp.min(jnp.where(pf == v1, row_e, eg), axis=0, keepdims=True)
        pf2 = jnp.where(row_e == j1, -1.0, pf)
        v2 = jnp.max(pf2, axis=0, keepdims=True)
        j2 = jnp.min(jnp.where(pf2 == v2, row_e, eg), axis=0, keepdims=True)
        denom = v1 + v2
        e1 = g_idx * eg + j1
        e2 = g_idx * eg + j2
        eidx_ref[0:1, rows] = e1
        eidx_ref[1:2, rows] = e2
        gate_ref[0:1, rows] = pc_top * v1 / denom
        gate_ref[1:2, rows] = pc_top * v2 / denom

        oh1 = row_x == e1
        oh2 = row_x == e2
        pre1 = _dot(jnp.where(oh1, 1.0, 0.0).astype(BF16), upper)
        pre2 = _dot(jnp.where(oh2, 1.0, 0.0).astype(BF16), upper)
        cnt1 = pre1[:, sub - 1:sub]
        cnt2 = pre2[:, sub - 1:sub]
        rank1 = jnp.sum(jnp.where(oh1, base + pre1 - 1.0, 0.0), axis=0, keepdims=True)
        rank2 = jnp.sum(jnp.where(oh2, base + cnt1 + pre2 - 1.0, 0.0), axis=0, keepdims=True)
        rank_ref[0:1, rows] = rank1.astype(jnp.int32)
        rank_ref[1:2, rows] = rank2.astype(jnp.int32)
        base = base + (cnt1 + cnt2)

    new_base = jnp.broadcast_to(base, base_ref.shape)
    base_ref[...] = new_base
    cnt_ref[...] = new_base


def _post(a, w_out, h, gain, w_coarse, b_coarse, w_fine, b_fine, tm, sub):
    t, d = h.shape
    wr = jnp.concatenate([w_coarse, jnp.zeros((d, SUBLANES - MOE_GROUPS), F32), w_fine,
                          jnp.zeros((d, LANES - ROUTER_ROWS), F32)], axis=1)
    wr = jnp.concatenate(_split_bf16(wr), axis=1)
    br = jnp.zeros((ROUTER_ROWS, 1), F32)
    br = br.at[0:MOE_GROUPS, 0].set(b_coarse).at[SUBLANES:, 0].set(b_fine)
    const = lambda i: (0, 0)
    row = lambda i: (i, 0)
    col = lambda i: (0, i)
    return pl.pallas_call(
        functools.partial(_post_kernel, tm=tm, sub=sub, d=d),
        grid=(t // tm,),
        in_specs=[
            pl.BlockSpec((tm, a.shape[1]), row),
            pl.BlockSpec(w_out.shape, const),
            pl.BlockSpec((tm, d), row),
            pl.BlockSpec((1, d), const),
            pl.BlockSpec((d, 2 * LANES), const),
            pl.BlockSpec((ROUTER_ROWS, 1), const),
        ],
        out_specs=[
            pl.BlockSpec((tm, d), row),
            pl.BlockSpec((tm * SUBLANES, LANES), row),
            pl.BlockSpec((MOE_TOP_K, tm), col),
            pl.BlockSpec((MOE_TOP_K, tm), col),
            pl.BlockSpec((MOE_TOP_K, tm), col),
            pl.BlockSpec((MOE_EXPERTS, LANES), const),
        ],
        out_shape=[
            jax.ShapeDtypeStruct((t, d), F32),
            jax.ShapeDtypeStruct((t * SUBLANES, LANES), F32),
            jax.ShapeDtypeStruct((MOE_TOP_K, t), jnp.int32),
            jax.ShapeDtypeStruct((MOE_TOP_K, t), F32),
            jax.ShapeDtypeStruct((MOE_TOP_K, t), jnp.int32),
            jax.ShapeDtypeStruct((MOE_EXPERTS, LANES), F32),
        ],
        scratch_shapes=[pltpu.VMEM((MOE_EXPERTS, LANES), F32)],
        compiler_params=_cparams(("arbitrary",)),
        name="post",
    )(a, w_out.astype(BF16), h, gain.reshape(1, d), wr, br)


def _row_tile(ref, idx):
    return ref.at[pl.ds(pl.multiple_of(idx * SUBLANES, SUBLANES), SUBLANES), :]


def _block_rows(ref, blk):
    return ref.at[pl.ds(pl.multiple_of(blk * (MOE_BLOCK * SUBLANES), MOE_BLOCK * SUBLANES), MOE_BLOCK * SUBLANES), :]


def _dispatch_kernel(pe_ref, nu_ref, dest_ref, xr_ref, xbuf_ref, zero_ref, sem, zsem, *, tm, n_blocks):
    @pl.when(pl.program_id(0) == 0)
    def _():
        zero_ref[...] = jnp.zeros_like(zero_ref)

        def last_block(e):
            return pe_ref[e] // MOE_BLOCK - 1

        def has_rows(e):
            return pe_ref[e] > jnp.where(e == 0, 0, pe_ref[jnp.maximum(e - 1, 0)])

        def zero_block(blk):
            return pltpu.make_async_copy(zero_ref, _block_rows(xbuf_ref, blk), zsem)

        def start_e(e, carry):
            @pl.when(has_rows(e))
            def _():
                zero_block(last_block(e)).start()
            return carry

        def wait_e(e, carry):
            @pl.when(has_rows(e))
            def _():
                zero_block(last_block(e)).wait()
            return carry

        def start_b(blk, carry):
            zero_block(blk).start()
            return carry

        def wait_b(blk, carry):
            zero_block(blk).wait()
            return carry

        lax.fori_loop(0, MOE_EXPERTS, start_e, 0)
        lax.fori_loop(nu_ref[0], n_blocks, start_b, 0)
        lax.fori_loop(0, MOE_EXPERTS, wait_e, 0)
        lax.fori_loop(nu_ref[0], n_blocks, wait_b, 0)

    def issue(tk, carry):
        for s in range(MOE_TOP_K):
            pltpu.make_async_copy(_row_tile(xr_ref, tk), _row_tile(xbuf_ref, dest_ref[0, s, tk]), sem).start(priority=s)
        return carry

    lax.fori_loop(0, tm, issue, 0, unroll=8)
    for s in range(MOE_TOP_K):
        pltpu.make_async_copy(xr_ref, xbuf_ref.at[pl.ds(0, tm * SUBLANES), :], sem).wait()


def _dispatch(xr, dest3, pad_end, n_used, n_blocks, tm):
    nt = dest3.shape[0]
    grid_spec = pltpu.PrefetchScalarGridSpec(
        num_scalar_prefetch=2,
        grid=(nt,),
        in_specs=[
            pl.BlockSpec((1, MOE_TOP_K, tm), lambda i, pe, nu: (i, 0, 0), memory_space=pltpu.SMEM),
            pl.BlockSpec((tm * SUBLANES, LANES), lambda i, pe, nu: (i, 0)),
        ],
        out_specs=pl.BlockSpec(memory_space=pl.ANY),
        scratch_shapes=[pltpu.VMEM((MOE_BLOCK * SUBLANES, LANES), F32), pltpu.SemaphoreType.DMA,
                        pltpu.SemaphoreType.DMA],
    )
    return pl.pallas_call(
        functools.partial(_dispatch_kernel, tm=tm, n_blocks=n_blocks),
        grid_spec=grid_spec,
        out_shape=jax.ShapeDtypeStruct((n_blocks * MOE_BLOCK * SUBLANES, LANES), F32),
        compiler_params=_cparams(("arbitrary",)),
        name="dispatch",
    )(pad_end, n_used, dest3, xr)


def _experts_kernel(be_ref, nu_ref, nxt_ref, par_ref, x_ref, wg_hbm, wu_hbm, wd_hbm, y_ref,
                    x2_ref, wgb_ref, wub_ref, wdb_ref, wgf_ref, wuf_ref, wdf_ref, sems, *, d, layer):
    rows = MOE_BLOCK
    b = pl.program_id(0)
    live = b < nu_ref[0]
    e = be_ref[b]
    new_expert = jnp.logical_or(b == 0, e != be_ref[jnp.maximum(b - 1, 0)])

    def weight_copies(ex, slot):
        return [pltpu.make_async_copy(src.at[layer, ex], dst.at[slot], sems.at[slot, j])
                for j, (src, dst) in enumerate(((wg_hbm, wgf_ref), (wu_hbm, wuf_ref), (wd_hbm, wdf_ref)))]

    @pl.when(jnp.logical_and(live, new_expert))
    def _():
        slot = par_ref[e]

        @pl.when(b == 0)
        def _():
            for cp in weight_copies(e, slot):
                cp.start()

        @pl.when(nxt_ref[e] >= 0)
        def _():
            for cp in weight_copies(nxt_ref[e], 1 - slot):
                cp.start()

        for cp in weight_copies(e, slot):
            cp.wait()
        wgb_ref[...] = wgf_ref[slot].astype(BF16)
        wub_ref[...] = wuf_ref[slot].astype(BF16)
        wdb_ref[...] = wdf_ref[slot].astype(BF16)

    @pl.when(live)
    def _():
        for c in range(d // LANES):
            x2_ref[:, c * LANES:(c + 1) * LANES] = x_ref[pl.ds(c, rows, stride=SUBLANES), :].astype(BF16)
        x2 = x2_ref[...]
        gp = _dot(x2, wgb_ref[...])
        up = _dot(x2, wub_ref[...])
        hid = (gp * _sigmoid(gp) * up).astype(BF16)
        y = _dot(hid, wdb_ref[...])
        for c in range(d // LANES):
            y_ref[pl.ds(c, rows, stride=SUBLANES), :] = y[:, c * LANES:(c + 1) * LANES]


def _experts(xbuf, block_e, n_used, padded, w_gate, w_up, w_down, layer):
    n_blocks = block_e.shape[0]
    d, dff = w_gate.shape[2], w_gate.shape[3]
    owns = padded > 0
    ids = jnp.arange(MOE_EXPERTS, dtype=jnp.int32)
    later = jnp.where(owns, ids, MOE_EXPERTS)
    nxt = lax.cummin(jnp.concatenate([later[1:], jnp.full((1,), MOE_EXPERTS, jnp.int32)]), reverse=True)
    nxt = jnp.where(nxt < MOE_EXPERTS, nxt, -1).astype(jnp.int32)
    par = ((jnp.cumsum(owns.astype(jnp.int32)) - owns.astype(jnp.int32)) % 2).astype(jnp.int32)
    xmap = lambda b, be, nu, nx, pa: (jnp.minimum(b, nu[0] - 1), 0)
    grid_spec = pltpu.PrefetchScalarGridSpec(
        num_scalar_prefetch=4,
        grid=(n_blocks,),
        in_specs=[
            pl.BlockSpec((MOE_BLOCK * SUBLANES, LANES), xmap),
            pl.BlockSpec(memory_space=pl.ANY),
            pl.BlockSpec(memory_space=pl.ANY),
            pl.BlockSpec(memory_space=pl.ANY),
        ],
        out_specs=pl.BlockSpec((MOE_BLOCK * SUBLANES, LANES), xmap),
        scratch_shapes=[pltpu.VMEM((MOE_BLOCK, d), BF16), pltpu.VMEM((d, dff), BF16), pltpu.VMEM((d, dff), BF16),
                        pltpu.VMEM((dff, d), BF16), pltpu.VMEM((2, d, dff), F32), pltpu.VMEM((2, d, dff), F32),
                        pltpu.VMEM((2, dff, d), F32), pltpu.SemaphoreType.DMA((2, 3))],
    )
    return pl.pallas_call(
        functools.partial(_experts_kernel, d=d, layer=layer),
        grid_spec=grid_spec,
        out_shape=jax.ShapeDtypeStruct(xbuf.shape, F32),
        input_output_aliases={4: 0},
        compiler_params=_cparams(("arbitrary",)),
        name="experts",
    )(block_e, n_used, nxt, par, xbuf, w_gate, w_up, w_down)


def _gather_combine(dcur_ref, dnxt_ref, ybuf_ref, h_ref, gate_ref, y_ref, sems, *, tm, d):
    i = pl.program_id(0)

    def start(dref, slot):
        def issue(tk, carry):
            for s in range(MOE_TOP_K):
                pltpu.make_async_copy(_row_tile(ybuf_ref, dref[0, s, tk]), _row_tile(y_ref.at[slot, s], tk),
                                      sems.at[slot]).start(priority=s)
            return carry

        lax.fori_loop(0, tm, issue, 0, unroll=8)

    @pl.when(i == 0)
    def _():
        start(dcur_ref, 0)

    @pl.when(i + 1 < pl.num_programs(0))
    def _():
        start(dnxt_ref, (i + 1) % 2)

    slot = i % 2
    for s in range(MOE_TOP_K):
        pltpu.make_async_copy(ybuf_ref.at[pl.ds(0, tm * SUBLANES), :], y_ref.at[slot, s], sems.at[slot]).wait()
    g0 = gate_ref[:, 0:1]
    g1 = gate_ref[:, 1:2]
    cols = []
    for c in range(d // LANES):
        y0 = y_ref[slot, 0, pl.ds(c, tm, stride=SUBLANES), :]
        y1 = y_ref[slot, 1, pl.ds(c, tm, stride=SUBLANES), :]
        cols.append(h_ref[:, c * LANES:(c + 1) * LANES] + (y0 * g0 + y1 * g1))
    return jnp.concatenate(cols, axis=1)


def _combine_specs(tm, d, nt):
    in_specs = [
        pl.BlockSpec((1, MOE_TOP_K, tm), lambda i: (i, 0, 0), memory_space=pltpu.SMEM),
        pl.BlockSpec((1, MOE_TOP_K, tm), lambda i: (jnp.minimum(i + 1, nt - 1), 0, 0), memory_space=pltpu.SMEM),
        pl.BlockSpec(memory_space=pl.ANY),
        pl.BlockSpec((tm, d), lambda i: (i, 0)),
        pl.BlockSpec((tm, MOE_TOP_K), lambda i: (i, 0)),
    ]
    scratch = [pltpu.VMEM((2, MOE_TOP_K, tm * SUBLANES, LANES), F32), pltpu.SemaphoreType.DMA((2,))]
    return in_specs, scratch


def _final_kernel(dcur_ref, dnxt_ref, ybuf_ref, h_ref, gate_ref, fg_ref, o_ref, y_ref, sems, *, tm, d):
    out = _gather_combine(dcur_ref, dnxt_ref, ybuf_ref, h_ref, gate_ref, y_ref, sems, tm=tm, d=d)
    o_ref[...] = _rms_unit(out) * fg_ref[...]


def _final(moe, final_gain, tm):
    ybuf, dest3, gate_t, h = moe
    t, d = h.shape
    nt = dest3.shape[0]
    in_specs, scratch = _combine_specs(tm, d, nt)
    return pl.pallas_call(
        functools.partial(_final_kernel, tm=tm, d=d),
        grid=(nt,),
        in_specs=in_specs + [pl.BlockSpec((1, d), lambda i: (0, 0))],
        out_specs=pl.BlockSpec((tm, d), lambda i: (i, 0)),
        out_shape=jax.ShapeDtypeStruct((t, d), F32),
        scratch_shapes=scratch,
        compiler_params=_cparams(("arbitrary",)),
        name="final",
    )(dest3, dest3, ybuf, h, gate_t, final_gain.reshape(1, d))


def _dest_kernel(ps_ref, eidx_ref, rank_ref, dest_ref, *, tm):
    eidx = eidx_ref[...]
    dest = rank_ref[...]
    for e in range(MOE_EXPERTS):
        dest = dest + jnp.where(eidx == e, ps_ref[e], 0)
    for i in range(dest_ref.shape[0]):
        dest_ref[i] = dest[:, i * tm:(i + 1) * tm]


def _dest(pad_start, eidx, rank, tm):
    t = eidx.shape[1]
    whole = lambda i, ps: (0, 0)
    grid_spec = pltpu.PrefetchScalarGridSpec(
        num_scalar_prefetch=1,
        grid=(1,),
        in_specs=[pl.BlockSpec((MOE_TOP_K, t), whole), pl.BlockSpec((MOE_TOP_K, t), whole)],
        out_specs=pl.BlockSpec((t // tm, MOE_TOP_K, tm), lambda i, ps: (0, 0, 0)),
    )
    return pl.pallas_call(
        functools.partial(_dest_kernel, tm=tm),
        grid_spec=grid_spec,
        out_shape=jax.ShapeDtypeStruct((t // tm, MOE_TOP_K, tm), jnp.int32),
        compiler_params=_cparams(("arbitrary",)),
        name="dest",
    )(pad_start, eidx, rank)


def _moe_experts(xr, eidx, gate, rank, counts, h, w_gate, w_up, w_down, layer, tm_row):
    t = h.shape[0]
    n_slot = t * MOE_TOP_K
    n_blocks = -(-n_slot // MOE_BLOCK) + MOE_EXPERTS
    cnt = counts[:, 0].astype(jnp.int32)
    padded = (cnt + MOE_BLOCK - 1) // MOE_BLOCK * MOE_BLOCK
    pad_end = jnp.cumsum(padded)
    pad_start = pad_end - padded
    block_start = jnp.arange(n_blocks, dtype=jnp.int32) * MOE_BLOCK
    block_e = jnp.minimum(jnp.sum((pad_end[None, :] <= block_start[:, None]).astype(jnp.int32), axis=1),
                          MOE_EXPERTS - 1)
    n_used = (pad_end[-1:] // MOE_BLOCK).astype(jnp.int32)
    dest3 = _dest(pad_start.astype(jnp.int32), eidx, rank, tm_row)
    xbuf = _dispatch(xr, dest3, pad_end.astype(jnp.int32), n_used, n_blocks, tm_row)
    ybuf = _experts(xbuf, block_e, n_used, padded, w_gate, w_up, w_down, layer)
    return ybuf, dest3, gate.T, h


def _fox_in_kernel(dcur_ref, dnxt_ref, ybuf_ref, h_ref, gate_ref, gkv_ref, gq_ref, wkv_ref, wf_ref, bf_ref, wq_ref,
                   place_ref, hn_ref, k_ref, v_ref, q_ref, og_ref, ka_ref, y_ref, sems, carry_ref,
                   *, tm, sub, d, width, tiles_per_seq, qscale):
    i = pl.program_id(0)

    @pl.when(i % tiles_per_seq == 0)
    def _():
        carry_ref[...] = jnp.zeros_like(carry_ref)

    hn_ref[...] = _gather_combine(dcur_ref, dnxt_ref, ybuf_ref, h_ref, gate_ref, y_ref, sems, tm=tm, d=d)
    ri = lax.broadcasted_iota(jnp.int32, (sub, sub), 0)
    ci = lax.broadcasted_iota(jnp.int32, (sub, sub), 1)
    ltri = jnp.where(ri >= ci, 1.0, 0.0).astype(BF16)
    carry = carry_ref[0:1, :]
    for u in range(tm // sub):
        rows = slice(u * sub, (u + 1) * sub)
        y = _rms_unit(hn_ref[rows, :])
        xkv = (y * gkv_ref[...]).astype(BF16)
        xq = (y * gq_ref[...]).astype(BF16)
        k_ref[rows, :] = _dot(xkv, wkv_ref[:, 0:width]).astype(BF16)
        v_ref[rows, :] = _dot(xkv, wkv_ref[:, width:2 * width]).astype(BF16)
        q_ref[rows, :] = (_dot(xq, wq_ref[:, 0:width]) * qscale).astype(BF16)
        og_ref[rows, :] = _dot(xq, wq_ref[:, width:2 * width]).astype(BF16)

        log_f = _log_sigmoid(_dot(xkv, wf_ref[...]) + bf_ref[...])
        f_hi, f_mid = _split_bf16(log_f)
        f_lo = (log_f - f_hi.astype(F32) - f_mid.astype(F32)).astype(BF16)
        c = carry + (_dot(ltri, f_hi) + (_dot(ltri, f_mid) + _dot(ltri, f_lo)))
        carry = c[sub - 1:sub, :]
        nc = c * (-LOG2_E)
        n_hi, n_mid = _split_bf16(nc)
        n_lo = (nc - n_hi.astype(F32) - n_mid.astype(F32)).astype(BF16)
        ka_ref[rows, :] = _dot(jnp.concatenate([n_hi, n_mid, n_lo], axis=1), place_ref[...]).astype(BF16)
    carry_ref[0:1, :] = carry


ATTN_BIAS_LANES = 3
LOG2_E = 1.4426950408889634


def _bias_lane0(hh, dh):
    return dh if hh == 0 else 0


def _fox_in(moe, kv_gain, q_gain, kv_w, kv_b_f, w_in, seq, tm):
    ybuf, dest3, gate_t, h = moe
    t, d = h.shape
    nt = dest3.shape[0]
    width = w_in.shape[1] // 2
    heads = kv_w.shape[1] - 2 * width
    dh = width // heads
    place = np.zeros((ATTN_BIAS_LANES * heads, width), np.float32)
    for part in range(ATTN_BIAS_LANES):
        for hd in range(heads):
            place[part * heads + hd, (hd // 2) * LANES + _bias_lane0(hd % 2, dh) + part] = 1.0
    const = lambda i: (0, 0)
    row = lambda i: (i, 0)
    big = jax.ShapeDtypeStruct((t, width), BF16)
    comb_specs, comb_scratch = _combine_specs(tm, d, nt)
    return pl.pallas_call(
        functools.partial(_fox_in_kernel, tm=tm, sub=_tile(tm, 256), d=d, width=width, tiles_per_seq=seq // tm, qscale=dh ** -0.5 * LOG2_E),
        grid=(nt,),
        in_specs=comb_specs + [
            pl.BlockSpec((1, d), const),
            pl.BlockSpec((1, d), const),
            pl.BlockSpec((d, 2 * width), const),
            pl.BlockSpec((d, heads), const),
            pl.BlockSpec((1, heads), const),
            pl.BlockSpec((d, 2 * width), const),
            pl.BlockSpec(place.shape, const),
        ],
        out_specs=[pl.BlockSpec((tm, d), row)] + [pl.BlockSpec((tm, width), row)] * 5,
        out_shape=[jax.ShapeDtypeStruct((t, d), F32)] + [big] * 5,
        scratch_shapes=comb_scratch + [pltpu.VMEM((SUBLANES, heads), F32)],
        compiler_params=_cparams(("arbitrary",)),
        name="fox_in",
    )(dest3, dest3, ybuf, h, gate_t, kv_gain.reshape(1, d), q_gain.reshape(1, d), kv_w[:, :2 * width].astype(BF16),
      kv_w[:, 2 * width:].astype(BF16), kv_b_f.reshape(1, heads), w_in.astype(BF16), jnp.asarray(place, BF16))


ATTN_KV_CHUNK = 512


def _attn_kernel(q_ref, k_ref, v_ref, og_ref, kb_ref, o_ref, qa_ref, ka_ref, va_ref, s_ref, *, seq, tq, dh, pairs):
    lane = lax.broadcasted_iota(jnp.int32, (seq, LANES), 1)
    heads_here = [(pr, hh) for pr in range(pairs) for hh in range(2)]
    for n, (pr, hh) in enumerate(heads_here):
        cols = slice(pr * LANES, (pr + 1) * LANES)
        a0 = _bias_lane0(hh, dh)
        own = (lane < dh) if hh == 0 else (lane >= dh)
        ones_q = jnp.where((lane >= a0) & (lane < a0 + ATTN_BIAS_LANES), 1.0, 0.0).astype(BF16)
        qa_ref[n] = jnp.where(own, q_ref[:, cols], ones_q)
        ka_ref[n] = jnp.where(own, k_ref[:, cols], kb_ref[:, cols])
        va_ref[n] = jnp.where(own, v_ref[:, cols], jnp.where(lane == a0, 1.0, 0.0).astype(BF16))

    ri = lax.broadcasted_iota(jnp.int32, (tq, tq), 0)
    ci = lax.broadcasted_iota(jnp.int32, (tq, tq), 1)
    causal = ri >= ci
    lane_q = lax.broadcasted_iota(jnp.int32, (tq, LANES), 1)

    def fold_max(mx, s):
        for g in range(s.shape[1] // LANES):
            mx = jnp.maximum(mx, s[:, g * LANES:(g + 1) * LANES])
        return mx

    for qi in range(seq // tq):
        rows = slice(qi * tq, (qi + 1) * tq)
        past = [(c0, min(c0 + ATTN_KV_CHUNK, qi * tq)) for c0 in range(0, qi * tq, ATTN_KV_CHUNK)]
        row_max = []
        for n in range(len(heads_here)):
            qa = qa_ref[n, rows, :]
            mx = jnp.full((tq, LANES), -jnp.inf, F32)
            for c0, c1 in past:
                s = _dot_nt(qa, ka_ref[n, c0:c1, :])
                s_ref[n, :, c0:c1] = s
                mx = fold_max(mx, s)
            s = jnp.where(causal, _dot_nt(qa, ka_ref[n, rows, :]), -jnp.inf)
            s_ref[n, :, rows] = s
            row_max.append(jnp.max(fold_max(mx, s), axis=1, keepdims=True))
        outs = []
        for n, (pr, hh) in enumerate(heads_here):
            acc = jnp.zeros((tq, LANES), F32)
            for c0, c1 in past + [(qi * tq, (qi + 1) * tq)]:
                p = jnp.exp2(s_ref[n, :, c0:c1] - row_max[n]).astype(BF16)
                acc = acc + _dot(p, va_ref[n, c0:c1, :])
            a0 = _bias_lane0(hh, dh)
            outs.append(acc / acc[:, a0:a0 + 1])
        for pr in range(pairs):
            cols = slice(pr * LANES, (pr + 1) * LANES)
            o = jnp.where(lane_q < dh, outs[2 * pr], outs[2 * pr + 1])
            o_ref[rows, cols] = (o * _sigmoid(og_ref[rows, cols].astype(F32))).astype(BF16)


ATTN_PAIRS = 2


def _attn(q, k, v, og, kb, heads, batch, seq, tq):
    t, width = q.shape
    dh = width // heads
    assert 2 * dh == LANES, "two heads per 128-lane block"
    pairs = ATTN_PAIRS
    blk = pl.BlockSpec((seq, pairs * LANES), lambda b, hp: (b, hp))
    return pl.pallas_call(
        functools.partial(_attn_kernel, seq=seq, tq=tq, dh=dh, pairs=pairs),
        grid=(batch, heads // (2 * pairs)),
        in_specs=[blk] * 5,
        out_specs=blk,
        out_shape=jax.ShapeDtypeStruct((t, width), BF16),
        scratch_shapes=[pltpu.VMEM((2 * pairs, seq, LANES), BF16)] * 3 + [pltpu.VMEM((2 * pairs, tq, seq), F32)],
        compiler_params=_cparams(("arbitrary", "arbitrary")),
        name="attn",
    )(q, k, v, og, kb)


def _tile(n, pref):
    while n % pref:
        pref //= 2
    return pref


def kernel(x, gla_norm, gla_w_in, gla_w_a2, gla_b_a, gla_out_norm, gla_w_out, kv_norm, kv_w, kv_b_f, fox_norm,
           fox_w_in, fox_w_out, ffn_norm, router_coarse_w, router_coarse_b, router_fine_w, router_fine_b,
           expert_w_gate, expert_w_up, expert_w_down, final_norm):
    batch, seq, d = x.shape
    t = batch * seq
    assert gla_norm.shape[0] == 1 and ffn_norm.shape[0] == 2, "trunk is GLA layer + FoX layer"
    tm = _tile(seq, 512)

    def ffn(a, w_out, h, layer):
        h, xr, eidx, gate, rank, counts = _post(a, w_out, h, ffn_norm[layer], router_coarse_w[layer],
                                                router_coarse_b[layer], router_fine_w[layer],
                                                router_fine_b[layer], _tile(seq, 1024), tm)
        return _moe_experts(xr, eidx, gate, rank, counts, h, expert_w_gate, expert_w_up, expert_w_down, layer, tm)

    h = x.reshape(t, d)
    q, k, v, r, la = _gla_in(h, gla_norm[0], gla_w_in[0], gla_w_a2[0], gla_b_a[0], tm)
    a = _gla_core(q, k, v, r, la, gla_out_norm[0], batch, seq, tm)
    moe = ffn(a, gla_w_out[0], h, 0)
    h, kk, vv, qq, og, kb = _fox_in(moe, kv_norm, fox_norm[0], kv_w, kv_b_f, fox_w_in[0], seq, tm)
    a = _attn(qq, kk, vv, og, kb, kv_b_f.shape[0], batch, seq, _tile(seq, 256))
    moe = ffn(a, fox_w_out[0], h, 1)
    return _final(moe, final_norm, tm).reshape(batch, seq, d)
```

```python
import functools

import jax
import jax.numpy as jnp
import numpy as np
from jax import lax
from jax.experimental import pallas as pl
from jax.experimental.pallas import tpu as pltpu

F32 = jnp.float32
BF16 = jnp.bfloat16

RMS_EPS = 1e-6
GLA_HEADS = 4
GLA_GATE_RANK = 16
GLA_GATE_TAU = 16.0
FOX_HEADS = 16
MOE_GROUPS = 4
MOE_EXPERTS_PER_GROUP = 8
MOE_EXPERTS = MOE_GROUPS * MOE_EXPERTS_PER_GROUP
MOE_TOP_K = 2
MOE_BLOCK = 512

LANES = 128
SUBLANES = 8
VMEM_LIMIT = 56 * 1024 * 1024

GLA_CHUNK = 128
ROUTER_ROWS = 40


def _cparams(sem):
    return pltpu.CompilerParams(dimension_semantics=sem, vmem_limit_bytes=VMEM_LIMIT)


def _rms_unit(x):
    return x * lax.rsqrt(jnp.mean(x * x, axis=-1, keepdims=True) + RMS_EPS)


def _log_sigmoid(z):
    return jnp.minimum(z, 0.0) - jnp.log1p(jnp.exp(-jnp.abs(z)))


def _sigmoid(z):
    return 1.0 / (1.0 + jnp.exp(-z))


def _dot(a, b):
    return jnp.dot(a, b, preferred_element_type=F32)


def _dot_nt(a, b):
    return lax.dot_general(a, b, (((1,), (1,)), ((), ())), preferred_element_type=F32)


def _dot_tn(a, b):
    return lax.dot_general(a, b, (((0,), (0,)), ((), ())), preferred_element_type=F32)


def _split_bf16(x):
    hi = x.astype(BF16)
    lo = (x - hi.astype(F32)).astype(BF16)
    return hi, lo


def _gla_in_kernel(h_ref, g_ref, w_ref, wa1_ref, wa2_ref, ba_ref, q_ref, k_ref, v_ref, r_ref, la_ref, *, qk, vw):
    xn = (_rms_unit(h_ref[...]) * g_ref[...]).astype(BF16)
    q_ref[...] = _dot(xn, w_ref[:, 0:qk]).astype(BF16)
    k_ref[...] = _dot(xn, w_ref[:, qk:2 * qk]).astype(BF16)
    v_ref[...] = _dot(xn, w_ref[:, 2 * qk:2 * qk + vw]).astype(BF16)
    r_ref[...] = _dot(xn, w_ref[:, 2 * qk + vw:2 * qk + 2 * vw]).astype(BF16)
    a = _dot(xn, wa1_ref[...])
    z = _dot(a.astype(BF16), wa2_ref[...]) + ba_ref[...]
    la_ref[...] = _log_sigmoid(z) * (1.0 / GLA_GATE_TAU)


def _gla_in(h, gain, w_in, w_a2, b_a, tm):
    t, d = h.shape
    qk = w_a2.shape[1]
    vw = (w_in.shape[1] - 2 * qk - GLA_GATE_RANK) // 2
    w_main = w_in[:, :2 * qk + 2 * vw].astype(BF16)
    w_a1 = w_in[:, 2 * qk + 2 * vw:].astype(BF16)
    const = lambda i: (0, 0)
    row = lambda i: (i, 0)
    return pl.pallas_call(
        functools.partial(_gla_in_kernel, qk=qk, vw=vw),
        grid=(t // tm,),
        in_specs=[
            pl.BlockSpec((tm, d), row),
            pl.BlockSpec((1, d), const),
            pl.BlockSpec(w_main.shape, const),
            pl.BlockSpec(w_a1.shape, const),
            pl.BlockSpec(w_a2.shape, const),
            pl.BlockSpec((1, qk), const),
        ],
        out_specs=[
            pl.BlockSpec((tm, qk), row),
            pl.BlockSpec((tm, qk), row),
            pl.BlockSpec((tm, vw), row),
            pl.BlockSpec((tm, vw), row),
            pl.BlockSpec((tm, qk), row),
        ],
        out_shape=[
            jax.ShapeDtypeStruct((t, qk), BF16),
            jax.ShapeDtypeStruct((t, qk), BF16),
            jax.ShapeDtypeStruct((t, vw), BF16),
            jax.ShapeDtypeStruct((t, vw), BF16),
            jax.ShapeDtypeStruct((t, qk), F32),
        ],
        compiler_params=_cparams(("arbitrary",)),
        name="gla_in",
    )(h, gain.reshape(1, d), w_main, w_a1, w_a2.astype(BF16), b_a.reshape(1, qk))


def _gla_core_kernel(q_ref, k_ref, v_ref, r_ref, la_ref, gn_ref, o_ref, st_ref, *, n_chunks, heads, dk, dv, scale):
    c_len = GLA_CHUNK

    @pl.when(pl.program_id(1) == 0)
    def _():
        st_ref[...] = jnp.zeros_like(st_ref)

    ri = lax.broadcasted_iota(jnp.int32, (c_len, c_len), 0)
    ci = lax.broadcasted_iota(jnp.int32, (c_len, c_len), 1)
    causal = ri >= ci
    ltri = jnp.where(causal, 1.0, 0.0).astype(BF16)

    for c in range(n_chunks):
        sl = pl.ds(c * c_len, c_len)
        for hd in range(heads):
            ks_ = slice(hd * dk, (hd + 1) * dk)
            vs_ = slice(hd * dv, (hd + 1) * dv)
            g_hi, g_lo = _split_bf16(la_ref[sl, ks_])
            cum = _dot(ltri, g_hi) + _dot(ltri, g_lo)
            last = cum[c_len - 1:c_len, :]
            mid = cum[c_len // 2 - 1:c_len // 2, :]
            qf = q_ref[sl, ks_].astype(F32) * scale
            kf = k_ref[sl, ks_].astype(F32)
            vb = v_ref[sl, vs_]
            qs = (qf * jnp.exp(cum - mid)).astype(BF16)
            ks = (kf * jnp.exp(mid - cum)).astype(BF16)
            scores = jnp.where(causal, _dot_nt(qs, ks), 0.0)
            o = _dot(scores.astype(BF16), vb)
            st = st_ref[hd]
            qd = (qf * jnp.exp(cum)).astype(BF16)
            o = o + _dot_nt(qd, st.astype(BF16))
            kd = (kf * jnp.exp(last - cum)).astype(BF16)
            st_ref[hd] = st * jnp.exp(last) + _dot_tn(vb, kd)
            on = _rms_unit(o) * gn_ref[hd:hd + 1, :]
            rr = r_ref[sl, vs_].astype(F32)
            o_ref[sl, vs_] = (on * (rr * _sigmoid(rr))).astype(BF16)


def _gla_core(q, k, v, r, la, out_norm, batch, seq, lc):
    t, qk = q.shape
    vw = v.shape[1]
    heads = GLA_HEADS
    dk, dv = qk // heads, vw // heads
    nj = seq // lc
    tok = lambda b, j: (b * nj + j, 0)
    return pl.pallas_call(
        functools.partial(_gla_core_kernel, n_chunks=lc // GLA_CHUNK, heads=heads, dk=dk, dv=dv, scale=dk ** -0.5),
        grid=(batch, nj),
        in_specs=[
            pl.BlockSpec((lc, qk), tok),
            pl.BlockSpec((lc, qk), tok),
            pl.BlockSpec((lc, vw), tok),
            pl.BlockSpec((lc, vw), tok),
            pl.BlockSpec((lc, qk), tok),
            pl.BlockSpec((heads, dv), lambda b, j: (0, 0)),
        ],
        out_specs=pl.BlockSpec((lc, vw), tok),
        out_shape=jax.ShapeDtypeStruct((t, vw), BF16),
        scratch_shapes=[pltpu.VMEM((heads, dv, dk), F32)],
        compiler_params=_cparams(("arbitrary", "arbitrary")),
        name="gla_core",
    )(q, k, v, r, la, out_norm)


def _post_kernel(a_ref, w_ref, h_ref, g_ref, wr_ref, br_ref,
                 hn_ref, xr_ref, eidx_ref, gate_ref, rank_ref, cnt_ref, base_ref, *, tm, sub, d):
    @pl.when(pl.program_id(0) == 0)
    def _():
        base_ref[...] = jnp.zeros_like(base_ref)

    eg = MOE_EXPERTS_PER_GROUP
    ti = lax.broadcasted_iota(jnp.int32, (sub, sub), 0)
    tj = lax.broadcasted_iota(jnp.int32, (sub, sub), 1)
    upper = jnp.where(ti <= tj, 1.0, 0.0).astype(BF16)
    row_g = lax.broadcasted_iota(jnp.int32, (MOE_GROUPS, sub), 0)
    row_e = lax.broadcasted_iota(jnp.int32, (eg, sub), 0)
    row_x = lax.broadcasted_iota(jnp.int32, (MOE_EXPERTS, sub), 0)
    base = base_ref[:, 0:1]

    for u in range(tm // sub):
        rows = slice(u * sub, (u + 1) * sub)
        hn = h_ref[rows, :] + _dot(a_ref[rows, :], w_ref[...])
        hn_ref[rows, :] = hn
        xn = _rms_unit(hn) * g_ref[...]
        for c in range(d // LANES):
            xr_ref[pl.ds(u * sub * SUBLANES + c, sub, stride=SUBLANES), :] = xn[:, c * LANES:(c + 1) * LANES]

        x_hi, x_lo = _split_bf16(xn)
        r_hi = _dot(x_hi, wr_ref[...])
        r_lo = _dot(x_lo, wr_ref[...])
        lg_t = (r_hi[:, 0:LANES] + r_hi[:, LANES:]) + (r_lo[:, 0:LANES] + r_lo[:, LANES:])
        lg = lg_t.T[0:ROUTER_ROWS] + br_ref[...]
        lc = lg[0:MOE_GROUPS]
        m_c = jnp.max(lc, axis=0, keepdims=True)
        pc_top = 1.0 / jnp.sum(jnp.exp(lc - m_c), axis=0, keepdims=True)
        g_idx = jnp.min(jnp.where(lc == m_c, row_g, MOE_GROUPS), axis=0, keepdims=True)
        lf = jnp.zeros((eg, sub), F32)
        for g in range(MOE_GROUPS):
            lf = lf + jnp.where(g_idx == g, lg[SUBLANES + g * eg:SUBLANES + (g + 1) * eg], 0.0)
        ef = jnp.exp(lf - jnp.max(lf, axis=0, keepdims=True))
        pf = ef / jnp.sum(ef, axis=0, keepdims=True)
        v1 = jnp.max(pf, axis=0, keepdims=True)
        j1 = jnp.min(jnp.where(pf == v1, row_e, eg), axis=0, keepdims=True)
        pf2 = jnp.where(row_e == j1, -1.0, pf)
        v2 = jnp.max(pf2, axis=0, keepdims=True)
        j2 = jnp.min(jnp.where(pf2 == v2, row_e, eg), axis=0, keepdims=True)
        denom = v1 + v2
        e1 = g_idx * eg + j1
        e2 = g_idx * eg + j2
        eidx_ref[0:1, rows] = e1
        eidx_ref[1:2, rows] = e2
        gate_ref[0:1, rows] = pc_top * v1 / denom
        gate_ref[1:2, rows] = pc_top * v2 / denom

        oh1 = row_x == e1
        oh2 = row_x == e2
        pre1 = _dot(jnp.where(oh1, 1.0, 0.0).astype(BF16), upper)
        pre2 = _dot(jnp.where(oh2, 1.0, 0.0).astype(BF16), upper)
        cnt1 = pre1[:, sub - 1:sub]
        cnt2 = pre2[:, sub - 1:sub]
        rank1 = jnp.sum(jnp.where(oh1, base + pre1 - 1.0, 0.0), axis=0, keepdims=True)
        rank2 = jnp.sum(jnp.where(oh2, base + cnt1 + pre2 - 1.0, 0.0), axis=0, keepdims=True)
        rank_ref[0:1, rows] = rank1.astype(jnp.int32)
        rank_ref[1:2, rows] = rank2.astype(jnp.int32)
        base = base + (cnt1 + cnt2)

    new_base = jnp.broadcast_to(base, base_ref.shape)
    base_ref[...] = new_base
    cnt_ref[...] = new_base


def _post(a, w_out, h, gain, w_coarse, b_coarse, w_fine, b_fine, tm, sub):
    t, d = h.shape
    wr = jnp.concatenate([w_coarse, jnp.zeros((d, SUBLANES - MOE_GROUPS), F32), w_fine,
                          jnp.zeros((d, LANES - ROUTER_ROWS), F32)], axis=1)
    wr = jnp.concatenate(_split_bf16(wr), axis=1)
    br = jnp.zeros((ROUTER_ROWS, 1), F32)
    br = br.at[0:MOE_GROUPS, 0].set(b_coarse).at[SUBLANES:, 0].set(b_fine)
    const = lambda i: (0, 0)
    row = lambda i: (i, 0)
    col = lambda i: (0, i)
    return pl.pallas_call(
        functools.partial(_post_kernel, tm=tm, sub=sub, d=d),
        grid=(t // tm,),
        in_specs=[
            pl.BlockSpec((tm, a.shape[1]), row),
            pl.BlockSpec(w_out.shape, const),
            pl.BlockSpec((tm, d), row),
            pl.BlockSpec((1, d), const),
            pl.BlockSpec((d, 2 * LANES), const),
            pl.BlockSpec((ROUTER_ROWS, 1), const),
        ],
        out_specs=[
            pl.BlockSpec((tm, d), row),
            pl.BlockSpec((tm * SUBLANES, LANES), row),
            pl.BlockSpec((MOE_TOP_K, tm), col),
            pl.BlockSpec((MOE_TOP_K, tm), col),
            pl.BlockSpec((MOE_TOP_K, tm), col),
            pl.BlockSpec((MOE_EXPERTS, LANES), const),
        ],
        out_shape=[
            jax.ShapeDtypeStruct((t, d), F32),
            jax.ShapeDtypeStruct((t * SUBLANES, LANES), F32),
            jax.ShapeDtypeStruct((MOE_TOP_K, t), jnp.int32),
            jax.ShapeDtypeStruct((MOE_TOP_K, t), F32),
            jax.ShapeDtypeStruct((MOE_TOP_K, t), jnp.int32),
            jax.ShapeDtypeStruct((MOE_EXPERTS, LANES), F32),
        ],
        scratch_shapes=[pltpu.VMEM((MOE_EXPERTS, LANES), F32)],
        compiler_params=_cparams(("arbitrary",)),
        name="post",
    )(a, w_out.astype(BF16), h, gain.reshape(1, d), wr, br)


def _row_tile(ref, idx):
    return ref.at[pl.ds(pl.multiple_of(idx * SUBLANES, SUBLANES), SUBLANES), :]


def _block_rows(ref, blk):
    return ref.at[pl.ds(pl.multiple_of(blk * (MOE_BLOCK * SUBLANES), MOE_BLOCK * SUBLANES), MOE_BLOCK * SUBLANES), :]


def _dispatch_kernel(pe_ref, nu_ref, dest_ref, xr_ref, xbuf_ref, zero_ref, sem, zsem, *, tm, n_blocks):
    @pl.when(pl.program_id(0) == 0)
    def _():
        zero_ref[...] = jnp.zeros_like(zero_ref)

        def last_block(e):
            return pe_ref[e] // MOE_BLOCK - 1

        def has_rows(e):
            return pe_ref[e] > jnp.where(e == 0, 0, pe_ref[jnp.maximum(e - 1, 0)])

        def zero_block(blk):
            return pltpu.make_async_copy(zero_ref, _block_rows(xbuf_ref, blk), zsem)

        def start_e(e, carry):
            @pl.when(has_rows(e))
            def _():
                zero_block(last_block(e)).start()
            return carry

        def wait_e(e, carry):
            @pl.when(has_rows(e))
            def _():
                zero_block(last_block(e)).wait()
            return carry

        def start_b(blk, carry):
            zero_block(blk).start()
            return carry

        def wait_b(blk, carry):
            zero_block(blk).wait()
            return carry

        lax.fori_loop(0, MOE_EXPERTS, start_e, 0)
        lax.fori_loop(nu_ref[0], n_blocks, start_b, 0)
        lax.fori_loop(0, MOE_EXPERTS, wait_e, 0)
        lax.fori_loop(nu_ref[0], n_blocks, wait_b, 0)

    def issue(tk, carry):
        for s in range(MOE_TOP_K):
            pltpu.make_async_copy(_row_tile(xr_ref, tk), _row_tile(xbuf_ref, dest_ref[0, s, tk]), sem).start(priority=s)
        return carry

    lax.fori_loop(0, tm, issue, 0, unroll=8)
    for s in range(MOE_TOP_K):
        pltpu.make_async_copy(xr_ref, xbuf_ref.at[pl.ds(0, tm * SUBLANES), :], sem).wait()


def _dispatch(xr, dest3, pad_end, n_used, n_blocks, tm):
    nt = dest3.shape[0]
    grid_spec = pltpu.PrefetchScalarGridSpec(
        num_scalar_prefetch=2,
        grid=(nt,),
        in_specs=[
            pl.BlockSpec((1, MOE_TOP_K, tm), lambda i, pe, nu: (i, 0, 0), memory_space=pltpu.SMEM),
            pl.BlockSpec((tm * SUBLANES, LANES), lambda i, pe, nu: (i, 0)),
        ],
        out_specs=pl.BlockSpec(memory_space=pl.ANY),
        scratch_shapes=[pltpu.VMEM((MOE_BLOCK * SUBLANES, LANES), F32), pltpu.SemaphoreType.DMA,
                        pltpu.SemaphoreType.DMA],
    )
    return pl.pallas_call(
        functools.partial(_dispatch_kernel, tm=tm, n_blocks=n_blocks),
        grid_spec=grid_spec,
        out_shape=jax.ShapeDtypeStruct((n_blocks * MOE_BLOCK * SUBLANES, LANES), F32),
        compiler_params=_cparams(("arbitrary",)),
        name="dispatch",
    )(pad_end, n_used, dest3, xr)


def _experts_kernel(be_ref, nu_ref, nxt_ref, par_ref, x_ref, wg_hbm, wu_hbm, wd_hbm, y_ref,
                    x2_ref, wgb_ref, wub_ref, wdb_ref, wgf_ref, wuf_ref, wdf_ref, sems, *, d, layer):
    rows = MOE_BLOCK
    b = pl.program_id(0)
    live = b < nu_ref[0]
    e = be_ref[b]
    new_expert = jnp.logical_or(b == 0, e != be_ref[jnp.maximum(b - 1, 0)])

    def weight_copies(ex, slot):
        return [pltpu.make_async_copy(src.at[layer, ex], dst.at[slot], sems.at[slot, j])
                for j, (src, dst) in enumerate(((wg_hbm, wgf_ref), (wu_hbm, wuf_ref), (wd_hbm, wdf_ref)))]

    @pl.when(jnp.logical_and(live, new_expert))
    def _():
        slot = par_ref[e]

        @pl.when(b == 0)
        def _():
            for cp in weight_copies(e, slot):
                cp.start()

        @pl.when(nxt_ref[e] >= 0)
        def _():
            for cp in weight_copies(nxt_ref[e], 1 - slot):
                cp.start()

        for cp in weight_copies(e, slot):
            cp.wait()
        wgb_ref[...] = wgf_ref[slot].astype(BF16)
        wub_ref[...] = wuf_ref[slot].astype(BF16)
        wdb_ref[...] = wdf_ref[slot].astype(BF16)

    @pl.when(live)
    def _():
        for c in range(d // LANES):
            x2_ref[:, c * LANES:(c + 1) * LANES] = x_ref[pl.ds(c, rows, stride=SUBLANES), :].astype(BF16)
        x2 = x2_ref[...]
        gp = _dot(x2, wgb_ref[...])
        up = _dot(x2, wub_ref[...])
        hid = (gp * _sigmoid(gp) * up).astype(BF16)
        y = _dot(hid, wdb_ref[...])
        for c in range(d // LANES):
            y_ref[pl.ds(c, rows, stride=SUBLANES), :] = y[:, c * LANES:(c + 1) * LANES]


def _experts(xbuf, block_e, n_used, padded, w_gate, w_up, w_down, layer):
    n_blocks = block_e.shape[0]
    d, dff = w_gate.shape[2], w_gate.shape[3]
    owns = padded > 0
    ids = jnp.arange(MOE_EXPERTS, dtype=jnp.int32)
    later = jnp.where(owns, ids, MOE_EXPERTS)
    nxt = lax.cummin(jnp.concatenate([later[1:], jnp.full((1,), MOE_EXPERTS, jnp.int32)]), reverse=True)
    nxt = jnp.where(nxt < MOE_EXPERTS, nxt, -1).astype(jnp.int32)
    par = ((jnp.cumsum(owns.astype(jnp.int32)) - owns.astype(jnp.int32)) % 2).astype(jnp.int32)
    xmap = lambda b, be, nu, nx, pa: (jnp.minimum(b, nu[0] - 1), 0)
    grid_spec = pltpu.PrefetchScalarGridSpec(
        num_scalar_prefetch=4,
        grid=(n_blocks,),
        in_specs=[
            pl.BlockSpec((MOE_BLOCK * SUBLANES, LANES), xmap),
            pl.BlockSpec(memory_space=pl.ANY),
            pl.BlockSpec(memory_space=pl.ANY),
            pl.BlockSpec(memory_space=pl.ANY),
        ],
        out_specs=pl.BlockSpec((MOE_BLOCK * SUBLANES, LANES), xmap),
        scratch_shapes=[pltpu.VMEM((MOE_BLOCK, d), BF16), pltpu.VMEM((d, dff), BF16), pltpu.VMEM((d, dff), BF16),
                        pltpu.VMEM((dff, d), BF16), pltpu.VMEM((2, d, dff), F32), pltpu.VMEM((2, d, dff), F32),
                        pltpu.VMEM((2, dff, d), F32), pltpu.SemaphoreType.DMA((2, 3))],
    )
    return pl.pallas_call(
        functools.partial(_experts_kernel, d=d, layer=layer),
        grid_spec=grid_spec,
        out_shape=jax.ShapeDtypeStruct(xbuf.shape, F32),
        input_output_aliases={4: 0},
        compiler_params=_cparams(("arbitrary",)),
        name="experts",
    )(block_e, n_used, nxt, par, xbuf, w_gate, w_up, w_down)


def _gather_combine(dcur_ref, dnxt_ref, ybuf_ref, h_ref, gate_ref, y_ref, sems, *, tm, d):
    i = pl.program_id(0)

    def start(dref, slot):
        def issue(tk, carry):
            for s in range(MOE_TOP_K):
                pltpu.make_async_copy(_row_tile(ybuf_ref, dref[0, s, tk]), _row_tile(y_ref.at[slot, s], tk),
                                      sems.at[slot]).start(priority=s)
            return carry

        for tk in range(tm):
            issue(tk, 0)

    @pl.when(i == 0)
    def _():
        start(dcur_ref, 0)

    @pl.when(i + 1 < pl.num_programs(0))
    def _():
        start(dnxt_ref, (i + 1) % 2)

    slot = i % 2
    for s in range(MOE_TOP_K):
        pltpu.make_async_copy(ybuf_ref.at[pl.ds(0, tm * SUBLANES), :], y_ref.at[slot, s], sems.at[slot]).wait()
    g0 = gate_ref[:, 0:1]
    g1 = gate_ref[:, 1:2]
    cols = []
    for c in range(d // LANES):
        y0 = y_ref[slot, 0, pl.ds(c, tm, stride=SUBLANES), :]
        y1 = y_ref[slot, 1, pl.ds(c, tm, stride=SUBLANES), :]
        cols.append(h_ref[:, c * LANES:(c + 1) * LANES] + (y0 * g0 + y1 * g1))
    return jnp.concatenate(cols, axis=1)


def _combine_specs(tm, d, nt):
    in_specs = [
        pl.BlockSpec((1, MOE_TOP_K, tm), lambda i: (i, 0, 0), memory_space=pltpu.SMEM),
        pl.BlockSpec((1, MOE_TOP_K, tm), lambda i: (jnp.minimum(i + 1, nt - 1), 0, 0), memory_space=pltpu.SMEM),
        pl.BlockSpec(memory_space=pl.ANY),
        pl.BlockSpec((tm, d), lambda i: (i, 0)),
        pl.BlockSpec((tm, MOE_TOP_K), lambda i: (i, 0)),
    ]
    scratch = [pltpu.VMEM((2, MOE_TOP_K, tm * SUBLANES, LANES), F32), pltpu.SemaphoreType.DMA((2,))]
    return in_specs, scratch


def _final_kernel(dcur_ref, dnxt_ref, ybuf_ref, h_ref, gate_ref, fg_ref, o_ref, y_ref, sems, *, tm, d):
    out = _gather_combine(dcur_ref, dnxt_ref, ybuf_ref, h_ref, gate_ref, y_ref, sems, tm=tm, d=d)
    o_ref[...] = _rms_unit(out) * fg_ref[...]


def _final(moe, final_gain, tm):
    ybuf, dest3, gate_t, h = moe
    t, d = h.shape
    nt = dest3.shape[0]
    in_specs, scratch = _combine_specs(tm, d, nt)
    return pl.pallas_call(
        functools.partial(_final_kernel, tm=tm, d=d),
        grid=(nt,),
        in_specs=in_specs + [pl.BlockSpec((1, d), lambda i: (0, 0))],
        out_specs=pl.BlockSpec((tm, d), lambda i: (i, 0)),
        out_shape=jax.ShapeDtypeStruct((t, d), F32),
        scratch_shapes=scratch,
        compiler_params=_cparams(("arbitrary",)),
        name="final",
    )(dest3, dest3, ybuf, h, gate_t, final_gain.reshape(1, d))


def _dest_kernel(ps_ref, eidx_ref, rank_ref, dest_ref, *, tm):
    eidx = eidx_ref[...]
    dest = rank_ref[...]
    for e in range(MOE_EXPERTS):
        dest = dest + jnp.where(eidx == e, ps_ref[e], 0)
    for i in range(dest_ref.shape[0]):
        dest_ref[i] = dest[:, i * tm:(i + 1) * tm]


def _dest(pad_start, eidx, rank, tm):
    t = eidx.shape[1]
    whole = lambda i, ps: (0, 0)
    grid_spec = pltpu.PrefetchScalarGridSpec(
        num_scalar_prefetch=1,
        grid=(1,),
        in_specs=[pl.BlockSpec((MOE_TOP_K, t), whole), pl.BlockSpec((MOE_TOP_K, t), whole)],
        out_specs=pl.BlockSpec((t // tm, MOE_TOP_K, tm), lambda i, ps: (0, 0, 0)),
    )
    return pl.pallas_call(
        functools.partial(_dest_kernel, tm=tm),
        grid_spec=grid_spec,
        out_shape=jax.ShapeDtypeStruct((t // tm, MOE_TOP_K, tm), jnp.int32),
        compiler_params=_cparams(("arbitrary",)),
        name="dest",
    )(pad_start, eidx, rank)


def _moe_experts(xr, eidx, gate, rank, counts, h, w_gate, w_up, w_down, layer, tm_row):
    t = h.shape[0]
    n_slot = t * MOE_TOP_K
    n_blocks = -(-n_slot // MOE_BLOCK) + MOE_EXPERTS
    cnt = counts[:, 0].astype(jnp.int32)
    padded = (cnt + MOE_BLOCK - 1) // MOE_BLOCK * MOE_BLOCK
    pad_end = jnp.cumsum(padded)
    pad_start = pad_end - padded
    block_start = jnp.arange(n_blocks, dtype=jnp.int32) * MOE_BLOCK
    block_e = jnp.minimum(jnp.sum((pad_end[None, :] <= block_start[:, None]).astype(jnp.int32), axis=1),
                          MOE_EXPERTS - 1)
    n_used = (pad_end[-1:] // MOE_BLOCK).astype(jnp.int32)
    dest3 = _dest(pad_start.astype(jnp.int32), eidx, rank, tm_row)
    xbuf = _dispatch(xr, dest3, pad_end.astype(jnp.int32), n_used, n_blocks, tm_row)
    ybuf = _experts(xbuf, block_e, n_used, padded, w_gate, w_up, w_down, layer)
    return ybuf, dest3, gate.T, h


def _fox_in_kernel(dcur_ref, dnxt_ref, ybuf_ref, h_ref, gate_ref, gkv_ref, gq_ref, wkv_ref, wf_ref, bf_ref, wq_ref,
                   place_ref, hn_ref, k_ref, v_ref, q_ref, og_ref, ka_ref, y_ref, sems, carry_ref,
                   *, tm, sub, d, width, tiles_per_seq, qscale):
    i = pl.program_id(0)

    @pl.when(i % tiles_per_seq == 0)
    def _():
        carry_ref[...] = jnp.zeros_like(carry_ref)

    hn_ref[...] = _gather_combine(dcur_ref, dnxt_ref, ybuf_ref, h_ref, gate_ref, y_ref, sems, tm=tm, d=d)
    ri = lax.broadcasted_iota(jnp.int32, (sub, sub), 0)
    ci = lax.broadcasted_iota(jnp.int32, (sub, sub), 1)
    ltri = jnp.where(ri >= ci, 1.0, 0.0).astype(BF16)
    carry = carry_ref[0:1, :]
    for u in range(tm // sub):
        rows = slice(u * sub, (u + 1) * sub)
        y = _rms_unit(hn_ref[rows, :])
        xkv = (y * gkv_ref[...]).astype(BF16)
        xq = (y * gq_ref[...]).astype(BF16)
        k_ref[rows, :] = _dot(xkv, wkv_ref[:, 0:width]).astype(BF16)
        v_ref[rows, :] = _dot(xkv, wkv_ref[:, width:2 * width]).astype(BF16)
        q_ref[rows, :] = (_dot(xq, wq_ref[:, 0:width]) * qscale).astype(BF16)
        og_ref[rows, :] = _dot(xq, wq_ref[:, width:2 * width]).astype(BF16)

        log_f = _log_sigmoid(_dot(xkv, wf_ref[...]) + bf_ref[...])
        f_hi, f_mid = _split_bf16(log_f)
        f_lo = (log_f - f_hi.astype(F32) - f_mid.astype(F32)).astype(BF16)
        c = carry + (_dot(ltri, f_hi) + (_dot(ltri, f_mid) + _dot(ltri, f_lo)))
        carry = c[sub - 1:sub, :]
        nc = c * (-LOG2_E)
        n_hi, n_mid = _split_bf16(nc)
        n_lo = (nc - n_hi.astype(F32) - n_mid.astype(F32)).astype(BF16)
        ka_ref[rows, :] = _dot(jnp.concatenate([n_hi, n_mid, n_lo], axis=1), place_ref[...]).astype(BF16)
    carry_ref[0:1, :] = carry


ATTN_BIAS_LANES = 3
LOG2_E = 1.4426950408889634


def _bias_lane0(hh, dh):
    return dh if hh == 0 else 0


def _fox_in(moe, kv_gain, q_gain, kv_w, kv_b_f, w_in, seq, tm):
    ybuf, dest3, gate_t, h = moe
    t, d = h.shape
    nt = dest3.shape[0]
    width = w_in.shape[1] // 2
    heads = kv_w.shape[1] - 2 * width
    dh = width // heads
    place = np.zeros((ATTN_BIAS_LANES * heads, width), np.float32)
    for part in range(ATTN_BIAS_LANES):
        for hd in range(heads):
            place[part * heads + hd, (hd // 2) * LANES + _bias_lane0(hd % 2, dh) + part] = 1.0
    const = lambda i: (0, 0)
    row = lambda i: (i, 0)
    big = jax.ShapeDtypeStruct((t, width), BF16)
    comb_specs, comb_scratch = _combine_specs(tm, d, nt)
    return pl.pallas_call(
        functools.partial(_fox_in_kernel, tm=tm, sub=_tile(tm, 256), d=d, width=width, tiles_per_seq=seq // tm, qscale=dh ** -0.5 * LOG2_E),
        grid=(nt,),
        in_specs=comb_specs + [
            pl.BlockSpec((1, d), const),
            pl.BlockSpec((1, d), const),
            pl.BlockSpec((d, 2 * width), const),
            pl.BlockSpec((d, heads), const),
            pl.BlockSpec((1, heads), const),
            pl.BlockSpec((d, 2 * width), const),
            pl.BlockSpec(place.shape, const),
        ],
        out_specs=[pl.BlockSpec((tm, d), row)] + [pl.BlockSpec((tm, width), row)] * 5,
        out_shape=[jax.ShapeDtypeStruct((t, d), F32)] + [big] * 5,
        scratch_shapes=comb_scratch + [pltpu.VMEM((SUBLANES, heads), F32)],
        compiler_params=_cparams(("arbitrary",)),
        name="fox_in",
    )(dest3, dest3, ybuf, h, gate_t, kv_gain.reshape(1, d), q_gain.reshape(1, d), kv_w[:, :2 * width].astype(BF16),
      kv_w[:, 2 * width:].astype(BF16), kv_b_f.reshape(1, heads), w_in.astype(BF16), jnp.asarray(place, BF16))


ATTN_KV_CHUNK = 512


def _attn_kernel(q_ref, k_ref, v_ref, og_ref, kb_ref, o_ref, qa_ref, ka_ref, va_ref, s_ref, *, seq, tq, dh, pairs):
    lane = lax.broadcasted_iota(jnp.int32, (seq, LANES), 1)
    heads_here = [(pr, hh) for pr in range(pairs) for hh in range(2)]
    for n, (pr, hh) in enumerate(heads_here):
        cols = slice(pr * LANES, (pr + 1) * LANES)
        a0 = _bias_lane0(hh, dh)
        own = (lane < dh) if hh == 0 else (lane >= dh)
        ones_q = jnp.where((lane >= a0) & (lane < a0 + ATTN_BIAS_LANES), 1.0, 0.0).astype(BF16)
        qa_ref[n] = jnp.where(own, q_ref[:, cols], ones_q)
        ka_ref[n] = jnp.where(own, k_ref[:, cols], kb_ref[:, cols])
        va_ref[n] = jnp.where(own, v_ref[:, cols], jnp.where(lane == a0, 1.0, 0.0).astype(BF16))

    ri = lax.broadcasted_iota(jnp.int32, (tq, tq), 0)
    ci = lax.broadcasted_iota(jnp.int32, (tq, tq), 1)
    causal = ri >= ci
    lane_q = lax.broadcasted_iota(jnp.int32, (tq, LANES), 1)

    def fold_max(mx, s):
        for g in range(s.shape[1] // LANES):
            mx = jnp.maximum(mx, s[:, g * LANES:(g + 1) * LANES])
        return mx

    for qi in range(seq // tq):
        rows = slice(qi * tq, (qi + 1) * tq)
        past = [(c0, min(c0 + ATTN_KV_CHUNK, qi * tq)) for c0 in range(0, qi * tq, ATTN_KV_CHUNK)]
        row_max = []
        for n in range(len(heads_here)):
            qa = qa_ref[n, rows, :]
            mx = jnp.full((tq, LANES), -jnp.inf, F32)
            for c0, c1 in past:
                s = _dot_nt(qa, ka_ref[n, c0:c1, :])
                s_ref[n, :, c0:c1] = s
                mx = fold_max(mx, s)
            s = jnp.where(causal, _dot_nt(qa, ka_ref[n, rows, :]), -jnp.inf)
            s_ref[n, :, rows] = s
            row_max.append(jnp.max(fold_max(mx, s), axis=1, keepdims=True))
        outs = []
        for n, (pr, hh) in enumerate(heads_here):
            acc = jnp.zeros((tq, LANES), F32)
            for c0, c1 in past + [(qi * tq, (qi + 1) * tq)]:
                p = jnp.exp2(s_ref[n, :, c0:c1] - row_max[n]).astype(BF16)
                acc = acc + _dot(p, va_ref[n, c0:c1, :])
            a0 = _bias_lane0(hh, dh)
            outs.append(acc / acc[:, a0:a0 + 1])
        for pr in range(pairs):
            cols = slice(pr * LANES, (pr + 1) * LANES)
            o = jnp.where(lane_q < dh, outs[2 * pr], outs[2 * pr + 1])
            o_ref[rows, cols] = (o * _sigmoid(og_ref[rows, cols].astype(F32))).astype(BF16)


ATTN_PAIRS = 2


def _attn(q, k, v, og, kb, heads, batch, seq, tq):
    t, width = q.shape
    dh = width // heads
    assert 2 * dh == LANES, "two heads per 128-lane block"
    pairs = ATTN_PAIRS
    blk = pl.BlockSpec((seq, pairs * LANES), lambda b, hp: (b, hp))
    return pl.pallas_call(
        functools.partial(_attn_kernel, seq=seq, tq=tq, dh=dh, pairs=pairs),
        grid=(batch, heads // (2 * pairs)),
        in_specs=[blk] * 5,
        out_specs=blk,
        out_shape=jax.ShapeDtypeStruct((t, width), BF16),
        scratch_shapes=[pltpu.VMEM((2 * pairs, seq, LANES), BF16)] * 3 + [pltpu.VMEM((2 * pairs, tq, seq), F32)],
        compiler_params=_cparams(("arbitrary", "arbitrary")),
        name="attn",
    )(q, k, v, og, kb)


def _tile(n, pref):
    while n % pref:
        pref //= 2
    return pref


def kernel(x, gla_norm, gla_w_in, gla_w_a2, gla_b_a, gla_out_norm, gla_w_out, kv_norm, kv_w, kv_b_f, fox_norm,
           fox_w_in, fox_w_out, ffn_norm, router_coarse_w, router_coarse_b, router_fine_w, router_fine_b,
           expert_w_gate, expert_w_up, expert_w_down, final_norm):
    batch, seq, d = x.shape
    t = batch * seq
    assert gla_norm.shape[0] == 1 and ffn_norm.shape[0] == 2, "trunk is GLA layer + FoX layer"
    tm = _tile(seq, 512)

    def ffn(a, w_out, h, layer):
        h, xr, eidx, gate, rank, counts = _post(a, w_out, h, ffn_norm[layer], router_coarse_w[layer],
                                                router_coarse_b[layer], router_fine_w[layer],
                                                router_fine_b[layer], _tile(seq, 1024), tm)
        return _moe_experts(xr, eidx, gate, rank, counts, h, expert_w_gate, expert_w_up, expert_w_down, layer, tm)

    h = x.reshape(t, d)
    q, k, v, r, la = _gla_in(h, gla_norm[0], gla_w_in[0], gla_w_a2[0], gla_b_a[0], tm)
    a = _gla_core(q, k, v, r, la, gla_out_norm[0], batch, seq, tm)
    moe = ffn(a, gla_w_out[0], h, 0)
    h, kk, vv, qq, og, kb = _fox_in(moe, kv_norm, fox_norm[0], kv_w, kv_b_f, fox_w_in[0], seq, tm)
    a = _attn(qq, kk, vv, og, kb, kv_b_f.shape[0], batch, seq, _tile(seq, 256))
    moe = ffn(a, fox_w_out[0], h, 1)
    return _final(moe, final_norm, tm).reshape(batch, seq, d)
```

```python
import functools

import jax
import jax.numpy as jnp
import numpy as np
from jax import lax
from jax.experimental import pallas as pl
from jax.experimental.pallas import tpu as pltpu

F32 = jnp.float32
BF16 = jnp.bfloat16

RMS_EPS = 1e-6
GLA_HEADS = 4
GLA_GATE_RANK = 16
GLA_GATE_TAU = 16.0
FOX_HEADS = 16
MOE_GROUPS = 4
MOE_EXPERTS_PER_GROUP = 8
MOE_EXPERTS = MOE_GROUPS * MOE_EXPERTS_PER_GROUP
MOE_TOP_K = 2
MOE_BLOCK = 512

LANES = 128
SUBLANES = 8
VMEM_LIMIT = 56 * 1024 * 1024

GLA_CHUNK = 128
ROUTER_ROWS = 40


def _cparams(sem):
    return pltpu.CompilerParams(dimension_semantics=sem, vmem_limit_bytes=VMEM_LIMIT)


def _rms_unit(x):
    return x * lax.rsqrt(jnp.mean(x * x, axis=-1, keepdims=True) + RMS_EPS)


def _log_sigmoid(z):
    return jnp.minimum(z, 0.0) - jnp.log1p(jnp.exp(-jnp.abs(z)))


def _sigmoid(z):
    return 1.0 / (1.0 + jnp.exp(-z))


def _dot(a, b):
    return jnp.dot(a, b, preferred_element_type=F32)


def _dot_nt(a, b):
    return lax.dot_general(a, b, (((1,), (1,)), ((), ())), preferred_element_type=F32)


def _dot_tn(a, b):
    return lax.dot_general(a, b, (((0,), (0,)), ((), ())), preferred_element_type=F32)


def _split_bf16(x):
    hi = x.astype(BF16)
    lo = (x - hi.astype(F32)).astype(BF16)
    return hi, lo


def _gla_in_kernel(h_ref, g_ref, w_ref, wa1_ref, wa2_ref, ba_ref, q_ref, k_ref, v_ref, r_ref, la_ref, *, qk, vw):
    xn = (_rms_unit(h_ref[...]) * g_ref[...]).astype(BF16)
    q_ref[...] = _dot(xn, w_ref[:, 0:qk]).astype(BF16)
    k_ref[...] = _dot(xn, w_ref[:, qk:2 * qk]).astype(BF16)
    v_ref[...] = _dot(xn, w_ref[:, 2 * qk:2 * qk + vw]).astype(BF16)
    r_ref[...] = _dot(xn, w_ref[:, 2 * qk + vw:2 * qk + 2 * vw]).astype(BF16)
    a = _dot(xn, wa1_ref[...])
    z = _dot(a.astype(BF16), wa2_ref[...]) + ba_ref[...]
    la_ref[...] = _log_sigmoid(z) * (1.0 / GLA_GATE_TAU)


def _gla_in(h, gain, w_in, w_a2, b_a, tm):
    t, d = h.shape
    qk = w_a2.shape[1]
    vw = (w_in.shape[1] - 2 * qk - GLA_GATE_RANK) // 2
    w_main = w_in[:, :2 * qk + 2 * vw].astype(BF16)
    w_a1 = w_in[:, 2 * qk + 2 * vw:].astype(BF16)
    const = lambda i: (0, 0)
    row = lambda i: (i, 0)
    return pl.pallas_call(
        functools.partial(_gla_in_kernel, qk=qk, vw=vw),
        grid=(t // tm,),
        in_specs=[
            pl.BlockSpec((tm, d), row),
            pl.BlockSpec((1, d), const),
            pl.BlockSpec(w_main.shape, const),
            pl.BlockSpec(w_a1.shape, const),
            pl.BlockSpec(w_a2.shape, const),
            pl.BlockSpec((1, qk), const),
        ],
        out_specs=[
            pl.BlockSpec((tm, qk), row),
            pl.BlockSpec((tm, qk), row),
            pl.BlockSpec((tm, vw), row),
            pl.BlockSpec((tm, vw), row),
            pl.BlockSpec((tm, qk), row),
        ],
        out_shape=[
            jax.ShapeDtypeStruct((t, qk), BF16),
            jax.ShapeDtypeStruct((t, qk), BF16),
            jax.ShapeDtypeStruct((t, vw), BF16),
            jax.ShapeDtypeStruct((t, vw), BF16),
            jax.ShapeDtypeStruct((t, qk), F32),
        ],
        compiler_params=_cparams(("arbitrary",)),
        name="gla_in",
    )(h, gain.reshape(1, d), w_main, w_a1, w_a2.astype(BF16), b_a.reshape(1, qk))


def _gla_core_kernel(q_ref, k_ref, v_ref, r_ref, la_ref, gn_ref, o_ref, st_ref, *, n_chunks, heads, dk, dv, scale):
    c_len = GLA_CHUNK

    @pl.when(pl.program_id(1) == 0)
    def _():
        st_ref[...] = jnp.zeros_like(st_ref)

    ri = lax.broadcasted_iota(jnp.int32, (c_len, c_len), 0)
    ci = lax.broadcasted_iota(jnp.int32, (c_len, c_len), 1)
    causal = ri >= ci
    ltri = jnp.where(causal, 1.0, 0.0).astype(BF16)

    for c in range(n_chunks):
        sl = pl.ds(c * c_len, c_len)
        for hd in range(heads):
            ks_ = slice(hd * dk, (hd + 1) * dk)
            vs_ = slice(hd * dv, (hd + 1) * dv)
            g_hi, g_lo = _split_bf16(la_ref[sl, ks_])
            cum = _dot(ltri, g_hi) + _dot(ltri, g_lo)
            last = cum[c_len - 1:c_len, :]
            mid = cum[c_len // 2 - 1:c_len // 2, :]
            qf = q_ref[sl, ks_].astype(F32) * scale
            kf = k_ref[sl, ks_].astype(F32)
            vb = v_ref[sl, vs_]
            qs = (qf * jnp.exp(cum - mid)).astype(BF16)
            ks = (kf * jnp.exp(mid - cum)).astype(BF16)
            scores = jnp.where(causal, _dot_nt(qs, ks), 0.0)
            o = _dot(scores.astype(BF16), vb)
            st = st_ref[hd]
            qd = (qf * jnp.exp(cum)).astype(BF16)
            o = o + _dot_nt(qd, st.astype(BF16))
            kd = (kf * jnp.exp(last - cum)).astype(BF16)
            st_ref[hd] = st * jnp.exp(last) + _dot_tn(vb, kd)
            on = _rms_unit(o) * gn_ref[hd:hd + 1, :]
            rr = r_ref[sl, vs_].astype(F32)
            o_ref[sl, vs_] = (on * (rr * _sigmoid(rr))).astype(BF16)


def _gla_core(q, k, v, r, la, out_norm, batch, seq, lc):
    t, qk = q.shape
    vw = v.shape[1]
    heads = GLA_HEADS
    dk, dv = qk // heads, vw // heads
    nj = seq // lc
    tok = lambda b, j: (b * nj + j, 0)
    return pl.pallas_call(
        functools.partial(_gla_core_kernel, n_chunks=lc // GLA_CHUNK, heads=heads, dk=dk, dv=dv, scale=dk ** -0.5),
        grid=(batch, nj),
        in_specs=[
            pl.BlockSpec((lc, qk), tok),
            pl.BlockSpec((lc, qk), tok),
            pl.BlockSpec((lc, vw), tok),
            pl.BlockSpec((lc, vw), tok),
            pl.BlockSpec((lc, qk), tok),
            pl.BlockSpec((heads, dv), lambda b, j: (0, 0)),
        ],
        out_specs=pl.BlockSpec((lc, vw), tok),
        out_shape=jax.ShapeDtypeStruct((t, vw), BF16),
        scratch_shapes=[pltpu.VMEM((heads, dv, dk), F32)],
        compiler_params=_cparams(("arbitrary", "arbitrary")),
        name="gla_core",
    )(q, k, v, r, la, out_norm)


def _post_kernel(a_ref, w_ref, h_ref, g_ref, wr_ref, br_ref,
                 hn_ref, xr_ref, eidx_ref, gate_ref, rank_ref, cnt_ref, base_ref, *, tm, sub, d):
    @pl.when(pl.program_id(0) == 0)
    def _():
        base_ref[...] = jnp.zeros_like(base_ref)

    eg = MOE_EXPERTS_PER_GROUP
    ti = lax.broadcasted_iota(jnp.int32, (sub, sub), 0)
    tj = lax.broadcasted_iota(jnp.int32, (sub, sub), 1)
    upper = jnp.where(ti <= tj, 1.0, 0.0).astype(BF16)
    row_g = lax.broadcasted_iota(jnp.int32, (MOE_GROUPS, sub), 0)
    row_e = lax.broadcasted_iota(jnp.int32, (eg, sub), 0)
    row_x = lax.broadcasted_iota(jnp.int32, (MOE_EXPERTS, sub), 0)
    base = base_ref[:, 0:1]

    for u in range(tm // sub):
        rows = slice(u * sub, (u + 1) * sub)
        hn = h_ref[rows, :] + _dot(a_ref[rows, :], w_ref[...])
        hn_ref[rows, :] = hn
        xn = _rms_unit(hn) * g_ref[...]
        for c in range(d // LANES):
            xr_ref[pl.ds(u * sub * SUBLANES + c, sub, stride=SUBLANES), :] = xn[:, c * LANES:(c + 1) * LANES]

        x_hi, x_lo = _split_bf16(xn)
        r_hi = _dot(x_hi, wr_ref[...])
        r_lo = _dot(x_lo, wr_ref[...])
        lg_t = (r_hi[:, 0:LANES] + r_hi[:, LANES:]) + (r_lo[:, 0:LANES] + r_lo[:, LANES:])
        lg = lg_t.T[0:ROUTER_ROWS] + br_ref[...]
        lc = lg[0:MOE_GROUPS]
        m_c = jnp.max(lc, axis=0, keepdims=True)
        pc_top = 1.0 / jnp.sum(jnp.exp(lc - m_c), axis=0, keepdims=True)
        g_idx = jnp.min(jnp.where(lc == m_c, row_g, MOE_GROUPS), axis=0, keepdims=True)
        lf = jnp.zeros((eg, sub), F32)
        for g in range(MOE_GROUPS):
            lf = lf + jnp.where(g_idx == g, lg[SUBLANES + g * eg:SUBLANES + (g + 1) * eg], 0.0)
        ef = jnp.exp(lf - jnp.max(lf, axis=0, keepdims=True))
        pf = ef / jnp.sum(ef, axis=0, keepdims=True)
        v1 = jnp.max(pf, axis=0, keepdims=True)
        j1 = jnp.min(jnp.where(pf == v1, row_e, eg), axis=0, keepdims=True)
        pf2 = jnp.where(row_e == j1, -1.0, pf)
        v2 = jnp.max(pf2, axis=0, keepdims=True)
        j2 = jnp.min(jnp.where(pf2 == v2, row_e, eg), axis=0, keepdims=True)
        denom = v1 + v2
        e1 = g_idx * eg + j1
        e2 = g_idx * eg + j2
        eidx_ref[0:1, rows] = e1
        eidx_ref[1:2, rows] = e2
        gate_ref[0:1, rows] = pc_top * v1 / denom
        gate_ref[1:2, rows] = pc_top * v2 / denom

        oh1 = row_x == e1
        oh2 = row_x == e2
        pre1 = _dot(jnp.where(oh1, 1.0, 0.0).astype(BF16), upper)
        pre2 = _dot(jnp.where(oh2, 1.0, 0.0).astype(BF16), upper)
        cnt1 = pre1[:, sub - 1:sub]
        cnt2 = pre2[:, sub - 1:sub]
        rank1 = jnp.sum(jnp.where(oh1, base + pre1 - 1.0, 0.0), axis=0, keepdims=True)
        rank2 = jnp.sum(jnp.where(oh2, base + cnt1 + pre2 - 1.0, 0.0), axis=0, keepdims=True)
        rank_ref[0:1, rows] = rank1.astype(jnp.int32)
        rank_ref[1:2, rows] = rank2.astype(jnp.int32)
        base = base + (cnt1 + cnt2)

    new_base = jnp.broadcast_to(base, base_ref.shape)
    base_ref[...] = new_base
    cnt_ref[...] = new_base


def _post(a, w_out, h, gain, w_coarse, b_coarse, w_fine, b_fine, tm, sub):
    t, d = h.shape
    wr = jnp.concatenate([w_coarse, jnp.zeros((d, SUBLANES - MOE_GROUPS), F32), w_fine,
                          jnp.zeros((d, LANES - ROUTER_ROWS), F32)], axis=1)
    wr = jnp.concatenate(_split_bf16(wr), axis=1)
    br = jnp.zeros((ROUTER_ROWS, 1), F32)
    br = br.at[0:MOE_GROUPS, 0].set(b_coarse).at[SUBLANES:, 0].set(b_fine)
    const = lambda i: (0, 0)
    row = lambda i: (i, 0)
    col = lambda i: (0, i)
    return pl.pallas_call(
        functools.partial(_post_kernel, tm=tm, sub=sub, d=d),
        grid=(t // tm,),
        in_specs=[
            pl.BlockSpec((tm, a.shape[1]), row),
            pl.BlockSpec(w_out.shape, const),
            pl.BlockSpec((tm, d), row),
            pl.BlockSpec((1, d), const),
            pl.BlockSpec((d, 2 * LANES), const),
            pl.BlockSpec((ROUTER_ROWS, 1), const),
        ],
        out_specs=[
            pl.BlockSpec((tm, d), row),
            pl.BlockSpec((tm * SUBLANES, LANES), row),
            pl.BlockSpec((MOE_TOP_K, tm), col),
            pl.BlockSpec((MOE_TOP_K, tm), col),
            pl.BlockSpec((MOE_TOP_K, tm), col),
            pl.BlockSpec((MOE_EXPERTS, LANES), const),
        ],
        out_shape=[
            jax.ShapeDtypeStruct((t, d), F32),
            jax.ShapeDtypeStruct((t * SUBLANES, LANES), F32),
            jax.ShapeDtypeStruct((MOE_TOP_K, t), jnp.int32),
            jax.ShapeDtypeStruct((MOE_TOP_K, t), F32),
            jax.ShapeDtypeStruct((MOE_TOP_K, t), jnp.int32),
            jax.ShapeDtypeStruct((MOE_EXPERTS, LANES), F32),
        ],
        scratch_shapes=[pltpu.VMEM((MOE_EXPERTS, LANES), F32)],
        compiler_params=_cparams(("arbitrary",)),
        name="post",
    )(a, w_out.astype(BF16), h, gain.reshape(1, d), wr, br)


def _row_tile(ref, idx):
    return ref.at[pl.ds(pl.multiple_of(idx * SUBLANES, SUBLANES), SUBLANES), :]


def _block_rows(ref, blk):
    return ref.at[pl.ds(pl.multiple_of(blk * (MOE_BLOCK * SUBLANES), MOE_BLOCK * SUBLANES), MOE_BLOCK * SUBLANES), :]


def _dispatch_kernel(pe_ref, nu_ref, dest_ref, xr_ref, xbuf_ref, zero_ref, sem, zsem, *, tm, n_blocks):
    @pl.when(pl.program_id(0) == 0)
    def _():
        zero_ref[...] = jnp.zeros_like(zero_ref)

        def last_block(e):
            return pe_ref[e] // MOE_BLOCK - 1

        def has_rows(e):
            return pe_ref[e] > jnp.where(e == 0, 0, pe_ref[jnp.maximum(e - 1, 0)])

        def zero_block(blk):
            return pltpu.make_async_copy(zero_ref, _block_rows(xbuf_ref, blk), zsem)

        def start_e(e, carry):
            @pl.when(has_rows(e))
            def _():
                zero_block(last_block(e)).start()
            return carry

        def wait_e(e, carry):
            @pl.when(has_rows(e))
            def _():
                zero_block(last_block(e)).wait()
            return carry

        def start_b(blk, carry):
            zero_block(blk).start()
            return carry

        def wait_b(blk, carry):
            zero_block(blk).wait()
            return carry

        lax.fori_loop(0, MOE_EXPERTS, start_e, 0)
        lax.fori_loop(nu_ref[0], n_blocks, start_b, 0)
        lax.fori_loop(0, MOE_EXPERTS, wait_e, 0)
        lax.fori_loop(nu_ref[0], n_blocks, wait_b, 0)

    def issue(tk, carry):
        for s in range(MOE_TOP_K):
            pltpu.make_async_copy(_row_tile(xr_ref, tk), _row_tile(xbuf_ref, dest_ref[0, s, tk]), sem).start(priority=s)
        return carry

    for tk in range(tm):
        issue(tk, 0)
    for s in range(MOE_TOP_K):
        pltpu.make_async_copy(xr_ref, xbuf_ref.at[pl.ds(0, tm * SUBLANES), :], sem).wait()


def _dispatch(xr, dest3, pad_end, n_used, n_blocks, tm):
    nt = dest3.shape[0]
    grid_spec = pltpu.PrefetchScalarGridSpec(
        num_scalar_prefetch=2,
        grid=(nt,),
        in_specs=[
            pl.BlockSpec((1, MOE_TOP_K, tm), lambda i, pe, nu: (i, 0, 0), memory_space=pltpu.SMEM),
            pl.BlockSpec((tm * SUBLANES, LANES), lambda i, pe, nu: (i, 0)),
        ],
        out_specs=pl.BlockSpec(memory_space=pl.ANY),
        scratch_shapes=[pltpu.VMEM((MOE_BLOCK * SUBLANES, LANES), F32), pltpu.SemaphoreType.DMA,
                        pltpu.SemaphoreType.DMA],
    )
    return pl.pallas_call(
        functools.partial(_dispatch_kernel, tm=tm, n_blocks=n_blocks),
        grid_spec=grid_spec,
        out_shape=jax.ShapeDtypeStruct((n_blocks * MOE_BLOCK * SUBLANES, LANES), F32),
        compiler_params=_cparams(("arbitrary",)),
        name="dispatch",
    )(pad_end, n_used, dest3, xr)


def _experts_kernel(be_ref, nu_ref, nxt_ref, par_ref, x_ref, wg_hbm, wu_hbm, wd_hbm, y_ref,
                    x2_ref, wgb_ref, wub_ref, wdb_ref, wgf_ref, wuf_ref, wdf_ref, sems, *, d, layer):
    rows = MOE_BLOCK
    b = pl.program_id(0)
    live = b < nu_ref[0]
    e = be_ref[b]
    new_expert = jnp.logical_or(b == 0, e != be_ref[jnp.maximum(b - 1, 0)])

    def weight_copies(ex, slot):
        return [pltpu.make_async_copy(src.at[layer, ex], dst.at[slot], sems.at[slot, j])
                for j, (src, dst) in enumerate(((wg_hbm, wgf_ref), (wu_hbm, wuf_ref), (wd_hbm, wdf_ref)))]

    @pl.when(jnp.logical_and(live, new_expert))
    def _():
        slot = par_ref[e]

        @pl.when(b == 0)
        def _():
            for cp in weight_copies(e, slot):
                cp.start()

        @pl.when(nxt_ref[e] >= 0)
        def _():
            for cp in weight_copies(nxt_ref[e], 1 - slot):
                cp.start()

        for cp in weight_copies(e, slot):
            cp.wait()
        wgb_ref[...] = wgf_ref[slot].astype(BF16)
        wub_ref[...] = wuf_ref[slot].astype(BF16)
        wdb_ref[...] = wdf_ref[slot].astype(BF16)

    @pl.when(live)
    def _():
        for c in range(d // LANES):
            x2_ref[:, c * LANES:(c + 1) * LANES] = x_ref[pl.ds(c, rows, stride=SUBLANES), :].astype(BF16)
        x2 = x2_ref[...]
        gp = _dot(x2, wgb_ref[...])
        up = _dot(x2, wub_ref[...])
        hid = (gp * _sigmoid(gp) * up).astype(BF16)
        y = _dot(hid, wdb_ref[...])
        for c in range(d // LANES):
            y_ref[pl.ds(c, rows, stride=SUBLANES), :] = y[:, c * LANES:(c + 1) * LANES]


def _experts(xbuf, block_e, n_used, padded, w_gate, w_up, w_down, layer):
    n_blocks = block_e.shape[0]
    d, dff = w_gate.shape[2], w_gate.shape[3]
    owns = padded > 0
    ids = jnp.arange(MOE_EXPERTS, dtype=jnp.int32)
    later = jnp.where(owns, ids, MOE_EXPERTS)
    nxt = lax.cummin(jnp.concatenate([later[1:], jnp.full((1,), MOE_EXPERTS, jnp.int32)]), reverse=True)
    nxt = jnp.where(nxt < MOE_EXPERTS, nxt, -1).astype(jnp.int32)
    par = ((jnp.cumsum(owns.astype(jnp.int32)) - owns.astype(jnp.int32)) % 2).astype(jnp.int32)
    xmap = lambda b, be, nu, nx, pa: (jnp.minimum(b, nu[0] - 1), 0)
    grid_spec = pltpu.PrefetchScalarGridSpec(
        num_scalar_prefetch=4,
        grid=(n_blocks,),
        in_specs=[
            pl.BlockSpec((MOE_BLOCK * SUBLANES, LANES), xmap),
            pl.BlockSpec(memory_space=pl.ANY),
            pl.BlockSpec(memory_space=pl.ANY),
            pl.BlockSpec(memory_space=pl.ANY),
        ],
        out_specs=pl.BlockSpec((MOE_BLOCK * SUBLANES, LANES), xmap),
        scratch_shapes=[pltpu.VMEM((MOE_BLOCK, d), BF16), pltpu.VMEM((d, dff), BF16), pltpu.VMEM((d, dff), BF16),
                        pltpu.VMEM((dff, d), BF16), pltpu.VMEM((2, d, dff), F32), pltpu.VMEM((2, d, dff), F32),
                        pltpu.VMEM((2, dff, d), F32), pltpu.SemaphoreType.DMA((2, 3))],
    )
    return pl.pallas_call(
        functools.partial(_experts_kernel, d=d, layer=layer),
        grid_spec=grid_spec,
        out_shape=jax.ShapeDtypeStruct(xbuf.shape, F32),
        input_output_aliases={4: 0},
        compiler_params=_cparams(("arbitrary",)),
        name="experts",
    )(block_e, n_used, nxt, par, xbuf, w_gate, w_up, w_down)


def _gather_combine(dcur_ref, dnxt_ref, ybuf_ref, h_ref, gate_ref, y_ref, sems, *, tm, d):
    i = pl.program_id(0)

    def start(dref, slot):
        def issue(tk, carry):
            for s in range(MOE_TOP_K):
                pltpu.make_async_copy(_row_tile(ybuf_ref, dref[0, s, tk]), _row_tile(y_ref.at[slot, s], tk),
                                      sems.at[slot]).start(priority=s)
            return carry

        for tk in range(tm):
            issue(tk, 0)

    @pl.when(i == 0)
    def _():
        start(dcur_ref, 0)

    @pl.when(i + 1 < pl.num_programs(0))
    def _():
        start(dnxt_ref, (i + 1) % 2)

    slot = i % 2
    for s in range(MOE_TOP_K):
        pltpu.make_async_copy(ybuf_ref.at[pl.ds(0, tm * SUBLANES), :], y_ref.at[slot, s], sems.at[slot]).wait()
    g0 = gate_ref[:, 0:1]
    g1 = gate_ref[:, 1:2]
    cols = []
    for c in range(d // LANES):
        y0 = y_ref[slot, 0, pl.ds(c, tm, stride=SUBLANES), :]
        y1 = y_ref[slot, 1, pl.ds(c, tm, stride=SUBLANES), :]
        cols.append(h_ref[:, c * LANES:(c + 1) * LANES] + (y0 * g0 + y1 * g1))
    return jnp.concatenate(cols, axis=1)


def _combine_specs(tm, d, nt):
    in_specs = [
        pl.BlockSpec((1, MOE_TOP_K, tm), lambda i: (i, 0, 0), memory_space=pltpu.SMEM),
        pl.BlockSpec((1, MOE_TOP_K, tm), lambda i: (jnp.minimum(i + 1, nt - 1), 0, 0), memory_space=pltpu.SMEM),
        pl.BlockSpec(memory_space=pl.ANY),
        pl.BlockSpec((tm, d), lambda i: (i, 0)),
        pl.BlockSpec((tm, MOE_TOP_K), lambda i: (i, 0)),
    ]
    scratch = [pltpu.VMEM((2, MOE_TOP_K, tm * SUBLANES, LANES), F32), pltpu.SemaphoreType.DMA((2,))]
    return in_specs, scratch


def _final_kernel(dcur_ref, dnxt_ref, ybuf_ref, h_ref, gate_ref, fg_ref, o_ref, y_ref, sems, *, tm, d):
    out = _gather_combine(dcur_ref, dnxt_ref, ybuf_ref, h_ref, gate_ref, y_ref, sems, tm=tm, d=d)
    o_ref[...] = _rms_unit(out) * fg_ref[...]


def _final(moe, final_gain, tm):
    ybuf, dest3, gate_t, h = moe
    t, d = h.shape
    nt = dest3.shape[0]
    in_specs, scratch = _combine_specs(tm, d, nt)
    return pl.pallas_call(
        functools.partial(_final_kernel, tm=tm, d=d),
        grid=(nt,),
        in_specs=in_specs + [pl.BlockSpec((1, d), lambda i: (0, 0))],
        out_specs=pl.BlockSpec((tm, d), lambda i: (i, 0)),
        out_shape=jax.ShapeDtypeStruct((t, d), F32),
        scratch_shapes=scratch,
        compiler_params=_cparams(("arbitrary",)),
        name="final",
    )(dest3, dest3, ybuf, h, gate_t, final_gain.reshape(1, d))


def _dest_kernel(ps_ref, eidx_ref, rank_ref, dest_ref, *, tm):
    eidx = eidx_ref[...]
    dest = rank_ref[...]
    for e in range(MOE_EXPERTS):
        dest = dest + jnp.where(eidx == e, ps_ref[e], 0)
    for i in range(dest_ref.shape[0]):
        dest_ref[i] = dest[:, i * tm:(i + 1) * tm]


def _dest(pad_start, eidx, rank, tm):
    t = eidx.shape[1]
    whole = lambda i, ps: (0, 0)
    grid_spec = pltpu.PrefetchScalarGridSpec(
        num_scalar_prefetch=1,
        grid=(1,),
        in_specs=[pl.BlockSpec((MOE_TOP_K, t), whole), pl.BlockSpec((MOE_TOP_K, t), whole)],
        out_specs=pl.BlockSpec((t // tm, MOE_TOP_K, tm), lambda i, ps: (0, 0, 0)),
    )
    return pl.pallas_call(
        functools.partial(_dest_kernel, tm=tm),
        grid_spec=grid_spec,
        out_shape=jax.ShapeDtypeStruct((t // tm, MOE_TOP_K, tm), jnp.int32),
        compiler_params=_cparams(("arbitrary",)),
        name="dest",
    )(pad_start, eidx, rank)


def _moe_experts(xr, eidx, gate, rank, counts, h, w_gate, w_up, w_down, layer, tm_row):
    t = h.shape[0]
    n_slot = t * MOE_TOP_K
    n_blocks = -(-n_slot // MOE_BLOCK) + MOE_EXPERTS
    cnt = counts[:, 0].astype(jnp.int32)
    padded = (cnt + MOE_BLOCK - 1) // MOE_BLOCK * MOE_BLOCK
    pad_end = jnp.cumsum(padded)
    pad_start = pad_end - padded
    block_start = jnp.arange(n_blocks, dtype=jnp.int32) * MOE_BLOCK
    block_e = jnp.minimum(jnp.sum((pad_end[None, :] <= block_start[:, None]).astype(jnp.int32), axis=1),
                          MOE_EXPERTS - 1)
    n_used = (pad_end[-1:] // MOE_BLOCK).astype(jnp.int32)
    dest3 = _dest(pad_start.astype(jnp.int32), eidx, rank, tm_row)
    xbuf = _dispatch(xr, dest3, pad_end.astype(jnp.int32), n_used, n_blocks, tm_row)
    ybuf = _experts(xbuf, block_e, n_used, padded, w_gate, w_up, w_down, layer)
    return ybuf, dest3, gate.T, h


def _fox_in_kernel(dcur_ref, dnxt_ref, ybuf_ref, h_ref, gate_ref, gkv_ref, gq_ref, wkv_ref, wf_ref, bf_ref, wq_ref,
                   place_ref, hn_ref, k_ref, v_ref, q_ref, og_ref, ka_ref, y_ref, sems, carry_ref,
                   *, tm, sub, d, width, tiles_per_seq, qscale):
    i = pl.program_id(0)

    @pl.when(i % tiles_per_seq == 0)
    def _():
        carry_ref[...] = jnp.zeros_like(carry_ref)

    hn_ref[...] = _gather_combine(dcur_ref, dnxt_ref, ybuf_ref, h_ref, gate_ref, y_ref, sems, tm=tm, d=d)
    ri = lax.broadcasted_iota(jnp.int32, (sub, sub), 0)
    ci = lax.broadcasted_iota(jnp.int32, (sub, sub), 1)
    ltri = jnp.where(ri >= ci, 1.0, 0.0).astype(BF16)
    carry = carry_ref[0:1, :]
    for u in range(tm // sub):
        rows = slice(u * sub, (u + 1) * sub)
        y = _rms_unit(hn_ref[rows, :])
        xkv = (y * gkv_ref[...]).astype(BF16)
        xq = (y * gq_ref[...]).astype(BF16)
        k_ref[rows, :] = _dot(xkv, wkv_ref[:, 0:width]).astype(BF16)
        v_ref[rows, :] = _dot(xkv, wkv_ref[:, width:2 * width]).astype(BF16)
        q_ref[rows, :] = (_dot(xq, wq_ref[:, 0:width]) * qscale).astype(BF16)
        og_ref[rows, :] = _dot(xq, wq_ref[:, width:2 * width]).astype(BF16)

        log_f = _log_sigmoid(_dot(xkv, wf_ref[...]) + bf_ref[...])
        f_hi, f_mid = _split_bf16(log_f)
        f_lo = (log_f - f_hi.astype(F32) - f_mid.astype(F32)).astype(BF16)
        c = carry + (_dot(ltri, f_hi) + (_dot(ltri, f_mid) + _dot(ltri, f_lo)))
        carry = c[sub - 1:sub, :]
        nc = c * (-LOG2_E)
        n_hi, n_mid = _split_bf16(nc)
        n_lo = (nc - n_hi.astype(F32) - n_mid.astype(F32)).astype(BF16)
        ka_ref[rows, :] = _dot(jnp.concatenate([n_hi, n_mid, n_lo], axis=1), place_ref[...]).astype(BF16)
    carry_ref[0:1, :] = carry


ATTN_BIAS_LANES = 3
LOG2_E = 1.4426950408889634


def _bias_lane0(hh, dh):
    return dh if hh == 0 else 0


def _fox_in(moe, kv_gain, q_gain, kv_w, kv_b_f, w_in, seq, tm):
    ybuf, dest3, gate_t, h = moe
    t, d = h.shape
    nt = dest3.shape[0]
    width = w_in.shape[1] // 2
    heads = kv_w.shape[1] - 2 * width
    dh = width // heads
    place = np.zeros((ATTN_BIAS_LANES * heads, width), np.float32)
    for part in range(ATTN_BIAS_LANES):
        for hd in range(heads):
            place[part * heads + hd, (hd // 2) * LANES + _bias_lane0(hd % 2, dh) + part] = 1.0
    const = lambda i: (0, 0)
    row = lambda i: (i, 0)
    big = jax.ShapeDtypeStruct((t, width), BF16)
    comb_specs, comb_scratch = _combine_specs(tm, d, nt)
    return pl.pallas_call(
        functools.partial(_fox_in_kernel, tm=tm, sub=_tile(tm, 256), d=d, width=width, tiles_per_seq=seq // tm, qscale=dh ** -0.5 * LOG2_E),
        grid=(nt,),
        in_specs=comb_specs + [
            pl.BlockSpec((1, d), const),
            pl.BlockSpec((1, d), const),
            pl.BlockSpec((d, 2 * width), const),
            pl.BlockSpec((d, heads), const),
            pl.BlockSpec((1, heads), const),
            pl.BlockSpec((d, 2 * width), const),
            pl.BlockSpec(place.shape, const),
        ],
        out_specs=[pl.BlockSpec((tm, d), row)] + [pl.BlockSpec((tm, width), row)] * 5,
        out_shape=[jax.ShapeDtypeStruct((t, d), F32)] + [big] * 5,
        scratch_shapes=comb_scratch + [pltpu.VMEM((SUBLANES, heads), F32)],
        compiler_params=_cparams(("arbitrary",)),
        name="fox_in",
    )(dest3, dest3, ybuf, h, gate_t, kv_gain.reshape(1, d), q_gain.reshape(1, d), kv_w[:, :2 * width].astype(BF16),
      kv_w[:, 2 * width:].astype(BF16), kv_b_f.reshape(1, heads), w_in.astype(BF16), jnp.asarray(place, BF16))


ATTN_KV_CHUNK = 512


def _attn_kernel(q_ref, k_ref, v_ref, og_ref, kb_ref, o_ref, qa_ref, ka_ref, va_ref, s_ref, *, seq, tq, dh, pairs):
    lane = lax.broadcasted_iota(jnp.int32, (seq, LANES), 1)
    heads_here = [(pr, hh) for pr in range(pairs) for hh in range(2)]
    for n, (pr, hh) in enumerate(heads_here):
        cols = slice(pr * LANES, (pr + 1) * LANES)
        a0 = _bias_lane0(hh, dh)
        own = (lane < dh) if hh == 0 else (lane >= dh)
        ones_q = jnp.where((lane >= a0) & (lane < a0 + ATTN_BIAS_LANES), 1.0, 0.0).astype(BF16)
        qa_ref[n] = jnp.where(own, q_ref[:, cols], ones_q)
        ka_ref[n] = jnp.where(own, k_ref[:, cols], kb_ref[:, cols])
        va_ref[n] = jnp.where(own, v_ref[:, cols], jnp.where(lane == a0, 1.0, 0.0).astype(BF16))

    ri = lax.broadcasted_iota(jnp.int32, (tq, tq), 0)
    ci = lax.broadcasted_iota(jnp.int32, (tq, tq), 1)
    causal = ri >= ci
    lane_q = lax.broadcasted_iota(jnp.int32, (tq, LANES), 1)

    def fold_max(mx, s):
        for g in range(s.shape[1] // LANES):
            mx = jnp.maximum(mx, s[:, g * LANES:(g + 1) * LANES])
        return mx

    for qi in range(seq // tq):
        rows = slice(qi * tq, (qi + 1) * tq)
        past = [(c0, min(c0 + ATTN_KV_CHUNK, qi * tq)) for c0 in range(0, qi * tq, ATTN_KV_CHUNK)]
        row_max = []
        for n in range(len(heads_here)):
            qa = qa_ref[n, rows, :]
            mx = jnp.full((tq, LANES), -jnp.inf, F32)
            for c0, c1 in past:
                s = _dot_nt(qa, ka_ref[n, c0:c1, :])
                s_ref[n, :, c0:c1] = s
                mx = fold_max(mx, s)
            s = jnp.where(causal, _dot_nt(qa, ka_ref[n, rows, :]), -jnp.inf)
            s_ref[n, :, rows] = s
            row_max.append(jnp.max(fold_max(mx, s), axis=1, keepdims=True))
        outs = []
        for n, (pr, hh) in enumerate(heads_here):
            acc = jnp.zeros((tq, LANES), F32)
            for c0, c1 in past + [(qi * tq, (qi + 1) * tq)]:
                p = jnp.exp2(s_ref[n, :, c0:c1] - row_max[n]).astype(BF16)
                acc = acc + _dot(p, va_ref[n, c0:c1, :])
            a0 = _bias_lane0(hh, dh)
            outs.append(acc / acc[:, a0:a0 + 1])
        for pr in range(pairs):
            cols = slice(pr * LANES, (pr + 1) * LANES)
            o = jnp.where(lane_q < dh, outs[2 * pr], outs[2 * pr + 1])
            o_ref[rows, cols] = (o * _sigmoid(og_ref[rows, cols].astype(F32))).astype(BF16)


ATTN_PAIRS = 2


def _attn(q, k, v, og, kb, heads, batch, seq, tq):
    t, width = q.shape
    dh = width // heads
    assert 2 * dh == LANES, "two heads per 128-lane block"
    pairs = ATTN_PAIRS
    blk = pl.BlockSpec((seq, pairs * LANES), lambda b, hp: (b, hp))
    return pl.pallas_call(
        functools.partial(_attn_kernel, seq=seq, tq=tq, dh=dh, pairs=pairs),
        grid=(batch, heads // (2 * pairs)),
        in_specs=[blk] * 5,
        out_specs=blk,
        out_shape=jax.ShapeDtypeStruct((t, width), BF16),
        scratch_shapes=[pltpu.VMEM((2 * pairs, seq, LANES), BF16)] * 3 + [pltpu.VMEM((2 * pairs, tq, seq), F32)],
        compiler_params=_cparams(("arbitrary", "arbitrary")),
        name="attn",
    )(q, k, v, og, kb)


def _tile(n, pref):
    while n % pref:
        pref //= 2
    return pref


def kernel(x, gla_norm, gla_w_in, gla_w_a2, gla_b_a, gla_out_norm, gla_w_out, kv_norm, kv_w, kv_b_f, fox_norm,
           fox_w_in, fox_w_out, ffn_norm, router_coarse_w, router_coarse_b, router_fine_w, router_fine_b,
           expert_w_gate, expert_w_up, expert_w_down, final_norm):
    batch, seq, d = x.shape
    t = batch * seq
    assert gla_norm.shape[0] == 1 and ffn_norm.shape[0] == 2, "trunk is GLA layer + FoX layer"
    tm = _tile(seq, 512)

    def ffn(a, w_out, h, layer):
        h, xr, eidx, gate, rank, counts = _post(a, w_out, h, ffn_norm[layer], router_coarse_w[layer],
                                                router_coarse_b[layer], router_fine_w[layer],
                                                router_fine_b[layer], _tile(seq, 1024), tm)
        return _moe_experts(xr, eidx, gate, rank, counts, h, expert_w_gate, expert_w_up, expert_w_down, layer, tm)

    h = x.reshape(t, d)
    q, k, v, r, la = _gla_in(h, gla_norm[0], gla_w_in[0], gla_w_a2[0], gla_b_a[0], tm)
    a = _gla_core(q, k, v, r, la, gla_out_norm[0], batch, seq, tm)
    moe = ffn(a, gla_w_out[0], h, 0)
    h, kk, vv, qq, og, kb = _fox_in(moe, kv_norm, fox_norm[0], kv_w, kv_b_f, fox_w_in[0], seq, tm)
    a = _attn(qq, kk, vv, og, kb, kv_b_f.shape[0], batch, seq, _tile(seq, 256))
    moe = ffn(a, fox_w_out[0], h, 1)
    return _final(moe, final_norm, tm).reshape(batch, seq, d)
```

```python
import functools

import jax
import jax.numpy as jnp
import numpy as np
from jax import lax
from jax.experimental import pallas as pl
from jax.experimental.pallas import tpu as pltpu

F32 = jnp.float32
BF16 = jnp.bfloat16

RMS_EPS = 1e-6
GLA_HEADS = 4
GLA_GATE_RANK = 16
GLA_GATE_TAU = 16.0
FOX_HEADS = 16
MOE_GROUPS = 4
MOE_EXPERTS_PER_GROUP = 8
MOE_EXPERTS = MOE_GROUPS * MOE_EXPERTS_PER_GROUP
MOE_TOP_K = 2
MOE_BLOCK = 512

LANES = 128
SUBLANES = 8
VMEM_LIMIT = 56 * 1024 * 1024

GLA_CHUNK = 128
ROUTER_ROWS = 40


def _cparams(sem):
    return pltpu.CompilerParams(dimension_semantics=sem, vmem_limit_bytes=VMEM_LIMIT)


def _rms_unit(x):
    return x * lax.rsqrt(jnp.mean(x * x, axis=-1, keepdims=True) + RMS_EPS)


def _log_sigmoid(z):
    return jnp.minimum(z, 0.0) - jnp.log1p(jnp.exp(-jnp.abs(z)))


def _sigmoid(z):
    return 1.0 / (1.0 + jnp.exp(-z))


def _dot(a, b):
    return jnp.dot(a, b, preferred_element_type=F32)


def _dot_nt(a, b):
    return lax.dot_general(a, b, (((1,), (1,)), ((), ())), preferred_element_type=F32)


def _dot_tn(a, b):
    return lax.dot_general(a, b, (((0,), (0,)), ((), ())), preferred_element_type=F32)


def _split_bf16(x):
    hi = x.astype(BF16)
    lo = (x - hi.astype(F32)).astype(BF16)
    return hi, lo


def _gla_in_kernel(h_ref, g_ref, w_ref, wa1_ref, wa2_ref, ba_ref, q_ref, k_ref, v_ref, r_ref, la_ref, *, qk, vw):
    xn = (_rms_unit(h_ref[...]) * g_ref[...]).astype(BF16)
    q_ref[...] = _dot(xn, w_ref[:, 0:qk]).astype(BF16)
    k_ref[...] = _dot(xn, w_ref[:, qk:2 * qk]).astype(BF16)
    v_ref[...] = _dot(xn, w_ref[:, 2 * qk:2 * qk + vw]).astype(BF16)
    r_ref[...] = _dot(xn, w_ref[:, 2 * qk + vw:2 * qk + 2 * vw]).astype(BF16)
    a = _dot(xn, wa1_ref[...])
    z = _dot(a.astype(BF16), wa2_ref[...]) + ba_ref[...]
    la_ref[...] = _log_sigmoid(z) * (1.0 / GLA_GATE_TAU)


def _gla_in(h, gain, w_in, w_a2, b_a, tm):
    t, d = h.shape
    qk = w_a2.shape[1]
    vw = (w_in.shape[1] - 2 * qk - GLA_GATE_RANK) // 2
    w_main = w_in[:, :2 * qk + 2 * vw].astype(BF16)
    w_a1 = w_in[:, 2 * qk + 2 * vw:].astype(BF16)
    const = lambda i: (0, 0)
    row = lambda i: (i, 0)
    return pl.pallas_call(
        functools.partial(_gla_in_kernel, qk=qk, vw=vw),
        grid=(t // tm,),
        in_specs=[
            pl.BlockSpec((tm, d), row),
            pl.BlockSpec((1, d), const),
            pl.BlockSpec(w_main.shape, const),
            pl.BlockSpec(w_a1.shape, const),
            pl.BlockSpec(w_a2.shape, const),
            pl.BlockSpec((1, qk), const),
        ],
        out_specs=[
            pl.BlockSpec((tm, qk), row),
            pl.BlockSpec((tm, qk), row),
            pl.BlockSpec((tm, vw), row),
            pl.BlockSpec((tm, vw), row),
            pl.BlockSpec((tm, qk), row),
        ],
        out_shape=[
            jax.ShapeDtypeStruct((t, qk), BF16),
            jax.ShapeDtypeStruct((t, qk), BF16),
            jax.ShapeDtypeStruct((t, vw), BF16),
            jax.ShapeDtypeStruct((t, vw), BF16),
            jax.ShapeDtypeStruct((t, qk), F32),
        ],
        compiler_params=_cparams(("arbitrary",)),
        name="gla_in",
    )(h, gain.reshape(1, d), w_main, w_a1, w_a2.astype(BF16), b_a.reshape(1, qk))


def _gla_core_kernel(q_ref, k_ref, v_ref, r_ref, la_ref, gn_ref, o_ref, st_ref,
                     *, n_seq, n_chunks, heads, dk, dv, scale):
    c_len = GLA_CHUNK

    @pl.when(pl.program_id(1) == 0)
    def _():
        st_ref[...] = jnp.zeros_like(st_ref)

    ri = lax.broadcasted_iota(jnp.int32, (c_len, c_len), 0)
    ci = lax.broadcasted_iota(jnp.int32, (c_len, c_len), 1)
    causal = ri >= ci
    ltri = jnp.where(causal, 1.0, 0.0).astype(BF16)

    for c in range(n_chunks):
        sl = pl.ds(c * c_len, c_len)
        for sq in range(n_seq):
            for hd in range(heads):
                ks_ = slice(hd * dk, (hd + 1) * dk)
                vs_ = slice(hd * dv, (hd + 1) * dv)
                g_hi, g_lo = _split_bf16(la_ref[sq, sl, ks_])
                cum = _dot(ltri, g_hi) + _dot(ltri, g_lo)
                last = cum[c_len - 1:c_len, :]
                mid = cum[c_len // 2 - 1:c_len // 2, :]
                qf = q_ref[sq, sl, ks_].astype(F32) * scale
                kf = k_ref[sq, sl, ks_].astype(F32)
                vb = v_ref[sq, sl, vs_]
                qs = (qf * jnp.exp(cum - mid)).astype(BF16)
                ks = (kf * jnp.exp(mid - cum)).astype(BF16)
                scores = jnp.where(causal, _dot_nt(qs, ks), 0.0)
                o = _dot(scores.astype(BF16), vb)
                st = st_ref[sq, hd]
                qd = (qf * jnp.exp(cum)).astype(BF16)
                o = o + _dot_nt(qd, st.astype(BF16))
                kd = (kf * jnp.exp(last - cum)).astype(BF16)
                st_ref[sq, hd] = st * jnp.exp(last) + _dot_tn(vb, kd)
                on = _rms_unit(o) * gn_ref[hd:hd + 1, :]
                rr = r_ref[sq, sl, vs_].astype(F32)
                o_ref[sq, sl, vs_] = (on * (rr * _sigmoid(rr))).astype(BF16)


def _gla_core(q, k, v, r, la, out_norm, batch, seq, lc, n_seq):
    t, qk = q.shape
    vw = v.shape[1]
    heads = GLA_HEADS
    dk, dv = qk // heads, vw // heads
    tok = lambda b, j: (b, j, 0)
    seqs = lambda a: a.reshape(batch, seq, a.shape[1])
    out = pl.pallas_call(
        functools.partial(_gla_core_kernel, n_seq=n_seq, n_chunks=lc // GLA_CHUNK, heads=heads, dk=dk, dv=dv,
                          scale=dk ** -0.5),
        grid=(batch // n_seq, seq // lc),
        in_specs=[
            pl.BlockSpec((n_seq, lc, qk), tok),
            pl.BlockSpec((n_seq, lc, qk), tok),
            pl.BlockSpec((n_seq, lc, vw), tok),
            pl.BlockSpec((n_seq, lc, vw), tok),
            pl.BlockSpec((n_seq, lc, qk), tok),
            pl.BlockSpec((heads, dv), lambda b, j: (0, 0)),
        ],
        out_specs=pl.BlockSpec((n_seq, lc, vw), tok),
        out_shape=jax.ShapeDtypeStruct((batch, seq, vw), BF16),
        scratch_shapes=[pltpu.VMEM((n_seq, heads, dv, dk), F32)],
        compiler_params=_cparams(("arbitrary", "arbitrary")),
        name="gla_core",
    )(seqs(q), seqs(k), seqs(v), seqs(r), seqs(la), out_norm)
    return out.reshape(t, vw)


def _post_kernel(a_ref, w_ref, h_ref, g_ref, wr_ref, br_ref,
                 hn_ref, xr_ref, eidx_ref, gate_ref, rank_ref, cnt_ref, base_ref, *, tm, sub, d):
    @pl.when(pl.program_id(0) == 0)
    def _():
        base_ref[...] = jnp.zeros_like(base_ref)

    eg = MOE_EXPERTS_PER_GROUP
    ti = lax.broadcasted_iota(jnp.int32, (sub, sub), 0)
    tj = lax.broadcasted_iota(jnp.int32, (sub, sub), 1)
    upper = jnp.where(ti <= tj, 1.0, 0.0).astype(BF16)
    row_g = lax.broadcasted_iota(jnp.int32, (MOE_GROUPS, sub), 0)
    row_e = lax.broadcasted_iota(jnp.int32, (eg, sub), 0)
    row_x = lax.broadcasted_iota(jnp.int32, (MOE_EXPERTS, sub), 0)
    base = base_ref[:, 0:1]

    for u in range(tm // sub):
        rows = slice(u * sub, (u + 1) * sub)
        hn = h_ref[rows, :] + _dot(a_ref[rows, :], w_ref[...])
        hn_ref[rows, :] = hn
        xn = _rms_unit(hn) * g_ref[...]
        for c in range(d // LANES):
            xr_ref[pl.ds(u * sub * SUBLANES + c, sub, stride=SUBLANES), :] = xn[:, c * LANES:(c + 1) * LANES]

        x_hi, x_lo = _split_bf16(xn)
        r_hi = _dot(x_hi, wr_ref[...])
        r_lo = _dot(x_lo, wr_ref[...])
        lg_t = (r_hi[:, 0:LANES] + r_hi[:, LANES:]) + (r_lo[:, 0:LANES] + r_lo[:, LANES:])
        lg = lg_t.T[0:ROUTER_ROWS] + br_ref[...]
        lc = lg[0:MOE_GROUPS]
        m_c = jnp.max(lc, axis=0, keepdims=True)
        pc_top = 1.0 / jnp.sum(jnp.exp(lc - m_c), axis=0, keepdims=True)
        g_idx = jnp.min(jnp.where(lc == m_c, row_g, MOE_GROUPS), axis=0, keepdims=True)
        lf = jnp.zeros((eg, sub), F32)
        for g in range(MOE_GROUPS):
            lf = lf + jnp.where(g_idx == g, lg[SUBLANES + g * eg:SUBLANES + (g + 1) * eg], 0.0)
        ef = jnp.exp(lf - jnp.max(lf, axis=0, keepdims=True))
        pf = ef / jnp.sum(ef, axis=0, keepdims=True)
        v1 = jnp.max(pf, axis=0, keepdims=True)
        j1 = jnp.min(jnp.where(pf == v1, row_e, eg), axis=0, keepdims=True)
        pf2 = jnp.where(row_e == j1, -1.0, pf)
        v2 = jnp.max(pf2, axis=0, keepdims=True)
        j2 = jnp.min(jnp.where(pf2 == v2, row_e, eg), axis=0, keepdims=True)
        denom = v1 + v2
        e1 = g_idx * eg + j1
        e2 = g_idx * eg + j2
        eidx_ref[0:1, rows] = e1
        eidx_ref[1:2, rows] = e2
        gate_ref[0:1, rows] = pc_top * v1 / denom
        gate_ref[1:2, rows] = pc_top * v2 / denom

        oh1 = row_x == e1
        oh2 = row_x == e2
        pre1 = _dot(jnp.where(oh1, 1.0, 0.0).astype(BF16), upper)
        pre2 = _dot(jnp.where(oh2, 1.0, 0.0).astype(BF16), upper)
        cnt1 = pre1[:, sub - 1:sub]
        cnt2 = pre2[:, sub - 1:sub]
        rank1 = jnp.sum(jnp.where(oh1, base + pre1 - 1.0, 0.0), axis=0, keepdims=True)
        rank2 = jnp.sum(jnp.where(oh2, base + cnt1 + pre2 - 1.0, 0.0), axis=0, keepdims=True)
        rank_ref[0:1, rows] = rank1.astype(jnp.int32)
        rank_ref[1:2, rows] = rank2.astype(jnp.int32)
        base = base + (cnt1 + cnt2)

    new_base = jnp.broadcast_to(base, base_ref.shape)
    base_ref[...] = new_base
    cnt_ref[...] = new_base


def _post(a, w_out, h, gain, w_coarse, b_coarse, w_fine, b_fine, tm, sub):
    t, d = h.shape
    wr = jnp.concatenate([w_coarse, jnp.zeros((d, SUBLANES - MOE_GROUPS), F32), w_fine,
                          jnp.zeros((d, LANES - ROUTER_ROWS), F32)], axis=1)
    wr = jnp.concatenate(_split_bf16(wr), axis=1)
    br = jnp.zeros((ROUTER_ROWS, 1), F32)
    br = br.at[0:MOE_GROUPS, 0].set(b_coarse).at[SUBLANES:, 0].set(b_fine)
    const = lambda i: (0, 0)
    row = lambda i: (i, 0)
    col = lambda i: (0, i)
    return pl.pallas_call(
        functools.partial(_post_kernel, tm=tm, sub=sub, d=d),
        grid=(t // tm,),
        in_specs=[
            pl.BlockSpec((tm, a.shape[1]), row),
            pl.BlockSpec(w_out.shape, const),
            pl.BlockSpec((tm, d), row),
            pl.BlockSpec((1, d), const),
            pl.BlockSpec((d, 2 * LANES), const),
            pl.BlockSpec((ROUTER_ROWS, 1), const),
        ],
        out_specs=[
            pl.BlockSpec((tm, d), row),
            pl.BlockSpec((tm * SUBLANES, LANES), row),
            pl.BlockSpec((MOE_TOP_K, tm), col),
            pl.BlockSpec((MOE_TOP_K, tm), col),
            pl.BlockSpec((MOE_TOP_K, tm), col),
            pl.BlockSpec((MOE_EXPERTS, LANES), const),
        ],
        out_shape=[
            jax.ShapeDtypeStruct((t, d), F32),
            jax.ShapeDtypeStruct((t * SUBLANES, LANES), F32),
            jax.ShapeDtypeStruct((MOE_TOP_K, t), jnp.int32),
            jax.ShapeDtypeStruct((MOE_TOP_K, t), F32),
            jax.ShapeDtypeStruct((MOE_TOP_K, t), jnp.int32),
            jax.ShapeDtypeStruct((MOE_EXPERTS, LANES), F32),
        ],
        scratch_shapes=[pltpu.VMEM((MOE_EXPERTS, LANES), F32)],
        compiler_params=_cparams(("arbitrary",)),
        name="post",
    )(a, w_out.astype(BF16), h, gain.reshape(1, d), wr, br)


def _row_tile(ref, idx):
    return ref.at[pl.ds(pl.multiple_of(idx * SUBLANES, SUBLANES), SUBLANES), :]


def _block_rows(ref, blk):
    return ref.at[pl.ds(pl.multiple_of(blk * (MOE_BLOCK * SUBLANES), MOE_BLOCK * SUBLANES), MOE_BLOCK * SUBLANES), :]


def _dispatch_kernel(pe_ref, nu_ref, dest_ref, xr_ref, xbuf_ref, zero_ref, sem, zsem, *, tm, n_blocks):
    @pl.when(pl.program_id(0) == 0)
    def _():
        zero_ref[...] = jnp.zeros_like(zero_ref)

        def last_block(e):
            return pe_ref[e] // MOE_BLOCK - 1

        def has_rows(e):
            return pe_ref[e] > jnp.where(e == 0, 0, pe_ref[jnp.maximum(e - 1, 0)])

        def zero_block(blk):
            return pltpu.make_async_copy(zero_ref, _block_rows(xbuf_ref, blk), zsem)

        def start_e(e, carry):
            @pl.when(has_rows(e))
            def _():
                zero_block(last_block(e)).start()
            return carry

        def wait_e(e, carry):
            @pl.when(has_rows(e))
            def _():
                zero_block(last_block(e)).wait()
            return carry

        def start_b(blk, carry):
            zero_block(blk).start()
            return carry

        def wait_b(blk, carry):
            zero_block(blk).wait()
            return carry

        lax.fori_loop(0, MOE_EXPERTS, start_e, 0)
        lax.fori_loop(nu_ref[0], n_blocks, start_b, 0)
        lax.fori_loop(0, MOE_EXPERTS, wait_e, 0)
        lax.fori_loop(nu_ref[0], n_blocks, wait_b, 0)

    def issue(tk, carry):
        for s in range(MOE_TOP_K):
            pltpu.make_async_copy(_row_tile(xr_ref, tk), _row_tile(xbuf_ref, dest_ref[0, s, tk]), sem).start(priority=s)
        return carry

    for tk in range(tm):
        issue(tk, 0)
    for s in range(MOE_TOP_K):
        pltpu.make_async_copy(xr_ref, xbuf_ref.at[pl.ds(0, tm * SUBLANES), :], sem).wait()


def _dispatch(xr, dest3, pad_end, n_used, n_blocks, tm):
    nt = dest3.shape[0]
    grid_spec = pltpu.PrefetchScalarGridSpec(
        num_scalar_prefetch=2,
        grid=(nt,),
        in_specs=[
            pl.BlockSpec((1, MOE_TOP_K, tm), lambda i, pe, nu: (i, 0, 0), memory_space=pltpu.SMEM),
            pl.BlockSpec((tm * SUBLANES, LANES), lambda i, pe, nu: (i, 0)),
        ],
        out_specs=pl.BlockSpec(memory_space=pl.ANY),
        scratch_shapes=[pltpu.VMEM((MOE_BLOCK * SUBLANES, LANES), F32), pltpu.SemaphoreType.DMA,
                        pltpu.SemaphoreType.DMA],
    )
    return pl.pallas_call(
        functools.partial(_dispatch_kernel, tm=tm, n_blocks=n_blocks),
        grid_spec=grid_spec,
        out_shape=jax.ShapeDtypeStruct((n_blocks * MOE_BLOCK * SUBLANES, LANES), F32),
        compiler_params=_cparams(("arbitrary",)),
        name="dispatch",
    )(pad_end, n_used, dest3, xr)


def _experts_kernel(be_ref, nu_ref, nxt_ref, par_ref, x_ref, wg_hbm, wu_hbm, wd_hbm, y_ref,
                    x2_ref, wgb_ref, wub_ref, wdb_ref, wgf_ref, wuf_ref, wdf_ref, sems, *, d, layer):
    rows = MOE_BLOCK
    b = pl.program_id(0)
    live = b < nu_ref[0]
    e = be_ref[b]
    new_expert = jnp.logical_or(b == 0, e != be_ref[jnp.maximum(b - 1, 0)])

    def weight_copies(ex, slot):
        return [pltpu.make_async_copy(src.at[layer, ex], dst.at[slot], sems.at[slot, j])
                for j, (src, dst) in enumerate(((wg_hbm, wgf_ref), (wu_hbm, wuf_ref), (wd_hbm, wdf_ref)))]

    @pl.when(jnp.logical_and(live, new_expert))
    def _():
        slot = par_ref[e]

        @pl.when(b == 0)
        def _():
            for cp in weight_copies(e, slot):
                cp.start()

        @pl.when(nxt_ref[e] >= 0)
        def _():
            for cp in weight_copies(nxt_ref[e], 1 - slot):
                cp.start()

        for cp in weight_copies(e, slot):
            cp.wait()
        wgb_ref[...] = wgf_ref[slot].astype(BF16)
        wub_ref[...] = wuf_ref[slot].astype(BF16)
        wdb_ref[...] = wdf_ref[slot].astype(BF16)

    @pl.when(live)
    def _():
        for c in range(d // LANES):
            x2_ref[:, c * LANES:(c + 1) * LANES] = x_ref[pl.ds(c, rows, stride=SUBLANES), :].astype(BF16)
        x2 = x2_ref[...]
        gp = _dot(x2, wgb_ref[...])
        up = _dot(x2, wub_ref[...])
        hid = (gp * _sigmoid(gp) * up).astype(BF16)
        y = _dot(hid, wdb_ref[...])
        for c in range(d // LANES):
            y_ref[pl.ds(c, rows, stride=SUBLANES), :] = y[:, c * LANES:(c + 1) * LANES]


def _experts(xbuf, block_e, n_used, padded, w_gate, w_up, w_down, layer):
    n_blocks = block_e.shape[0]
    d, dff = w_gate.shape[2], w_gate.shape[3]
    owns = padded > 0
    ids = jnp.arange(MOE_EXPERTS, dtype=jnp.int32)
    later = jnp.where(owns, ids, MOE_EXPERTS)
    nxt = lax.cummin(jnp.concatenate([later[1:], jnp.full((1,), MOE_EXPERTS, jnp.int32)]), reverse=True)
    nxt = jnp.where(nxt < MOE_EXPERTS, nxt, -1).astype(jnp.int32)
    par = ((jnp.cumsum(owns.astype(jnp.int32)) - owns.astype(jnp.int32)) % 2).astype(jnp.int32)
    xmap = lambda b, be, nu, nx, pa: (jnp.minimum(b, nu[0] - 1), 0)
    grid_spec = pltpu.PrefetchScalarGridSpec(
        num_scalar_prefetch=4,
        grid=(n_blocks,),
        in_specs=[
            pl.BlockSpec((MOE_BLOCK * SUBLANES, LANES), xmap),
            pl.BlockSpec(memory_space=pl.ANY),
            pl.BlockSpec(memory_space=pl.ANY),
            pl.BlockSpec(memory_space=pl.ANY),
        ],
        out_specs=pl.BlockSpec((MOE_BLOCK * SUBLANES, LANES), xmap),
        scratch_shapes=[pltpu.VMEM((MOE_BLOCK, d), BF16), pltpu.VMEM((d, dff), BF16), pltpu.VMEM((d, dff), BF16),
                        pltpu.VMEM((dff, d), BF16), pltpu.VMEM((2, d, dff), F32), pltpu.VMEM((2, d, dff), F32),
                        pltpu.VMEM((2, dff, d), F32), pltpu.SemaphoreType.DMA((2, 3))],
    )
    return pl.pallas_call(
        functools.partial(_experts_kernel, d=d, layer=layer),
        grid_spec=grid_spec,
        out_shape=jax.ShapeDtypeStruct(xbuf.shape, F32),
        input_output_aliases={4: 0},
        compiler_params=_cparams(("arbitrary",)),
        name="experts",
    )(block_e, n_used, nxt, par, xbuf, w_gate, w_up, w_down)


def _gather_combine(dcur_ref, dnxt_ref, ybuf_ref, h_ref, gate_ref, y_ref, sems, *, tm, d):
    i = pl.program_id(0)

    def start(dref, slot):
        def issue(tk, carry):
            for s in range(MOE_TOP_K):
                pltpu.make_async_copy(_row_tile(ybuf_ref, dref[0, s, tk]), _row_tile(y_ref.at[slot, s], tk),
                                      sems.at[slot]).start(priority=s)
            return carry

        for tk in range(tm):
            issue(tk, 0)

    @pl.when(i == 0)
    def _():
        start(dcur_ref, 0)

    @pl.when(i + 1 < pl.num_programs(0))
    def _():
        start(dnxt_ref, (i + 1) % 2)

    slot = i % 2
    for s in range(MOE_TOP_K):
        pltpu.make_async_copy(ybuf_ref.at[pl.ds(0, tm * SUBLANES), :], y_ref.at[slot, s], sems.at[slot]).wait()
    gate_t = jnp.concatenate([gate_ref[...], jnp.zeros((SUBLANES - MOE_TOP_K, tm), F32)], axis=0).T
    g0 = gate_t[:, 0:1]
    g1 = gate_t[:, 1:2]
    cols = []
    for c in range(d // LANES):
        y0 = y_ref[slot, 0, pl.ds(c, tm, stride=SUBLANES), :]
        y1 = y_ref[slot, 1, pl.ds(c, tm, stride=SUBLANES), :]
        cols.append(h_ref[:, c * LANES:(c + 1) * LANES] + (y0 * g0 + y1 * g1))
    return jnp.concatenate(cols, axis=1)


def _combine_specs(tm, d, nt):
    in_specs = [
        pl.BlockSpec((1, MOE_TOP_K, tm), lambda i: (i, 0, 0), memory_space=pltpu.SMEM),
        pl.BlockSpec((1, MOE_TOP_K, tm), lambda i: (jnp.minimum(i + 1, nt - 1), 0, 0), memory_space=pltpu.SMEM),
        pl.BlockSpec(memory_space=pl.ANY),
        pl.BlockSpec((tm, d), lambda i: (i, 0)),
        pl.BlockSpec((MOE_TOP_K, tm), lambda i: (0, i)),
    ]
    scratch = [pltpu.VMEM((2, MOE_TOP_K, tm * SUBLANES, LANES), F32), pltpu.SemaphoreType.DMA((2,))]
    return in_specs, scratch


def _final_kernel(dcur_ref, dnxt_ref, ybuf_ref, h_ref, gate_ref, fg_ref, o_ref, y_ref, sems, *, tm, d):
    out = _gather_combine(dcur_ref, dnxt_ref, ybuf_ref, h_ref, gate_ref, y_ref, sems, tm=tm, d=d)
    o_ref[...] = _rms_unit(out) * fg_ref[...]


def _final(moe, final_gain, tm):
    ybuf, dest3, gate, h = moe
    t, d = h.shape
    nt = dest3.shape[0]
    in_specs, scratch = _combine_specs(tm, d, nt)
    return pl.pallas_call(
        functools.partial(_final_kernel, tm=tm, d=d),
        grid=(nt,),
        in_specs=in_specs + [pl.BlockSpec((1, d), lambda i: (0, 0))],
        out_specs=pl.BlockSpec((tm, d), lambda i: (i, 0)),
        out_shape=jax.ShapeDtypeStruct((t, d), F32),
        scratch_shapes=scratch,
        compiler_params=_cparams(("arbitrary",)),
        name="final",
    )(dest3, dest3, ybuf, h, gate, final_gain.reshape(1, d))


def _dest_kernel(ps_ref, eidx_ref, rank_ref, dest_ref, *, tm):
    eidx = eidx_ref[...]
    dest = rank_ref[...]
    for e in range(MOE_EXPERTS):
        dest = dest + jnp.where(eidx == e, ps_ref[e], 0)
    for i in range(dest_ref.shape[0]):
        dest_ref[i] = dest[:, i * tm:(i + 1) * tm]


def _dest(pad_start, eidx, rank, tm):
    t = eidx.shape[1]
    whole = lambda i, ps: (0, 0)
    grid_spec = pltpu.PrefetchScalarGridSpec(
        num_scalar_prefetch=1,
        grid=(1,),
        in_specs=[pl.BlockSpec((MOE_TOP_K, t), whole), pl.BlockSpec((MOE_TOP_K, t), whole)],
        out_specs=pl.BlockSpec((t // tm, MOE_TOP_K, tm), lambda i, ps: (0, 0, 0)),
    )
    return pl.pallas_call(
        functools.partial(_dest_kernel, tm=tm),
        grid_spec=grid_spec,
        out_shape=jax.ShapeDtypeStruct((t // tm, MOE_TOP_K, tm), jnp.int32),
        compiler_params=_cparams(("arbitrary",)),
        name="dest",
    )(pad_start, eidx, rank)


def _moe_experts(xr, eidx, gate, rank, counts, h, w_gate, w_up, w_down, layer, tm_row):
    t = h.shape[0]
    n_slot = t * MOE_TOP_K
    n_blocks = -(-n_slot // MOE_BLOCK) + MOE_EXPERTS
    cnt = counts[:, 0].astype(jnp.int32)
    padded = (cnt + MOE_BLOCK - 1) // MOE_BLOCK * MOE_BLOCK
    pad_end = jnp.cumsum(padded)
    pad_start = pad_end - padded
    block_start = jnp.arange(n_blocks, dtype=jnp.int32) * MOE_BLOCK
    block_e = jnp.minimum(jnp.sum((pad_end[None, :] <= block_start[:, None]).astype(jnp.int32), axis=1),
                          MOE_EXPERTS - 1)
    n_used = (pad_end[-1:] // MOE_BLOCK).astype(jnp.int32)
    dest3 = _dest(pad_start.astype(jnp.int32), eidx, rank, tm_row)
    xbuf = _dispatch(xr, dest3, pad_end.astype(jnp.int32), n_used, n_blocks, tm_row)
    ybuf = _experts(xbuf, block_e, n_used, padded, w_gate, w_up, w_down, layer)
    return ybuf, dest3, gate, h


def _fox_in_kernel(dcur_ref, dnxt_ref, ybuf_ref, h_ref, gate_ref, gkv_ref, gq_ref, wkv_ref, wf_ref, bf_ref, wq_ref,
                   place_ref, hn_ref, k_ref, v_ref, q_ref, og_ref, ka_ref, y_ref, sems, carry_ref,
                   *, tm, sub, d, width, tiles_per_seq, qscale):
    i = pl.program_id(0)

    @pl.when(i % tiles_per_seq == 0)
    def _():
        carry_ref[...] = jnp.zeros_like(carry_ref)

    hn_ref[...] = _gather_combine(dcur_ref, dnxt_ref, ybuf_ref, h_ref, gate_ref, y_ref, sems, tm=tm, d=d)
    ri = lax.broadcasted_iota(jnp.int32, (sub, sub), 0)
    ci = lax.broadcasted_iota(jnp.int32, (sub, sub), 1)
    ltri = jnp.where(ri >= ci, 1.0, 0.0).astype(BF16)
    carry = carry_ref[0:1, :]
    for u in range(tm // sub):
        rows = slice(u * sub, (u + 1) * sub)
        y = _rms_unit(hn_ref[rows, :])
        xkv = (y * gkv_ref[...]).astype(BF16)
        xq = (y * gq_ref[...]).astype(BF16)
        k_ref[rows, :] = _dot(xkv, wkv_ref[:, 0:width]).astype(BF16)
        v_ref[rows, :] = _dot(xkv, wkv_ref[:, width:2 * width]).astype(BF16)
        q_ref[rows, :] = (_dot(xq, wq_ref[:, 0:width]) * qscale).astype(BF16)
        og_ref[rows, :] = _dot(xq, wq_ref[:, width:2 * width]).astype(BF16)

        log_f = _log_sigmoid(_dot(xkv, wf_ref[...]) + bf_ref[...])
        f_hi, f_mid = _split_bf16(log_f)
        f_lo = (log_f - f_hi.astype(F32) - f_mid.astype(F32)).astype(BF16)
        c = carry + (_dot(ltri, f_hi) + (_dot(ltri, f_mid) + _dot(ltri, f_lo)))
        carry = c[sub - 1:sub, :]
        nc = c * (-LOG2_E)
        n_hi, n_mid = _split_bf16(nc)
        n_lo = (nc - n_hi.astype(F32) - n_mid.astype(F32)).astype(BF16)
        ka_ref[rows, :] = _dot(jnp.concatenate([n_hi, n_mid, n_lo], axis=1), place_ref[...]).astype(BF16)
    carry_ref[0:1, :] = carry


ATTN_BIAS_LANES = 3
LOG2_E = 1.4426950408889634


def _bias_lane0(hh, dh):
    return dh if hh == 0 else 0


def _fox_in(moe, kv_gain, q_gain, kv_w, kv_b_f, w_in, seq, tm):
    ybuf, dest3, gate, h = moe
    t, d = h.shape
    nt = dest3.shape[0]
    width = w_in.shape[1] // 2
    heads = kv_w.shape[1] - 2 * width
    dh = width // heads
    place = np.zeros((ATTN_BIAS_LANES * heads, width), np.float32)
    for part in range(ATTN_BIAS_LANES):
        for hd in range(heads):
            place[part * heads + hd, (hd // 2) * LANES + _bias_lane0(hd % 2, dh) + part] = 1.0
    const = lambda i: (0, 0)
    row = lambda i: (i, 0)
    big = jax.ShapeDtypeStruct((t, width), BF16)
    comb_specs, comb_scratch = _combine_specs(tm, d, nt)
    return pl.pallas_call(
        functools.partial(_fox_in_kernel, tm=tm, sub=_tile(tm, 256), d=d, width=width, tiles_per_seq=seq // tm, qscale=dh ** -0.5 * LOG2_E),
        grid=(nt,),
        in_specs=comb_specs + [
            pl.BlockSpec((1, d), const),
            pl.BlockSpec((1, d), const),
            pl.BlockSpec((d, 2 * width), const),
            pl.BlockSpec((d, heads), const),
            pl.BlockSpec((1, heads), const),
            pl.BlockSpec((d, 2 * width), const),
            pl.BlockSpec(place.shape, const),
        ],
        out_specs=[pl.BlockSpec((tm, d), row)] + [pl.BlockSpec((tm, width), row)] * 5,
        out_shape=[jax.ShapeDtypeStruct((t, d), F32)] + [big] * 5,
        scratch_shapes=comb_scratch + [pltpu.VMEM((SUBLANES, heads), F32)],
        compiler_params=_cparams(("arbitrary",)),
        name="fox_in",
    )(dest3, dest3, ybuf, h, gate, kv_gain.reshape(1, d), q_gain.reshape(1, d), kv_w[:, :2 * width].astype(BF16),
      kv_w[:, 2 * width:].astype(BF16), kv_b_f.reshape(1, heads), w_in.astype(BF16), jnp.asarray(place, BF16))


ATTN_KV_CHUNK = 512


def _attn_kernel(q_ref, k_ref, v_ref, og_ref, kb_ref, o_ref, qa_ref, ka_ref, va_ref, s_ref, *, seq, tq, dh, pairs):
    lane = lax.broadcasted_iota(jnp.int32, (seq, LANES), 1)
    heads_here = [(pr, hh) for pr in range(pairs) for hh in range(2)]
    for n, (pr, hh) in enumerate(heads_here):
        cols = slice(pr * LANES, (pr + 1) * LANES)
        a0 = _bias_lane0(hh, dh)
        own = (lane < dh) if hh == 0 else (lane >= dh)
        ones_q = jnp.where((lane >= a0) & (lane < a0 + ATTN_BIAS_LANES), 1.0, 0.0).astype(BF16)
        qa_ref[n] = jnp.where(own, q_ref[:, cols], ones_q)
        ka_ref[n] = jnp.where(own, k_ref[:, cols], kb_ref[:, cols])
        va_ref[n] = jnp.where(own, v_ref[:, cols], jnp.where(lane == a0, 1.0, 0.0).astype(BF16))

    ri = lax.broadcasted_iota(jnp.int32, (tq, tq), 0)
    ci = lax.broadcasted_iota(jnp.int32, (tq, tq), 1)
    causal = ri >= ci
    lane_q = lax.broadcasted_iota(jnp.int32, (tq, LANES), 1)

    def fold_max(mx, s):
        for g in range(s.shape[1] // LANES):
            mx = jnp.maximum(mx, s[:, g * LANES:(g + 1) * LANES])
        return mx

    for qi in range(seq // tq):
        rows = slice(qi * tq, (qi + 1) * tq)
        past = [(c0, min(c0 + ATTN_KV_CHUNK, qi * tq)) for c0 in range(0, qi * tq, ATTN_KV_CHUNK)]
        row_max = []
        for n in range(len(heads_here)):
            qa = qa_ref[n, rows, :]
            mx = jnp.full((tq, LANES), -jnp.inf, F32)
            for c0, c1 in past:
                s = _dot_nt(qa, ka_ref[n, c0:c1, :])
                s_ref[n, :, c0:c1] = s
                mx = fold_max(mx, s)
            s = jnp.where(causal, _dot_nt(qa, ka_ref[n, rows, :]), -jnp.inf)
            s_ref[n, :, rows] = s
            row_max.append(jnp.max(fold_max(mx, s), axis=1, keepdims=True))
        outs = []
        for n, (pr, hh) in enumerate(heads_here):
            acc = jnp.zeros((tq, LANES), F32)
            for c0, c1 in past + [(qi * tq, (qi + 1) * tq)]:
                p = jnp.exp2(s_ref[n, :, c0:c1] - row_max[n]).astype(BF16)
                acc = acc + _dot(p, va_ref[n, c0:c1, :])
            a0 = _bias_lane0(hh, dh)
            outs.append(acc / acc[:, a0:a0 + 1])
        for pr in range(pairs):
            cols = slice(pr * LANES, (pr + 1) * LANES)
            o = jnp.where(lane_q < dh, outs[2 * pr], outs[2 * pr + 1])
            o_ref[rows, cols] = (o * _sigmoid(og_ref[rows, cols].astype(F32))).astype(BF16)


ATTN_PAIRS = 2


def _attn(q, k, v, og, kb, heads, batch, seq, tq):
    t, width = q.shape
    dh = width // heads
    assert 2 * dh == LANES, "two heads per 128-lane block"
    pairs = ATTN_PAIRS
    blk = pl.BlockSpec((seq, pairs * LANES), lambda b, hp: (b, hp))
    return pl.pallas_call(
        functools.partial(_attn_kernel, seq=seq, tq=tq, dh=dh, pairs=pairs),
        grid=(batch, heads // (2 * pairs)),
        in_specs=[blk] * 5,
        out_specs=blk,
        out_shape=jax.ShapeDtypeStruct((t, width), BF16),
        scratch_shapes=[pltpu.VMEM((2 * pairs, seq, LANES), BF16)] * 3 + [pltpu.VMEM((2 * pairs, tq, seq), F32)],
        compiler_params=_cparams(("arbitrary", "arbitrary")),
        name="attn",
    )(q, k, v, og, kb)


def _tile(n, pref):
    while n % pref:
        pref //= 2
    return pref


def kernel(x, gla_norm, gla_w_in, gla_w_a2, gla_b_a, gla_out_norm, gla_w_out, kv_norm, kv_w, kv_b_f, fox_norm,
           fox_w_in, fox_w_out, ffn_norm, router_coarse_w, router_coarse_b, router_fine_w, router_fine_b,
           expert_w_gate, expert_w_up, expert_w_down, final_norm):
    batch, seq, d = x.shape
    t = batch * seq
    assert gla_norm.shape[0] == 1 and ffn_norm.shape[0] == 2, "trunk is GLA layer + FoX layer"
    tm = _tile(seq, 512)

    def ffn(a, w_out, h, layer):
        h, xr, eidx, gate, rank, counts = _post(a, w_out, h, ffn_norm[layer], router_coarse_w[layer],
                                                router_coarse_b[layer], router_fine_w[layer],
                                                router_fine_b[layer], _tile(seq, 1024), tm)
        return _moe_experts(xr, eidx, gate, rank, counts, h, expert_w_gate, expert_w_up, expert_w_down, layer, tm)

    h = x.reshape(t, d)
    q, k, v, r, la = _gla_in(h, gla_norm[0], gla_w_in[0], gla_w_a2[0], gla_b_a[0], tm)
    a = _gla_core(q, k, v, r, la, gla_out_norm[0], batch, seq, _tile(seq, 1024), 1)
    moe = ffn(a, gla_w_out[0], h, 0)
    h, kk, vv, qq, og, kb = _fox_in(moe, kv_norm, fox_norm[0], kv_w, kv_b_f, fox_w_in[0], seq, tm)
    a = _attn(qq, kk, vv, og, kb, kv_b_f.shape[0], batch, seq, _tile(seq, 256))
    moe = ffn(a, fox_w_out[0], h, 1)
    return _final(moe, final_norm, tm).reshape(batch, seq, d)
```

```python
import functools

import jax
import jax.numpy as jnp
import numpy as np
from jax import lax
from jax.experimental import pallas as pl
from jax.experimental.pallas import tpu as pltpu

F32 = jnp.float32
BF16 = jnp.bfloat16

RMS_EPS = 1e-6
GLA_HEADS = 4
GLA_GATE_RANK = 16
GLA_GATE_TAU = 16.0
FOX_HEADS = 16
MOE_GROUPS = 4
MOE_EXPERTS_PER_GROUP = 8
MOE_EXPERTS = MOE_GROUPS * MOE_EXPERTS_PER_GROUP
MOE_TOP_K = 2
MOE_BLOCK = 512

LANES = 128
SUBLANES = 8
VMEM_LIMIT = 56 * 1024 * 1024

GLA_CHUNK = 128
ROUTER_ROWS = 40


def _cparams(sem):
    return pltpu.CompilerParams(dimension_semantics=sem, vmem_limit_bytes=VMEM_LIMIT)


def _rms_unit(x):
    return x * lax.rsqrt(jnp.mean(x * x, axis=-1, keepdims=True) + RMS_EPS)


def _log_sigmoid(z):
    return jnp.minimum(z, 0.0) - jnp.log1p(jnp.exp(-jnp.abs(z)))


def _sigmoid(z):
    return 1.0 / (1.0 + jnp.exp(-z))


def _dot(a, b):
    return jnp.dot(a, b, preferred_element_type=F32)


def _dot_nt(a, b):
    return lax.dot_general(a, b, (((1,), (1,)), ((), ())), preferred_element_type=F32)


def _dot_tn(a, b):
    return lax.dot_general(a, b, (((0,), (0,)), ((), ())), preferred_element_type=F32)


def _split_bf16(x):
    hi = x.astype(BF16)
    lo = (x - hi.astype(F32)).astype(BF16)
    return hi, lo


def _gla_in_kernel(h_ref, g_ref, w_ref, wa1_ref, wa2_ref, ba_ref, q_ref, k_ref, v_ref, r_ref, la_ref, *, qk, vw):
    xn = (_rms_unit(h_ref[...]) * g_ref[...]).astype(BF16)
    q_ref[...] = _dot(xn, w_ref[:, 0:qk]).astype(BF16)
    k_ref[...] = _dot(xn, w_ref[:, qk:2 * qk]).astype(BF16)
    v_ref[...] = _dot(xn, w_ref[:, 2 * qk:2 * qk + vw]).astype(BF16)
    r_ref[...] = _dot(xn, w_ref[:, 2 * qk + vw:2 * qk + 2 * vw]).astype(BF16)
    a = _dot(xn, wa1_ref[...])
    z = _dot(a.astype(BF16), wa2_ref[...]) + ba_ref[...]
    la_ref[...] = _log_sigmoid(z) * (1.0 / GLA_GATE_TAU)


def _gla_in(h, gain, w_in, w_a2, b_a, tm):
    t, d = h.shape
    qk = w_a2.shape[1]
    vw = (w_in.shape[1] - 2 * qk - GLA_GATE_RANK) // 2
    w_main = w_in[:, :2 * qk + 2 * vw].astype(BF16)
    w_a1 = w_in[:, 2 * qk + 2 * vw:].astype(BF16)
    const = lambda i: (0, 0)
    row = lambda i: (i, 0)
    return pl.pallas_call(
        functools.partial(_gla_in_kernel, qk=qk, vw=vw),
        grid=(t // tm,),
        in_specs=[
            pl.BlockSpec((tm, d), row),
            pl.BlockSpec((1, d), const),
            pl.BlockSpec(w_main.shape, const),
            pl.BlockSpec(w_a1.shape, const),
            pl.BlockSpec(w_a2.shape, const),
            pl.BlockSpec((1, qk), const),
        ],
        out_specs=[
            pl.BlockSpec((tm, qk), row),
            pl.BlockSpec((tm, qk), row),
            pl.BlockSpec((tm, vw), row),
            pl.BlockSpec((tm, vw), row),
            pl.BlockSpec((tm, qk), row),
        ],
        out_shape=[
            jax.ShapeDtypeStruct((t, qk), BF16),
            jax.ShapeDtypeStruct((t, qk), BF16),
            jax.ShapeDtypeStruct((t, vw), BF16),
            jax.ShapeDtypeStruct((t, vw), BF16),
            jax.ShapeDtypeStruct((t, qk), F32),
        ],
        compiler_params=_cparams(("arbitrary",)),
        name="gla_in",
    )(h, gain.reshape(1, d), w_main, w_a1, w_a2.astype(BF16), b_a.reshape(1, qk))


def _gla_core_kernel(q_ref, k_ref, v_ref, r_ref, la_ref, gn_ref, o_ref, st_ref,
                     *, n_seq, n_chunks, heads, dk, dv, scale):
    c_len = GLA_CHUNK

    @pl.when(pl.program_id(1) == 0)
    def _():
        st_ref[...] = jnp.zeros_like(st_ref)

    ri = lax.broadcasted_iota(jnp.int32, (c_len, c_len), 0)
    ci = lax.broadcasted_iota(jnp.int32, (c_len, c_len), 1)
    causal = ri >= ci
    ltri = jnp.where(causal, 1.0, 0.0).astype(BF16)

    for c in range(n_chunks):
        sl = pl.ds(c * c_len, c_len)
        for sq in range(n_seq):
            for hd in range(heads):
                ks_ = slice(hd * dk, (hd + 1) * dk)
                vs_ = slice(hd * dv, (hd + 1) * dv)
                g_hi, g_lo = _split_bf16(la_ref[sq, sl, ks_])
                cum = _dot(ltri, g_hi) + _dot(ltri, g_lo)
                last = cum[c_len - 1:c_len, :]
                mid = cum[c_len // 2 - 1:c_len // 2, :]
                qf = q_ref[sq, sl, ks_].astype(F32) * scale
                kf = k_ref[sq, sl, ks_].astype(F32)
                vb = v_ref[sq, sl, vs_]
                qs = (qf * jnp.exp(cum - mid)).astype(BF16)
                ks = (kf * jnp.exp(mid - cum)).astype(BF16)
                scores = jnp.where(causal, _dot_nt(qs, ks), 0.0)
                o = _dot(scores.astype(BF16), vb)
                st = st_ref[sq, hd]
                qd = (qf * jnp.exp(cum)).astype(BF16)
                o = o + _dot_nt(qd, st.astype(BF16))
                kd = (kf * jnp.exp(last - cum)).astype(BF16)
                st_ref[sq, hd] = st * jnp.exp(last) + _dot_tn(vb, kd)
                on = _rms_unit(o) * gn_ref[hd:hd + 1, :]
                rr = r_ref[sq, sl, vs_].astype(F32)
                o_ref[sq, sl, vs_] = (on * (rr * _sigmoid(rr))).astype(BF16)


def _gla_core(q, k, v, r, la, out_norm, batch, seq, lc, n_seq):
    t, qk = q.shape
    vw = v.shape[1]
    heads = GLA_HEADS
    dk, dv = qk // heads, vw // heads
    tok = lambda b, j: (b, j, 0)
    seqs = lambda a: a.reshape(batch, seq, a.shape[1])
    out = pl.pallas_call(
        functools.partial(_gla_core_kernel, n_seq=n_seq, n_chunks=lc // GLA_CHUNK, heads=heads, dk=dk, dv=dv,
                          scale=dk ** -0.5),
        grid=(batch // n_seq, seq // lc),
        in_specs=[
            pl.BlockSpec((n_seq, lc, qk), tok),
            pl.BlockSpec((n_seq, lc, qk), tok),
            pl.BlockSpec((n_seq, lc, vw), tok),
            pl.BlockSpec((n_seq, lc, vw), tok),
            pl.BlockSpec((n_seq, lc, qk), tok),
            pl.BlockSpec((heads, dv), lambda b, j: (0, 0)),
        ],
        out_specs=pl.BlockSpec((n_seq, lc, vw), tok),
        out_shape=jax.ShapeDtypeStruct((batch, seq, vw), BF16),
        scratch_shapes=[pltpu.VMEM((n_seq, heads, dv, dk), F32)],
        compiler_params=_cparams(("arbitrary", "arbitrary")),
        name="gla_core",
    )(seqs(q), seqs(k), seqs(v), seqs(r), seqs(la), out_norm)
    return out.reshape(t, vw)


def _post_kernel(a_ref, w_ref, h_ref, g_ref, wr_ref, br_ref,
                 hn_ref, xr_ref, eidx_ref, gate_ref, rank_ref, cnt_ref, base_ref, *, tm, sub, d):
    @pl.when(pl.program_id(0) == 0)
    def _():
        base_ref[...] = jnp.zeros_like(base_ref)

    eg = MOE_EXPERTS_PER_GROUP
    ti = lax.broadcasted_iota(jnp.int32, (sub, sub), 0)
    tj = lax.broadcasted_iota(jnp.int32, (sub, sub), 1)
    upper = jnp.where(ti <= tj, 1.0, 0.0).astype(BF16)
    row_g = lax.broadcasted_iota(jnp.int32, (MOE_GROUPS, sub), 0)
    row_e = lax.broadcasted_iota(jnp.int32, (eg, sub), 0)
    row_x = lax.broadcasted_iota(jnp.int32, (MOE_EXPERTS, sub), 0)
    base = base_ref[:, 0:1]

    for u in range(tm // sub):
        rows = slice(u * sub, (u + 1) * sub)
        hn = h_ref[rows, :] + _dot(a_ref[rows, :], w_ref[...])
        hn_ref[rows, :] = hn
        xn = _rms_unit(hn) * g_ref[...]
        for c in range(d // LANES):
            xr_ref[pl.ds(u * sub * SUBLANES + c, sub, stride=SUBLANES), :] = xn[:, c * LANES:(c + 1) * LANES]

        x_hi, x_lo = _split_bf16(xn)
        r_hi = _dot(x_hi, wr_ref[...])
        r_lo = _dot(x_lo, wr_ref[...])
        lg_t = (r_hi[:, 0:LANES] + r_hi[:, LANES:]) + (r_lo[:, 0:LANES] + r_lo[:, LANES:])
        lg = lg_t.T[0:ROUTER_ROWS] + br_ref[...]
        lc = lg[0:MOE_GROUPS]
        m_c = jnp.max(lc, axis=0, keepdims=True)
        pc_top = 1.0 / jnp.sum(jnp.exp(lc - m_c), axis=0, keepdims=True)
        g_idx = jnp.min(jnp.where(lc == m_c, row_g, MOE_GROUPS), axis=0, keepdims=True)
        lf = jnp.zeros((eg, sub), F32)
        for g in range(MOE_GROUPS):
            lf = lf + jnp.where(g_idx == g, lg[SUBLANES + g * eg:SUBLANES + (g + 1) * eg], 0.0)
        ef = jnp.exp(lf - jnp.max(lf, axis=0, keepdims=True))
        pf = ef / jnp.sum(ef, axis=0, keepdims=True)
        v1 = jnp.max(pf, axis=0, keepdims=True)
        j1 = jnp.min(jnp.where(pf == v1, row_e, eg), axis=0, keepdims=True)
        pf2 = jnp.where(row_e == j1, -1.0, pf)
        v2 = jnp.max(pf2, axis=0, keepdims=True)
        j2 = jnp.min(jnp.where(pf2 == v2, row_e, eg), axis=0, keepdims=True)
        denom = v1 + v2
        e1 = g_idx * eg + j1
        e2 = g_idx * eg + j2
        eidx_ref[0:1, rows] = e1
        eidx_ref[1:2, rows] = e2
        gate_ref[0:1, rows] = pc_top * v1 / denom
        gate_ref[1:2, rows] = pc_top * v2 / denom

        oh1 = row_x == e1
        oh2 = row_x == e2
        pre1 = _dot(jnp.where(oh1, 1.0, 0.0).astype(BF16), upper)
        pre2 = _dot(jnp.where(oh2, 1.0, 0.0).astype(BF16), upper)
        cnt1 = pre1[:, sub - 1:sub]
        cnt2 = pre2[:, sub - 1:sub]
        rank1 = jnp.sum(jnp.where(oh1, base + pre1 - 1.0, 0.0), axis=0, keepdims=True)
        rank2 = jnp.sum(jnp.where(oh2, base + cnt1 + pre2 - 1.0, 0.0), axis=0, keepdims=True)
        rank_ref[0:1, rows] = rank1.astype(jnp.int32)
        rank_ref[1:2, rows] = rank2.astype(jnp.int32)
        base = base + (cnt1 + cnt2)

    new_base = jnp.broadcast_to(base, base_ref.shape)
    base_ref[...] = new_base
    cnt_ref[...] = new_base


def _post(a, w_out, h, gain, w_coarse, b_coarse, w_fine, b_fine, tm, sub):
    t, d = h.shape
    wr = jnp.concatenate([w_coarse, jnp.zeros((d, SUBLANES - MOE_GROUPS), F32), w_fine,
                          jnp.zeros((d, LANES - ROUTER_ROWS), F32)], axis=1)
    wr = jnp.concatenate(_split_bf16(wr), axis=1)
    br = jnp.zeros((ROUTER_ROWS, 1), F32)
    br = br.at[0:MOE_GROUPS, 0].set(b_coarse).at[SUBLANES:, 0].set(b_fine)
    const = lambda i: (0, 0)
    row = lambda i: (i, 0)
    col = lambda i: (0, i)
    return pl.pallas_call(
        functools.partial(_post_kernel, tm=tm, sub=sub, d=d),
        grid=(t // tm,),
        in_specs=[
            pl.BlockSpec((tm, a.shape[1]), row),
            pl.BlockSpec(w_out.shape, const),
            pl.BlockSpec((tm, d), row),
            pl.BlockSpec((1, d), const),
            pl.BlockSpec((d, 2 * LANES), const),
            pl.BlockSpec((ROUTER_ROWS, 1), const),
        ],
        out_specs=[
            pl.BlockSpec((tm, d), row),
            pl.BlockSpec((tm * SUBLANES, LANES), row),
            pl.BlockSpec((MOE_TOP_K, tm), col),
            pl.BlockSpec((MOE_TOP_K, tm), col),
            pl.BlockSpec((MOE_TOP_K, tm), col),
            pl.BlockSpec((MOE_EXPERTS, LANES), const),
        ],
        out_shape=[
            jax.ShapeDtypeStruct((t, d), F32),
            jax.ShapeDtypeStruct((t * SUBLANES, LANES), F32),
            jax.ShapeDtypeStruct((MOE_TOP_K, t), jnp.int32),
            jax.ShapeDtypeStruct((MOE_TOP_K, t), F32),
            jax.ShapeDtypeStruct((MOE_TOP_K, t), jnp.int32),
            jax.ShapeDtypeStruct((MOE_EXPERTS, LANES), F32),
        ],
        scratch_shapes=[pltpu.VMEM((MOE_EXPERTS, LANES), F32)],
        compiler_params=_cparams(("arbitrary",)),
        name="post",
    )(a, w_out.astype(BF16), h, gain.reshape(1, d), wr, br)


def _row_tile(ref, idx):
    return ref.at[pl.ds(pl.multiple_of(idx * SUBLANES, SUBLANES), SUBLANES), :]


def _block_rows(ref, blk):
    return ref.at[pl.ds(pl.multiple_of(blk * (MOE_BLOCK * SUBLANES), MOE_BLOCK * SUBLANES), MOE_BLOCK * SUBLANES), :]


def _dispatch_kernel(pe_ref, nu_ref, dest_ref, xr_ref, xr_hbm, xbuf_ref, zero_ref, sem, zsem, *, tm, n_blocks):
    @pl.when(pl.program_id(0) == 0)
    def _():
        zero_ref[...] = jnp.zeros_like(zero_ref)

        def last_block(e):
            return pe_ref[e] // MOE_BLOCK - 1

        def has_rows(e):
            return pe_ref[e] > jnp.where(e == 0, 0, pe_ref[jnp.maximum(e - 1, 0)])

        def zero_block(blk):
            return pltpu.make_async_copy(zero_ref, _block_rows(xbuf_ref, blk), zsem)

        def start_e(e, carry):
            @pl.when(has_rows(e))
            def _():
                zero_block(last_block(e)).start()
            return carry

        def wait_e(e, carry):
            @pl.when(has_rows(e))
            def _():
                zero_block(last_block(e)).wait()
            return carry

        def start_b(blk, carry):
            zero_block(blk).start()
            return carry

        def wait_b(blk, carry):
            zero_block(blk).wait()
            return carry

        lax.fori_loop(0, MOE_EXPERTS, start_e, 0)
        lax.fori_loop(nu_ref[0], n_blocks, start_b, 0)
        lax.fori_loop(0, MOE_EXPERTS, wait_e, 0)
        lax.fori_loop(nu_ref[0], n_blocks, wait_b, 0)

    base = pl.program_id(0) * tm

    def issue(tk, carry):
        pltpu.make_async_copy(_row_tile(xr_ref, tk), _row_tile(xbuf_ref, dest_ref[0, 0, tk]), sem).start(priority=0)
        pltpu.make_async_copy(_row_tile(xr_hbm, base + tk), _row_tile(xbuf_ref, dest_ref[0, 1, tk]),
                              sem).start(priority=1)
        return carry

    for tk in range(tm):
        issue(tk, 0)
    for s in range(MOE_TOP_K):
        pltpu.make_async_copy(xr_ref, xbuf_ref.at[pl.ds(0, tm * SUBLANES), :], sem).wait()


def _dispatch(xr, dest3, pad_end, n_used, n_blocks, tm):
    nt = dest3.shape[0]
    grid_spec = pltpu.PrefetchScalarGridSpec(
        num_scalar_prefetch=2,
        grid=(nt,),
        in_specs=[
            pl.BlockSpec((1, MOE_TOP_K, tm), lambda i, pe, nu: (i, 0, 0), memory_space=pltpu.SMEM),
            pl.BlockSpec((tm * SUBLANES, LANES), lambda i, pe, nu: (i, 0)),
            pl.BlockSpec(memory_space=pl.ANY),
        ],
        out_specs=pl.BlockSpec(memory_space=pl.ANY),
        scratch_shapes=[pltpu.VMEM((MOE_BLOCK * SUBLANES, LANES), F32), pltpu.SemaphoreType.DMA,
                        pltpu.SemaphoreType.DMA],
    )
    return pl.pallas_call(
        functools.partial(_dispatch_kernel, tm=tm, n_blocks=n_blocks),
        grid_spec=grid_spec,
        out_shape=jax.ShapeDtypeStruct((n_blocks * MOE_BLOCK * SUBLANES, LANES), F32),
        compiler_params=_cparams(("arbitrary",)),
        name="dispatch",
    )(pad_end, n_used, dest3, xr, xr)


def _experts_kernel(be_ref, nu_ref, nxt_ref, par_ref, x_ref, wg_hbm, wu_hbm, wd_hbm, y_ref,
                    x2_ref, wgb_ref, wub_ref, wdb_ref, wgf_ref, wuf_ref, wdf_ref, sems, *, d, layer):
    rows = MOE_BLOCK
    b = pl.program_id(0)
    live = b < nu_ref[0]
    e = be_ref[b]
    new_expert = jnp.logical_or(b == 0, e != be_ref[jnp.maximum(b - 1, 0)])

    def weight_copies(ex, slot):
        return [pltpu.make_async_copy(src.at[layer, ex], dst.at[slot], sems.at[slot, j])
                for j, (src, dst) in enumerate(((wg_hbm, wgf_ref), (wu_hbm, wuf_ref), (wd_hbm, wdf_ref)))]

    @pl.when(jnp.logical_and(live, new_expert))
    def _():
        slot = par_ref[e]

        @pl.when(b == 0)
        def _():
            for cp in weight_copies(e, slot):
                cp.start()

        @pl.when(nxt_ref[e] >= 0)
        def _():
            for cp in weight_copies(nxt_ref[e], 1 - slot):
                cp.start()

        for cp in weight_copies(e, slot):
            cp.wait()
        wgb_ref[...] = wgf_ref[slot].astype(BF16)
        wub_ref[...] = wuf_ref[slot].astype(BF16)
        wdb_ref[...] = wdf_ref[slot].astype(BF16)

    @pl.when(live)
    def _():
        for c in range(d // LANES):
            x2_ref[:, c * LANES:(c + 1) * LANES] = x_ref[pl.ds(c, rows, stride=SUBLANES), :].astype(BF16)
        x2 = x2_ref[...]
        gp = _dot(x2, wgb_ref[...])
        up = _dot(x2, wub_ref[...])
        hid = (gp * _sigmoid(gp) * up).astype(BF16)
        y = _dot(hid, wdb_ref[...])
        for c in range(d // LANES):
            y_ref[pl.ds(c, rows, stride=SUBLANES), :] = y[:, c * LANES:(c + 1) * LANES]


def _experts(xbuf, block_e, n_used, padded, w_gate, w_up, w_down, layer):
    n_blocks = block_e.shape[0]
    d, dff = w_gate.shape[2], w_gate.shape[3]
    owns = padded > 0
    ids = jnp.arange(MOE_EXPERTS, dtype=jnp.int32)
    later = jnp.where(owns, ids, MOE_EXPERTS)
    nxt = lax.cummin(jnp.concatenate([later[1:], jnp.full((1,), MOE_EXPERTS, jnp.int32)]), reverse=True)
    nxt = jnp.where(nxt < MOE_EXPERTS, nxt, -1).astype(jnp.int32)
    par = ((jnp.cumsum(owns.astype(jnp.int32)) - owns.astype(jnp.int32)) % 2).astype(jnp.int32)
    xmap = lambda b, be, nu, nx, pa: (jnp.minimum(b, nu[0] - 1), 0)
    grid_spec = pltpu.PrefetchScalarGridSpec(
        num_scalar_prefetch=4,
        grid=(n_blocks,),
        in_specs=[
            pl.BlockSpec((MOE_BLOCK * SUBLANES, LANES), xmap),
            pl.BlockSpec(memory_space=pl.ANY),
            pl.BlockSpec(memory_space=pl.ANY),
            pl.BlockSpec(memory_space=pl.ANY),
        ],
        out_specs=pl.BlockSpec((MOE_BLOCK * SUBLANES, LANES), xmap),
        scratch_shapes=[pltpu.VMEM((MOE_BLOCK, d), BF16), pltpu.VMEM((d, dff), BF16), pltpu.VMEM((d, dff), BF16),
                        pltpu.VMEM((dff, d), BF16), pltpu.VMEM((2, d, dff), F32), pltpu.VMEM((2, d, dff), F32),
                        pltpu.VMEM((2, dff, d), F32), pltpu.SemaphoreType.DMA((2, 3))],
    )
    return pl.pallas_call(
        functools.partial(_experts_kernel, d=d, layer=layer),
        grid_spec=grid_spec,
        out_shape=jax.ShapeDtypeStruct(xbuf.shape, F32),
        input_output_aliases={4: 0},
        compiler_params=_cparams(("arbitrary",)),
        name="experts",
    )(block_e, n_used, nxt, par, xbuf, w_gate, w_up, w_down)


def _gather_combine(dcur_ref, dnxt_ref, ybuf_ref, h_ref, gate_ref, y_ref, sems, *, tm, d):
    i = pl.program_id(0)

    def start(dref, slot):
        def issue(tk, carry):
            for s in range(MOE_TOP_K):
                pltpu.make_async_copy(_row_tile(ybuf_ref, dref[0, s, tk]), _row_tile(y_ref.at[slot, s], tk),
                                      sems.at[slot]).start(priority=s)
            return carry

        for tk in range(tm):
            issue(tk, 0)

    @pl.when(i == 0)
    def _():
        start(dcur_ref, 0)

    @pl.when(i + 1 < pl.num_programs(0))
    def _():
        start(dnxt_ref, (i + 1) % 2)

    slot = i % 2
    for s in range(MOE_TOP_K):
        pltpu.make_async_copy(ybuf_ref.at[pl.ds(0, tm * SUBLANES), :], y_ref.at[slot, s], sems.at[slot]).wait()
    gate_t = jnp.concatenate([gate_ref[...], jnp.zeros((SUBLANES - MOE_TOP_K, tm), F32)], axis=0).T
    g0 = gate_t[:, 0:1]
    g1 = gate_t[:, 1:2]
    cols = []
    for c in range(d // LANES):
        y0 = y_ref[slot, 0, pl.ds(c, tm, stride=SUBLANES), :]
        y1 = y_ref[slot, 1, pl.ds(c, tm, stride=SUBLANES), :]
        cols.append(h_ref[:, c * LANES:(c + 1) * LANES] + (y0 * g0 + y1 * g1))
    return jnp.concatenate(cols, axis=1)


def _combine_specs(tm, d, nt):
    in_specs = [
        pl.BlockSpec((1, MOE_TOP_K, tm), lambda i: (i, 0, 0), memory_space=pltpu.SMEM),
        pl.BlockSpec((1, MOE_TOP_K, tm), lambda i: (jnp.minimum(i + 1, nt - 1), 0, 0), memory_space=pltpu.SMEM),
        pl.BlockSpec(memory_space=pl.ANY),
        pl.BlockSpec((tm, d), lambda i: (i, 0)),
        pl.BlockSpec((MOE_TOP_K, tm), lambda i: (0, i)),
    ]
    scratch = [pltpu.VMEM((2, MOE_TOP_K, tm * SUBLANES, LANES), F32), pltpu.SemaphoreType.DMA((2,))]
    return in_specs, scratch


def _final_kernel(dcur_ref, dnxt_ref, ybuf_ref, h_ref, gate_ref, fg_ref, o_ref, y_ref, sems, *, tm, d):
    out = _gather_combine(dcur_ref, dnxt_ref, ybuf_ref, h_ref, gate_ref, y_ref, sems, tm=tm, d=d)
    o_ref[...] = _rms_unit(out) * fg_ref[...]


def _final(moe, final_gain, tm):
    ybuf, dest3, gate, h = moe
    t, d = h.shape
    nt = dest3.shape[0]
    in_specs, scratch = _combine_specs(tm, d, nt)
    return pl.pallas_call(
        functools.partial(_final_kernel, tm=tm, d=d),
        grid=(nt,),
        in_specs=in_specs + [pl.BlockSpec((1, d), lambda i: (0, 0))],
        out_specs=pl.BlockSpec((tm, d), lambda i: (i, 0)),
        out_shape=jax.ShapeDtypeStruct((t, d), F32),
        scratch_shapes=scratch,
        compiler_params=_cparams(("arbitrary",)),
        name="final",
    )(dest3, dest3, ybuf, h, gate, final_gain.reshape(1, d))


def _dest_kernel(ps_ref, eidx_ref, rank_ref, dest_ref, *, tm):
    eidx = eidx_ref[...]
    dest = rank_ref[...]
    for e in range(MOE_EXPERTS):
        dest = dest + jnp.where(eidx == e, ps_ref[e], 0)
    for i in range(dest_ref.shape[0]):
        dest_ref[i] = dest[:, i * tm:(i + 1) * tm]


def _dest(pad_start, eidx, rank, tm):
    t = eidx.shape[1]
    whole = lambda i, ps: (0, 0)
    grid_spec = pltpu.PrefetchScalarGridSpec(
        num_scalar_prefetch=1,
        grid=(1,),
        in_specs=[pl.BlockSpec((MOE_TOP_K, t), whole), pl.BlockSpec((MOE_TOP_K, t), whole)],
        out_specs=pl.BlockSpec((t // tm, MOE_TOP_K, tm), lambda i, ps: (0, 0, 0)),
    )
    return pl.pallas_call(
        functools.partial(_dest_kernel, tm=tm),
        grid_spec=grid_spec,
        out_shape=jax.ShapeDtypeStruct((t // tm, MOE_TOP_K, tm), jnp.int32),
        compiler_params=_cparams(("arbitrary",)),
        name="dest",
    )(pad_start, eidx, rank)


def _moe_experts(xr, eidx, gate, rank, counts, h, w_gate, w_up, w_down, layer, tm_row):
    t = h.shape[0]
    n_slot = t * MOE_TOP_K
    n_blocks = -(-n_slot // MOE_BLOCK) + MOE_EXPERTS
    cnt = counts[:, 0].astype(jnp.int32)
    padded = (cnt + MOE_BLOCK - 1) // MOE_BLOCK * MOE_BLOCK
    pad_end = jnp.cumsum(padded)
    pad_start = pad_end - padded
    block_start = jnp.arange(n_blocks, dtype=jnp.int32) * MOE_BLOCK
    block_e = jnp.minimum(jnp.sum((pad_end[None, :] <= block_start[:, None]).astype(jnp.int32), axis=1),
                          MOE_EXPERTS - 1)
    n_used = (pad_end[-1:] // MOE_BLOCK).astype(jnp.int32)
    dest3 = _dest(pad_start.astype(jnp.int32), eidx, rank, tm_row)
    xbuf = _dispatch(xr, dest3, pad_end.astype(jnp.int32), n_used, n_blocks, tm_row)
    ybuf = _experts(xbuf, block_e, n_used, padded, w_gate, w_up, w_down, layer)
    return ybuf, dest3, gate, h


def _fox_in_kernel(dcur_ref, dnxt_ref, ybuf_ref, h_ref, gate_ref, gkv_ref, gq_ref, wkv_ref, wf_ref, bf_ref, wq_ref,
                   place_ref, hn_ref, k_ref, v_ref, q_ref, og_ref, ka_ref, y_ref, sems, carry_ref,
                   *, tm, sub, d, width, tiles_per_seq, qscale):
    i = pl.program_id(0)

    @pl.when(i % tiles_per_seq == 0)
    def _():
        carry_ref[...] = jnp.zeros_like(carry_ref)

    hn_ref[...] = _gather_combine(dcur_ref, dnxt_ref, ybuf_ref, h_ref, gate_ref, y_ref, sems, tm=tm, d=d)
    ri = lax.broadcasted_iota(jnp.int32, (sub, sub), 0)
    ci = lax.broadcasted_iota(jnp.int32, (sub, sub), 1)
    ltri = jnp.where(ri >= ci, 1.0, 0.0).astype(BF16)
    carry = carry_ref[0:1, :]
    for u in range(tm // sub):
        rows = slice(u * sub, (u + 1) * sub)
        y = _rms_unit(hn_ref[rows, :])
        xkv = (y * gkv_ref[...]).astype(BF16)
        xq = (y * gq_ref[...]).astype(BF16)
        k_ref[rows, :] = _dot(xkv, wkv_ref[:, 0:width]).astype(BF16)
        v_ref[rows, :] = _dot(xkv, wkv_ref[:, width:2 * width]).astype(BF16)
        q_ref[rows, :] = (_dot(xq, wq_ref[:, 0:width]) * qscale).astype(BF16)
        og_ref[rows, :] = _dot(xq, wq_ref[:, width:2 * width]).astype(BF16)

        log_f = _log_sigmoid(_dot(xkv, wf_ref[...]) + bf_ref[...])
        f_hi, f_mid = _split_bf16(log_f)
        f_lo = (log_f - f_hi.astype(F32) - f_mid.astype(F32)).astype(BF16)
        c = carry + (_dot(ltri, f_hi) + (_dot(ltri, f_mid) + _dot(ltri, f_lo)))
        carry = c[sub - 1:sub, :]
        nc = c * (-LOG2_E)
        n_hi, n_mid = _split_bf16(nc)
        n_lo = (nc - n_hi.astype(F32) - n_mid.astype(F32)).astype(BF16)
        ka_ref[rows, :] = _dot(jnp.concatenate([n_hi, n_mid, n_lo], axis=1), place_ref[...]).astype(BF16)
    carry_ref[0:1, :] = carry


ATTN_BIAS_LANES = 3
LOG2_E = 1.4426950408889634


def _bias_lane0(hh, dh):
    return dh if hh == 0 else 0


def _fox_in(moe, kv_gain, q_gain, kv_w, kv_b_f, w_in, seq, tm):
    ybuf, dest3, gate, h = moe
    t, d = h.shape
    nt = dest3.shape[0]
    width = w_in.shape[1] // 2
    heads = kv_w.shape[1] - 2 * width
    dh = width // heads
    place = np.zeros((ATTN_BIAS_LANES * heads, width), np.float32)
    for part in range(ATTN_BIAS_LANES):
        for hd in range(heads):
            place[part * heads + hd, (hd // 2) * LANES + _bias_lane0(hd % 2, dh) + part] = 1.0
    const = lambda i: (0, 0)
    row = lambda i: (i, 0)
    big = jax.ShapeDtypeStruct((t, width), BF16)
    comb_specs, comb_scratch = _combine_specs(tm, d, nt)
    return pl.pallas_call(
        functools.partial(_fox_in_kernel, tm=tm, sub=_tile(tm, 256), d=d, width=width, tiles_per_seq=seq // tm, qscale=dh ** -0.5 * LOG2_E),
        grid=(nt,),
        in_specs=comb_specs + [
            pl.BlockSpec((1, d), const),
            pl.BlockSpec((1, d), const),
            pl.BlockSpec((d, 2 * width), const),
            pl.BlockSpec((d, heads), const),
            pl.BlockSpec((1, heads), const),
            pl.BlockSpec((d, 2 * width), const),
            pl.BlockSpec(place.shape, const),
        ],
        out_specs=[pl.BlockSpec((tm, d), row)] + [pl.BlockSpec((tm, width), row)] * 5,
        out_shape=[jax.ShapeDtypeStruct((t, d), F32)] + [big] * 5,
        scratch_shapes=comb_scratch + [pltpu.VMEM((SUBLANES, heads), F32)],
        compiler_params=_cparams(("arbitrary",)),
        name="fox_in",
    )(dest3, dest3, ybuf, h, gate, kv_gain.reshape(1, d), q_gain.reshape(1, d), kv_w[:, :2 * width].astype(BF16),
      kv_w[:, 2 * width:].astype(BF16), kv_b_f.reshape(1, heads), w_in.astype(BF16), jnp.asarray(place, BF16))


ATTN_KV_CHUNK = 512


def _attn_kernel(q_ref, k_ref, v_ref, og_ref, kb_ref, o_ref, qa_ref, ka_ref, va_ref, s_ref, *, seq, tq, dh, pairs):
    lane = lax.broadcasted_iota(jnp.int32, (seq, LANES), 1)
    heads_here = [(pr, hh) for pr in range(pairs) for hh in range(2)]
    for n, (pr, hh) in enumerate(heads_here):
        cols = slice(pr * LANES, (pr + 1) * LANES)
        a0 = _bias_lane0(hh, dh)
        own = (lane < dh) if hh == 0 else (lane >= dh)
        ones_q = jnp.where((lane >= a0) & (lane < a0 + ATTN_BIAS_LANES), 1.0, 0.0).astype(BF16)
        qa_ref[n] = jnp.where(own, q_ref[:, cols], ones_q)
        ka_ref[n] = jnp.where(own, k_ref[:, cols], kb_ref[:, cols])
        va_ref[n] = jnp.where(own, v_ref[:, cols], jnp.where(lane == a0, 1.0, 0.0).astype(BF16))

    ri = lax.broadcasted_iota(jnp.int32, (tq, tq), 0)
    ci = lax.broadcasted_iota(jnp.int32, (tq, tq), 1)
    causal = ri >= ci
    lane_q = lax.broadcasted_iota(jnp.int32, (tq, LANES), 1)

    def fold_max(mx, s):
        for g in range(s.shape[1] // LANES):
            mx = jnp.maximum(mx, s[:, g * LANES:(g + 1) * LANES])
        return mx

    for qi in range(seq // tq):
        rows = slice(qi * tq, (qi + 1) * tq)
        past = [(c0, min(c0 + ATTN_KV_CHUNK, qi * tq)) for c0 in range(0, qi * tq, ATTN_KV_CHUNK)]
        row_max = []
        for n in range(len(heads_here)):
            qa = qa_ref[n, rows, :]
            mx = jnp.full((tq, LANES), -jnp.inf, F32)
            for c0, c1 in past:
                s = _dot_nt(qa, ka_ref[n, c0:c1, :])
                s_ref[n, :, c0:c1] = s
                mx = fold_max(mx, s)
            s = jnp.where(causal, _dot_nt(qa, ka_ref[n, rows, :]), -jnp.inf)
            s_ref[n, :, rows] = s
            row_max.append(jnp.max(fold_max(mx, s), axis=1, keepdims=True))
        outs = []
        for n, (pr, hh) in enumerate(heads_here):
            acc = jnp.zeros((tq, LANES), F32)
            for c0, c1 in past + [(qi * tq, (qi + 1) * tq)]:
                p = jnp.exp2(s_ref[n, :, c0:c1] - row_max[n]).astype(BF16)
                acc = acc + _dot(p, va_ref[n, c0:c1, :])
            a0 = _bias_lane0(hh, dh)
            outs.append(acc / acc[:, a0:a0 + 1])
        for pr in range(pairs):
            cols = slice(pr * LANES, (pr + 1) * LANES)
            o = jnp.where(lane_q < dh, outs[2 * pr], outs[2 * pr + 1])
            o_ref[rows, cols] = (o * _sigmoid(og_ref[rows, cols].astype(F32))).astype(BF16)


ATTN_PAIRS = 2


def _attn(q, k, v, og, kb, heads, batch, seq, tq):
    t, width = q.shape
    dh = width // heads
    assert 2 * dh == LANES, "two heads per 128-lane block"
    pairs = ATTN_PAIRS
    blk = pl.BlockSpec((seq, pairs * LANES), lambda b, hp: (b, hp))
    return pl.pallas_call(
        functools.partial(_attn_kernel, seq=seq, tq=tq, dh=dh, pairs=pairs),
        grid=(batch, heads // (2 * pairs)),
        in_specs=[blk] * 5,
        out_specs=blk,
        out_shape=jax.ShapeDtypeStruct((t, width), BF16),
        scratch_shapes=[pltpu.VMEM((2 * pairs, seq, LANES), BF16)] * 3 + [pltpu.VMEM((2 * pairs, tq, seq), F32)],
        compiler_params=_cparams(("arbitrary", "arbitrary")),
        name="attn",
    )(q, k, v, og, kb)


def _tile(n, pref):
    while n % pref:
        pref //= 2
    return pref


def kernel(x, gla_norm, gla_w_in, gla_w_a2, gla_b_a, gla_out_norm, gla_w_out, kv_norm, kv_w, kv_b_f, fox_norm,
           fox_w_in, fox_w_out, ffn_norm, router_coarse_w, router_coarse_b, router_fine_w, router_fine_b,
           expert_w_gate, expert_w_up, expert_w_down, final_norm):
    batch, seq, d = x.shape
    t = batch * seq
    assert gla_norm.shape[0] == 1 and ffn_norm.shape[0] == 2, "trunk is GLA layer + FoX layer"
    tm = _tile(seq, 512)

    def ffn(a, w_out, h, layer):
        h, xr, eidx, gate, rank, counts = _post(a, w_out, h, ffn_norm[layer], router_coarse_w[layer],
                                                router_coarse_b[layer], router_fine_w[layer],
                                                router_fine_b[layer], _tile(seq, 1024), tm)
        return _moe_experts(xr, eidx, gate, rank, counts, h, expert_w_gate, expert_w_up, expert_w_down, layer, tm)

    h = x.reshape(t, d)
    q, k, v, r, la = _gla_in(h, gla_norm[0], gla_w_in[0], gla_w_a2[0], gla_b_a[0], tm)
    a = _gla_core(q, k, v, r, la, gla_out_norm[0], batch, seq, _tile(seq, 1024), 1)
    moe = ffn(a, gla_w_out[0], h, 0)
    h, kk, vv, qq, og, kb = _fox_in(moe, kv_norm, fox_norm[0], kv_w, kv_b_f, fox_w_in[0], seq, tm)
    a = _attn(qq, kk, vv, og, kb, kv_b_f.shape[0], batch, seq, _tile(seq, 256))
    moe = ffn(a, fox_w_out[0], h, 1)
    return _final(moe, final_norm, tm).reshape(batch, seq, d)
```

```python
import functools

import jax
import jax.numpy as jnp
import numpy as np
from jax import lax
from jax.experimental import pallas as pl
from jax.experimental.pallas import tpu as pltpu

F32 = jnp.float32
BF16 = jnp.bfloat16

RMS_EPS = 1e-6
GLA_HEADS = 4
GLA_GATE_RANK = 16
GLA_GATE_TAU = 16.0
FOX_HEADS = 16
MOE_GROUPS = 4
MOE_EXPERTS_PER_GROUP = 8
MOE_EXPERTS = MOE_GROUPS * MOE_EXPERTS_PER_GROUP
MOE_TOP_K = 2
MOE_BLOCK = 512

LANES = 128
SUBLANES = 8
VMEM_LIMIT = 56 * 1024 * 1024

GLA_CHUNK = 128
ROUTER_ROWS = 40


def _cparams(sem):
    return pltpu.CompilerParams(dimension_semantics=sem, vmem_limit_bytes=VMEM_LIMIT)


def _rms_unit(x):
    return x * lax.rsqrt(jnp.mean(x * x, axis=-1, keepdims=True) + RMS_EPS)


def _log_sigmoid(z):
    return jnp.minimum(z, 0.0) - jnp.log1p(jnp.exp(-jnp.abs(z)))


def _sigmoid(z):
    return 1.0 / (1.0 + jnp.exp(-z))


def _dot(a, b):
    return jnp.dot(a, b, preferred_element_type=F32)


def _dot_nt(a, b):
    return lax.dot_general(a, b, (((1,), (1,)), ((), ())), preferred_element_type=F32)


def _dot_tn(a, b):
    return lax.dot_general(a, b, (((0,), (0,)), ((), ())), preferred_element_type=F32)


def _split_bf16(x):
    hi = x.astype(BF16)
    lo = (x - hi.astype(F32)).astype(BF16)
    return hi, lo


def _gla_in_kernel(h_ref, g_ref, w_ref, wa1_ref, wa2_ref, ba_ref, q_ref, k_ref, v_ref, r_ref, la_ref, *, qk, vw):
    xn = (_rms_unit(h_ref[...]) * g_ref[...]).astype(BF16)
    q_ref[...] = _dot(xn, w_ref[:, 0:qk]).astype(BF16)
    k_ref[...] = _dot(xn, w_ref[:, qk:2 * qk]).astype(BF16)
    v_ref[...] = _dot(xn, w_ref[:, 2 * qk:2 * qk + vw]).astype(BF16)
    r_ref[...] = _dot(xn, w_ref[:, 2 * qk + vw:2 * qk + 2 * vw]).astype(BF16)
    a = _dot(xn, wa1_ref[...])
    z = _dot(a.astype(BF16), wa2_ref[...]) + ba_ref[...]
    la_ref[...] = _log_sigmoid(z) * (1.0 / GLA_GATE_TAU)


def _gla_in(h, gain, w_in, w_a2, b_a, tm):
    t, d = h.shape
    qk = w_a2.shape[1]
    vw = (w_in.shape[1] - 2 * qk - GLA_GATE_RANK) // 2
    w_main = w_in[:, :2 * qk + 2 * vw].astype(BF16)
    w_a1 = w_in[:, 2 * qk + 2 * vw:].astype(BF16)
    const = lambda i: (0, 0)
    row = lambda i: (i, 0)
    return pl.pallas_call(
        functools.partial(_gla_in_kernel, qk=qk, vw=vw),
        grid=(t // tm,),
        in_specs=[
            pl.BlockSpec((tm, d), row),
            pl.BlockSpec((1, d), const),
            pl.BlockSpec(w_main.shape, const),
            pl.BlockSpec(w_a1.shape, const),
            pl.BlockSpec(w_a2.shape, const),
            pl.BlockSpec((1, qk), const),
        ],
        out_specs=[
            pl.BlockSpec((tm, qk), row),
            pl.BlockSpec((tm, qk), row),
            pl.BlockSpec((tm, vw), row),
            pl.BlockSpec((tm, vw), row),
            pl.BlockSpec((tm, qk), row),
        ],
        out_shape=[
            jax.ShapeDtypeStruct((t, qk), BF16),
            jax.ShapeDtypeStruct((t, qk), BF16),
            jax.ShapeDtypeStruct((t, vw), BF16),
            jax.ShapeDtypeStruct((t, vw), BF16),
            jax.ShapeDtypeStruct((t, qk), F32),
        ],
        compiler_params=_cparams(("arbitrary",)),
        name="gla_in",
    )(h, gain.reshape(1, d), w_main, w_a1, w_a2.astype(BF16), b_a.reshape(1, qk))


def _gla_core_kernel(q_ref, k_ref, v_ref, r_ref, la_ref, gn_ref, o_ref, st_ref,
                     *, n_seq, n_chunks, heads, dk, dv, scale):
    c_len = GLA_CHUNK

    @pl.when(pl.program_id(1) == 0)
    def _():
        st_ref[...] = jnp.zeros_like(st_ref)

    ri = lax.broadcasted_iota(jnp.int32, (c_len, c_len), 0)
    ci = lax.broadcasted_iota(jnp.int32, (c_len, c_len), 1)
    causal = ri >= ci
    ltri = jnp.where(causal, 1.0, 0.0).astype(BF16)

    for c in range(n_chunks):
        sl = pl.ds(c * c_len, c_len)
        for sq in range(n_seq):
            for hd in range(heads):
                ks_ = slice(hd * dk, (hd + 1) * dk)
                vs_ = slice(hd * dv, (hd + 1) * dv)
                g_hi, g_lo = _split_bf16(la_ref[sq, sl, ks_])
                cum = _dot(ltri, g_hi) + _dot(ltri, g_lo)
                last = cum[c_len - 1:c_len, :]
                mid = cum[c_len // 2 - 1:c_len // 2, :]
                qf = q_ref[sq, sl, ks_].astype(F32) * scale
                kf = k_ref[sq, sl, ks_].astype(F32)
                vb = v_ref[sq, sl, vs_]
                qs = (qf * jnp.exp(cum - mid)).astype(BF16)
                ks = (kf * jnp.exp(mid - cum)).astype(BF16)
                scores = jnp.where(causal, _dot_nt(qs, ks), 0.0)
                o = _dot(scores.astype(BF16), vb)
                st = st_ref[sq, hd]
                qd = (qf * jnp.exp(cum)).astype(BF16)
                o = o + _dot_nt(qd, st.astype(BF16))
                kd = (kf * jnp.exp(last - cum)).astype(BF16)
                st_ref[sq, hd] = st * jnp.exp(last) + _dot_tn(vb, kd)
                on = _rms_unit(o) * gn_ref[hd:hd + 1, :]
                rr = r_ref[sq, sl, vs_].astype(F32)
                o_ref[sq, sl, vs_] = (on * (rr * _sigmoid(rr))).astype(BF16)


def _gla_core(q, k, v, r, la, out_norm, batch, seq, lc, n_seq):
    t, qk = q.shape
    vw = v.shape[1]
    heads = GLA_HEADS
    dk, dv = qk // heads, vw // heads
    tok = lambda b, j: (b, j, 0)
    seqs = lambda a: a.reshape(batch, seq, a.shape[1])
    out = pl.pallas_call(
        functools.partial(_gla_core_kernel, n_seq=n_seq, n_chunks=lc // GLA_CHUNK, heads=heads, dk=dk, dv=dv,
                          scale=dk ** -0.5),
        grid=(batch // n_seq, seq // lc),
        in_specs=[
            pl.BlockSpec((n_seq, lc, qk), tok),
            pl.BlockSpec((n_seq, lc, qk), tok),
            pl.BlockSpec((n_seq, lc, vw), tok),
            pl.BlockSpec((n_seq, lc, vw), tok),
            pl.BlockSpec((n_seq, lc, qk), tok),
            pl.BlockSpec((heads, dv), lambda b, j: (0, 0)),
        ],
        out_specs=pl.BlockSpec((n_seq, lc, vw), tok),
        out_shape=jax.ShapeDtypeStruct((batch, seq, vw), BF16),
        scratch_shapes=[pltpu.VMEM((n_seq, heads, dv, dk), F32)],
        compiler_params=_cparams(("arbitrary", "arbitrary")),
        name="gla_core",
    )(seqs(q), seqs(k), seqs(v), seqs(r), seqs(la), out_norm)
    return out.reshape(t, vw)


def _post_kernel(a_ref, w_ref, h_ref, g_ref, wr_ref, br_ref,
                 hn_ref, xr_ref, eidx_ref, gate_ref, rank_ref, cnt_ref, base_ref, *, tm, sub, d):
    @pl.when(pl.program_id(0) == 0)
    def _():
        base_ref[...] = jnp.zeros_like(base_ref)

    eg = MOE_EXPERTS_PER_GROUP
    ti = lax.broadcasted_iota(jnp.int32, (sub, sub), 0)
    tj = lax.broadcasted_iota(jnp.int32, (sub, sub), 1)
    upper = jnp.where(ti <= tj, 1.0, 0.0).astype(BF16)
    row_g = lax.broadcasted_iota(jnp.int32, (MOE_GROUPS, sub), 0)
    row_e = lax.broadcasted_iota(jnp.int32, (eg, sub), 0)
    row_x = lax.broadcasted_iota(jnp.int32, (MOE_EXPERTS, sub), 0)
    base = base_ref[:, 0:1]

    for u in range(tm // sub):
        rows = slice(u * sub, (u + 1) * sub)
        hn = h_ref[rows, :] + _dot(a_ref[rows, :], w_ref[...])
        hn_ref[rows, :] = hn
        xn = _rms_unit(hn) * g_ref[...]
        for c in range(d // LANES):
            xr_ref[pl.ds(u * sub * SUBLANES + c, sub, stride=SUBLANES), :] = xn[:, c * LANES:(c + 1) * LANES]

        x_hi, x_lo = _split_bf16(xn)
        r_hi = _dot(x_hi, wr_ref[...])
        r_lo = _dot(x_lo, wr_ref[...])
        lg_t = (r_hi[:, 0:LANES] + r_hi[:, LANES:]) + (r_lo[:, 0:LANES] + r_lo[:, LANES:])
        lg = lg_t.T[0:ROUTER_ROWS] + br_ref[...]
        lc = lg[0:MOE_GROUPS]
        m_c = jnp.max(lc, axis=0, keepdims=True)
        pc_top = 1.0 / jnp.sum(jnp.exp(lc - m_c), axis=0, keepdims=True)
        g_idx = jnp.min(jnp.where(lc == m_c, row_g, MOE_GROUPS), axis=0, keepdims=True)
        lf = jnp.zeros((eg, sub), F32)
        for g in range(MOE_GROUPS):
            lf = lf + jnp.where(g_idx == g, lg[SUBLANES + g * eg:SUBLANES + (g + 1) * eg], 0.0)
        ef = jnp.exp(lf - jnp.max(lf, axis=0, keepdims=True))
        pf = ef / jnp.sum(ef, axis=0, keepdims=True)
        v1 = jnp.max(pf, axis=0, keepdims=True)
        j1 = jnp.min(jnp.where(pf == v1, row_e, eg), axis=0, keepdims=True)
        pf2 = jnp.where(row_e == j1, -1.0, pf)
        v2 = jnp.max(pf2, axis=0, keepdims=True)
        j2 = jnp.min(jnp.where(pf2 == v2, row_e, eg), axis=0, keepdims=True)
        denom = v1 + v2
        e1 = g_idx * eg + j1
        e2 = g_idx * eg + j2
        eidx_ref[0:1, rows] = e1
        eidx_ref[1:2, rows] = e2
        gate_ref[0:1, rows] = pc_top * v1 / denom
        gate_ref[1:2, rows] = pc_top * v2 / denom

        oh1 = row_x == e1
        oh2 = row_x == e2
        pre1 = _dot(jnp.where(oh1, 1.0, 0.0).astype(BF16), upper)
        pre2 = _dot(jnp.where(oh2, 1.0, 0.0).astype(BF16), upper)
        cnt1 = pre1[:, sub - 1:sub]
        cnt2 = pre2[:, sub - 1:sub]
        rank1 = jnp.sum(jnp.where(oh1, base + pre1 - 1.0, 0.0), axis=0, keepdims=True)
        rank2 = jnp.sum(jnp.where(oh2, base + cnt1 + pre2 - 1.0, 0.0), axis=0, keepdims=True)
        rank_ref[0:1, rows] = rank1.astype(jnp.int32)
        rank_ref[1:2, rows] = rank2.astype(jnp.int32)
        base = base + (cnt1 + cnt2)

    new_base = jnp.broadcast_to(base, base_ref.shape)
    base_ref[...] = new_base
    cnt_ref[...] = new_base


def _post(a, w_out, h, gain, w_coarse, b_coarse, w_fine, b_fine, tm, sub):
    t, d = h.shape
    wr = jnp.concatenate([w_coarse, jnp.zeros((d, SUBLANES - MOE_GROUPS), F32), w_fine,
                          jnp.zeros((d, LANES - ROUTER_ROWS), F32)], axis=1)
    wr = jnp.concatenate(_split_bf16(wr), axis=1)
    br = jnp.zeros((ROUTER_ROWS, 1), F32)
    br = br.at[0:MOE_GROUPS, 0].set(b_coarse).at[SUBLANES:, 0].set(b_fine)
    const = lambda i: (0, 0)
    row = lambda i: (i, 0)
    col = lambda i: (0, i)
    return pl.pallas_call(
        functools.partial(_post_kernel, tm=tm, sub=sub, d=d),
        grid=(t // tm,),
        in_specs=[
            pl.BlockSpec((tm, a.shape[1]), row),
            pl.BlockSpec(w_out.shape, const),
            pl.BlockSpec((tm, d), row),
            pl.BlockSpec((1, d), const),
            pl.BlockSpec((d, 2 * LANES), const),
            pl.BlockSpec((ROUTER_ROWS, 1), const),
        ],
        out_specs=[
            pl.BlockSpec((tm, d), row),
            pl.BlockSpec((tm * SUBLANES, LANES), row),
            pl.BlockSpec((MOE_TOP_K, tm), col),
            pl.BlockSpec((MOE_TOP_K, tm), col),
            pl.BlockSpec((MOE_TOP_K, tm), col),
            pl.BlockSpec((MOE_EXPERTS, LANES), const),
        ],
        out_shape=[
            jax.ShapeDtypeStruct((t, d), F32),
            jax.ShapeDtypeStruct((t * SUBLANES, LANES), F32),
            jax.ShapeDtypeStruct((MOE_TOP_K, t), jnp.int32),
            jax.ShapeDtypeStruct((MOE_TOP_K, t), F32),
            jax.ShapeDtypeStruct((MOE_TOP_K, t), jnp.int32),
            jax.ShapeDtypeStruct((MOE_EXPERTS, LANES), F32),
        ],
        scratch_shapes=[pltpu.VMEM((MOE_EXPERTS, LANES), F32)],
        compiler_params=_cparams(("arbitrary",)),
        name="post",
    )(a, w_out.astype(BF16), h, gain.reshape(1, d), wr, br)


def _row_tile(ref, idx):
    return ref.at[pl.ds(pl.multiple_of(idx * SUBLANES, SUBLANES), SUBLANES), :]


def _block_rows(ref, blk):
    return ref.at[pl.ds(pl.multiple_of(blk * (MOE_BLOCK * SUBLANES), MOE_BLOCK * SUBLANES), MOE_BLOCK * SUBLANES), :]


def _dispatch_kernel(pe_ref, nu_ref, dest_ref, xr_ref, xbuf_ref, zero_ref, sem, zsem, *, tm, n_blocks):
    @pl.when(pl.program_id(0) == 0)
    def _():
        zero_ref[...] = jnp.zeros_like(zero_ref)

        def last_block(e):
            return pe_ref[e] // MOE_BLOCK - 1

        def has_rows(e):
            return pe_ref[e] > jnp.where(e == 0, 0, pe_ref[jnp.maximum(e - 1, 0)])

        def zero_block(blk):
            return pltpu.make_async_copy(zero_ref, _block_rows(xbuf_ref, blk), zsem)

        def start_e(e, carry):
            @pl.when(has_rows(e))
            def _():
                zero_block(last_block(e)).start()
            return carry

        def wait_e(e, carry):
            @pl.when(has_rows(e))
            def _():
                zero_block(last_block(e)).wait()
            return carry

        def start_b(blk, carry):
            zero_block(blk).start()
            return carry

        def wait_b(blk, carry):
            zero_block(blk).wait()
            return carry

        lax.fori_loop(0, MOE_EXPERTS, start_e, 0)
        lax.fori_loop(nu_ref[0], n_blocks, start_b, 0)
        lax.fori_loop(0, MOE_EXPERTS, wait_e, 0)
        lax.fori_loop(nu_ref[0], n_blocks, wait_b, 0)

    dtile = dest_ref.shape[2]

    def issue(tk, carry):
        for s in range(MOE_TOP_K):
            pltpu.make_async_copy(_row_tile(xr_ref, tk), _row_tile(xbuf_ref, dest_ref[tk // dtile, s, tk % dtile]),
                                  sem).start(priority=s)
        return carry

    for tk in range(tm):
        issue(tk, 0)
    for s in range(MOE_TOP_K):
        pltpu.make_async_copy(xr_ref, xbuf_ref.at[pl.ds(0, tm * SUBLANES), :], sem).wait()


def _dispatch(xr, dest3, pad_end, n_used, n_blocks, tm):
    tiles = tm // dest3.shape[2]
    nt = dest3.shape[0] // tiles
    grid_spec = pltpu.PrefetchScalarGridSpec(
        num_scalar_prefetch=2,
        grid=(nt,),
        in_specs=[
            pl.BlockSpec((tiles, MOE_TOP_K, dest3.shape[2]), lambda i, pe, nu: (i, 0, 0), memory_space=pltpu.SMEM),
            pl.BlockSpec((tm * SUBLANES, LANES), lambda i, pe, nu: (i, 0)),
        ],
        out_specs=pl.BlockSpec(memory_space=pl.ANY),
        scratch_shapes=[pltpu.VMEM((MOE_BLOCK * SUBLANES, LANES), F32), pltpu.SemaphoreType.DMA,
                        pltpu.SemaphoreType.DMA],
    )
    return pl.pallas_call(
        functools.partial(_dispatch_kernel, tm=tm, n_blocks=n_blocks),
        grid_spec=grid_spec,
        out_shape=jax.ShapeDtypeStruct((n_blocks * MOE_BLOCK * SUBLANES, LANES), F32),
        compiler_params=_cparams(("arbitrary",)),
        name="dispatch",
    )(pad_end, n_used, dest3, xr)


def _experts_kernel(be_ref, nu_ref, nxt_ref, par_ref, x_ref, wg_hbm, wu_hbm, wd_hbm, y_ref,
                    x2_ref, wgb_ref, wub_ref, wdb_ref, wgf_ref, wuf_ref, wdf_ref, sems, *, d, layer):
    rows = MOE_BLOCK
    b = pl.program_id(0)
    live = b < nu_ref[0]
    e = be_ref[b]
    new_expert = jnp.logical_or(b == 0, e != be_ref[jnp.maximum(b - 1, 0)])

    def weight_copies(ex, slot):
        return [pltpu.make_async_copy(src.at[layer, ex], dst.at[slot], sems.at[slot, j])
                for j, (src, dst) in enumerate(((wg_hbm, wgf_ref), (wu_hbm, wuf_ref), (wd_hbm, wdf_ref)))]

    @pl.when(jnp.logical_and(live, new_expert))
    def _():
        slot = par_ref[e]

        @pl.when(b == 0)
        def _():
            for cp in weight_copies(e, slot):
                cp.start()

        @pl.when(nxt_ref[e] >= 0)
        def _():
            for cp in weight_copies(nxt_ref[e], 1 - slot):
                cp.start()

        for cp in weight_copies(e, slot):
            cp.wait()
        wgb_ref[...] = wgf_ref[slot].astype(BF16)
        wub_ref[...] = wuf_ref[slot].astype(BF16)
        wdb_ref[...] = wdf_ref[slot].astype(BF16)

    @pl.when(live)
    def _():
        for c in range(d // LANES):
            x2_ref[:, c * LANES:(c + 1) * LANES] = x_ref[pl.ds(c, rows, stride=SUBLANES), :].astype(BF16)
        x2 = x2_ref[...]
        gp = _dot(x2, wgb_ref[...])
        up = _dot(x2, wub_ref[...])
        hid = (gp * _sigmoid(gp) * up).astype(BF16)
        y = _dot(hid, wdb_ref[...])
        for c in range(d // LANES):
            y_ref[pl.ds(c, rows, stride=SUBLANES), :] = y[:, c * LANES:(c + 1) * LANES]


def _experts(xbuf, block_e, n_used, padded, w_gate, w_up, w_down, layer):
    n_blocks = block_e.shape[0]
    d, dff = w_gate.shape[2], w_gate.shape[3]
    owns = padded > 0
    ids = jnp.arange(MOE_EXPERTS, dtype=jnp.int32)
    later = jnp.where(owns, ids, MOE_EXPERTS)
    nxt = lax.cummin(jnp.concatenate([later[1:], jnp.full((1,), MOE_EXPERTS, jnp.int32)]), reverse=True)
    nxt = jnp.where(nxt < MOE_EXPERTS, nxt, -1).astype(jnp.int32)
    par = ((jnp.cumsum(owns.astype(jnp.int32)) - owns.astype(jnp.int32)) % 2).astype(jnp.int32)
    xmap = lambda b, be, nu, nx, pa: (jnp.minimum(b, nu[0] - 1), 0)
    grid_spec = pltpu.PrefetchScalarGridSpec(
        num_scalar_prefetch=4,
        grid=(n_blocks,),
        in_specs=[
            pl.BlockSpec((MOE_BLOCK * SUBLANES, LANES), xmap),
            pl.BlockSpec(memory_space=pl.ANY),
            pl.BlockSpec(memory_space=pl.ANY),
            pl.BlockSpec(memory_space=pl.ANY),
        ],
        out_specs=pl.BlockSpec((MOE_BLOCK * SUBLANES, LANES), xmap),
        scratch_shapes=[pltpu.VMEM((MOE_BLOCK, d), BF16), pltpu.VMEM((d, dff), BF16), pltpu.VMEM((d, dff), BF16),
                        pltpu.VMEM((dff, d), BF16), pltpu.VMEM((2, d, dff), F32), pltpu.VMEM((2, d, dff), F32),
                        pltpu.VMEM((2, dff, d), F32), pltpu.SemaphoreType.DMA((2, 3))],
    )
    return pl.pallas_call(
        functools.partial(_experts_kernel, d=d, layer=layer),
        grid_spec=grid_spec,
        out_shape=jax.ShapeDtypeStruct(xbuf.shape, F32),
        input_output_aliases={4: 0},
        compiler_params=_cparams(("arbitrary",)),
        name="experts",
    )(block_e, n_used, nxt, par, xbuf, w_gate, w_up, w_down)


def _gather_combine(dcur_ref, dnxt_ref, ybuf_ref, h_ref, gate_ref, y_ref, sems, *, tm, d):
    i = pl.program_id(0)
    dtile = dcur_ref.shape[2]

    def start(dref, slot):
        def issue(tk, carry):
            for s in range(MOE_TOP_K):
                pltpu.make_async_copy(_row_tile(ybuf_ref, dref[tk // dtile, s, tk % dtile]),
                                      _row_tile(y_ref.at[slot, s], tk), sems.at[slot]).start(priority=s)
            return carry

        for tk in range(tm):
            issue(tk, 0)

    @pl.when(i == 0)
    def _():
        start(dcur_ref, 0)

    @pl.when(i + 1 < pl.num_programs(0))
    def _():
        start(dnxt_ref, (i + 1) % 2)

    slot = i % 2
    for s in range(MOE_TOP_K):
        pltpu.make_async_copy(ybuf_ref.at[pl.ds(0, tm * SUBLANES), :], y_ref.at[slot, s], sems.at[slot]).wait()
    gate_t = jnp.concatenate([gate_ref[...], jnp.zeros((SUBLANES - MOE_TOP_K, tm), F32)], axis=0).T
    g0 = gate_t[:, 0:1]
    g1 = gate_t[:, 1:2]
    cols = []
    for c in range(d // LANES):
        y0 = y_ref[slot, 0, pl.ds(c, tm, stride=SUBLANES), :]
        y1 = y_ref[slot, 1, pl.ds(c, tm, stride=SUBLANES), :]
        cols.append(h_ref[:, c * LANES:(c + 1) * LANES] + (y0 * g0 + y1 * g1))
    return jnp.concatenate(cols, axis=1)


def _combine_specs(tm, d, dest3):
    tiles = tm // dest3.shape[2]
    nt = dest3.shape[0] // tiles
    dblk = (tiles, MOE_TOP_K, dest3.shape[2])
    in_specs = [
        pl.BlockSpec(dblk, lambda i: (i, 0, 0), memory_space=pltpu.SMEM),
        pl.BlockSpec(dblk, lambda i: (jnp.minimum(i + 1, nt - 1), 0, 0), memory_space=pltpu.SMEM),
        pl.BlockSpec(memory_space=pl.ANY),
        pl.BlockSpec((tm, d), lambda i: (i, 0)),
        pl.BlockSpec((MOE_TOP_K, tm), lambda i: (0, i)),
    ]
    scratch = [pltpu.VMEM((2, MOE_TOP_K, tm * SUBLANES, LANES), F32), pltpu.SemaphoreType.DMA((2,))]
    return nt, in_specs, scratch


def _final_kernel(dcur_ref, dnxt_ref, ybuf_ref, h_ref, gate_ref, fg_ref, o_ref, y_ref, sems, *, tm, d):
    out = _gather_combine(dcur_ref, dnxt_ref, ybuf_ref, h_ref, gate_ref, y_ref, sems, tm=tm, d=d)
    o_ref[...] = _rms_unit(out) * fg_ref[...]


def _final(moe, final_gain, tm):
    ybuf, dest3, gate, h = moe
    t, d = h.shape
    nt, in_specs, scratch = _combine_specs(tm, d, dest3)
    return pl.pallas_call(
        functools.partial(_final_kernel, tm=tm, d=d),
        grid=(nt,),
        in_specs=in_specs + [pl.BlockSpec((1, d), lambda i: (0, 0))],
        out_specs=pl.BlockSpec((tm, d), lambda i: (i, 0)),
        out_shape=jax.ShapeDtypeStruct((t, d), F32),
        scratch_shapes=scratch,
        compiler_params=_cparams(("arbitrary",)),
        name="final",
    )(dest3, dest3, ybuf, h, gate, final_gain.reshape(1, d))


def _dest_kernel(ps_ref, eidx_ref, rank_ref, dest_ref, *, tm):
    eidx = eidx_ref[...]
    dest = rank_ref[...]
    for e in range(MOE_EXPERTS):
        dest = dest + jnp.where(eidx == e, ps_ref[e], 0)
    for i in range(dest_ref.shape[0]):
        dest_ref[i] = dest[:, i * tm:(i + 1) * tm]


def _dest(pad_start, eidx, rank, tm):
    t = eidx.shape[1]
    whole = lambda i, ps: (0, 0)
    grid_spec = pltpu.PrefetchScalarGridSpec(
        num_scalar_prefetch=1,
        grid=(1,),
        in_specs=[pl.BlockSpec((MOE_TOP_K, t), whole), pl.BlockSpec((MOE_TOP_K, t), whole)],
        out_specs=pl.BlockSpec((t // tm, MOE_TOP_K, tm), lambda i, ps: (0, 0, 0)),
    )
    return pl.pallas_call(
        functools.partial(_dest_kernel, tm=tm),
        grid_spec=grid_spec,
        out_shape=jax.ShapeDtypeStruct((t // tm, MOE_TOP_K, tm), jnp.int32),
        compiler_params=_cparams(("arbitrary",)),
        name="dest",
    )(pad_start, eidx, rank)


def _moe_experts(xr, eidx, gate, rank, counts, h, w_gate, w_up, w_down, layer, tm_row):
    t = h.shape[0]
    n_slot = t * MOE_TOP_K
    n_blocks = -(-n_slot // MOE_BLOCK) + MOE_EXPERTS
    cnt = counts[:, 0].astype(jnp.int32)
    padded = (cnt + MOE_BLOCK - 1) // MOE_BLOCK * MOE_BLOCK
    pad_end = jnp.cumsum(padded)
    pad_start = pad_end - padded
    block_start = jnp.arange(n_blocks, dtype=jnp.int32) * MOE_BLOCK
    block_e = jnp.minimum(jnp.sum((pad_end[None, :] <= block_start[:, None]).astype(jnp.int32), axis=1),
                          MOE_EXPERTS - 1)
    n_used = (pad_end[-1:] // MOE_BLOCK).astype(jnp.int32)
    dest3 = _dest(pad_start.astype(jnp.int32), eidx, rank, tm_row)
    xbuf = _dispatch(xr, dest3, pad_end.astype(jnp.int32), n_used, n_blocks, 2 * tm_row)
    ybuf = _experts(xbuf, block_e, n_used, padded, w_gate, w_up, w_down, layer)
    return ybuf, dest3, gate, h


def _fox_in_kernel(dcur_ref, dnxt_ref, ybuf_ref, h_ref, gate_ref, gkv_ref, gq_ref, wkv_ref, wf_ref, bf_ref, wq_ref,
                   place_ref, hn_ref, k_ref, v_ref, q_ref, og_ref, ka_ref, y_ref, sems, carry_ref,
                   *, tm, sub, d, width, tiles_per_seq, qscale):
    i = pl.program_id(0)

    @pl.when(i % tiles_per_seq == 0)
    def _():
        carry_ref[...] = jnp.zeros_like(carry_ref)

    hn_ref[...] = _gather_combine(dcur_ref, dnxt_ref, ybuf_ref, h_ref, gate_ref, y_ref, sems, tm=tm, d=d)
    ri = lax.broadcasted_iota(jnp.int32, (sub, sub), 0)
    ci = lax.broadcasted_iota(jnp.int32, (sub, sub), 1)
    ltri = jnp.where(ri >= ci, 1.0, 0.0).astype(BF16)
    carry = carry_ref[0:1, :]
    for u in range(tm // sub):
        rows = slice(u * sub, (u + 1) * sub)
        y = _rms_unit(hn_ref[rows, :])
        xkv = (y * gkv_ref[...]).astype(BF16)
        xq = (y * gq_ref[...]).astype(BF16)
        k_ref[rows, :] = _dot(xkv, wkv_ref[:, 0:width]).astype(BF16)
        v_ref[rows, :] = _dot(xkv, wkv_ref[:, width:2 * width]).astype(BF16)
        q_ref[rows, :] = (_dot(xq, wq_ref[:, 0:width]) * qscale).astype(BF16)
        og_ref[rows, :] = _dot(xq, wq_ref[:, width:2 * width]).astype(BF16)

        log_f = _log_sigmoid(_dot(xkv, wf_ref[...]) + bf_ref[...])
        f_hi, f_mid = _split_bf16(log_f)
        f_lo = (log_f - f_hi.astype(F32) - f_mid.astype(F32)).astype(BF16)
        c = carry + (_dot(ltri, f_hi) + (_dot(ltri, f_mid) + _dot(ltri, f_lo)))
        carry = c[sub - 1:sub, :]
        nc = c * (-LOG2_E)
        n_hi, n_mid = _split_bf16(nc)
        n_lo = (nc - n_hi.astype(F32) - n_mid.astype(F32)).astype(BF16)
        ka_ref[rows, :] = _dot(jnp.concatenate([n_hi, n_mid, n_lo], axis=1), place_ref[...]).astype(BF16)
    carry_ref[0:1, :] = carry


ATTN_BIAS_LANES = 3
LOG2_E = 1.4426950408889634


def _bias_lane0(hh, dh):
    return dh if hh == 0 else 0


def _fox_in(moe, kv_gain, q_gain, kv_w, kv_b_f, w_in, seq, tm):
    ybuf, dest3, gate, h = moe
    t, d = h.shape
    width = w_in.shape[1] // 2
    heads = kv_w.shape[1] - 2 * width
    dh = width // heads
    place = np.zeros((ATTN_BIAS_LANES * heads, width), np.float32)
    for part in range(ATTN_BIAS_LANES):
        for hd in range(heads):
            place[part * heads + hd, (hd // 2) * LANES + _bias_lane0(hd % 2, dh) + part] = 1.0
    const = lambda i: (0, 0)
    row = lambda i: (i, 0)
    big = jax.ShapeDtypeStruct((t, width), BF16)
    nt, comb_specs, comb_scratch = _combine_specs(tm, d, dest3)
    return pl.pallas_call(
        functools.partial(_fox_in_kernel, tm=tm, sub=_tile(tm, 256), d=d, width=width, tiles_per_seq=seq // tm,
                          qscale=dh ** -0.5 * LOG2_E),
        grid=(nt,),
        in_specs=comb_specs + [
            pl.BlockSpec((1, d), const),
            pl.BlockSpec((1, d), const),
            pl.BlockSpec((d, 2 * width), const),
            pl.BlockSpec((d, heads), const),
            pl.BlockSpec((1, heads), const),
            pl.BlockSpec((d, 2 * width), const),
            pl.BlockSpec(place.shape, const),
        ],
        out_specs=[pl.BlockSpec((tm, d), row)] + [pl.BlockSpec((tm, width), row)] * 5,
        out_shape=[jax.ShapeDtypeStruct((t, d), F32)] + [big] * 5,
        scratch_shapes=comb_scratch + [pltpu.VMEM((SUBLANES, heads), F32)],
        compiler_params=_cparams(("arbitrary",)),
        name="fox_in",
    )(dest3, dest3, ybuf, h, gate, kv_gain.reshape(1, d), q_gain.reshape(1, d), kv_w[:, :2 * width].astype(BF16),
      kv_w[:, 2 * width:].astype(BF16), kv_b_f.reshape(1, heads), w_in.astype(BF16), jnp.asarray(place, BF16))


ATTN_KV_CHUNK = 512


def _attn_kernel(q_ref, k_ref, v_ref, og_ref, kb_ref, o_ref, qa_ref, ka_ref, va_ref, s_ref, *, seq, tq, dh, pairs):
    lane = lax.broadcasted_iota(jnp.int32, (seq, LANES), 1)
    heads_here = [(pr, hh) for pr in range(pairs) for hh in range(2)]
    for n, (pr, hh) in enumerate(heads_here):
        cols = slice(pr * LANES, (pr + 1) * LANES)
        a0 = _bias_lane0(hh, dh)
        own = (lane < dh) if hh == 0 else (lane >= dh)
        ones_q = jnp.where((lane >= a0) & (lane < a0 + ATTN_BIAS_LANES), 1.0, 0.0).astype(BF16)
        qa_ref[n] = jnp.where(own, q_ref[:, cols], ones_q)
        ka_ref[n] = jnp.where(own, k_ref[:, cols], kb_ref[:, cols])
        va_ref[n] = jnp.where(own, v_ref[:, cols], jnp.where(lane == a0, 1.0, 0.0).astype(BF16))

    ri = lax.broadcasted_iota(jnp.int32, (tq, tq), 0)
    ci = lax.broadcasted_iota(jnp.int32, (tq, tq), 1)
    causal = ri >= ci
    lane_q = lax.broadcasted_iota(jnp.int32, (tq, LANES), 1)

    def fold_max(mx, s):
        for g in range(s.shape[1] // LANES):
            mx = jnp.maximum(mx, s[:, g * LANES:(g + 1) * LANES])
        return mx

    for qi in range(seq // tq):
        rows = slice(qi * tq, (qi + 1) * tq)
        past = [(c0, min(c0 + ATTN_KV_CHUNK, qi * tq)) for c0 in range(0, qi * tq, ATTN_KV_CHUNK)]
        row_max = []
        for n in range(len(heads_here)):
            qa = qa_ref[n, rows, :]
            mx = jnp.full((tq, LANES), -jnp.inf, F32)
            for c0, c1 in past:
                s = _dot_nt(qa, ka_ref[n, c0:c1, :])
                s_ref[n, :, c0:c1] = s
                mx = fold_max(mx, s)
            s = jnp.where(causal, _dot_nt(qa, ka_ref[n, rows, :]), -jnp.inf)
            s_ref[n, :, rows] = s
            row_max.append(jnp.max(fold_max(mx, s), axis=1, keepdims=True))
        outs = []
        for n, (pr, hh) in enumerate(heads_here):
            acc = jnp.zeros((tq, LANES), F32)
            for c0, c1 in past + [(qi * tq, (qi + 1) * tq)]:
                p = jnp.exp2(s_ref[n, :, c0:c1] - row_max[n]).astype(BF16)
                acc = acc + _dot(p, va_ref[n, c0:c1, :])
            a0 = _bias_lane0(hh, dh)
            outs.append(acc / acc[:, a0:a0 + 1])
        for pr in range(pairs):
            cols = slice(pr * LANES, (pr + 1) * LANES)
            o = jnp.where(lane_q < dh, outs[2 * pr], outs[2 * pr + 1])
            o_ref[rows, cols] = (o * _sigmoid(og_ref[rows, cols].astype(F32))).astype(BF16)


ATTN_PAIRS = 2


def _attn(q, k, v, og, kb, heads, batch, seq, tq):
    t, width = q.shape
    dh = width // heads
    assert 2 * dh == LANES, "two heads per 128-lane block"
    pairs = ATTN_PAIRS
    blk = pl.BlockSpec((seq, pairs * LANES), lambda b, hp: (b, hp))
    return pl.pallas_call(
        functools.partial(_attn_kernel, seq=seq, tq=tq, dh=dh, pairs=pairs),
        grid=(batch, heads // (2 * pairs)),
        in_specs=[blk] * 5,
        out_specs=blk,
        out_shape=jax.ShapeDtypeStruct((t, width), BF16),
        scratch_shapes=[pltpu.VMEM((2 * pairs, seq, LANES), BF16)] * 3 + [pltpu.VMEM((2 * pairs, tq, seq), F32)],
        compiler_params=_cparams(("arbitrary", "arbitrary")),
        name="attn",
    )(q, k, v, og, kb)


def _tile(n, pref):
    while n % pref:
        pref //= 2
    return pref


def kernel(x, gla_norm, gla_w_in, gla_w_a2, gla_b_a, gla_out_norm, gla_w_out, kv_norm, kv_w, kv_b_f, fox_norm,
           fox_w_in, fox_w_out, ffn_norm, router_coarse_w, router_coarse_b, router_fine_w, router_fine_b,
           expert_w_gate, expert_w_up, expert_w_down, final_norm):
    batch, seq, d = x.shape
    t = batch * seq
    assert gla_norm.shape[0] == 1 and ffn_norm.shape[0] == 2, "trunk is GLA layer + FoX layer"
    tm = _tile(seq, 512)

    def ffn(a, w_out, h, layer):
        h, xr, eidx, gate, rank, counts = _post(a, w_out, h, ffn_norm[layer], router_coarse_w[layer],
                                                router_coarse_b[layer], router_fine_w[layer],
                                                router_fine_b[layer], _tile(seq, 1024), tm)
        return _moe_experts(xr, eidx, gate, rank, counts, h, expert_w_gate, expert_w_up, expert_w_down, layer, tm)

    h = x.reshape(t, d)
    q, k, v, r, la = _gla_in(h, gla_norm[0], gla_w_in[0], gla_w_a2[0], gla_b_a[0], _tile(seq, 1024))
    a = _gla_core(q, k, v, r, la, gla_out_norm[0], batch, seq, _tile(seq, 1024), 1)
    moe = ffn(a, gla_w_out[0], h, 0)
    h, kk, vv, qq, og, kb = _fox_in(moe, kv_norm, fox_norm[0], kv_w, kv_b_f, fox_w_in[0], seq, tm)
    a = _attn(qq, kk, vv, og, kb, kv_b_f.shape[0], batch, seq, _tile(seq, 256))
    moe = ffn(a, fox_w_out[0], h, 1)
    return _final(moe, final_norm, 2 * tm).reshape(batch, seq, d)
```

```python
import functools

import jax
import jax.numpy as jnp
import numpy as np
from jax import lax
from jax.experimental import pallas as pl
from jax.experimental.pallas import tpu as pltpu

F32 = jnp.float32
BF16 = jnp.bfloat16

RMS_EPS = 1e-6
GLA_HEADS = 4
GLA_GATE_RANK = 16
GLA_GATE_TAU = 16.0
FOX_HEADS = 16
MOE_GROUPS = 4
MOE_EXPERTS_PER_GROUP = 8
MOE_EXPERTS = MOE_GROUPS * MOE_EXPERTS_PER_GROUP
MOE_TOP_K = 2
MOE_BLOCK = 512

LANES = 128
SUBLANES = 8
VMEM_LIMIT = 56 * 1024 * 1024

GLA_CHUNK = 128
ROUTER_ROWS = 40


def _cparams(sem):
    return pltpu.CompilerParams(dimension_semantics=sem, vmem_limit_bytes=VMEM_LIMIT)


def _rms_unit(x):
    return x * lax.rsqrt(jnp.mean(x * x, axis=-1, keepdims=True) + RMS_EPS)


def _log_sigmoid(z):
    return jnp.minimum(z, 0.0) - jnp.log1p(jnp.exp(-jnp.abs(z)))


def _sigmoid(z):
    return 1.0 / (1.0 + jnp.exp(-z))


def _dot(a, b):
    return jnp.dot(a, b, preferred_element_type=F32)


def _dot_nt(a, b):
    return lax.dot_general(a, b, (((1,), (1,)), ((), ())), preferred_element_type=F32)


def _dot_tn(a, b):
    return lax.dot_general(a, b, (((0,), (0,)), ((), ())), preferred_element_type=F32)


def _split_bf16(x):
    hi = x.astype(BF16)
    lo = (x - hi.astype(F32)).astype(BF16)
    return hi, lo


def _gla_in_kernel(h_ref, g_ref, w_ref, wa1_ref, wa2_ref, ba_ref, q_ref, k_ref, v_ref, r_ref, la_ref, *, qk, vw):
    xn = (_rms_unit(h_ref[...]) * g_ref[...]).astype(BF16)
    q_ref[...] = _dot(xn, w_ref[:, 0:qk]).astype(BF16)
    k_ref[...] = _dot(xn, w_ref[:, qk:2 * qk]).astype(BF16)
    v_ref[...] = _dot(xn, w_ref[:, 2 * qk:2 * qk + vw]).astype(BF16)
    r_ref[...] = _dot(xn, w_ref[:, 2 * qk + vw:2 * qk + 2 * vw]).astype(BF16)
    a = _dot(xn, wa1_ref[...])
    z = _dot(a.astype(BF16), wa2_ref[...]) + ba_ref[...]
    la_ref[...] = _log_sigmoid(z) * (1.0 / GLA_GATE_TAU)


def _gla_in(h, gain, w_in, w_a2, b_a, tm):
    t, d = h.shape
    qk = w_a2.shape[1]
    vw = (w_in.shape[1] - 2 * qk - GLA_GATE_RANK) // 2
    w_main = w_in[:, :2 * qk + 2 * vw].astype(BF16)
    w_a1 = w_in[:, 2 * qk + 2 * vw:].astype(BF16)
    const = lambda i: (0, 0)
    row = lambda i: (i, 0)
    return pl.pallas_call(
        functools.partial(_gla_in_kernel, qk=qk, vw=vw),
        grid=(t // tm,),
        in_specs=[
            pl.BlockSpec((tm, d), row),
            pl.BlockSpec((1, d), const),
            pl.BlockSpec(w_main.shape, const),
            pl.BlockSpec(w_a1.shape, const),
            pl.BlockSpec(w_a2.shape, const),
            pl.BlockSpec((1, qk), const),
        ],
        out_specs=[
            pl.BlockSpec((tm, qk), row),
            pl.BlockSpec((tm, qk), row),
            pl.BlockSpec((tm, vw), row),
            pl.BlockSpec((tm, vw), row),
            pl.BlockSpec((tm, qk), row),
        ],
        out_shape=[
            jax.ShapeDtypeStruct((t, qk), BF16),
            jax.ShapeDtypeStruct((t, qk), BF16),
            jax.ShapeDtypeStruct((t, vw), BF16),
            jax.ShapeDtypeStruct((t, vw), BF16),
            jax.ShapeDtypeStruct((t, qk), F32),
        ],
        compiler_params=_cparams(("arbitrary",)),
        name="gla_in",
    )(h, gain.reshape(1, d), w_main, w_a1, w_a2.astype(BF16), b_a.reshape(1, qk))


def _gla_core_kernel(q_ref, k_ref, v_ref, r_ref, la_ref, gn_ref, o_ref, st_ref,
                     *, n_seq, n_chunks, heads, dk, dv, scale):
    c_len = GLA_CHUNK

    @pl.when(pl.program_id(1) == 0)
    def _():
        st_ref[...] = jnp.zeros_like(st_ref)

    ri = lax.broadcasted_iota(jnp.int32, (c_len, c_len), 0)
    ci = lax.broadcasted_iota(jnp.int32, (c_len, c_len), 1)
    causal = ri >= ci
    ltri = jnp.where(causal, 1.0, 0.0).astype(BF16)

    for c in range(n_chunks):
        sl = pl.ds(c * c_len, c_len)
        for sq in range(n_seq):
            for hd in range(heads):
                ks_ = slice(hd * dk, (hd + 1) * dk)
                vs_ = slice(hd * dv, (hd + 1) * dv)
                g_hi, g_lo = _split_bf16(la_ref[sq, sl, ks_])
                cum = _dot(ltri, g_hi) + _dot(ltri, g_lo)
                last = cum[c_len - 1:c_len, :]
                mid = cum[c_len // 2 - 1:c_len // 2, :]
                qf = q_ref[sq, sl, ks_].astype(F32) * scale
                kf = k_ref[sq, sl, ks_].astype(F32)
                vb = v_ref[sq, sl, vs_]
                qs = (qf * jnp.exp(cum - mid)).astype(BF16)
                ks = (kf * jnp.exp(mid - cum)).astype(BF16)
                scores = jnp.where(causal, _dot_nt(qs, ks), 0.0)
                o = _dot(scores.astype(BF16), vb)
                st = st_ref[sq, hd]
                qd = (qf * jnp.exp(cum)).astype(BF16)
                o = o + _dot_nt(qd, st.astype(BF16))
                kd = (kf * jnp.exp(last - cum)).astype(BF16)
                st_ref[sq, hd] = st * jnp.exp(last) + _dot_tn(vb, kd)
                on = _rms_unit(o) * gn_ref[hd:hd + 1, :]
                rr = r_ref[sq, sl, vs_].astype(F32)
                o_ref[sq, sl, vs_] = (on * (rr * _sigmoid(rr))).astype(BF16)


def _gla_core(q, k, v, r, la, out_norm, batch, seq, lc, n_seq):
    t, qk = q.shape
    vw = v.shape[1]
    heads = GLA_HEADS
    dk, dv = qk // heads, vw // heads
    tok = lambda b, j: (b, j, 0)
    seqs = lambda a: a.reshape(batch, seq, a.shape[1])
    out = pl.pallas_call(
        functools.partial(_gla_core_kernel, n_seq=n_seq, n_chunks=lc // GLA_CHUNK, heads=heads, dk=dk, dv=dv,
                          scale=dk ** -0.5),
        grid=(batch // n_seq, seq // lc),
        in_specs=[
            pl.BlockSpec((n_seq, lc, qk), tok),
            pl.BlockSpec((n_seq, lc, qk), tok),
            pl.BlockSpec((n_seq, lc, vw), tok),
            pl.BlockSpec((n_seq, lc, vw), tok),
            pl.BlockSpec((n_seq, lc, qk), tok),
            pl.BlockSpec((heads, dv), lambda b, j: (0, 0)),
        ],
        out_specs=pl.BlockSpec((n_seq, lc, vw), tok),
        out_shape=jax.ShapeDtypeStruct((batch, seq, vw), BF16),
        scratch_shapes=[pltpu.VMEM((n_seq, heads, dv, dk), F32)],
        compiler_params=_cparams(("arbitrary", "arbitrary")),
        name="gla_core",
    )(seqs(q), seqs(k), seqs(v), seqs(r), seqs(la), out_norm)
    return out.reshape(t, vw)


def _post_kernel(a_ref, w_ref, h_ref, g_ref, wr_ref, br_ref,
                 hn_ref, xr_ref, eidx_ref, gate_ref, rank_ref, cnt_ref, base_ref, *, tm, sub, d):
    @pl.when(pl.program_id(0) == 0)
    def _():
        base_ref[...] = jnp.zeros_like(base_ref)

    eg = MOE_EXPERTS_PER_GROUP
    ti = lax.broadcasted_iota(jnp.int32, (sub, sub), 0)
    tj = lax.broadcasted_iota(jnp.int32, (sub, sub), 1)
    upper = jnp.where(ti <= tj, 1.0, 0.0).astype(BF16)
    row_g = lax.broadcasted_iota(jnp.int32, (MOE_GROUPS, sub), 0)
    row_e = lax.broadcasted_iota(jnp.int32, (eg, sub), 0)
    row_x = lax.broadcasted_iota(jnp.int32, (MOE_EXPERTS, sub), 0)
    base = base_ref[:, 0:1]

    for u in range(tm // sub):
        rows = slice(u * sub, (u + 1) * sub)
        hn = h_ref[rows, :] + _dot(a_ref[rows, :], w_ref[...])
        hn_ref[rows, :] = hn
        xn = _rms_unit(hn) * g_ref[...]
        for c in range(d // LANES):
            xr_ref[pl.ds(u * sub * SUBLANES + c, sub, stride=SUBLANES), :] = xn[:, c * LANES:(c + 1) * LANES]

        x_hi, x_lo = _split_bf16(xn)
        r_hi = _dot(x_hi, wr_ref[...])
        r_lo = _dot(x_lo, wr_ref[...])
        lg_t = (r_hi[:, 0:LANES] + r_hi[:, LANES:]) + (r_lo[:, 0:LANES] + r_lo[:, LANES:])
        lg = lg_t.T[0:ROUTER_ROWS] + br_ref[...]
        lc = lg[0:MOE_GROUPS]
        m_c = jnp.max(lc, axis=0, keepdims=True)
        pc_top = 1.0 / jnp.sum(jnp.exp(lc - m_c), axis=0, keepdims=True)
        g_idx = jnp.min(jnp.where(lc == m_c, row_g, MOE_GROUPS), axis=0, keepdims=True)
        lf = jnp.zeros((eg, sub), F32)
        for g in range(MOE_GROUPS):
            lf = lf + jnp.where(g_idx == g, lg[SUBLANES + g * eg:SUBLANES + (g + 1) * eg], 0.0)
        ef = jnp.exp(lf - jnp.max(lf, axis=0, keepdims=True))
        pf = ef / jnp.sum(ef, axis=0, keepdims=True)
        v1 = jnp.max(pf, axis=0, keepdims=True)
        j1 = jnp.min(jnp.where(pf == v1, row_e, eg), axis=0, keepdims=True)
        pf2 = jnp.where(row_e == j1, -1.0, pf)
        v2 = jnp.max(pf2, axis=0, keepdims=True)
        j2 = jnp.min(jnp.where(pf2 == v2, row_e, eg), axis=0, keepdims=True)
        denom = v1 + v2
        e1 = g_idx * eg + j1
        e2 = g_idx * eg + j2
        eidx_ref[0:1, rows] = e1
        eidx_ref[1:2, rows] = e2
        gate_ref[0:1, rows] = pc_top * v1 / denom
        gate_ref[1:2, rows] = pc_top * v2 / denom

        oh1 = row_x == e1
        oh2 = row_x == e2
        pre1 = _dot(jnp.where(oh1, 1.0, 0.0).astype(BF16), upper)
        pre2 = _dot(jnp.where(oh2, 1.0, 0.0).astype(BF16), upper)
        cnt1 = pre1[:, sub - 1:sub]
        cnt2 = pre2[:, sub - 1:sub]
        rank1 = jnp.sum(jnp.where(oh1, base + pre1 - 1.0, 0.0), axis=0, keepdims=True)
        rank2 = jnp.sum(jnp.where(oh2, base + cnt1 + pre2 - 1.0, 0.0), axis=0, keepdims=True)
        rank_ref[0:1, rows] = rank1.astype(jnp.int32)
        rank_ref[1:2, rows] = rank2.astype(jnp.int32)
        base = base + (cnt1 + cnt2)

    new_base = jnp.broadcast_to(base, base_ref.shape)
    base_ref[...] = new_base
    cnt_ref[...] = new_base


def _post(a, w_out, h, gain, w_coarse, b_coarse, w_fine, b_fine, tm, sub):
    t, d = h.shape
    wr = jnp.concatenate([w_coarse, jnp.zeros((d, SUBLANES - MOE_GROUPS), F32), w_fine,
                          jnp.zeros((d, LANES - ROUTER_ROWS), F32)], axis=1)
    wr = jnp.concatenate(_split_bf16(wr), axis=1)
    br = jnp.zeros((ROUTER_ROWS, 1), F32)
    br = br.at[0:MOE_GROUPS, 0].set(b_coarse).at[SUBLANES:, 0].set(b_fine)
    const = lambda i: (0, 0)
    row = lambda i: (i, 0)
    col = lambda i: (0, i)
    return pl.pallas_call(
        functools.partial(_post_kernel, tm=tm, sub=sub, d=d),
        grid=(t // tm,),
        in_specs=[
            pl.BlockSpec((tm, a.shape[1]), row),
            pl.BlockSpec(w_out.shape, const),
            pl.BlockSpec((tm, d), row),
            pl.BlockSpec((1, d), const),
            pl.BlockSpec((d, 2 * LANES), const),
            pl.BlockSpec((ROUTER_ROWS, 1), const),
        ],
        out_specs=[
            pl.BlockSpec((tm, d), row),
            pl.BlockSpec((tm * SUBLANES, LANES), row),
            pl.BlockSpec((MOE_TOP_K, tm), col),
            pl.BlockSpec((MOE_TOP_K, tm), col),
            pl.BlockSpec((MOE_TOP_K, tm), col),
            pl.BlockSpec((MOE_EXPERTS, LANES), const),
        ],
        out_shape=[
            jax.ShapeDtypeStruct((t, d), F32),
            jax.ShapeDtypeStruct((t * SUBLANES, LANES), F32),
            jax.ShapeDtypeStruct((MOE_TOP_K, t), jnp.int32),
            jax.ShapeDtypeStruct((MOE_TOP_K, t), F32),
            jax.ShapeDtypeStruct((MOE_TOP_K, t), jnp.int32),
            jax.ShapeDtypeStruct((MOE_EXPERTS, LANES), F32),
        ],
        scratch_shapes=[pltpu.VMEM((MOE_EXPERTS, LANES), F32)],
        compiler_params=_cparams(("arbitrary",)),
        name="post",
    )(a, w_out.astype(BF16), h, gain.reshape(1, d), wr, br)


def _row_tile(ref, idx):
    return ref.at[pl.ds(pl.multiple_of(idx * SUBLANES, SUBLANES), SUBLANES), :]


def _block_rows(ref, blk):
    return ref.at[pl.ds(pl.multiple_of(blk * (MOE_BLOCK * SUBLANES), MOE_BLOCK * SUBLANES), MOE_BLOCK * SUBLANES), :]


def _dispatch_kernel(pe_ref, nu_ref, dest_ref, xr_ref, xbuf_ref, zero_ref, sem, zsem, *, tm, n_blocks):
    @pl.when(pl.program_id(0) == 0)
    def _():
        zero_ref[...] = jnp.zeros_like(zero_ref)

        def last_block(e):
            return pe_ref[e] // MOE_BLOCK - 1

        def has_rows(e):
            return pe_ref[e] > jnp.where(e == 0, 0, pe_ref[jnp.maximum(e - 1, 0)])

        def zero_block(blk):
            return pltpu.make_async_copy(zero_ref, _block_rows(xbuf_ref, blk), zsem)

        def start_e(e, carry):
            @pl.when(has_rows(e))
            def _():
                zero_block(last_block(e)).start()
            return carry

        def wait_e(e, carry):
            @pl.when(has_rows(e))
            def _():
                zero_block(last_block(e)).wait()
            return carry

        def start_b(blk, carry):
            zero_block(blk).start()
            return carry

        def wait_b(blk, carry):
            zero_block(blk).wait()
            return carry

        lax.fori_loop(0, MOE_EXPERTS, start_e, 0)
        lax.fori_loop(nu_ref[0], n_blocks, start_b, 0)
        lax.fori_loop(0, MOE_EXPERTS, wait_e, 0)
        lax.fori_loop(nu_ref[0], n_blocks, wait_b, 0)

    dtile = dest_ref.shape[2]

    def issue(tk, carry):
        for s in range(MOE_TOP_K):
            pltpu.make_async_copy(_row_tile(xr_ref, tk), _row_tile(xbuf_ref, dest_ref[tk // dtile, s, tk % dtile]),
                                  sem).start(priority=s)
        return carry

    for tk in range(tm):
        issue(tk, 0)
    for s in range(MOE_TOP_K):
        pltpu.make_async_copy(xr_ref, xbuf_ref.at[pl.ds(0, tm * SUBLANES), :], sem).wait()


def _dispatch(xr, dest3, pad_end, n_used, n_blocks, tm):
    tiles = tm // dest3.shape[2]
    nt = dest3.shape[0] // tiles
    grid_spec = pltpu.PrefetchScalarGridSpec(
        num_scalar_prefetch=2,
        grid=(nt,),
        in_specs=[
            pl.BlockSpec((tiles, MOE_TOP_K, dest3.shape[2]), lambda i, pe, nu: (i, 0, 0), memory_space=pltpu.SMEM),
            pl.BlockSpec((tm * SUBLANES, LANES), lambda i, pe, nu: (i, 0)),
        ],
        out_specs=pl.BlockSpec(memory_space=pl.ANY),
        scratch_shapes=[pltpu.VMEM((MOE_BLOCK * SUBLANES, LANES), F32), pltpu.SemaphoreType.DMA,
                        pltpu.SemaphoreType.DMA],
    )
    return pl.pallas_call(
        functools.partial(_dispatch_kernel, tm=tm, n_blocks=n_blocks),
        grid_spec=grid_spec,
        out_shape=jax.ShapeDtypeStruct((n_blocks * MOE_BLOCK * SUBLANES, LANES), F32),
        compiler_params=_cparams(("arbitrary",)),
        name="dispatch",
    )(pad_end, n_used, dest3, xr)


def _experts_kernel(be_ref, nu_ref, nxt_ref, par_ref, x_ref, wg_hbm, wu_hbm, wd_hbm, y_ref,
                    x2_ref, wgb_ref, wub_ref, wdb_ref, wgf_ref, wuf_ref, wdf_ref, sems, *, d, layer):
    rows = MOE_BLOCK
    b = pl.program_id(0)
    live = b < nu_ref[0]
    e = be_ref[b]
    new_expert = jnp.logical_or(b == 0, e != be_ref[jnp.maximum(b - 1, 0)])

    def weight_copies(ex, slot):
        return [pltpu.make_async_copy(src.at[layer, ex], dst.at[slot], sems.at[slot, j])
                for j, (src, dst) in enumerate(((wg_hbm, wgf_ref), (wu_hbm, wuf_ref), (wd_hbm, wdf_ref)))]

    @pl.when(jnp.logical_and(live, new_expert))
    def _():
        slot = par_ref[e]

        @pl.when(b == 0)
        def _():
            for cp in weight_copies(e, slot):
                cp.start()

        @pl.when(nxt_ref[e] >= 0)
        def _():
            for cp in weight_copies(nxt_ref[e], 1 - slot):
                cp.start()

        for cp in weight_copies(e, slot):
            cp.wait()
        wgb_ref[...] = wgf_ref[slot].astype(BF16)
        wub_ref[...] = wuf_ref[slot].astype(BF16)
        wdb_ref[...] = wdf_ref[slot].astype(BF16)

    @pl.when(live)
    def _():
        for c in range(d // LANES):
            x2_ref[:, c * LANES:(c + 1) * LANES] = x_ref[pl.ds(c, rows, stride=SUBLANES), :].astype(BF16)
        x2 = x2_ref[...]
        gp = _dot(x2, wgb_ref[...])
        up = _dot(x2, wub_ref[...])
        hid = (gp * _sigmoid(gp) * up).astype(BF16)
        y = _dot(hid, wdb_ref[...])
        for c in range(d // LANES):
            y_ref[pl.ds(c, rows, stride=SUBLANES), :] = y[:, c * LANES:(c + 1) * LANES]


def _experts(xbuf, block_e, n_used, padded, w_gate, w_up, w_down, layer):
    n_blocks = block_e.shape[0]
    d, dff = w_gate.shape[2], w_gate.shape[3]
    owns = padded > 0
    ids = jnp.arange(MOE_EXPERTS, dtype=jnp.int32)
    later = jnp.where(owns, ids, MOE_EXPERTS)
    nxt = lax.cummin(jnp.concatenate([later[1:], jnp.full((1,), MOE_EXPERTS, jnp.int32)]), reverse=True)
    nxt = jnp.where(nxt < MOE_EXPERTS, nxt, -1).astype(jnp.int32)
    par = ((jnp.cumsum(owns.astype(jnp.int32)) - owns.astype(jnp.int32)) % 2).astype(jnp.int32)
    xmap = lambda b, be, nu, nx, pa: (jnp.minimum(b, nu[0] - 1), 0)
    grid_spec = pltpu.PrefetchScalarGridSpec(
        num_scalar_prefetch=4,
        grid=(n_blocks,),
        in_specs=[
            pl.BlockSpec((MOE_BLOCK * SUBLANES, LANES), xmap),
            pl.BlockSpec(memory_space=pl.ANY),
            pl.BlockSpec(memory_space=pl.ANY),
            pl.BlockSpec(memory_space=pl.ANY),
        ],
        out_specs=pl.BlockSpec((MOE_BLOCK * SUBLANES, LANES), xmap),
        scratch_shapes=[pltpu.VMEM((MOE_BLOCK, d), BF16), pltpu.VMEM((d, dff), BF16), pltpu.VMEM((d, dff), BF16),
                        pltpu.VMEM((dff, d), BF16), pltpu.VMEM((2, d, dff), F32), pltpu.VMEM((2, d, dff), F32),
                        pltpu.VMEM((2, dff, d), F32), pltpu.SemaphoreType.DMA((2, 3))],
    )
    return pl.pallas_call(
        functools.partial(_experts_kernel, d=d, layer=layer),
        grid_spec=grid_spec,
        out_shape=jax.ShapeDtypeStruct(xbuf.shape, F32),
        input_output_aliases={4: 0},
        compiler_params=_cparams(("arbitrary",)),
        name="experts",
    )(block_e, n_used, nxt, par, xbuf, w_gate, w_up, w_down)


def _gather_combine(dcur_ref, dnxt_ref, ybuf_ref, h_ref, gate_ref, y_ref, sems, *, tm, d):
    i = pl.program_id(0)
    dtile = dcur_ref.shape[2]

    def start(dref, slot):
        def issue(tk, carry):
            for s in range(MOE_TOP_K):
                pltpu.make_async_copy(_row_tile(ybuf_ref, dref[tk // dtile, s, tk % dtile]),
                                      _row_tile(y_ref.at[slot, s], tk), sems.at[slot]).start(priority=s)
            return carry

        for tk in range(tm):
            issue(tk, 0)

    @pl.when(i == 0)
    def _():
        start(dcur_ref, 0)

    @pl.when(i + 1 < pl.num_programs(0))
    def _():
        start(dnxt_ref, (i + 1) % 2)

    slot = i % 2
    for s in range(MOE_TOP_K):
        pltpu.make_async_copy(ybuf_ref.at[pl.ds(0, tm * SUBLANES), :], y_ref.at[slot, s], sems.at[slot]).wait()
    gate_t = jnp.concatenate([gate_ref[...], jnp.zeros((SUBLANES - MOE_TOP_K, tm), F32)], axis=0).T
    g0 = gate_t[:, 0:1]
    g1 = gate_t[:, 1:2]
    cols = []
    for c in range(d // LANES):
        y0 = y_ref[slot, 0, pl.ds(c, tm, stride=SUBLANES), :]
        y1 = y_ref[slot, 1, pl.ds(c, tm, stride=SUBLANES), :]
        cols.append(h_ref[:, c * LANES:(c + 1) * LANES] + (y0 * g0 + y1 * g1))
    return jnp.concatenate(cols, axis=1)


def _combine_specs(tm, d, dest3):
    tiles = tm // dest3.shape[2]
    nt = dest3.shape[0] // tiles
    dblk = (tiles, MOE_TOP_K, dest3.shape[2])
    in_specs = [
        pl.BlockSpec(dblk, lambda i: (i, 0, 0), memory_space=pltpu.SMEM),
        pl.BlockSpec(dblk, lambda i: (jnp.minimum(i + 1, nt - 1), 0, 0), memory_space=pltpu.SMEM),
        pl.BlockSpec(memory_space=pl.ANY),
        pl.BlockSpec((tm, d), lambda i: (i, 0)),
        pl.BlockSpec((MOE_TOP_K, tm), lambda i: (0, i)),
    ]
    scratch = [pltpu.VMEM((2, MOE_TOP_K, tm * SUBLANES, LANES), F32), pltpu.SemaphoreType.DMA((2,))]
    return nt, in_specs, scratch


def _final_kernel(dcur_ref, dnxt_ref, ybuf_ref, h_ref, gate_ref, fg_ref, o_ref, y_ref, sems, *, tm, d):
    out = _gather_combine(dcur_ref, dnxt_ref, ybuf_ref, h_ref, gate_ref, y_ref, sems, tm=tm, d=d)
    o_ref[...] = _rms_unit(out) * fg_ref[...]


def _final(moe, final_gain, tm):
    ybuf, dest3, gate, h = moe
    t, d = h.shape
    nt, in_specs, scratch = _combine_specs(tm, d, dest3)
    return pl.pallas_call(
        functools.partial(_final_kernel, tm=tm, d=d),
        grid=(nt,),
        in_specs=in_specs + [pl.BlockSpec((1, d), lambda i: (0, 0))],
        out_specs=pl.BlockSpec((tm, d), lambda i: (i, 0)),
        out_shape=jax.ShapeDtypeStruct((t, d), F32),
        scratch_shapes=scratch,
        compiler_params=_cparams(("arbitrary",)),
        name="final",
    )(dest3, dest3, ybuf, h, gate, final_gain.reshape(1, d))


def _dest_kernel(ps_ref, eidx_ref, rank_ref, dest_ref, *, tm):
    eidx = eidx_ref[...]
    dest = rank_ref[...]
    for e in range(MOE_EXPERTS):
        dest = dest + jnp.where(eidx == e, ps_ref[e], 0)
    for i in range(dest_ref.shape[0]):
        dest_ref[i] = dest[:, i * tm:(i + 1) * tm]


def _dest(pad_start, eidx, rank, tm):
    t = eidx.shape[1]
    whole = lambda i, ps: (0, 0)
    grid_spec = pltpu.PrefetchScalarGridSpec(
        num_scalar_prefetch=1,
        grid=(1,),
        in_specs=[pl.BlockSpec((MOE_TOP_K, t), whole), pl.BlockSpec((MOE_TOP_K, t), whole)],
        out_specs=pl.BlockSpec((t // tm, MOE_TOP_K, tm), lambda i, ps: (0, 0, 0)),
    )
    return pl.pallas_call(
        functools.partial(_dest_kernel, tm=tm),
        grid_spec=grid_spec,
        out_shape=jax.ShapeDtypeStruct((t // tm, MOE_TOP_K, tm), jnp.int32),
        compiler_params=_cparams(("arbitrary",)),
        name="dest",
    )(pad_start, eidx, rank)


def _moe_experts(xr, eidx, gate, rank, counts, h, w_gate, w_up, w_down, layer, tm_row):
    t = h.shape[0]
    n_slot = t * MOE_TOP_K
    n_blocks = -(-n_slot // MOE_BLOCK) + MOE_EXPERTS
    cnt = counts[:, 0].astype(jnp.int32)
    padded = (cnt + MOE_BLOCK - 1) // MOE_BLOCK * MOE_BLOCK
    pad_end = jnp.cumsum(padded)
    pad_start = pad_end - padded
    block_start = jnp.arange(n_blocks, dtype=jnp.int32) * MOE_BLOCK
    block_e = jnp.minimum(jnp.sum((pad_end[None, :] <= block_start[:, None]).astype(jnp.int32), axis=1),
                          MOE_EXPERTS - 1)
    n_used = (pad_end[-1:] // MOE_BLOCK).astype(jnp.int32)
    dest3 = _dest(pad_start.astype(jnp.int32), eidx, rank, tm_row)
    tm_dispatch = 2 * tm_row if t % (2 * tm_row) == 0 else tm_row
    xbuf = _dispatch(xr, dest3, pad_end.astype(jnp.int32), n_used, n_blocks, tm_dispatch)
    ybuf = _experts(xbuf, block_e, n_used, padded, w_gate, w_up, w_down, layer)
    return ybuf, dest3, gate, h


def _fox_in_kernel(dcur_ref, dnxt_ref, ybuf_ref, h_ref, gate_ref, gkv_ref, gq_ref, wkv_ref, wf_ref, bf_ref, wq_ref,
                   place_ref, hn_ref, k_ref, v_ref, q_ref, og_ref, ka_ref, y_ref, sems, carry_ref,
                   *, tm, sub, d, width, tiles_per_seq, qscale):
    i = pl.program_id(0)

    @pl.when(i % tiles_per_seq == 0)
    def _():
        carry_ref[...] = jnp.zeros_like(carry_ref)

    hn_ref[...] = _gather_combine(dcur_ref, dnxt_ref, ybuf_ref, h_ref, gate_ref, y_ref, sems, tm=tm, d=d)
    ri = lax.broadcasted_iota(jnp.int32, (sub, sub), 0)
    ci = lax.broadcasted_iota(jnp.int32, (sub, sub), 1)
    ltri = jnp.where(ri >= ci, 1.0, 0.0).astype(BF16)
    carry = carry_ref[0:1, :]
    for u in range(tm // sub):
        rows = slice(u * sub, (u + 1) * sub)
        y = _rms_unit(hn_ref[rows, :])
        xkv = (y * gkv_ref[...]).astype(BF16)
        xq = (y * gq_ref[...]).astype(BF16)
        k_ref[rows, :] = _dot(xkv, wkv_ref[:, 0:width]).astype(BF16)
        v_ref[rows, :] = _dot(xkv, wkv_ref[:, width:2 * width]).astype(BF16)
        q_ref[rows, :] = (_dot(xq, wq_ref[:, 0:width]) * qscale).astype(BF16)
        og_ref[rows, :] = _dot(xq, wq_ref[:, width:2 * width]).astype(BF16)

        log_f = _log_sigmoid(_dot(xkv, wf_ref[...]) + bf_ref[...])
        f_hi, f_mid = _split_bf16(log_f)
        f_lo = (log_f - f_hi.astype(F32) - f_mid.astype(F32)).astype(BF16)
        c = carry + (_dot(ltri, f_hi) + (_dot(ltri, f_mid) + _dot(ltri, f_lo)))
        carry = c[sub - 1:sub, :]
        nc = c * (-LOG2_E)
        n_hi, n_mid = _split_bf16(nc)
        n_lo = (nc - n_hi.astype(F32) - n_mid.astype(F32)).astype(BF16)
        ka_ref[rows, :] = _dot(jnp.concatenate([n_hi, n_mid, n_lo], axis=1), place_ref[...]).astype(BF16)
    carry_ref[0:1, :] = carry


ATTN_BIAS_LANES = 3
LOG2_E = 1.4426950408889634


def _bias_lane0(hh, dh):
    return dh if hh == 0 else 0


def _fox_in(moe, kv_gain, q_gain, kv_w, kv_b_f, w_in, seq, tm):
    ybuf, dest3, gate, h = moe
    t, d = h.shape
    width = w_in.shape[1] // 2
    heads = kv_w.shape[1] - 2 * width
    dh = width // heads
    place = np.zeros((ATTN_BIAS_LANES * heads, width), np.float32)
    for part in range(ATTN_BIAS_LANES):
        for hd in range(heads):
            place[part * heads + hd, (hd // 2) * LANES + _bias_lane0(hd % 2, dh) + part] = 1.0
    const = lambda i: (0, 0)
    row = lambda i: (i, 0)
    big = jax.ShapeDtypeStruct((t, width), BF16)
    nt, comb_specs, comb_scratch = _combine_specs(tm, d, dest3)
    return pl.pallas_call(
        functools.partial(_fox_in_kernel, tm=tm, sub=_tile(tm, 256), d=d, width=width, tiles_per_seq=seq // tm,
                          qscale=dh ** -0.5 * LOG2_E),
        grid=(nt,),
        in_specs=comb_specs + [
            pl.BlockSpec((1, d), const),
            pl.BlockSpec((1, d), const),
            pl.BlockSpec((d, 2 * width), const),
            pl.BlockSpec((d, heads), const),
            pl.BlockSpec((1, heads), const),
            pl.BlockSpec((d, 2 * width), const),
            pl.BlockSpec(place.shape, const),
        ],
        out_specs=[pl.BlockSpec((tm, d), row)] + [pl.BlockSpec((tm, width), row)] * 5,
        out_shape=[jax.ShapeDtypeStruct((t, d), F32)] + [big] * 5,
        scratch_shapes=comb_scratch + [pltpu.VMEM((SUBLANES, heads), F32)],
        compiler_params=_cparams(("arbitrary",)),
        name="fox_in",
    )(dest3, dest3, ybuf, h, gate, kv_gain.reshape(1, d), q_gain.reshape(1, d), kv_w[:, :2 * width].astype(BF16),
      kv_w[:, 2 * width:].astype(BF16), kv_b_f.reshape(1, heads), w_in.astype(BF16), jnp.asarray(place, BF16))


ATTN_KV_CHUNK = 512


def _attn_kernel(q_ref, k_ref, v_ref, og_ref, kb_ref, o_ref, qa_ref, ka_ref, va_ref, s_ref, *, seq, tq, dh, pairs):
    lane = lax.broadcasted_iota(jnp.int32, (seq, LANES), 1)
    heads_here = [(pr, hh) for pr in range(pairs) for hh in range(2)]
    for n, (pr, hh) in enumerate(heads_here):
        cols = slice(pr * LANES, (pr + 1) * LANES)
        a0 = _bias_lane0(hh, dh)
        own = (lane < dh) if hh == 0 else (lane >= dh)
        ones_q = jnp.where((lane >= a0) & (lane < a0 + ATTN_BIAS_LANES), 1.0, 0.0).astype(BF16)
        qa_ref[n] = jnp.where(own, q_ref[:, cols], ones_q)
        ka_ref[n] = jnp.where(own, k_ref[:, cols], kb_ref[:, cols])
        va_ref[n] = jnp.where(own, v_ref[:, cols], jnp.where(lane == a0, 1.0, 0.0).astype(BF16))

    ri = lax.broadcasted_iota(jnp.int32, (tq, tq), 0)
    ci = lax.broadcasted_iota(jnp.int32, (tq, tq), 1)
    causal = ri >= ci
    lane_q = lax.broadcasted_iota(jnp.int32, (tq, LANES), 1)

    def fold_max(mx, s):
        for g in range(s.shape[1] // LANES):
            mx = jnp.maximum(mx, s[:, g * LANES:(g + 1) * LANES])
        return mx

    for qi in range(seq // tq):
        rows = slice(qi * tq, (qi + 1) * tq)
        past = [(c0, min(c0 + ATTN_KV_CHUNK, qi * tq)) for c0 in range(0, qi * tq, ATTN_KV_CHUNK)]
        row_max = []
        for n in range(len(heads_here)):
            qa = qa_ref[n, rows, :]
            mx = jnp.full((tq, LANES), -jnp.inf, F32)
            for c0, c1 in past:
                s = _dot_nt(qa, ka_ref[n, c0:c1, :])
                s_ref[n, :, c0:c1] = s
                mx = fold_max(mx, s)
            s = jnp.where(causal, _dot_nt(qa, ka_ref[n, rows, :]), -jnp.inf)
            s_ref[n, :, rows] = s
            row_max.append(jnp.max(fold_max(mx, s), axis=1, keepdims=True))
        outs = []
        for n, (pr, hh) in enumerate(heads_here):
            acc = jnp.zeros((tq, LANES), F32)
            for c0, c1 in past + [(qi * tq, (qi + 1) * tq)]:
                p = jnp.exp2(s_ref[n, :, c0:c1] - row_max[n]).astype(BF16)
                acc = acc + _dot(p, va_ref[n, c0:c1, :])
            a0 = _bias_lane0(hh, dh)
            outs.append(acc / acc[:, a0:a0 + 1])
        for pr in range(pairs):
            cols = slice(pr * LANES, (pr + 1) * LANES)
            o = jnp.where(lane_q < dh, outs[2 * pr], outs[2 * pr + 1])
            o_ref[rows, cols] = (o * _sigmoid(og_ref[rows, cols].astype(F32))).astype(BF16)


ATTN_PAIRS = 2


def _attn(q, k, v, og, kb, heads, batch, seq, tq):
    t, width = q.shape
    dh = width // heads
    assert 2 * dh == LANES, "two heads per 128-lane block"
    pairs = ATTN_PAIRS
    blk = pl.BlockSpec((seq, pairs * LANES), lambda b, hp: (b, hp))
    return pl.pallas_call(
        functools.partial(_attn_kernel, seq=seq, tq=tq, dh=dh, pairs=pairs),
        grid=(batch, heads // (2 * pairs)),
        in_specs=[blk] * 5,
        out_specs=blk,
        out_shape=jax.ShapeDtypeStruct((t, width), BF16),
        scratch_shapes=[pltpu.VMEM((2 * pairs, seq, LANES), BF16)] * 3 + [pltpu.VMEM((2 * pairs, tq, seq), F32)],
        compiler_params=_cparams(("arbitrary", "arbitrary")),
        name="attn",
    )(q, k, v, og, kb)


def _tile(n, pref):
    while n % pref:
        pref //= 2
    return pref


def kernel(x, gla_norm, gla_w_in, gla_w_a2, gla_b_a, gla_out_norm, gla_w_out, kv_norm, kv_w, kv_b_f, fox_norm,
           fox_w_in, fox_w_out, ffn_norm, router_coarse_w, router_coarse_b, router_fine_w, router_fine_b,
           expert_w_gate, expert_w_up, expert_w_down, final_norm):
    batch, seq, d = x.shape
    t = batch * seq
    assert gla_norm.shape[0] == 1 and ffn_norm.shape[0] == 2, "trunk is GLA layer + FoX layer"
    tm = _tile(seq, 512)

    def ffn(a, w_out, h, layer):
        h, xr, eidx, gate, rank, counts = _post(a, w_out, h, ffn_norm[layer], router_coarse_w[layer],
                                                router_coarse_b[layer], router_fine_w[layer],
                                                router_fine_b[layer], _tile(seq, 1024), tm)
        return _moe_experts(xr, eidx, gate, rank, counts, h, expert_w_gate, expert_w_up, expert_w_down, layer, tm)

    h = x.reshape(t, d)
    q, k, v, r, la = _gla_in(h, gla_norm[0], gla_w_in[0], gla_w_a2[0], gla_b_a[0], _tile(seq, 1024))
    a = _gla_core(q, k, v, r, la, gla_out_norm[0], batch, seq, _tile(seq, 1024), 1)
    moe = ffn(a, gla_w_out[0], h, 0)
    h, kk, vv, qq, og, kb = _fox_in(moe, kv_norm, fox_norm[0], kv_w, kv_b_f, fox_w_in[0], seq, tm)
    a = _attn(qq, kk, vv, og, kb, kv_b_f.shape[0], batch, seq, _tile(seq, 256))
    moe = ffn(a, fox_w_out[0], h, 1)
    return _final(moe, final_norm, tm).reshape(batch, seq, d)
```

```python
import functools

import jax
import jax.numpy as jnp
import numpy as np
from jax import lax
from jax.experimental import pallas as pl
from jax.experimental.pallas import tpu as pltpu

F32 = jnp.float32
BF16 = jnp.bfloat16

RMS_EPS = 1e-6
GLA_HEADS = 4
GLA_GATE_RANK = 16
GLA_GATE_TAU = 16.0
MOE_GROUPS = 4
MOE_EXPERTS_PER_GROUP = 8
MOE_EXPERTS = MOE_GROUPS * MOE_EXPERTS_PER_GROUP
MOE_TOP_K = 2
MOE_BLOCK = 512

LANES = 128
SUBLANES = 8
VMEM_LIMIT = 56 * 1024 * 1024

GLA_CHUNK = 128
ROUTER_ROWS = 40


def _cparams(sem):
    return pltpu.CompilerParams(dimension_semantics=sem, vmem_limit_bytes=VMEM_LIMIT)


def _rms_unit(x):
    return x * lax.rsqrt(jnp.mean(x * x, axis=-1, keepdims=True) + RMS_EPS)


def _log_sigmoid(z):
    return jnp.minimum(z, 0.0) - jnp.log1p(jnp.exp(-jnp.abs(z)))


def _sigmoid(z):
    return 1.0 / (1.0 + jnp.exp(-z))


def _dot(a, b):
    return jnp.dot(a, b, preferred_element_type=F32)


def _dot_nt(a, b):
    return lax.dot_general(a, b, (((1,), (1,)), ((), ())), preferred_element_type=F32)


def _dot_tn(a, b):
    return lax.dot_general(a, b, (((0,), (0,)), ((), ())), preferred_element_type=F32)


def _split_bf16(x):
    hi = x.astype(BF16)
    lo = (x - hi.astype(F32)).astype(BF16)
    return hi, lo


def _gla_in_kernel(h_ref, g_ref, w_ref, wa1_ref, wa2_ref, ba_ref, q_ref, k_ref, v_ref, r_ref, la_ref, *, qk, vw):
    xn = (_rms_unit(h_ref[...]) * g_ref[...]).astype(BF16)
    q_ref[...] = _dot(xn, w_ref[:, 0:qk]).astype(BF16)
    k_ref[...] = _dot(xn, w_ref[:, qk:2 * qk]).astype(BF16)
    v_ref[...] = _dot(xn, w_ref[:, 2 * qk:2 * qk + vw]).astype(BF16)
    r_ref[...] = _dot(xn, w_ref[:, 2 * qk + vw:2 * qk + 2 * vw]).astype(BF16)
    a = _dot(xn, wa1_ref[...])
    z = _dot(a.astype(BF16), wa2_ref[...]) + ba_ref[...]
    la_ref[...] = _log_sigmoid(z) * (1.0 / GLA_GATE_TAU)


def _gla_in(h, gain, w_in, w_a2, b_a, tm):
    t, d = h.shape
    qk = w_a2.shape[1]
    vw = (w_in.shape[1] - 2 * qk - GLA_GATE_RANK) // 2
    w_main = w_in[:, :2 * qk + 2 * vw].astype(BF16)
    w_a1 = w_in[:, 2 * qk + 2 * vw:].astype(BF16)
    const = lambda i: (0, 0)
    row = lambda i: (i, 0)
    return pl.pallas_call(
        functools.partial(_gla_in_kernel, qk=qk, vw=vw),
        grid=(t // tm,),
        in_specs=[
            pl.BlockSpec((tm, d), row),
            pl.BlockSpec((1, d), const),
            pl.BlockSpec(w_main.shape, const),
            pl.BlockSpec(w_a1.shape, const),
            pl.BlockSpec(w_a2.shape, const),
            pl.BlockSpec((1, qk), const),
        ],
        out_specs=[
            pl.BlockSpec((tm, qk), row),
            pl.BlockSpec((tm, qk), row),
            pl.BlockSpec((tm, vw), row),
            pl.BlockSpec((tm, vw), row),
            pl.BlockSpec((tm, qk), row),
        ],
        out_shape=[
            jax.ShapeDtypeStruct((t, qk), BF16),
            jax.ShapeDtypeStruct((t, qk), BF16),
            jax.ShapeDtypeStruct((t, vw), BF16),
            jax.ShapeDtypeStruct((t, vw), BF16),
            jax.ShapeDtypeStruct((t, qk), F32),
        ],
        compiler_params=_cparams(("arbitrary",)),
        name="gla_in",
    )(h, gain.reshape(1, d), w_main, w_a1, w_a2.astype(BF16), b_a.reshape(1, qk))


def _gla_core_kernel(q_ref, k_ref, v_ref, r_ref, la_ref, gn_ref, o_ref, st_ref,
                     *, n_seq, n_chunks, heads, dk, dv, scale):
    c_len = GLA_CHUNK

    @pl.when(pl.program_id(1) == 0)
    def _():
        st_ref[...] = jnp.zeros_like(st_ref)

    ri = lax.broadcasted_iota(jnp.int32, (c_len, c_len), 0)
    ci = lax.broadcasted_iota(jnp.int32, (c_len, c_len), 1)
    causal = ri >= ci
    ltri = jnp.where(causal, 1.0, 0.0).astype(BF16)

    for c in range(n_chunks):
        sl = pl.ds(c * c_len, c_len)
        for sq in range(n_seq):
            for hd in range(heads):
                ks_ = slice(hd * dk, (hd + 1) * dk)
                vs_ = slice(hd * dv, (hd + 1) * dv)
                g_hi, g_lo = _split_bf16(la_ref[sq, sl, ks_])
                cum = _dot(ltri, g_hi) + _dot(ltri, g_lo)
                last = cum[c_len - 1:c_len, :]
                mid = cum[c_len // 2 - 1:c_len // 2, :]
                qf = q_ref[sq, sl, ks_].astype(F32) * scale
                kf = k_ref[sq, sl, ks_].astype(F32)
                vb = v_ref[sq, sl, vs_]
                qs = (qf * jnp.exp(cum - mid)).astype(BF16)
                ks = (kf * jnp.exp(mid - cum)).astype(BF16)
                scores = jnp.where(causal, _dot_nt(qs, ks), 0.0)
                o = _dot(scores.astype(BF16), vb)
                st = st_ref[sq, hd]
                qd = (qf * jnp.exp(cum)).astype(BF16)
                o = o + _dot_nt(qd, st.astype(BF16))
                kd = (kf * jnp.exp(last - cum)).astype(BF16)
                st_ref[sq, hd] = st * jnp.exp(last) + _dot_tn(vb, kd)
                on = _rms_unit(o) * gn_ref[hd:hd + 1, :]
                rr = r_ref[sq, sl, vs_].astype(F32)
                o_ref[sq, sl, vs_] = (on * (rr * _sigmoid(rr))).astype(BF16)


def _gla_core(q, k, v, r, la, out_norm, batch, seq, lc, n_seq):
    t, qk = q.shape
    vw = v.shape[1]
    heads = GLA_HEADS
    dk, dv = qk // heads, vw // heads
    tok = lambda b, j: (b, j, 0)
    seqs = lambda a: a.reshape(batch, seq, a.shape[1])
    out = pl.pallas_call(
        functools.partial(_gla_core_kernel, n_seq=n_seq, n_chunks=lc // GLA_CHUNK, heads=heads, dk=dk, dv=dv,
                          scale=dk ** -0.5),
        grid=(batch // n_seq, seq // lc),
        in_specs=[
            pl.BlockSpec((n_seq, lc, qk), tok),
            pl.BlockSpec((n_seq, lc, qk), tok),
            pl.BlockSpec((n_seq, lc, vw), tok),
            pl.BlockSpec((n_seq, lc, vw), tok),
            pl.BlockSpec((n_seq, lc, qk), tok),
            pl.BlockSpec((heads, dv), lambda b, j: (0, 0)),
        ],
        out_specs=pl.BlockSpec((n_seq, lc, vw), tok),
        out_shape=jax.ShapeDtypeStruct((batch, seq, vw), BF16),
        scratch_shapes=[pltpu.VMEM((n_seq, heads, dv, dk), F32)],
        compiler_params=_cparams(("arbitrary", "arbitrary")),
        name="gla_core",
    )(seqs(q), seqs(k), seqs(v), seqs(r), seqs(la), out_norm)
    return out.reshape(t, vw)


def _post_kernel(a_ref, w_ref, h_ref, g_ref, wr_ref, br_ref,
                 hn_ref, xr_ref, eidx_ref, gate_ref, rank_ref, cnt_ref, base_ref, *, tm, sub, d):
    @pl.when(pl.program_id(0) == 0)
    def _():
        base_ref[...] = jnp.zeros_like(base_ref)

    eg = MOE_EXPERTS_PER_GROUP
    ti = lax.broadcasted_iota(jnp.int32, (sub, sub), 0)
    tj = lax.broadcasted_iota(jnp.int32, (sub, sub), 1)
    upper = jnp.where(ti <= tj, 1.0, 0.0).astype(BF16)
    row_g = lax.broadcasted_iota(jnp.int32, (MOE_GROUPS, sub), 0)
    row_e = lax.broadcasted_iota(jnp.int32, (eg, sub), 0)
    row_x = lax.broadcasted_iota(jnp.int32, (MOE_EXPERTS, sub), 0)
    base = base_ref[:, 0:1]

    for u in range(tm // sub):
        rows = slice(u * sub, (u + 1) * sub)
        hn = h_ref[rows, :] + _dot(a_ref[rows, :], w_ref[...])
        hn_ref[rows, :] = hn
        xn = _rms_unit(hn) * g_ref[...]
        for c in range(d // LANES):
            xr_ref[pl.ds(u * sub * SUBLANES + c, sub, stride=SUBLANES), :] = xn[:, c * LANES:(c + 1) * LANES]

        x_hi, x_lo = _split_bf16(xn)
        r_hi = _dot(x_hi, wr_ref[...])
        r_lo = _dot(x_lo, wr_ref[...])
        lg_t = (r_hi[:, 0:LANES] + r_hi[:, LANES:]) + (r_lo[:, 0:LANES] + r_lo[:, LANES:])
        lg = lg_t.T[0:ROUTER_ROWS] + br_ref[...]
        lc = lg[0:MOE_GROUPS]
        m_c = jnp.max(lc, axis=0, keepdims=True)
        pc_top = 1.0 / jnp.sum(jnp.exp(lc - m_c), axis=0, keepdims=True)
        g_idx = jnp.min(jnp.where(lc == m_c, row_g, MOE_GROUPS), axis=0, keepdims=True)
        lf = jnp.zeros((eg, sub), F32)
        for g in range(MOE_GROUPS):
            lf = lf + jnp.where(g_idx == g, lg[SUBLANES + g * eg:SUBLANES + (g + 1) * eg], 0.0)
        ef = jnp.exp(lf - jnp.max(lf, axis=0, keepdims=True))
        pf = ef / jnp.sum(ef, axis=0, keepdims=True)
        v1 = jnp.max(pf, axis=0, keepdims=True)
        j1 = jnp.min(jnp.where(pf == v1, row_e, eg), axis=0, keepdims=True)
        pf2 = jnp.where(row_e == j1, -1.0, pf)
        v2 = jnp.max(pf2, axis=0, keepdims=True)
        j2 = jnp.min(jnp.where(pf2 == v2, row_e, eg), axis=0, keepdims=True)
        denom = v1 + v2
        e1 = g_idx * eg + j1
        e2 = g_idx * eg + j2
        eidx_ref[0:1, rows] = e1
        eidx_ref[1:2, rows] = e2
        gate_ref[0:1, rows] = pc_top * v1 / denom
        gate_ref[1:2, rows] = pc_top * v2 / denom

        oh1 = row_x == e1
        oh2 = row_x == e2
        pre1 = _dot(jnp.where(oh1, 1.0, 0.0).astype(BF16), upper)
        pre2 = _dot(jnp.where(oh2, 1.0, 0.0).astype(BF16), upper)
        cnt1 = pre1[:, sub - 1:sub]
        cnt2 = pre2[:, sub - 1:sub]
        rank1 = jnp.sum(jnp.where(oh1, base + pre1 - 1.0, 0.0), axis=0, keepdims=True)
        rank2 = jnp.sum(jnp.where(oh2, base + cnt1 + pre2 - 1.0, 0.0), axis=0, keepdims=True)
        rank_ref[0:1, rows] = rank1.astype(jnp.int32)
        rank_ref[1:2, rows] = rank2.astype(jnp.int32)
        base = base + (cnt1 + cnt2)

    new_base = jnp.broadcast_to(base, base_ref.shape)
    base_ref[...] = new_base
    cnt_ref[...] = new_base


def _post(a, w_out, h, gain, w_coarse, b_coarse, w_fine, b_fine, tm, sub):
    t, d = h.shape
    wr = jnp.concatenate([w_coarse, jnp.zeros((d, SUBLANES - MOE_GROUPS), F32), w_fine,
                          jnp.zeros((d, LANES - ROUTER_ROWS), F32)], axis=1)
    wr = jnp.concatenate(_split_bf16(wr), axis=1)
    br = jnp.zeros((ROUTER_ROWS, 1), F32)
    br = br.at[0:MOE_GROUPS, 0].set(b_coarse).at[SUBLANES:, 0].set(b_fine)
    const = lambda i: (0, 0)
    row = lambda i: (i, 0)
    col = lambda i: (0, i)
    return pl.pallas_call(
        functools.partial(_post_kernel, tm=tm, sub=sub, d=d),
        grid=(t // tm,),
        in_specs=[
            pl.BlockSpec((tm, a.shape[1]), row),
            pl.BlockSpec(w_out.shape, const),
            pl.BlockSpec((tm, d), row),
            pl.BlockSpec((1, d), const),
            pl.BlockSpec((d, 2 * LANES), const),
            pl.BlockSpec((ROUTER_ROWS, 1), const),
        ],
        out_specs=[
            pl.BlockSpec((tm, d), row),
            pl.BlockSpec((tm * SUBLANES, LANES), row),
            pl.BlockSpec((MOE_TOP_K, tm), col),
            pl.BlockSpec((MOE_TOP_K, tm), col),
            pl.BlockSpec((MOE_TOP_K, tm), col),
            pl.BlockSpec((MOE_EXPERTS, LANES), const),
        ],
        out_shape=[
            jax.ShapeDtypeStruct((t, d), F32),
            jax.ShapeDtypeStruct((t * SUBLANES, LANES), F32),
            jax.ShapeDtypeStruct((MOE_TOP_K, t), jnp.int32),
            jax.ShapeDtypeStruct((MOE_TOP_K, t), F32),
            jax.ShapeDtypeStruct((MOE_TOP_K, t), jnp.int32),
            jax.ShapeDtypeStruct((MOE_EXPERTS, LANES), F32),
        ],
        scratch_shapes=[pltpu.VMEM((MOE_EXPERTS, LANES), F32)],
        compiler_params=_cparams(("arbitrary",)),
        name="post",
    )(a, w_out.astype(BF16), h, gain.reshape(1, d), wr, br)


def _row_tile(ref, idx):
    return ref.at[pl.ds(pl.multiple_of(idx * SUBLANES, SUBLANES), SUBLANES), :]


def _block_rows(ref, blk):
    return ref.at[pl.ds(pl.multiple_of(blk * (MOE_BLOCK * SUBLANES), MOE_BLOCK * SUBLANES), MOE_BLOCK * SUBLANES), :]


def _dispatch_kernel(pe_ref, nu_ref, dest_ref, xr_ref, xbuf_ref, zero_ref, sem, zsem, *, tm, n_blocks):
    @pl.when(pl.program_id(0) == 0)
    def _():
        zero_ref[...] = jnp.zeros_like(zero_ref)

        def last_block(e):
            return pe_ref[e] // MOE_BLOCK - 1

        def has_rows(e):
            return pe_ref[e] > jnp.where(e == 0, 0, pe_ref[jnp.maximum(e - 1, 0)])

        def zero_block(blk):
            return pltpu.make_async_copy(zero_ref, _block_rows(xbuf_ref, blk), zsem)

        def start_e(e, carry):
            @pl.when(has_rows(e))
            def _():
                zero_block(last_block(e)).start()
            return carry

        def wait_e(e, carry):
            @pl.when(has_rows(e))
            def _():
                zero_block(last_block(e)).wait()
            return carry

        def start_b(blk, carry):
            zero_block(blk).start()
            return carry

        def wait_b(blk, carry):
            zero_block(blk).wait()
            return carry

        lax.fori_loop(0, MOE_EXPERTS, start_e, 0)
        lax.fori_loop(nu_ref[0], n_blocks, start_b, 0)
        lax.fori_loop(0, MOE_EXPERTS, wait_e, 0)
        lax.fori_loop(nu_ref[0], n_blocks, wait_b, 0)

    dtile = dest_ref.shape[2]

    def issue(tk, carry):
        for s in range(MOE_TOP_K):
            pltpu.make_async_copy(_row_tile(xr_ref, tk), _row_tile(xbuf_ref, dest_ref[tk // dtile, s, tk % dtile]),
                                  sem).start(priority=s)
        return carry

    for tk in range(tm):
        issue(tk, 0)
    for s in range(MOE_TOP_K):
        pltpu.make_async_copy(xr_ref, xbuf_ref.at[pl.ds(0, tm * SUBLANES), :], sem).wait()


def _dispatch(xr, dest3, pad_end, n_used, n_blocks, tm):
    tiles = tm // dest3.shape[2]
    nt = dest3.shape[0] // tiles
    grid_spec = pltpu.PrefetchScalarGridSpec(
        num_scalar_prefetch=2,
        grid=(nt,),
        in_specs=[
            pl.BlockSpec((tiles, MOE_TOP_K, dest3.shape[2]), lambda i, pe, nu: (i, 0, 0), memory_space=pltpu.SMEM),
            pl.BlockSpec((tm * SUBLANES, LANES), lambda i, pe, nu: (i, 0)),
        ],
        out_specs=pl.BlockSpec(memory_space=pl.ANY),
        scratch_shapes=[pltpu.VMEM((MOE_BLOCK * SUBLANES, LANES), F32), pltpu.SemaphoreType.DMA,
                        pltpu.SemaphoreType.DMA],
    )
    return pl.pallas_call(
        functools.partial(_dispatch_kernel, tm=tm, n_blocks=n_blocks),
        grid_spec=grid_spec,
        out_shape=jax.ShapeDtypeStruct((n_blocks * MOE_BLOCK * SUBLANES, LANES), F32),
        compiler_params=_cparams(("arbitrary",)),
        name="dispatch",
    )(pad_end, n_used, dest3, xr)


def _experts_kernel(be_ref, nu_ref, nxt_ref, par_ref, x_ref, wg_hbm, wu_hbm, wd_hbm, y_ref,
                    x2_ref, wgb_ref, wub_ref, wdb_ref, wgf_ref, wuf_ref, wdf_ref, sems, *, d, layer):
    rows = MOE_BLOCK
    b = pl.program_id(0)
    live = b < nu_ref[0]
    e = be_ref[b]
    new_expert = jnp.logical_or(b == 0, e != be_ref[jnp.maximum(b - 1, 0)])

    def weight_copies(ex, slot):
        return [pltpu.make_async_copy(src.at[layer, ex], dst.at[slot], sems.at[slot, j])
                for j, (src, dst) in enumerate(((wg_hbm, wgf_ref), (wu_hbm, wuf_ref), (wd_hbm, wdf_ref)))]

    @pl.when(jnp.logical_and(live, new_expert))
    def _():
        slot = par_ref[e]

        @pl.when(b == 0)
        def _():
            for cp in weight_copies(e, slot):
                cp.start()

        @pl.when(nxt_ref[e] >= 0)
        def _():
            for cp in weight_copies(nxt_ref[e], 1 - slot):
                cp.start()

        for cp in weight_copies(e, slot):
            cp.wait()
        wgb_ref[...] = wgf_ref[slot].astype(BF16)
        wub_ref[...] = wuf_ref[slot].astype(BF16)
        wdb_ref[...] = wdf_ref[slot].astype(BF16)

    @pl.when(live)
    def _():
        for c in range(d // LANES):
            x2_ref[:, c * LANES:(c + 1) * LANES] = x_ref[pl.ds(c, rows, stride=SUBLANES), :].astype(BF16)
        x2 = x2_ref[...]
        gp = _dot(x2, wgb_ref[...])
        up = _dot(x2, wub_ref[...])
        hid = (gp * _sigmoid(gp) * up).astype(BF16)
        y = _dot(hid, wdb_ref[...])
        for c in range(d // LANES):
            y_ref[pl.ds(c, rows, stride=SUBLANES), :] = y[:, c * LANES:(c + 1) * LANES]


def _experts(xbuf, block_e, n_used, padded, w_gate, w_up, w_down, layer):
    n_blocks = block_e.shape[0]
    d, dff = w_gate.shape[2], w_gate.shape[3]
    owns = padded > 0
    ids = jnp.arange(MOE_EXPERTS, dtype=jnp.int32)
    later = jnp.where(owns, ids, MOE_EXPERTS)
    nxt = lax.cummin(jnp.concatenate([later[1:], jnp.full((1,), MOE_EXPERTS, jnp.int32)]), reverse=True)
    nxt = jnp.where(nxt < MOE_EXPERTS, nxt, -1).astype(jnp.int32)
    par = ((jnp.cumsum(owns.astype(jnp.int32)) - owns.astype(jnp.int32)) % 2).astype(jnp.int32)
    xmap = lambda b, be, nu, nx, pa: (jnp.minimum(b, nu[0] - 1), 0)
    grid_spec = pltpu.PrefetchScalarGridSpec(
        num_scalar_prefetch=4,
        grid=(n_blocks,),
        in_specs=[
            pl.BlockSpec((MOE_BLOCK * SUBLANES, LANES), xmap),
            pl.BlockSpec(memory_space=pl.ANY),
            pl.BlockSpec(memory_space=pl.ANY),
            pl.BlockSpec(memory_space=pl.ANY),
        ],
        out_specs=pl.BlockSpec((MOE_BLOCK * SUBLANES, LANES), xmap),
        scratch_shapes=[pltpu.VMEM((MOE_BLOCK, d), BF16), pltpu.VMEM((d, dff), BF16), pltpu.VMEM((d, dff), BF16),
                        pltpu.VMEM((dff, d), BF16), pltpu.VMEM((2, d, dff), F32), pltpu.VMEM((2, d, dff), F32),
                        pltpu.VMEM((2, dff, d), F32), pltpu.SemaphoreType.DMA((2, 3))],
    )
    return pl.pallas_call(
        functools.partial(_experts_kernel, d=d, layer=layer),
        grid_spec=grid_spec,
        out_shape=jax.ShapeDtypeStruct(xbuf.shape, F32),
        input_output_aliases={4: 0},
        compiler_params=_cparams(("arbitrary",)),
        name="experts",
    )(block_e, n_used, nxt, par, xbuf, w_gate, w_up, w_down)


def _gather_combine(dcur_ref, dnxt_ref, ybuf_ref, h_ref, gate_ref, y_ref, sems, *, tm, d):
    i = pl.program_id(0)
    dtile = dcur_ref.shape[2]

    def start(dref, slot):
        def issue(tk, carry):
            for s in range(MOE_TOP_K):
                pltpu.make_async_copy(_row_tile(ybuf_ref, dref[tk // dtile, s, tk % dtile]),
                                      _row_tile(y_ref.at[slot, s], tk), sems.at[slot]).start(priority=s)
            return carry

        for tk in range(tm):
            issue(tk, 0)

    @pl.when(i == 0)
    def _():
        start(dcur_ref, 0)

    @pl.when(i + 1 < pl.num_programs(0))
    def _():
        start(dnxt_ref, (i + 1) % 2)

    slot = i % 2
    for s in range(MOE_TOP_K):
        pltpu.make_async_copy(ybuf_ref.at[pl.ds(0, tm * SUBLANES), :], y_ref.at[slot, s], sems.at[slot]).wait()
    gate_t = jnp.concatenate([gate_ref[...], jnp.zeros((SUBLANES - MOE_TOP_K, tm), F32)], axis=0).T
    g0 = gate_t[:, 0:1]
    g1 = gate_t[:, 1:2]
    cols = []
    for c in range(d // LANES):
        y0 = y_ref[slot, 0, pl.ds(c, tm, stride=SUBLANES), :]
        y1 = y_ref[slot, 1, pl.ds(c, tm, stride=SUBLANES), :]
        cols.append(h_ref[:, c * LANES:(c + 1) * LANES] + (y0 * g0 + y1 * g1))
    return jnp.concatenate(cols, axis=1)


def _combine_specs(tm, d, dest3):
    tiles = tm // dest3.shape[2]
    nt = dest3.shape[0] // tiles
    dblk = (tiles, MOE_TOP_K, dest3.shape[2])
    in_specs = [
        pl.BlockSpec(dblk, lambda i: (i, 0, 0), memory_space=pltpu.SMEM),
        pl.BlockSpec(dblk, lambda i: (jnp.minimum(i + 1, nt - 1), 0, 0), memory_space=pltpu.SMEM),
        pl.BlockSpec(memory_space=pl.ANY),
        pl.BlockSpec((tm, d), lambda i: (i, 0)),
        pl.BlockSpec((MOE_TOP_K, tm), lambda i: (0, i)),
    ]
    scratch = [pltpu.VMEM((2, MOE_TOP_K, tm * SUBLANES, LANES), F32), pltpu.SemaphoreType.DMA((2,))]
    return nt, in_specs, scratch


def _final_kernel(dcur_ref, dnxt_ref, ybuf_ref, h_ref, gate_ref, fg_ref, o_ref, y_ref, sems, *, tm, d):
    out = _gather_combine(dcur_ref, dnxt_ref, ybuf_ref, h_ref, gate_ref, y_ref, sems, tm=tm, d=d)
    o_ref[...] = _rms_unit(out) * fg_ref[...]


def _final(moe, final_gain, tm):
    ybuf, dest3, gate, h = moe
    t, d = h.shape
    nt, in_specs, scratch = _combine_specs(tm, d, dest3)
    return pl.pallas_call(
        functools.partial(_final_kernel, tm=tm, d=d),
        grid=(nt,),
        in_specs=in_specs + [pl.BlockSpec((1, d), lambda i: (0, 0))],
        out_specs=pl.BlockSpec((tm, d), lambda i: (i, 0)),
        out_shape=jax.ShapeDtypeStruct((t, d), F32),
        scratch_shapes=scratch,
        compiler_params=_cparams(("arbitrary",)),
        name="final",
    )(dest3, dest3, ybuf, h, gate, final_gain.reshape(1, d))


def _dest_kernel(ps_ref, eidx_ref, rank_ref, dest_ref, *, tm):
    eidx = eidx_ref[...]
    dest = rank_ref[...]
    for e in range(MOE_EXPERTS):
        dest = dest + jnp.where(eidx == e, ps_ref[e], 0)
    for i in range(dest_ref.shape[0]):
        dest_ref[i] = dest[:, i * tm:(i + 1) * tm]


def _dest(pad_start, eidx, rank, tm):
    t = eidx.shape[1]
    whole = lambda i, ps: (0, 0)
    grid_spec = pltpu.PrefetchScalarGridSpec(
        num_scalar_prefetch=1,
        grid=(1,),
        in_specs=[pl.BlockSpec((MOE_TOP_K, t), whole), pl.BlockSpec((MOE_TOP_K, t), whole)],
        out_specs=pl.BlockSpec((t // tm, MOE_TOP_K, tm), lambda i, ps: (0, 0, 0)),
    )
    return pl.pallas_call(
        functools.partial(_dest_kernel, tm=tm),
        grid_spec=grid_spec,
        out_shape=jax.ShapeDtypeStruct((t // tm, MOE_TOP_K, tm), jnp.int32),
        compiler_params=_cparams(("arbitrary",)),
        name="dest",
    )(pad_start, eidx, rank)


def _moe_experts(xr, eidx, gate, rank, counts, h, w_gate, w_up, w_down, layer, tm_row):
    t = h.shape[0]
    n_slot = t * MOE_TOP_K
    n_blocks = -(-n_slot // MOE_BLOCK) + MOE_EXPERTS
    cnt = counts[:, 0].astype(jnp.int32)
    padded = (cnt + MOE_BLOCK - 1) // MOE_BLOCK * MOE_BLOCK
    pad_end = jnp.cumsum(padded)
    pad_start = pad_end - padded
    block_start = jnp.arange(n_blocks, dtype=jnp.int32) * MOE_BLOCK
    block_e = jnp.minimum(jnp.sum((pad_end[None, :] <= block_start[:, None]).astype(jnp.int32), axis=1),
                          MOE_EXPERTS - 1)
    n_used = (pad_end[-1:] // MOE_BLOCK).astype(jnp.int32)
    dest3 = _dest(pad_start.astype(jnp.int32), eidx, rank, tm_row)
    tm_dispatch = WIDE_TILE if (WIDE_TILE % tm_row == 0 and t % WIDE_TILE == 0) else tm_row
    xbuf = _dispatch(xr, dest3, pad_end.astype(jnp.int32), n_used, n_blocks, tm_dispatch)
    ybuf = _experts(xbuf, block_e, n_used, padded, w_gate, w_up, w_down, layer)
    return ybuf, dest3, gate, h


def _fox_in_kernel(dcur_ref, dnxt_ref, ybuf_ref, h_ref, gate_ref, gkv_ref, gq_ref, wkv_ref, wf_ref, bf_ref, wq_ref,
                   place_ref, hn_ref, k_ref, v_ref, q_ref, og_ref, ka_ref, y_ref, sems, carry_ref,
                   *, tm, sub, d, width, tiles_per_seq, qscale):
    i = pl.program_id(0)

    @pl.when(i % tiles_per_seq == 0)
    def _():
        carry_ref[...] = jnp.zeros_like(carry_ref)

    hn_ref[...] = _gather_combine(dcur_ref, dnxt_ref, ybuf_ref, h_ref, gate_ref, y_ref, sems, tm=tm, d=d)
    ri = lax.broadcasted_iota(jnp.int32, (sub, sub), 0)
    ci = lax.broadcasted_iota(jnp.int32, (sub, sub), 1)
    ltri = jnp.where(ri >= ci, 1.0, 0.0).astype(BF16)
    carry = carry_ref[0:1, :]
    for u in range(tm // sub):
        rows = slice(u * sub, (u + 1) * sub)
        y = _rms_unit(hn_ref[rows, :])
        xkv = (y * gkv_ref[...]).astype(BF16)
        xq = (y * gq_ref[...]).astype(BF16)
        k_ref[rows, :] = _dot(xkv, wkv_ref[:, 0:width]).astype(BF16)
        v_ref[rows, :] = _dot(xkv, wkv_ref[:, width:2 * width]).astype(BF16)
        q_ref[rows, :] = (_dot(xq, wq_ref[:, 0:width]) * qscale).astype(BF16)
        og_ref[rows, :] = _dot(xq, wq_ref[:, width:2 * width]).astype(BF16)

        log_f = _log_sigmoid(_dot(xkv, wf_ref[...]) + bf_ref[...])
        f_hi, f_mid = _split_bf16(log_f)
        f_lo = (log_f - f_hi.astype(F32) - f_mid.astype(F32)).astype(BF16)
        c = carry + (_dot(ltri, f_hi) + (_dot(ltri, f_mid) + _dot(ltri, f_lo)))
        carry = c[sub - 1:sub, :]
        nc = c * (-LOG2_E)
        n_hi, n_mid = _split_bf16(nc)
        n_lo = (nc - n_hi.astype(F32) - n_mid.astype(F32)).astype(BF16)
        ka_ref[rows, :] = _dot(jnp.concatenate([n_hi, n_mid, n_lo], axis=1), place_ref[...]).astype(BF16)
    carry_ref[0:1, :] = carry


ATTN_BIAS_LANES = 3
LOG2_E = 1.4426950408889634


def _bias_lane0(hh, dh):
    return dh if hh == 0 else 0


def _fox_in(moe, kv_gain, q_gain, kv_w, kv_b_f, w_in, seq, tm):
    ybuf, dest3, gate, h = moe
    t, d = h.shape
    width = w_in.shape[1] // 2
    heads = kv_w.shape[1] - 2 * width
    dh = width // heads
    place = np.zeros((ATTN_BIAS_LANES * heads, width), np.float32)
    for part in range(ATTN_BIAS_LANES):
        for hd in range(heads):
            place[part * heads + hd, (hd // 2) * LANES + _bias_lane0(hd % 2, dh) + part] = 1.0
    const = lambda i: (0, 0)
    row = lambda i: (i, 0)
    big = jax.ShapeDtypeStruct((t, width), BF16)
    nt, comb_specs, comb_scratch = _combine_specs(tm, d, dest3)
    return pl.pallas_call(
        functools.partial(_fox_in_kernel, tm=tm, sub=_tile(tm, SUB_TILE), d=d, width=width, tiles_per_seq=seq // tm,
                          qscale=dh ** -0.5 * LOG2_E),
        grid=(nt,),
        in_specs=comb_specs + [
            pl.BlockSpec((1, d), const),
            pl.BlockSpec((1, d), const),
            pl.BlockSpec((d, 2 * width), const),
            pl.BlockSpec((d, heads), const),
            pl.BlockSpec((1, heads), const),
            pl.BlockSpec((d, 2 * width), const),
            pl.BlockSpec(place.shape, const),
        ],
        out_specs=[pl.BlockSpec((tm, d), row)] + [pl.BlockSpec((tm, width), row)] * 5,
        out_shape=[jax.ShapeDtypeStruct((t, d), F32)] + [big] * 5,
        scratch_shapes=comb_scratch + [pltpu.VMEM((SUBLANES, heads), F32)],
        compiler_params=_cparams(("arbitrary",)),
        name="fox_in",
    )(dest3, dest3, ybuf, h, gate, kv_gain.reshape(1, d), q_gain.reshape(1, d), kv_w[:, :2 * width].astype(BF16),
      kv_w[:, 2 * width:].astype(BF16), kv_b_f.reshape(1, heads), w_in.astype(BF16), jnp.asarray(place, BF16))


ATTN_KV_CHUNK = 512


def _attn_kernel(q_ref, k_ref, v_ref, og_ref, kb_ref, o_ref, qa_ref, ka_ref, va_ref, s_ref, *, seq, tq, dh, pairs):
    lane = lax.broadcasted_iota(jnp.int32, (seq, LANES), 1)
    heads_here = [(pr, hh) for pr in range(pairs) for hh in range(2)]
    for n, (pr, hh) in enumerate(heads_here):
        cols = slice(pr * LANES, (pr + 1) * LANES)
        a0 = _bias_lane0(hh, dh)
        own = (lane < dh) if hh == 0 else (lane >= dh)
        ones_q = jnp.where((lane >= a0) & (lane < a0 + ATTN_BIAS_LANES), 1.0, 0.0).astype(BF16)
        qa_ref[n] = jnp.where(own, q_ref[:, cols], ones_q)
        ka_ref[n] = jnp.where(own, k_ref[:, cols], kb_ref[:, cols])
        va_ref[n] = jnp.where(own, v_ref[:, cols], jnp.where(lane == a0, 1.0, 0.0).astype(BF16))

    ri = lax.broadcasted_iota(jnp.int32, (tq, tq), 0)
    ci = lax.broadcasted_iota(jnp.int32, (tq, tq), 1)
    causal = ri >= ci
    lane_q = lax.broadcasted_iota(jnp.int32, (tq, LANES), 1)

    def fold_max(mx, s):
        for g in range(s.shape[1] // LANES):
            mx = jnp.maximum(mx, s[:, g * LANES:(g + 1) * LANES])
        return mx

    for qi in range(seq // tq):
        rows = slice(qi * tq, (qi + 1) * tq)
        past = [(c0, min(c0 + ATTN_KV_CHUNK, qi * tq)) for c0 in range(0, qi * tq, ATTN_KV_CHUNK)]
        row_max = []
        for n in range(len(heads_here)):
            qa = qa_ref[n, rows, :]
            mx = jnp.full((tq, LANES), -jnp.inf, F32)
            for c0, c1 in past:
                s = _dot_nt(qa, ka_ref[n, c0:c1, :])
                s_ref[n, :, c0:c1] = s
                mx = fold_max(mx, s)
            s = jnp.where(causal, _dot_nt(qa, ka_ref[n, rows, :]), -jnp.inf)
            s_ref[n, :, rows] = s
            row_max.append(jnp.max(fold_max(mx, s), axis=1, keepdims=True))
        outs = []
        for n, (pr, hh) in enumerate(heads_here):
            acc = jnp.zeros((tq, LANES), F32)
            for c0, c1 in past + [(qi * tq, (qi + 1) * tq)]:
                p = jnp.exp2(s_ref[n, :, c0:c1] - row_max[n]).astype(BF16)
                acc = acc + _dot(p, va_ref[n, c0:c1, :])
            a0 = _bias_lane0(hh, dh)
            outs.append(acc / acc[:, a0:a0 + 1])
        for pr in range(pairs):
            cols = slice(pr * LANES, (pr + 1) * LANES)
            o = jnp.where(lane_q < dh, outs[2 * pr], outs[2 * pr + 1])
            o_ref[rows, cols] = (o * _sigmoid(og_ref[rows, cols].astype(F32))).astype(BF16)


ATTN_PAIRS = 2


def _attn(q, k, v, og, kb, heads, batch, seq, tq):
    t, width = q.shape
    dh = width // heads
    assert 2 * dh == LANES, "two heads per 128-lane block"
    pairs = ATTN_PAIRS
    blk = pl.BlockSpec((seq, pairs * LANES), lambda b, hp: (b, hp))
    return pl.pallas_call(
        functools.partial(_attn_kernel, seq=seq, tq=tq, dh=dh, pairs=pairs),
        grid=(batch, heads // (2 * pairs)),
        in_specs=[blk] * 5,
        out_specs=blk,
        out_shape=jax.ShapeDtypeStruct((t, width), BF16),
        scratch_shapes=[pltpu.VMEM((2 * pairs, seq, LANES), BF16)] * 3 + [pltpu.VMEM((2 * pairs, tq, seq), F32)],
        compiler_params=_cparams(("arbitrary", "arbitrary")),
        name="attn",
    )(q, k, v, og, kb)


ROW_TILE = 512
WIDE_TILE = 1024
SUB_TILE = 256
ATTN_Q_BLOCK = 256


def _tile(n, pref):
    while n % pref:
        pref //= 2
    return pref


def kernel(x, gla_norm, gla_w_in, gla_w_a2, gla_b_a, gla_out_norm, gla_w_out, kv_norm, kv_w, kv_b_f, fox_norm,
           fox_w_in, fox_w_out, ffn_norm, router_coarse_w, router_coarse_b, router_fine_w, router_fine_b,
           expert_w_gate, expert_w_up, expert_w_down, final_norm):
    batch, seq, d = x.shape
    t = batch * seq
    assert gla_norm.shape[0] == 1 and ffn_norm.shape[0] == 2, "trunk is GLA layer + FoX layer"
    tm = _tile(seq, ROW_TILE)
    wide = _tile(seq, WIDE_TILE)

    def ffn(a, w_out, h, layer):
        h, xr, eidx, gate, rank, counts = _post(a, w_out, h, ffn_norm[layer], router_coarse_w[layer],
                                                router_coarse_b[layer], router_fine_w[layer],
                                                router_fine_b[layer], wide, tm)
        return _moe_experts(xr, eidx, gate, rank, counts, h, expert_w_gate, expert_w_up, expert_w_down, layer, tm)

    h = x.reshape(t, d)
    q, k, v, r, la = _gla_in(h, gla_norm[0], gla_w_in[0], gla_w_a2[0], gla_b_a[0], wide)
    a = _gla_core(q, k, v, r, la, gla_out_norm[0], batch, seq, wide, 1)
    moe = ffn(a, gla_w_out[0], h, 0)
    h, kk, vv, qq, og, kb = _fox_in(moe, kv_norm, fox_norm[0], kv_w, kv_b_f, fox_w_in[0], seq, tm)
    a = _attn(qq, kk, vv, og, kb, kv_b_f.shape[0], batch, seq, _tile(seq, ATTN_Q_BLOCK))
    moe = ffn(a, fox_w_out[0], h, 1)
    return _final(moe, final_norm, tm).reshape(batch, seq, d)
```

```python
import functools

import jax
import jax.numpy as jnp
import numpy as np
from jax import lax
from jax.experimental import pallas as pl
from jax.experimental.pallas import tpu as pltpu

F32 = jnp.float32
BF16 = jnp.bfloat16

RMS_EPS = 1e-6
GLA_HEADS = 4
GLA_GATE_RANK = 16
GLA_GATE_TAU = 16.0
MOE_GROUPS = 4
MOE_EXPERTS_PER_GROUP = 8
MOE_EXPERTS = MOE_GROUPS * MOE_EXPERTS_PER_GROUP
MOE_TOP_K = 2
MOE_BLOCK = 512

LANES = 128
SUBLANES = 8
VMEM_LIMIT = 56 * 1024 * 1024

GLA_CHUNK = 128
ROUTER_ROWS = 40


def _cparams(sem):
    return pltpu.CompilerParams(dimension_semantics=sem, vmem_limit_bytes=VMEM_LIMIT)


def _rms_unit(x):
    return x * lax.rsqrt(jnp.mean(x * x, axis=-1, keepdims=True) + RMS_EPS)


def _log_sigmoid(z):
    return jnp.minimum(z, 0.0) - jnp.log1p(jnp.exp(-jnp.abs(z)))


def _sigmoid(z):
    return 1.0 / (1.0 + jnp.exp(-z))


def _dot(a, b):
    return jnp.dot(a, b, preferred_element_type=F32)


def _dot_nt(a, b):
    return lax.dot_general(a, b, (((1,), (1,)), ((), ())), preferred_element_type=F32)


def _dot_tn(a, b):
    return lax.dot_general(a, b, (((0,), (0,)), ((), ())), preferred_element_type=F32)


def _split_bf16(x):
    hi = x.astype(BF16)
    lo = (x - hi.astype(F32)).astype(BF16)
    return hi, lo


def _gla_in_kernel(h_ref, g_ref, w_ref, wa1_ref, wa2_ref, ba_ref, q_ref, k_ref, v_ref, r_ref, la_ref, *, qk, vw):
    xn = (_rms_unit(h_ref[...]) * g_ref[...]).astype(BF16)
    q_ref[...] = _dot(xn, w_ref[:, 0:qk]).astype(BF16)
    k_ref[...] = _dot(xn, w_ref[:, qk:2 * qk]).astype(BF16)
    v_ref[...] = _dot(xn, w_ref[:, 2 * qk:2 * qk + vw]).astype(BF16)
    r_ref[...] = _dot(xn, w_ref[:, 2 * qk + vw:2 * qk + 2 * vw]).astype(BF16)
    a = _dot(xn, wa1_ref[...])
    z = _dot(a.astype(BF16), wa2_ref[...]) + ba_ref[...]
    la_ref[...] = _log_sigmoid(z) * (1.0 / GLA_GATE_TAU)


def _gla_in(h, gain, w_in, w_a2, b_a, tm):
    t, d = h.shape
    qk = w_a2.shape[1]
    vw = (w_in.shape[1] - 2 * qk - GLA_GATE_RANK) // 2
    w_main = w_in[:, :2 * qk + 2 * vw].astype(BF16)
    w_a1 = w_in[:, 2 * qk + 2 * vw:].astype(BF16)
    const = lambda i: (0, 0)
    row = lambda i: (i, 0)
    return pl.pallas_call(
        functools.partial(_gla_in_kernel, qk=qk, vw=vw),
        grid=(t // tm,),
        in_specs=[
            pl.BlockSpec((tm, d), row),
            pl.BlockSpec((1, d), const),
            pl.BlockSpec(w_main.shape, const, pipeline_mode=pl.Buffered(1)),
            pl.BlockSpec(w_a1.shape, const),
            pl.BlockSpec(w_a2.shape, const),
            pl.BlockSpec((1, qk), const),
        ],
        out_specs=[
            pl.BlockSpec((tm, qk), row),
            pl.BlockSpec((tm, qk), row),
            pl.BlockSpec((tm, vw), row),
            pl.BlockSpec((tm, vw), row),
            pl.BlockSpec((tm, qk), row),
        ],
        out_shape=[
            jax.ShapeDtypeStruct((t, qk), BF16),
            jax.ShapeDtypeStruct((t, qk), BF16),
            jax.ShapeDtypeStruct((t, vw), BF16),
            jax.ShapeDtypeStruct((t, vw), BF16),
            jax.ShapeDtypeStruct((t, qk), F32),
        ],
        compiler_params=_cparams(("arbitrary",)),
        name="gla_in",
    )(h, gain.reshape(1, d), w_main, w_a1, w_a2.astype(BF16), b_a.reshape(1, qk))


def _gla_core_kernel(q_ref, k_ref, v_ref, r_ref, la_ref, gn_ref, o_ref, st_ref,
                     *, n_seq, n_chunks, heads, dk, dv, scale):
    c_len = GLA_CHUNK

    @pl.when(pl.program_id(1) == 0)
    def _():
        st_ref[...] = jnp.zeros_like(st_ref)

    ri = lax.broadcasted_iota(jnp.int32, (c_len, c_len), 0)
    ci = lax.broadcasted_iota(jnp.int32, (c_len, c_len), 1)
    causal = ri >= ci
    ltri = jnp.where(causal, 1.0, 0.0).astype(BF16)

    for c in range(n_chunks):
        sl = pl.ds(c * c_len, c_len)
        for sq in range(n_seq):
            for hd in range(heads):
                ks_ = slice(hd * dk, (hd + 1) * dk)
                vs_ = slice(hd * dv, (hd + 1) * dv)
                g_hi, g_lo = _split_bf16(la_ref[sq, sl, ks_])
                cum = _dot(ltri, g_hi) + _dot(ltri, g_lo)
                last = cum[c_len - 1:c_len, :]
                mid = cum[c_len // 2 - 1:c_len // 2, :]
                qf = q_ref[sq, sl, ks_].astype(F32) * scale
                kf = k_ref[sq, sl, ks_].astype(F32)
                vb = v_ref[sq, sl, vs_]
                qs = (qf * jnp.exp(cum - mid)).astype(BF16)
                ks = (kf * jnp.exp(mid - cum)).astype(BF16)
                scores = jnp.where(causal, _dot_nt(qs, ks), 0.0)
                o = _dot(scores.astype(BF16), vb)
                st = st_ref[sq, hd]
                qd = (qf * jnp.exp(cum)).astype(BF16)
                o = o + _dot_nt(qd, st.astype(BF16))
                kd = (kf * jnp.exp(last - cum)).astype(BF16)
                st_ref[sq, hd] = st * jnp.exp(last) + _dot_tn(vb, kd)
                on = _rms_unit(o) * gn_ref[hd:hd + 1, :]
                rr = r_ref[sq, sl, vs_].astype(F32)
                o_ref[sq, sl, vs_] = (on * (rr * _sigmoid(rr))).astype(BF16)


def _gla_core(q, k, v, r, la, out_norm, batch, seq, lc, n_seq):
    t, qk = q.shape
    vw = v.shape[1]
    heads = GLA_HEADS
    dk, dv = qk // heads, vw // heads
    tok = lambda b, j: (b, j, 0)
    seqs = lambda a: a.reshape(batch, seq, a.shape[1])
    out = pl.pallas_call(
        functools.partial(_gla_core_kernel, n_seq=n_seq, n_chunks=lc // GLA_CHUNK, heads=heads, dk=dk, dv=dv,
                          scale=dk ** -0.5),
        grid=(batch // n_seq, seq // lc),
        in_specs=[
            pl.BlockSpec((n_seq, lc, qk), tok),
            pl.BlockSpec((n_seq, lc, qk), tok),
            pl.BlockSpec((n_seq, lc, vw), tok),
            pl.BlockSpec((n_seq, lc, vw), tok),
            pl.BlockSpec((n_seq, lc, qk), tok),
            pl.BlockSpec((heads, dv), lambda b, j: (0, 0)),
        ],
        out_specs=pl.BlockSpec((n_seq, lc, vw), tok),
        out_shape=jax.ShapeDtypeStruct((batch, seq, vw), BF16),
        scratch_shapes=[pltpu.VMEM((n_seq, heads, dv, dk), F32)],
        compiler_params=_cparams(("arbitrary", "arbitrary")),
        name="gla_core",
    )(seqs(q), seqs(k), seqs(v), seqs(r), seqs(la), out_norm)
    return out.reshape(t, vw)


def _post_kernel(a_ref, w_ref, h_ref, g_ref, wr_ref, br_ref,
                 hn_ref, xr_ref, eidx_ref, gate_ref, rank_ref, cnt_ref, base_ref, *, tm, sub, d):
    @pl.when(pl.program_id(0) == 0)
    def _():
        base_ref[...] = jnp.zeros_like(base_ref)

    eg = MOE_EXPERTS_PER_GROUP
    ti = lax.broadcasted_iota(jnp.int32, (sub, sub), 0)
    tj = lax.broadcasted_iota(jnp.int32, (sub, sub), 1)
    upper = jnp.where(ti <= tj, 1.0, 0.0).astype(BF16)
    row_g = lax.broadcasted_iota(jnp.int32, (MOE_GROUPS, sub), 0)
    row_e = lax.broadcasted_iota(jnp.int32, (eg, sub), 0)
    row_x = lax.broadcasted_iota(jnp.int32, (MOE_EXPERTS, sub), 0)
    base = base_ref[:, 0:1]

    for u in range(tm // sub):
        rows = slice(u * sub, (u + 1) * sub)
        hn = h_ref[rows, :] + _dot(a_ref[rows, :], w_ref[...])
        hn_ref[rows, :] = hn
        xn = _rms_unit(hn) * g_ref[...]
        for c in range(d // LANES):
            xr_ref[pl.ds(u * sub * SUBLANES + c, sub, stride=SUBLANES), :] = xn[:, c * LANES:(c + 1) * LANES]

        x_hi, x_lo = _split_bf16(xn)
        r_hi = _dot(x_hi, wr_ref[...])
        r_lo = _dot(x_lo, wr_ref[...])
        lg_t = (r_hi[:, 0:LANES] + r_hi[:, LANES:]) + (r_lo[:, 0:LANES] + r_lo[:, LANES:])
        lg = lg_t.T[0:ROUTER_ROWS] + br_ref[...]
        lc = lg[0:MOE_GROUPS]
        m_c = jnp.max(lc, axis=0, keepdims=True)
        pc_top = 1.0 / jnp.sum(jnp.exp(lc - m_c), axis=0, keepdims=True)
        g_idx = jnp.min(jnp.where(lc == m_c, row_g, MOE_GROUPS), axis=0, keepdims=True)
        lf = jnp.zeros((eg, sub), F32)
        for g in range(MOE_GROUPS):
            lf = lf + jnp.where(g_idx == g, lg[SUBLANES + g * eg:SUBLANES + (g + 1) * eg], 0.0)
        ef = jnp.exp(lf - jnp.max(lf, axis=0, keepdims=True))
        pf = ef / jnp.sum(ef, axis=0, keepdims=True)
        v1 = jnp.max(pf, axis=0, keepdims=True)
        j1 = jnp.min(jnp.where(pf == v1, row_e, eg), axis=0, keepdims=True)
        pf2 = jnp.where(row_e == j1, -1.0, pf)
        v2 = jnp.max(pf2, axis=0, keepdims=True)
        j2 = jnp.min(jnp.where(pf2 == v2, row_e, eg), axis=0, keepdims=True)
        denom = v1 + v2
        e1 = g_idx * eg + j1
        e2 = g_idx * eg + j2
        eidx_ref[0:1, rows] = e1
        eidx_ref[1:2, rows] = e2
        gate_ref[0:1, rows] = pc_top * v1 / denom
        gate_ref[1:2, rows] = pc_top * v2 / denom

        oh1 = row_x == e1
        oh2 = row_x == e2
        pre1 = _dot(jnp.where(oh1, 1.0, 0.0).astype(BF16), upper)
        pre2 = _dot(jnp.where(oh2, 1.0, 0.0).astype(BF16), upper)
        cnt1 = pre1[:, sub - 1:sub]
        cnt2 = pre2[:, sub - 1:sub]
        rank1 = jnp.sum(jnp.where(oh1, base + pre1 - 1.0, 0.0), axis=0, keepdims=True)
        rank2 = jnp.sum(jnp.where(oh2, base + cnt1 + pre2 - 1.0, 0.0), axis=0, keepdims=True)
        rank_ref[0:1, rows] = rank1.astype(jnp.int32)
        rank_ref[1:2, rows] = rank2.astype(jnp.int32)
        base = base + (cnt1 + cnt2)

    new_base = jnp.broadcast_to(base, base_ref.shape)
    base_ref[...] = new_base
    cnt_ref[...] = new_base


def _post(a, w_out, h, gain, w_coarse, b_coarse, w_fine, b_fine, tm, sub):
    t, d = h.shape
    wr = jnp.concatenate([w_coarse, jnp.zeros((d, SUBLANES - MOE_GROUPS), F32), w_fine,
                          jnp.zeros((d, LANES - ROUTER_ROWS), F32)], axis=1)
    wr = jnp.concatenate(_split_bf16(wr), axis=1)
    br = jnp.zeros((ROUTER_ROWS, 1), F32)
    br = br.at[0:MOE_GROUPS, 0].set(b_coarse).at[SUBLANES:, 0].set(b_fine)
    const = lambda i: (0, 0)
    row = lambda i: (i, 0)
    col = lambda i: (0, i)
    return pl.pallas_call(
        functools.partial(_post_kernel, tm=tm, sub=sub, d=d),
        grid=(t // tm,),
        in_specs=[
            pl.BlockSpec((tm, a.shape[1]), row),
            pl.BlockSpec(w_out.shape, const, pipeline_mode=pl.Buffered(1)),
            pl.BlockSpec((tm, d), row),
            pl.BlockSpec((1, d), const),
            pl.BlockSpec((d, 2 * LANES), const),
            pl.BlockSpec((ROUTER_ROWS, 1), const),
        ],
        out_specs=[
            pl.BlockSpec((tm, d), row),
            pl.BlockSpec((tm * SUBLANES, LANES), row),
            pl.BlockSpec((MOE_TOP_K, tm), col),
            pl.BlockSpec((MOE_TOP_K, tm), col),
            pl.BlockSpec((MOE_TOP_K, tm), col),
            pl.BlockSpec((MOE_EXPERTS, LANES), const),
        ],
        out_shape=[
            jax.ShapeDtypeStruct((t, d), F32),
            jax.ShapeDtypeStruct((t * SUBLANES, LANES), F32),
            jax.ShapeDtypeStruct((MOE_TOP_K, t), jnp.int32),
            jax.ShapeDtypeStruct((MOE_TOP_K, t), F32),
            jax.ShapeDtypeStruct((MOE_TOP_K, t), jnp.int32),
            jax.ShapeDtypeStruct((MOE_EXPERTS, LANES), F32),
        ],
        scratch_shapes=[pltpu.VMEM((MOE_EXPERTS, LANES), F32)],
        compiler_params=_cparams(("arbitrary",)),
        name="post",
    )(a, w_out.astype(BF16), h, gain.reshape(1, d), wr, br)


def _row_tile(ref, idx):
    return ref.at[pl.ds(pl.multiple_of(idx * SUBLANES, SUBLANES), SUBLANES), :]


def _block_rows(ref, blk):
    return ref.at[pl.ds(pl.multiple_of(blk * (MOE_BLOCK * SUBLANES), MOE_BLOCK * SUBLANES), MOE_BLOCK * SUBLANES), :]


def _dispatch_kernel(pe_ref, nu_ref, dest_ref, xr_ref, xbuf_ref, zero_ref, sem, zsem, *, tm, n_blocks):
    @pl.when(pl.program_id(0) == 0)
    def _():
        zero_ref[...] = jnp.zeros_like(zero_ref)

        def last_block(e):
            return pe_ref[e] // MOE_BLOCK - 1

        def has_rows(e):
            return pe_ref[e] > jnp.where(e == 0, 0, pe_ref[jnp.maximum(e - 1, 0)])

        def zero_block(blk):
            return pltpu.make_async_copy(zero_ref, _block_rows(xbuf_ref, blk), zsem)

        def start_e(e, carry):
            @pl.when(has_rows(e))
            def _():
                zero_block(last_block(e)).start()
            return carry

        def wait_e(e, carry):
            @pl.when(has_rows(e))
            def _():
                zero_block(last_block(e)).wait()
            return carry

        def start_b(blk, carry):
            zero_block(blk).start()
            return carry

        def wait_b(blk, carry):
            zero_block(blk).wait()
            return carry

        lax.fori_loop(0, MOE_EXPERTS, start_e, 0)
        lax.fori_loop(nu_ref[0], n_blocks, start_b, 0)
        lax.fori_loop(0, MOE_EXPERTS, wait_e, 0)
        lax.fori_loop(nu_ref[0], n_blocks, wait_b, 0)

    dtile = dest_ref.shape[2]

    def issue(tk, carry):
        for s in range(MOE_TOP_K):
            pltpu.make_async_copy(_row_tile(xr_ref, tk), _row_tile(xbuf_ref, dest_ref[tk // dtile, s, tk % dtile]),
                                  sem).start(priority=s)
        return carry

    for tk in range(tm):
        issue(tk, 0)
    for s in range(MOE_TOP_K):
        pltpu.make_async_copy(xr_ref, xbuf_ref.at[pl.ds(0, tm * SUBLANES), :], sem).wait()


def _dispatch(xr, dest3, pad_end, n_used, n_blocks, tm):
    tiles = tm // dest3.shape[2]
    nt = dest3.shape[0] // tiles
    grid_spec = pltpu.PrefetchScalarGridSpec(
        num_scalar_prefetch=2,
        grid=(nt,),
        in_specs=[
            pl.BlockSpec((tiles, MOE_TOP_K, dest3.shape[2]), lambda i, pe, nu: (i, 0, 0), memory_space=pltpu.SMEM),
            pl.BlockSpec((tm * SUBLANES, LANES), lambda i, pe, nu: (i, 0)),
        ],
        out_specs=pl.BlockSpec(memory_space=pl.ANY),
        scratch_shapes=[pltpu.VMEM((MOE_BLOCK * SUBLANES, LANES), F32), pltpu.SemaphoreType.DMA,
                        pltpu.SemaphoreType.DMA],
    )
    return pl.pallas_call(
        functools.partial(_dispatch_kernel, tm=tm, n_blocks=n_blocks),
        grid_spec=grid_spec,
        out_shape=jax.ShapeDtypeStruct((n_blocks * MOE_BLOCK * SUBLANES, LANES), F32),
        compiler_params=_cparams(("arbitrary",)),
        name="dispatch",
    )(pad_end, n_used, dest3, xr)


def _experts_kernel(be_ref, nu_ref, nxt_ref, par_ref, x_ref, wg_hbm, wu_hbm, wd_hbm, y_ref,
                    x2_ref, wgb_ref, wub_ref, wdb_ref, wgf_ref, wuf_ref, wdf_ref, sems, *, d, layer):
    rows = MOE_BLOCK
    b = pl.program_id(0)
    live = b < nu_ref[0]
    e = be_ref[b]
    new_expert = jnp.logical_or(b == 0, e != be_ref[jnp.maximum(b - 1, 0)])

    def weight_copies(ex, slot):
        return [pltpu.make_async_copy(src.at[layer, ex], dst.at[slot], sems.at[slot, j])
                for j, (src, dst) in enumerate(((wg_hbm, wgf_ref), (wu_hbm, wuf_ref), (wd_hbm, wdf_ref)))]

    @pl.when(jnp.logical_and(live, new_expert))
    def _():
        slot = par_ref[e]

        @pl.when(b == 0)
        def _():
            for cp in weight_copies(e, slot):
                cp.start()

        @pl.when(nxt_ref[e] >= 0)
        def _():
            for cp in weight_copies(nxt_ref[e], 1 - slot):
                cp.start()

        for cp in weight_copies(e, slot):
            cp.wait()
        wgb_ref[...] = wgf_ref[slot].astype(BF16)
        wub_ref[...] = wuf_ref[slot].astype(BF16)
        wdb_ref[...] = wdf_ref[slot].astype(BF16)

    @pl.when(live)
    def _():
        for c in range(d // LANES):
            x2_ref[:, c * LANES:(c + 1) * LANES] = x_ref[pl.ds(c, rows, stride=SUBLANES), :].astype(BF16)
        x2 = x2_ref[...]
        gp = _dot(x2, wgb_ref[...])
        up = _dot(x2, wub_ref[...])
        hid = (gp * _sigmoid(gp) * up).astype(BF16)
        y = _dot(hid, wdb_ref[...])
        for c in range(d // LANES):
            y_ref[pl.ds(c, rows, stride=SUBLANES), :] = y[:, c * LANES:(c + 1) * LANES]


def _experts(xbuf, block_e, n_used, padded, w_gate, w_up, w_down, layer):
    n_blocks = block_e.shape[0]
    d, dff = w_gate.shape[2], w_gate.shape[3]
    owns = padded > 0
    ids = jnp.arange(MOE_EXPERTS, dtype=jnp.int32)
    later = jnp.where(owns, ids, MOE_EXPERTS)
    nxt = lax.cummin(jnp.concatenate([later[1:], jnp.full((1,), MOE_EXPERTS, jnp.int32)]), reverse=True)
    nxt = jnp.where(nxt < MOE_EXPERTS, nxt, -1).astype(jnp.int32)
    par = ((jnp.cumsum(owns.astype(jnp.int32)) - owns.astype(jnp.int32)) % 2).astype(jnp.int32)
    xmap = lambda b, be, nu, nx, pa: (jnp.minimum(b, nu[0] - 1), 0)
    grid_spec = pltpu.PrefetchScalarGridSpec(
        num_scalar_prefetch=4,
        grid=(n_blocks,),
        in_specs=[
            pl.BlockSpec((MOE_BLOCK * SUBLANES, LANES), xmap),
            pl.BlockSpec(memory_space=pl.ANY),
            pl.BlockSpec(memory_space=pl.ANY),
            pl.BlockSpec(memory_space=pl.ANY),
        ],
        out_specs=pl.BlockSpec((MOE_BLOCK * SUBLANES, LANES), xmap),
        scratch_shapes=[pltpu.VMEM((MOE_BLOCK, d), BF16), pltpu.VMEM((d, dff), BF16), pltpu.VMEM((d, dff), BF16),
                        pltpu.VMEM((dff, d), BF16), pltpu.VMEM((2, d, dff), F32), pltpu.VMEM((2, d, dff), F32),
                        pltpu.VMEM((2, dff, d), F32), pltpu.SemaphoreType.DMA((2, 3))],
    )
    return pl.pallas_call(
        functools.partial(_experts_kernel, d=d, layer=layer),
        grid_spec=grid_spec,
        out_shape=jax.ShapeDtypeStruct(xbuf.shape, F32),
        input_output_aliases={4: 0},
        compiler_params=_cparams(("arbitrary",)),
        name="experts",
    )(block_e, n_used, nxt, par, xbuf, w_gate, w_up, w_down)


def _gather_combine(dcur_ref, dnxt_ref, ybuf_ref, h_ref, gate_ref, y_ref, sems, *, tm, d):
    i = pl.program_id(0)
    dtile = dcur_ref.shape[2]

    def start(dref, slot):
        def issue(tk, carry):
            for s in range(MOE_TOP_K):
                pltpu.make_async_copy(_row_tile(ybuf_ref, dref[tk // dtile, s, tk % dtile]),
                                      _row_tile(y_ref.at[slot, s], tk), sems.at[slot]).start(priority=s)
            return carry

        for tk in range(tm):
            issue(tk, 0)

    @pl.when(i == 0)
    def _():
        start(dcur_ref, 0)

    @pl.when(i + 1 < pl.num_programs(0))
    def _():
        start(dnxt_ref, (i + 1) % 2)

    slot = i % 2
    for s in range(MOE_TOP_K):
        pltpu.make_async_copy(ybuf_ref.at[pl.ds(0, tm * SUBLANES), :], y_ref.at[slot, s], sems.at[slot]).wait()
    gate_t = jnp.concatenate([gate_ref[...], jnp.zeros((SUBLANES - MOE_TOP_K, tm), F32)], axis=0).T
    g0 = gate_t[:, 0:1]
    g1 = gate_t[:, 1:2]
    cols = []
    for c in range(d // LANES):
        y0 = y_ref[slot, 0, pl.ds(c, tm, stride=SUBLANES), :]
        y1 = y_ref[slot, 1, pl.ds(c, tm, stride=SUBLANES), :]
        cols.append(h_ref[:, c * LANES:(c + 1) * LANES] + (y0 * g0 + y1 * g1))
    return jnp.concatenate(cols, axis=1)


def _combine_specs(tm, d, dest3):
    tiles = tm // dest3.shape[2]
    nt = dest3.shape[0] // tiles
    dblk = (tiles, MOE_TOP_K, dest3.shape[2])
    in_specs = [
        pl.BlockSpec(dblk, lambda i: (i, 0, 0), memory_space=pltpu.SMEM),
        pl.BlockSpec(dblk, lambda i: (jnp.minimum(i + 1, nt - 1), 0, 0), memory_space=pltpu.SMEM),
        pl.BlockSpec(memory_space=pl.ANY),
        pl.BlockSpec((tm, d), lambda i: (i, 0)),
        pl.BlockSpec((MOE_TOP_K, tm), lambda i: (0, i)),
    ]
    scratch = [pltpu.VMEM((2, MOE_TOP_K, tm * SUBLANES, LANES), F32), pltpu.SemaphoreType.DMA((2,))]
    return nt, in_specs, scratch


def _final_kernel(dcur_ref, dnxt_ref, ybuf_ref, h_ref, gate_ref, fg_ref, o_ref, y_ref, sems, *, tm, d):
    out = _gather_combine(dcur_ref, dnxt_ref, ybuf_ref, h_ref, gate_ref, y_ref, sems, tm=tm, d=d)
    o_ref[...] = _rms_unit(out) * fg_ref[...]


def _final(moe, final_gain, tm):
    ybuf, dest3, gate, h = moe
    t, d = h.shape
    nt, in_specs, scratch = _combine_specs(tm, d, dest3)
    return pl.pallas_call(
        functools.partial(_final_kernel, tm=tm, d=d),
        grid=(nt,),
        in_specs=in_specs + [pl.BlockSpec((1, d), lambda i: (0, 0))],
        out_specs=pl.BlockSpec((tm, d), lambda i: (i, 0)),
        out_shape=jax.ShapeDtypeStruct((t, d), F32),
        scratch_shapes=scratch,
        compiler_params=_cparams(("arbitrary",)),
        name="final",
    )(dest3, dest3, ybuf, h, gate, final_gain.reshape(1, d))


def _dest_kernel(ps_ref, eidx_ref, rank_ref, dest_ref, *, tm):
    eidx = eidx_ref[...]
    dest = rank_ref[...]
    for e in range(MOE_EXPERTS):
        dest = dest + jnp.where(eidx == e, ps_ref[e], 0)
    for i in range(dest_ref.shape[0]):
        dest_ref[i] = dest[:, i * tm:(i + 1) * tm]


def _dest(pad_start, eidx, rank, tm):
    t = eidx.shape[1]
    whole = lambda i, ps: (0, 0)
    grid_spec = pltpu.PrefetchScalarGridSpec(
        num_scalar_prefetch=1,
        grid=(1,),
        in_specs=[pl.BlockSpec((MOE_TOP_K, t), whole), pl.BlockSpec((MOE_TOP_K, t), whole)],
        out_specs=pl.BlockSpec((t // tm, MOE_TOP_K, tm), lambda i, ps: (0, 0, 0)),
    )
    return pl.pallas_call(
        functools.partial(_dest_kernel, tm=tm),
        grid_spec=grid_spec,
        out_shape=jax.ShapeDtypeStruct((t // tm, MOE_TOP_K, tm), jnp.int32),
        compiler_params=_cparams(("arbitrary",)),
        name="dest",
    )(pad_start, eidx, rank)


def _moe_experts(xr, eidx, gate, rank, counts, h, w_gate, w_up, w_down, layer, tm_row):
    t = h.shape[0]
    n_slot = t * MOE_TOP_K
    n_blocks = -(-n_slot // MOE_BLOCK) + MOE_EXPERTS
    cnt = counts[:, 0].astype(jnp.int32)
    padded = (cnt + MOE_BLOCK - 1) // MOE_BLOCK * MOE_BLOCK
    pad_end = jnp.cumsum(padded)
    pad_start = pad_end - padded
    block_start = jnp.arange(n_blocks, dtype=jnp.int32) * MOE_BLOCK
    block_e = jnp.minimum(jnp.sum((pad_end[None, :] <= block_start[:, None]).astype(jnp.int32), axis=1),
                          MOE_EXPERTS - 1)
    n_used = (pad_end[-1:] // MOE_BLOCK).astype(jnp.int32)
    dest3 = _dest(pad_start.astype(jnp.int32), eidx, rank, tm_row)
    tm_dispatch = WIDE_TILE if (WIDE_TILE % tm_row == 0 and t % WIDE_TILE == 0) else tm_row
    xbuf = _dispatch(xr, dest3, pad_end.astype(jnp.int32), n_used, n_blocks, tm_dispatch)
    ybuf = _experts(xbuf, block_e, n_used, padded, w_gate, w_up, w_down, layer)
    return ybuf, dest3, gate, h


def _fox_in_kernel(dcur_ref, dnxt_ref, ybuf_ref, h_ref, gate_ref, gkv_ref, gq_ref, wkv_ref, wf_ref, bf_ref, wq_ref,
                   place_ref, hn_ref, k_ref, v_ref, q_ref, og_ref, ka_ref, y_ref, sems, carry_ref,
                   *, tm, sub, d, width, tiles_per_seq, qscale):
    i = pl.program_id(0)

    @pl.when(i % tiles_per_seq == 0)
    def _():
        carry_ref[...] = jnp.zeros_like(carry_ref)

    hn_ref[...] = _gather_combine(dcur_ref, dnxt_ref, ybuf_ref, h_ref, gate_ref, y_ref, sems, tm=tm, d=d)
    ri = lax.broadcasted_iota(jnp.int32, (sub, sub), 0)
    ci = lax.broadcasted_iota(jnp.int32, (sub, sub), 1)
    ltri = jnp.where(ri >= ci, 1.0, 0.0).astype(BF16)
    carry = carry_ref[0:1, :]
    for u in range(tm // sub):
        rows = slice(u * sub, (u + 1) * sub)
        y = _rms_unit(hn_ref[rows, :])
        xkv = (y * gkv_ref[...]).astype(BF16)
        xq = (y * gq_ref[...]).astype(BF16)
        k_ref[rows, :] = _dot(xkv, wkv_ref[:, 0:width]).astype(BF16)
        v_ref[rows, :] = _dot(xkv, wkv_ref[:, width:2 * width]).astype(BF16)
        q_ref[rows, :] = (_dot(xq, wq_ref[:, 0:width]) * qscale).astype(BF16)
        og_ref[rows, :] = _dot(xq, wq_ref[:, width:2 * width]).astype(BF16)

        log_f = _log_sigmoid(_dot(xkv, wf_ref[...]) + bf_ref[...])
        f_hi, f_mid = _split_bf16(log_f)
        f_lo = (log_f - f_hi.astype(F32) - f_mid.astype(F32)).astype(BF16)
        c = carry + (_dot(ltri, f_hi) + (_dot(ltri, f_mid) + _dot(ltri, f_lo)))
        carry = c[sub - 1:sub, :]
        nc = c * (-LOG2_E)
        n_hi, n_mid = _split_bf16(nc)
        n_lo = (nc - n_hi.astype(F32) - n_mid.astype(F32)).astype(BF16)
        ka_ref[rows, :] = _dot(jnp.concatenate([n_hi, n_mid, n_lo], axis=1), place_ref[...]).astype(BF16)
    carry_ref[0:1, :] = carry


ATTN_BIAS_LANES = 3
LOG2_E = 1.4426950408889634


def _bias_lane0(hh, dh):
    return dh if hh == 0 else 0


def _fox_in(moe, kv_gain, q_gain, kv_w, kv_b_f, w_in, seq, tm):
    ybuf, dest3, gate, h = moe
    t, d = h.shape
    width = w_in.shape[1] // 2
    heads = kv_w.shape[1] - 2 * width
    dh = width // heads
    place = np.zeros((ATTN_BIAS_LANES * heads, width), np.float32)
    for part in range(ATTN_BIAS_LANES):
        for hd in range(heads):
            place[part * heads + hd, (hd // 2) * LANES + _bias_lane0(hd % 2, dh) + part] = 1.0
    const = lambda i: (0, 0)
    row = lambda i: (i, 0)
    big = jax.ShapeDtypeStruct((t, width), BF16)
    nt, comb_specs, comb_scratch = _combine_specs(tm, d, dest3)
    return pl.pallas_call(
        functools.partial(_fox_in_kernel, tm=tm, sub=_tile(tm, SUB_TILE), d=d, width=width, tiles_per_seq=seq // tm,
                          qscale=dh ** -0.5 * LOG2_E),
        grid=(nt,),
        in_specs=comb_specs + [
            pl.BlockSpec((1, d), const),
            pl.BlockSpec((1, d), const),
            pl.BlockSpec((d, 2 * width), const, pipeline_mode=pl.Buffered(1)),
            pl.BlockSpec((d, heads), const),
            pl.BlockSpec((1, heads), const),
            pl.BlockSpec((d, 2 * width), const, pipeline_mode=pl.Buffered(1)),
            pl.BlockSpec(place.shape, const),
        ],
        out_specs=[pl.BlockSpec((tm, d), row)] + [pl.BlockSpec((tm, width), row)] * 5,
        out_shape=[jax.ShapeDtypeStruct((t, d), F32)] + [big] * 5,
        scratch_shapes=comb_scratch + [pltpu.VMEM((SUBLANES, heads), F32)],
        compiler_params=_cparams(("arbitrary",)),
        name="fox_in",
    )(dest3, dest3, ybuf, h, gate, kv_gain.reshape(1, d), q_gain.reshape(1, d), kv_w[:, :2 * width].astype(BF16),
      kv_w[:, 2 * width:].astype(BF16), kv_b_f.reshape(1, heads), w_in.astype(BF16), jnp.asarray(place, BF16))


ATTN_KV_CHUNK = 512


def _attn_kernel(q_ref, k_ref, v_ref, og_ref, kb_ref, o_ref, qa_ref, ka_ref, va_ref, s_ref, *, seq, tq, dh, pairs):
    lane = lax.broadcasted_iota(jnp.int32, (seq, LANES), 1)
    heads_here = [(pr, hh) for pr in range(pairs) for hh in range(2)]
    for n, (pr, hh) in enumerate(heads_here):
        cols = slice(pr * LANES, (pr + 1) * LANES)
        a0 = _bias_lane0(hh, dh)
        own = (lane < dh) if hh == 0 else (lane >= dh)
        ones_q = jnp.where((lane >= a0) & (lane < a0 + ATTN_BIAS_LANES), 1.0, 0.0).astype(BF16)
        qa_ref[n] = jnp.where(own, q_ref[:, cols], ones_q)
        ka_ref[n] = jnp.where(own, k_ref[:, cols], kb_ref[:, cols])
        va_ref[n] = jnp.where(own, v_ref[:, cols], jnp.where(lane == a0, 1.0, 0.0).astype(BF16))

    ri = lax.broadcasted_iota(jnp.int32, (tq, tq), 0)
    ci = lax.broadcasted_iota(jnp.int32, (tq, tq), 1)
    causal = ri >= ci
    lane_q = lax.broadcasted_iota(jnp.int32, (tq, LANES), 1)

    def fold_max(mx, s):
        for g in range(s.shape[1] // LANES):
            mx = jnp.maximum(mx, s[:, g * LANES:(g + 1) * LANES])
        return mx

    for qi in range(seq // tq):
        rows = slice(qi * tq, (qi + 1) * tq)
        past = [(c0, min(c0 + ATTN_KV_CHUNK, qi * tq)) for c0 in range(0, qi * tq, ATTN_KV_CHUNK)]
        row_max = []
        for n in range(len(heads_here)):
            qa = qa_ref[n, rows, :]
            mx = jnp.full((tq, LANES), -jnp.inf, F32)
            for c0, c1 in past:
                s = _dot_nt(qa, ka_ref[n, c0:c1, :])
                s_ref[n, :, c0:c1] = s
                mx = fold_max(mx, s)
            s = jnp.where(causal, _dot_nt(qa, ka_ref[n, rows, :]), -jnp.inf)
            s_ref[n, :, rows] = s
            row_max.append(jnp.max(fold_max(mx, s), axis=1, keepdims=True))
        outs = []
        for n, (pr, hh) in enumerate(heads_here):
            acc = jnp.zeros((tq, LANES), F32)
            for c0, c1 in past + [(qi * tq, (qi + 1) * tq)]:
                p = jnp.exp2(s_ref[n, :, c0:c1] - row_max[n]).astype(BF16)
                acc = acc + _dot(p, va_ref[n, c0:c1, :])
            a0 = _bias_lane0(hh, dh)
            outs.append(acc / acc[:, a0:a0 + 1])
        for pr in range(pairs):
            cols = slice(pr * LANES, (pr + 1) * LANES)
            o = jnp.where(lane_q < dh, outs[2 * pr], outs[2 * pr + 1])
            o_ref[rows, cols] = (o * _sigmoid(og_ref[rows, cols].astype(F32))).astype(BF16)


ATTN_PAIRS = 2


def _attn(q, k, v, og, kb, heads, batch, seq, tq):
    t, width = q.shape
    dh = width // heads
    assert 2 * dh == LANES, "two heads per 128-lane block"
    pairs = ATTN_PAIRS
    blk = pl.BlockSpec((seq, pairs * LANES), lambda b, hp: (b, hp))
    return pl.pallas_call(
        functools.partial(_attn_kernel, seq=seq, tq=tq, dh=dh, pairs=pairs),
        grid=(batch, heads // (2 * pairs)),
        in_specs=[blk] * 5,
        out_specs=blk,
        out_shape=jax.ShapeDtypeStruct((t, width), BF16),
        scratch_shapes=[pltpu.VMEM((2 * pairs, seq, LANES), BF16)] * 3 + [pltpu.VMEM((2 * pairs, tq, seq), F32)],
        compiler_params=_cparams(("arbitrary", "arbitrary")),
        name="attn",
    )(q, k, v, og, kb)


ROW_TILE = 512
WIDE_TILE = 1024
SUB_TILE = 256
ATTN_Q_BLOCK = 256


def _tile(n, pref):
    while n % pref:
        pref //= 2
    return pref


def kernel(x, gla_norm, gla_w_in, gla_w_a2, gla_b_a, gla_out_norm, gla_w_out, kv_norm, kv_w, kv_b_f, fox_norm,
           fox_w_in, fox_w_out, ffn_norm, router_coarse_w, router_coarse_b, router_fine_w, router_fine_b,
           expert_w_gate, expert_w_up, expert_w_down, final_norm):
    batch, seq, d = x.shape
    t = batch * seq
    assert gla_norm.shape[0] == 1 and ffn_norm.shape[0] == 2, "trunk is GLA layer + FoX layer"
    tm = _tile(seq, ROW_TILE)
    wide = _tile(seq, WIDE_TILE)

    def ffn(a, w_out, h, layer):
        h, xr, eidx, gate, rank, counts = _post(a, w_out, h, ffn_norm[layer], router_coarse_w[layer],
                                                router_coarse_b[layer], router_fine_w[layer],
                                                router_fine_b[layer], wide, tm)
        return _moe_experts(xr, eidx, gate, rank, counts, h, expert_w_gate, expert_w_up, expert_w_down, layer, tm)

    h = x.reshape(t, d)
    q, k, v, r, la = _gla_in(h, gla_norm[0], gla_w_in[0], gla_w_a2[0], gla_b_a[0], wide)
    a = _gla_core(q, k, v, r, la, gla_out_norm[0], batch, seq, wide, 1)
    moe = ffn(a, gla_w_out[0], h, 0)
    h, kk, vv, qq, og, kb = _fox_in(moe, kv_norm, fox_norm[0], kv_w, kv_b_f, fox_w_in[0], seq, tm)
    a = _attn(qq, kk, vv, og, kb, kv_b_f.shape[0], batch, seq, _tile(seq, ATTN_Q_BLOCK))
    moe = ffn(a, fox_w_out[0], h, 1)
    return _final(moe, final_norm, tm).reshape(batch, seq, d)
```
